```python
import math
import jax
import jax.numpy as jnp
from jax import lax
import numpy as np

D_MODEL = 2048
BATCH = 2
SEQ = 8192
DEPTH = 4

GRID_W = 64
CTX_LEN = 256
NORM_EPS = 1e-6
D_FF = ((8 * D_MODEL // 3 + 255) // 256) * 256
MIX_A = D_MODEL // 2
M_HEADDIM = 64
M_HEADS = MIX_A // M_HEADDIM
M_GROUPS = 4
M_STATE = 128
M_CONV = 4
M_CHUNK = 128
M_XBC = MIX_A + 2 * M_GROUPS * M_STATE
M_NORM_EPS = 1e-5
MIX_B = D_MODEL // 2
R_HEADSIZE = 64
R_HEADS = MIX_B // R_HEADSIZE
R_DECAY_LORA = max(32, int(round(1.8 * D_MODEL ** 0.5 / 32)) * 32)
R_AAA_LORA = max(32, int(round(1.8 * D_MODEL ** 0.5 / 32)) * 32)
R_GATE_LORA = max(32, int(round(0.6 * D_MODEL ** 0.8 / 32)) * 32)
R_SHIFT_W = 3 * MIX_B + R_DECAY_LORA + R_AAA_LORA
R_LN_EPS = 64e-5
IN_AB = MIX_A + M_XBC + M_HEADS + R_SHIFT_W + R_GATE_LORA
C_WIDTH = D_MODEL
C_HEADS = 8
C_BLOCK = C_WIDTH // C_HEADS
C_CONV = 4
RG_C = 8.0

kernel_name = "hybrid_ssd_rwkv7_rglru_flow_trunk"


def rms_norm(x, w, eps=NORM_EPS):
    xf = x.astype(jnp.float32)
    y = xf * lax.rsqrt(jnp.mean(xf * xf, axis=-1, keepdims=True) + eps)
    return (y * w.astype(jnp.float32)).astype(x.dtype)


def modulate(x, w, shift, scale):
    return rms_norm(x, w) * (1.0 + scale) + shift


def swiglu(h, w_gate, w_up, w_down):
    return (jax.nn.silu(h @ w_gate) * (h @ w_up)) @ w_down


def causal_dwconv(x, w, b):
    k_w, ch = w.shape
    y = lax.conv_general_dilated(x, w[:, None, :].astype(x.dtype), window_strides=(1,),
                                 padding=[(k_w - 1, 0)], dimension_numbers=('NWC', 'WIO', 'NWC'),
                                 feature_group_count=ch)
    return y + b


def token_shift(f):
    return jnp.pad(f, ((0, 0), (1, 0), (0, 0)))[:, :-1]


def grid_to_col_major(t):
    bsz, n, ch = t.shape
    rows = n // GRID_W
    return t.reshape(bsz, rows, GRID_W, ch).transpose(0, 2, 1, 3).reshape(bsz, n, ch)


def grid_to_row_major(t):
    bsz, n, ch = t.shape
    rows = n // GRID_W
    return t.reshape(bsz, GRID_W, rows, ch).transpose(0, 2, 1, 3).reshape(bsz, n, ch)


def bidirectional(fn, p_fwd, p_bwd, feats_ctx, feats_lat, zero_state, with_ctx):
    flip = lambda t: jax.tree_util.tree_map(lambda a: jnp.flip(a, axis=1), t)
    add = lambda u, v: jax.tree_util.tree_map(jnp.add, u, v)
    out_cf, s_f = fn(p_fwd, feats_ctx, zero_state)
    out_cb, s_b = fn(p_bwd, flip(feats_ctx), zero_state)
    out_lf, _ = fn(p_fwd, feats_lat, s_f)
    out_lb, _ = fn(p_bwd, flip(feats_lat), s_b)
    out_ctx = add(out_cf, flip(out_cb)) if with_ctx else None
    return out_ctx, add(out_lf, flip(out_lb))


def ssd_chunked(xdt, dta, bm, cm, h0):
    bsz, seqlen, n_heads, hd = xdt.shape
    n_groups, d_state = bm.shape[2], bm.shape[3]
    hpg = n_heads // n_groups
    nc = seqlen // M_CHUNK
    X = xdt.reshape(bsz, nc, M_CHUNK, n_groups, hpg, hd)
    A = dta.astype(jnp.float32).reshape(bsz, nc, M_CHUNK, n_groups, hpg)
    Bc = bm.reshape(bsz, nc, M_CHUNK, n_groups, d_state)
    Cc = cm.reshape(bsz, nc, M_CHUNK, n_groups, d_state)
    a_cs = jnp.cumsum(A, axis=2)
    causal = jnp.tril(jnp.ones((M_CHUNK, M_CHUNK), dtype=bool))[:, :, None, None]
    seg = a_cs[:, :, :, None] - a_cs[:, :, None, :]
    decay_in = jnp.exp(jnp.where(causal, seg, -jnp.inf))
    cb = jnp.einsum('bclgn,bcsgn->bclsg', Cc, Bc)
    y_diag = jnp.einsum('bclsg,bclsgh,bcsghp->bclghp', cb, decay_in, X)
    decay_to_end = jnp.exp(a_cs[:, :, -1:] - a_cs)
    chunk_states = jnp.einsum('bclgn,bclgh,bclghp->bcghpn', Bc, decay_to_end, X)
    chunk_decay = jnp.exp(a_cs[:, :, -1])

    def step(h, inp):
        s, dec = inp
        return h * dec[..., None, None] + s, h

    h_last, h_in = lax.scan(step, h0, (jnp.moveaxis(chunk_states, 1, 0), jnp.moveaxis(chunk_decay, 1, 0)))
    h_in = jnp.moveaxis(h_in, 0, 1)
    y_off = jnp.einsum('bclgn,bcghpn,bclgh->bclghp', Cc, h_in, jnp.exp(a_cs))
    return (y_diag + y_off).reshape(bsz, seqlen, n_heads, hd), h_last


def mamba_direction(p, feats, h0):
    conv_w, conv_b, dt_bias, a_log, d_skip = p
    xbc_raw, dt_raw = feats
    bsz, seqlen, _ = xbc_raw.shape
    xbc = jax.nn.silu(causal_dwconv(xbc_raw, conv_w, conv_b))
    xs = xbc[..., :MIX_A].reshape(bsz, seqlen, M_HEADS, M_HEADDIM)
    bm = xbc[..., MIX_A:MIX_A + M_GROUPS * M_STATE].reshape(bsz, seqlen, M_GROUPS, M_STATE)
    cm = xbc[..., MIX_A + M_GROUPS * M_STATE:].reshape(bsz, seqlen, M_GROUPS, M_STATE)
    dt = jax.nn.softplus((dt_raw + dt_bias).astype(jnp.float32))
    a = -jnp.exp(a_log.astype(jnp.float32))
    y, h_last = ssd_chunked(xs * dt[..., None], dt * a, bm, cm, h0)
    y = y + xs * d_skip[:, None]
    return y.reshape(bsz, seqlen, MIX_A).astype(xbc_raw.dtype), h_last


def rwkv_direction(p, feats, s0):
    mu, w0, w2, a0, a2, k_k, k_a, r_k = p
    f = feats + (token_shift(feats) - feats) * mu
    bsz, seqlen, _ = f.shape
    r = f[..., :MIX_B]
    k = f[..., MIX_B:2 * MIX_B]
    v = f[..., 2 * MIX_B:3 * MIX_B]
    wl = f[..., 3 * MIX_B:3 * MIX_B + R_DECAY_LORA]
    al = f[..., 3 * MIX_B + R_DECAY_LORA:]
    w_log = -jax.nn.softplus(-(w0 + jnp.tanh(wl) @ w2).astype(jnp.float32)) - 0.5
    decay = jnp.exp(-jnp.exp(w_log))
    a = jax.nn.sigmoid((a0 + al @ a2).astype(jnp.float32))
    heads = lambda t: t.astype(jnp.float32).reshape(bsz, seqlen, R_HEADS, R_HEADSIZE)
    kk = heads(k * k_k)
    kk = kk / jnp.maximum(jnp.sqrt(jnp.sum(kk * kk, axis=-1, keepdims=True)), 1e-12)
    k = k * (1.0 + (a - 1.0) * k_a)
    r_h, k_h, v_h, w_h, a_h = heads(r), heads(k), heads(v), heads(decay), heads(a)
    b_h = kk * a_h

    def step(S, inp):
        r_t, w_t, k_t, v_t, kk_t, b_t = inp
        sa = jnp.einsum('bhvk,bhk->bhv', S, kk_t)
        S = S * w_t[:, :, None, :] - sa[..., None] * b_t[:, :, None, :] + v_t[..., None] * k_t[:, :, None, :]
        return S, jnp.einsum('bhvk,bhk->bhv', S, r_t)

    tm = lambda t: jnp.moveaxis(t, 1, 0)
    s_last, y = lax.scan(step, s0, (tm(r_h), tm(w_h), tm(k_h), tm(v_h), tm(kk), tm(b_h)))
    y = jnp.moveaxis(y, 0, 1)
    bonus = jnp.sum(r_h * k_h * r_k, axis=-1, keepdims=True) * v_h
    return (y, bonus), s_last


def rglru_direction(p, feats, h0):
    conv_w, conv_b, wa, ba, wx, bx, lam = p
    bsz, seqlen, _ = feats.shape
    xc = causal_dwconv(feats, conv_w, conv_b)
    xh = xc.reshape(bsz, seqlen, C_HEADS, C_BLOCK)
    gate_r = jax.nn.sigmoid((jnp.einsum('blhi,hij->blhj', xh, wa).reshape(bsz, seqlen, C_WIDTH) + ba).astype(jnp.float32))
    gate_i = jax.nn.sigmoid((jnp.einsum('blhi,hij->blhj', xh, wx).reshape(bsz, seqlen, C_WIDTH) + bx).astype(jnp.float32))
    log_a = -RG_C * gate_r * jax.nn.softplus(-lam.astype(jnp.float32))
    a = jnp.exp(log_a)
    mult = jnp.sqrt(jnp.maximum(-jnp.expm1(2.0 * log_a), 0.0))
    u = mult * gate_i * xc.astype(jnp.float32)
    u = u.at[:, 0].add(a[:, 0] * h0)

    def combine(left, right):
        return left[0] * right[0], right[0] * left[1] + right[1]

    _, h = lax.associative_scan(combine, (a, u), axis=1)
    return h.astype(feats.dtype), h[:, -1]


def grouped_rmsnorm(y, w, groups):
    bsz, seqlen, ch = y.shape
    yf = y.astype(jnp.float32).reshape(bsz, seqlen, groups, ch // groups)
    yf = yf * lax.rsqrt(jnp.mean(yf * yf, axis=-1, keepdims=True) + M_NORM_EPS)
    return (yf.reshape(bsz, seqlen, ch) * w).astype(y.dtype)


def rwkv_readout(y, bonus, gl, lnx_w, lnx_b, g2):
    bsz, seqlen = y.shape[:2]
    mu = jnp.mean(y, axis=-1, keepdims=True)
    var = jnp.mean(jnp.square(y - mu), axis=-1, keepdims=True)
    yn = ((y - mu) * lax.rsqrt(var + R_LN_EPS)).reshape(bsz, seqlen, MIX_B) * lnx_w + lnx_b
    out = yn + bonus.reshape(bsz, seqlen, MIX_B)
    gate = jax.nn.sigmoid(gl) @ g2
    return (out * gate).astype(gl.dtype)


def mixer_ssd_rwkv(h_ctx, h_lat, w_in, w_out, m_conv_w, m_conv_b, m_dt_bias, m_a_log, m_d, m_norm_w,
                   r_mu, r_w0, r_w2, r_a0, r_a2, r_kk, r_ka, r_rk, r_g2, r_lnx_w, r_lnx_b, with_ctx):
    bsz = h_lat.shape[0]
    o1 = MIX_A
    o2 = o1 + M_XBC
    o3 = o2 + M_HEADS
    o4 = o3 + R_SHIFT_W

    def project(h):
        pr = h @ w_in
        return pr[..., :o1], pr[..., o1:o2], pr[..., o2:o3], pr[..., o3:o4], pr[..., o4:]

    z_c, xbc_c, dt_c, rf_c, gl_c = project(h_ctx)
    z_l, xbc_l, dt_l, rf_l, gl_l = project(h_lat)
    m_dir = lambda d: (m_conv_w[d], m_conv_b[d], m_dt_bias[d], m_a_log[d], m_d[d])
    r_dir = lambda d: (r_mu[d], r_w0[d], r_w2[d], r_a0[d], r_a2[d], r_kk[d], r_ka[d], r_rk[d])
    m_zero = jnp.zeros((bsz, M_GROUPS, M_HEADS // M_GROUPS, M_HEADDIM, M_STATE), jnp.float32)
    r_zero = jnp.zeros((bsz, R_HEADS, R_HEADSIZE, R_HEADSIZE), jnp.float32)
    ym_c, ym_l = bidirectional(mamba_direction, m_dir(0), m_dir(1), (xbc_c, dt_c), (xbc_l, dt_l), m_zero, with_ctx)
    yr_c, yr_l = bidirectional(rwkv_direction, r_dir(0), r_dir(1), rf_c, rf_l, r_zero, with_ctx)

    def readout(ym, z, yr, gl):
        a_out = grouped_rmsnorm(ym * jax.nn.silu(z), m_norm_w, M_GROUPS)
        b_out = rwkv_readout(yr[0], yr[1], gl, r_lnx_w, r_lnx_b, r_g2)
        return jnp.concatenate([a_out, b_out], axis=-1) @ w_out

    out_lat = readout(ym_l, z_l, yr_l, gl_l)
    out_ctx = readout(ym_c, z_c, yr_c, gl_c) if with_ctx else None
    return out_ctx, out_lat


def mixer_rglru(h_ctx, h_lat, w_in, w_out, conv_w, conv_b, wa, ba, wx, bx, lam, with_ctx):
    bsz = h_lat.shape[0]
    p_lat = h_lat @ w_in
    gy_lat = jax.nn.gelu(p_lat[..., :C_WIDTH])
    xb_lat = grid_to_col_major(p_lat[..., C_WIDTH:])
    if with_ctx:
        p_ctx = h_ctx @ w_in
        gy_ctx = jax.nn.gelu(p_ctx[..., :C_WIDTH])
        xb_ctx = p_ctx[..., C_WIDTH:]
    else:
        xb_ctx = h_ctx @ w_in[:, C_WIDTH:]
    dirp = lambda d: (conv_w[d], conv_b[d], wa[d], ba[d], wx[d], bx[d], lam[d])
    zero = jnp.zeros((bsz, C_WIDTH), jnp.float32)
    hs_ctx, hs_lat = bidirectional(rglru_direction, dirp(0), dirp(1), xb_ctx, xb_lat, zero, with_ctx)
    out_lat = (grid_to_row_major(hs_lat) * gy_lat) @ w_out
    out_ctx = (hs_ctx * gy_ctx) @ w_out if with_ctx else None
    return out_ctx, out_lat


def setup_inputs(seed: int = 0) -> dict:
    key = jax.random.key(seed)
    keys = iter(jax.random.split(key, 64))

    def nrm(shape, scale):
        return scale * jax.random.normal(next(keys), shape, jnp.float32)

    def uni(shape, lo, hi):
        return jax.random.uniform(next(keys), shape, jnp.float32, lo, hi)

    ne, no = (DEPTH + 1) // 2, DEPTH // 2
    D = D_MODEL
    dt = jnp.exp(uni((ne, 2, M_HEADS), math.log(1e-3), math.log(1e-1)))
    a_target = uni((no, 2, C_WIDTH), 0.9, 0.999) ** (1.0 / RG_C)
    return {
        'x': nrm((BATCH, SEQ, D), 1.0),
        'c': nrm((BATCH, D), 1.0),
        'ctx': nrm((BATCH, CTX_LEN, D), 1.0),
        'c_ctx': nrm((D,), 1.0),
        'ada_w': nrm((DEPTH, D, 6 * D), 0.5 * D ** -0.5),
        'ada_b': nrm((DEPTH, 6 * D), 0.01),
        'norm1_w': 1.0 + nrm((DEPTH, D), 0.02),
        'norm2_w': 1.0 + nrm((DEPTH, D), 0.02),
        'ffn_w_gate': nrm((DEPTH, D, D_FF), D ** -0.5),
        'ffn_w_up': nrm((DEPTH, D, D_FF), D ** -0.5),
        'ffn_w_down': nrm((DEPTH, D_FF, D), D_FF ** -0.5),
        'final_norm_w': 1.0 + nrm((D,), 0.02),
        'ab_w_in': nrm((ne, D, IN_AB), D ** -0.5),
        'ab_w_out': nrm((ne, MIX_A + MIX_B, D), (MIX_A + MIX_B) ** -0.5),
        'm_conv_w': nrm((ne, 2, M_CONV, M_XBC), M_CONV ** -0.5),
        'm_conv_b': nrm((ne, 2, M_XBC), 0.01),
        'm_dt_bias': dt + jnp.log(-jnp.expm1(-dt)),
        'm_a_log': jnp.log(uni((ne, 2, M_HEADS), 1.0, 16.0)),
        'm_d': 1.0 + nrm((ne, 2, M_HEADS), 0.1),
        'm_norm_w': 1.0 + nrm((ne, MIX_A), 0.02),
        'r_mu': uni((ne, 2, R_SHIFT_W), 0.1, 0.9),
        'r_w0': uni((ne, 2, MIX_B), -6.0, -1.0),
        'r_w2': nrm((ne, 2, R_DECAY_LORA, MIX_B), 0.5 * R_DECAY_LORA ** -0.5),
        'r_a0': nrm((ne, 2, MIX_B), 0.1),
        'r_a2': nrm((ne, 2, R_AAA_LORA, MIX_B), 0.5 * R_AAA_LORA ** -0.5),
        'r_kk': 0.85 + nrm((ne, 2, MIX_B), 0.02),
        'r_ka': 1.0 + nrm((ne, 2, MIX_B), 0.02),
        'r_rk': nrm((ne, 2, R_HEADS, R_HEADSIZE), 0.1),
        'r_g2': nrm((ne, R_GATE_LORA, MIX_B), R_GATE_LORA ** -0.5),
        'r_lnx_w': 1.0 + nrm((ne, MIX_B), 0.02),
        'r_lnx_b': nrm((ne, MIX_B), 0.01),
        'c_w_in': nrm((no, D, 2 * C_WIDTH), D ** -0.5),
        'c_w_out': nrm((no, C_WIDTH, D), C_WIDTH ** -0.5),
        'c_conv_w': nrm((no, 2, C_CONV, C_WIDTH), C_CONV ** -0.5),
        'c_conv_b': nrm((no, 2, C_WIDTH), 0.01),
        'c_wa': nrm((no, 2, C_HEADS, C_BLOCK, C_BLOCK), C_BLOCK ** -0.5),
        'c_ba': nrm((no, 2, C_WIDTH), 0.01),
        'c_wx': nrm((no, 2, C_HEADS, C_BLOCK, C_BLOCK), C_BLOCK ** -0.5),
        'c_bx': nrm((no, 2, C_WIDTH), 0.01),
        'c_lambda': jnp.log(a_target) - jnp.log1p(-a_target),
    }


def reference(x, c, ctx, c_ctx, ada_w, ada_b, norm1_w, norm2_w, ffn_w_gate, ffn_w_up, ffn_w_down,
              final_norm_w, ab_w_in, ab_w_out, m_conv_w, m_conv_b, m_dt_bias, m_a_log, m_d, m_norm_w,
              r_mu, r_w0, r_w2, r_a0, r_a2, r_kk, r_ka, r_rk, r_g2, r_lnx_w, r_lnx_b,
              c_w_in, c_w_out, c_conv_w, c_conv_b, c_wa, c_ba, c_wx, c_bx, c_lambda):
    silu_c = jax.nn.silu(c)
    silu_cc = jax.nn.silu(c_ctx)
    for layer in range(DEPTH):
        with_ctx = layer < DEPTH - 1
        mod = silu_c @ ada_w[layer] + ada_b[layer]
        sh1, sc1, g1, sh2, sc2, g2 = [t[:, None, :] for t in jnp.split(mod, 6, axis=-1)]
        mod_c = silu_cc @ ada_w[layer] + ada_b[layer]
        csh1, csc1, cg1, csh2, csc2, cg2 = jnp.split(mod_c, 6)
        h_lat = modulate(x, norm1_w[layer], sh1, sc1)
        h_ctx = modulate(ctx, norm1_w[layer], csh1, csc1)
        if layer % 2 == 0:
            e = layer // 2
            o_ctx, o_lat = mixer_ssd_rwkv(h_ctx, h_lat, ab_w_in[e], ab_w_out[e], m_conv_w[e], m_conv_b[e],
                                          m_dt_bias[e], m_a_log[e], m_d[e], m_norm_w[e], r_mu[e], r_w0[e],
                                          r_w2[e], r_a0[e], r_a2[e], r_kk[e], r_ka[e], r_rk[e], r_g2[e],
                                          r_lnx_w[e], r_lnx_b[e], with_ctx)
        else:
            o = layer // 2
            o_ctx, o_lat = mixer_rglru(h_ctx, h_lat, c_w_in[o], c_w_out[o], c_conv_w[o], c_conv_b[o],
                                       c_wa[o], c_ba[o], c_wx[o], c_bx[o], c_lambda[o], with_ctx)
        x = x + g1 * o_lat
        x = x + g2 * swiglu(modulate(x, norm2_w[layer], sh2, sc2),
                            ffn_w_gate[layer], ffn_w_up[layer], ffn_w_down[layer])
        if with_ctx:
            ctx = ctx + cg1 * o_ctx
            ctx = ctx + cg2 * swiglu(modulate(ctx, norm2_w[layer], csh2, csc2),
                                     ffn_w_gate[layer], ffn_w_up[layer], ffn_w_down[layer])
    return rms_norm(x, final_norm_w)
```

```python
import functools
import math

import jax
import jax.numpy as jnp
from jax import lax
from jax.experimental import pallas as pl
from jax.experimental.pallas import tpu as pltpu

F32 = jnp.float32
BF16 = jnp.bfloat16

NORM_EPS = 1e-6
GRID_W = 64
M_HEADDIM = 64
M_GROUPS = 4
M_STATE = 128
M_CONV = 4
M_CHUNK = 128
M_NORM_EPS = 1e-5
R_HEADSIZE = 64
R_CHUNK = 64
R_LN_EPS = 64e-5
C_HEADS = 8
C_CONV = 4
RG_C = 8.0

VMEM_LIMIT_BYTES = 56 * 1024 * 1024
HALO = 8


def _cparams(*sem):
    return pltpu.CompilerParams(dimension_semantics=sem, vmem_limit_bytes=VMEM_LIMIT_BYTES)


def _bdot(a, b):
    return jnp.dot(a.astype(BF16), b.astype(BF16), preferred_element_type=F32)


def _bdot_nt(a, b):
    return lax.dot_general(a.astype(BF16), b.astype(BF16), (((1,), (1,)), ((), ())),
                           preferred_element_type=F32)


def _split3(x):
    hi = x.astype(BF16)
    r1 = x - hi.astype(F32)
    mid = r1.astype(BF16)
    lo = (r1 - mid.astype(F32)).astype(BF16)
    return hi, mid, lo


def _tri_cumsum(tri_bf16, x):
    hi, mid, lo = _split3(x)
    dot = lambda p: jnp.dot(tri_bf16, p, preferred_element_type=F32)
    return dot(hi) + (dot(mid) + dot(lo))


def _silu(x):
    return x * jax.nn.sigmoid(x)


def _softplus(x):
    return jnp.maximum(x, 0.0) + jnp.log1p(jnp.exp(-jnp.abs(x)))


def _adaln_kernel(cb_ref, w_ref, b_ref, o_ref, *, n_vec, tn):
    rows = []
    for v in range(n_vec):
        cols = []
        for j in range(tn // 128):
            wj = w_ref[:, j * 128:(j + 1) * 128]
            cols.append(jnp.sum(wj * cb_ref[v], axis=0, keepdims=True))
        rows.append(jnp.concatenate(cols, axis=1))
    pad = jnp.zeros((HALO - n_vec, tn), F32)
    o_ref[...] = jnp.concatenate(rows + [pad], axis=0) + b_ref[...]


def _adaln(cond, ada_w, ada_b):
    depth, d, n6 = ada_w.shape
    n_vec = cond.shape[0]
    tn = 1024
    cb = jnp.broadcast_to(_silu(cond)[:, :, None], (n_vec, d, 128))
    return pl.pallas_call(
        functools.partial(_adaln_kernel, n_vec=n_vec, tn=tn),
        grid=(depth, n6 // tn),
        in_specs=[pl.BlockSpec((n_vec, d, 128), lambda l, j: (0, 0, 0)),
                  pl.BlockSpec((None, d, tn), lambda l, j: (l, 0, j)),
                  pl.BlockSpec((None, 1, tn), lambda l, j: (l, 0, j))],
        out_specs=pl.BlockSpec((None, HALO, tn), lambda l, j: (l, 0, j)),
        out_shape=jax.ShapeDtypeStruct((depth, HALO, n6), F32),
        compiler_params=_cparams("arbitrary", "arbitrary"),
    )(cb, ada_w, ada_b.reshape(depth, 1, n6))


def _modulated(x_ref, nw_ref, sh_ref, sc_ref):
    x = x_ref[...]
    y = x * lax.rsqrt(jnp.mean(x * x, axis=-1, keepdims=True) + NORM_EPS)
    return (y * nw_ref[...]) * (1.0 + sc_ref[...]) + sh_ref[...]


def _norm_mm_kernel(x_ref, nw_ref, sh_ref, sc_ref, w_ref, o_ref, h_ref):
    @pl.when(pl.program_id(1) == 0)
    def _():
        h_ref[...] = _modulated(x_ref, nw_ref, sh_ref, sc_ref).astype(BF16)

    o_ref[...] = jnp.dot(h_ref[...], w_ref[...], preferred_element_type=F32).astype(o_ref.dtype)


def _norm_swiglu_kernel(x_ref, nw_ref, sh_ref, sc_ref, wg_ref, wu_ref, o_ref, h_ref):
    @pl.when(pl.program_id(1) == 0)
    def _():
        h_ref[...] = _modulated(x_ref, nw_ref, sh_ref, sc_ref).astype(BF16)

    h = h_ref[...]
    g = jnp.dot(h, wg_ref[...], preferred_element_type=F32)
    u = jnp.dot(h, wu_ref[...], preferred_element_type=F32)
    o_ref[...] = (_silu(g) * u).astype(o_ref.dtype)


def _row_tile(m, cap):
    t = cap
    while m % t:
        t //= 2
    return t


def _norm_mm(x, nw, shift, scale, weights, rows_per_vec, out_dtype, tm_cap=512, tn=512):
    m, d = x.shape
    n = weights[0].shape[1]
    tm = _row_tile(rows_per_vec, tm_cap)
    vec_map = lambda i, j: ((i * tm) // rows_per_vec, 0, 0)
    kern = _norm_mm_kernel if len(weights) == 1 else _norm_swiglu_kernel
    return pl.pallas_call(
        kern,
        grid=(m // tm, n // tn),
        in_specs=[pl.BlockSpec((tm, d), lambda i, j: (i, 0)),
                  pl.BlockSpec((1, d), lambda i, j: (0, 0)),
                  pl.BlockSpec((None, 1, d), vec_map),
                  pl.BlockSpec((None, 1, d), vec_map)]
                 + [pl.BlockSpec((d, tn), lambda i, j: (0, j)) for _ in weights],
        out_specs=pl.BlockSpec((tm, tn), lambda i, j: (i, j)),
        out_shape=jax.ShapeDtypeStruct((m, n), out_dtype),
        scratch_shapes=[pltpu.VMEM((tm, d), BF16)],
        compiler_params=_cparams("arbitrary", "arbitrary"),
    )(x, nw.reshape(1, d), shift, scale, *weights)


def _mm_res_kernel(a_ref, w_ref, res_ref, g_ref, o_ref):
    acc = jnp.dot(a_ref[...], w_ref[...], preferred_element_type=F32)
    o_ref[...] = res_ref[...] + g_ref[...] * acc


def _mm_res(a, w, res, gate, rows_per_vec, tm_cap=512, tn=512):
    m, k = a.shape
    n = w.shape[1]
    tm = _row_tile(rows_per_vec, tm_cap)
    return pl.pallas_call(
        _mm_res_kernel,
        grid=(m // tm, n // tn),
        in_specs=[pl.BlockSpec((tm, k), lambda i, j: (i, 0)),
                  pl.BlockSpec((k, tn), lambda i, j: (0, j)),
                  pl.BlockSpec((tm, tn), lambda i, j: (i, j)),
                  pl.BlockSpec((None, 1, tn), lambda i, j: ((i * tm) // rows_per_vec, 0, j))],
        out_specs=pl.BlockSpec((tm, tn), lambda i, j: (i, j)),
        out_shape=jax.ShapeDtypeStruct((m, n), F32),
        compiler_params=_cparams("arbitrary", "arbitrary"),
    )(a, w, res, gate)


def _final_norm_kernel(x_ref, w_ref, o_ref):
    x = x_ref[...]
    o_ref[...] = x * lax.rsqrt(jnp.mean(x * x, axis=-1, keepdims=True) + NORM_EPS) * w_ref[...]


def _final_norm(x, w):
    m, d = x.shape
    tm = _row_tile(m, 512)
    return pl.pallas_call(
        _final_norm_kernel,
        grid=(m // tm,),
        in_specs=[pl.BlockSpec((tm, d), lambda i: (i, 0)), pl.BlockSpec((1, d), lambda i: (0, 0))],
        out_specs=pl.BlockSpec((tm, d), lambda i: (i, 0)),
        out_shape=jax.ShapeDtypeStruct((m, d), F32),
        compiler_params=_cparams("arbitrary"),
    )(x, w.reshape(1, d))


def _scan_block(i, n_blocks, rev):
    return (n_blocks - 1 - i) if rev else i


def _halo_block(i, n_blocks, rows, rev):
    per = rows // HALO
    if rev:
        return jnp.minimum((_scan_block(i, n_blocks, True) + 1) * per, n_blocks * per - 1)
    return jnp.maximum(i * per - 1, 0)


def _stage_with_halo(scr_ref, x_ref, halo_ref, keep, rows, rev):
    if rev:
        scr_ref[0:rows, :] = x_ref[...]
        scr_ref[rows:rows + HALO, :] = halo_ref[...] * keep
    else:
        scr_ref[0:HALO, :] = halo_ref[...] * keep
        scr_ref[HALO:HALO + rows, :] = x_ref[...]


def _lagged(scr_ref, lag, rows, rev):
    start = lag if rev else HALO - lag
    return scr_ref[start:start + rows, :]


def _rwkv_kernel(rkv_ref, sm_ref, rkv_h_ref, sm_h_ref, mu_rkv_ref, mu_sm_ref, w0_ref, w2_ref,
                 a0_ref, a2_ref, kkw_ref, kaw_ref, rkw_ref, s0_ref,
                 y_ref, bonus_ref, sl_ref, s_scr, rkv_scr, sm_scr, *, rev, n_heads, lora):
    i = pl.program_id(1)
    c = R_CHUNK
    n = R_HEADSIZE
    hw = n_heads * n

    @pl.when(i == 0)
    def _():
        s_scr[...] = s0_ref[...]

    keep = (i > 0).astype(F32)
    _stage_with_halo(rkv_scr, rkv_ref, rkv_h_ref, keep, c, rev)
    _stage_with_halo(sm_scr, sm_ref, sm_h_ref, keep, c, rev)
    x = rkv_ref[...]
    f = x + (_lagged(rkv_scr, 1, c, rev) - x) * mu_rkv_ref[...]
    xs = sm_ref[...]
    fs = xs + (_lagged(sm_scr, 1, c, rev) - xs) * mu_sm_ref[...]
    r = f[:, :hw]
    k = f[:, hw:2 * hw]
    v = f[:, 2 * hw:3 * hw]
    wl = fs[:, 256:256 + lora]
    al = fs[:, 384:384 + lora]

    ww = w0_ref[...] + _bdot(jnp.tanh(wl), w2_ref[...])
    lw = -jnp.exp(-_softplus(-ww) - 0.5)
    a = jax.nn.sigmoid(a0_ref[...] + _bdot(al, a2_ref[...]))
    kkr = k * kkw_ref[...]
    k2 = k * (1.0 + (a - 1.0) * kaw_ref[...])
    rk = r * k2 * rkw_ref[...]

    ti = lax.broadcasted_iota(jnp.int32, (c, c), 0)
    tj = lax.broadcasted_iota(jnp.int32, (c, c), 1)
    before = (tj > ti) if rev else (tj < ti)
    upto = (tj >= ti) if rev else (tj <= ti)
    eye = (ti == tj).astype(F32)
    g = _tri_cumsum(upto.astype(BF16), lw)
    gtot = g[0:1, :] if rev else g[c - 1:c, :]
    eg = jnp.exp(g)
    eneg = jnp.exp(-g)
    egp = jnp.exp(g - lw)
    etail = jnp.exp(gtot - g)
    dg = jnp.exp(gtot)
    rd_all = r * eg
    kp_all = k2 * eneg
    kpp_all = k2 * etail

    heads = range(n_heads)
    hsl = [slice(h * n, (h + 1) * n) for h in heads]
    tdot = lambda p, q: lax.dot_general(p.astype(BF16), q.astype(BF16), (((0,), (0,)), ((), ())),
                                        preferred_element_type=F32)
    kk = []
    for sl in hsl:
        kkh = kkr[:, sl]
        kk.append(kkh / jnp.maximum(jnp.sqrt(jnp.sum(kkh * kkh, axis=-1, keepdims=True)), 1e-12))
    bh = [kk[h] * a[:, hsl[h]] for h in heads]
    kkd = [kk[h] * egp[:, hsl[h]] for h in heads]
    bp = [bh[h] * eneg[:, hsl[h]] for h in heads]
    bpp = [bh[h] * etail[:, hsl[h]] for h in heads]
    rd = [rd_all[:, sl] for sl in hsl]
    vh = [v[:, sl] for sl in hsl]
    amat = [_bdot_nt(jnp.concatenate([kkd[h], rd[h]], axis=0),
                     jnp.concatenate([kp_all[:, hsl[h]], bp[h]], axis=0)) for h in heads]
    mk = [jnp.where(before, m[:c, :c], 0.0) for m in amat]
    nk = [jnp.where(upto, m[c:, :c], 0.0) for m in amat]
    nb = [jnp.where(upto, m[c:, c:], 0.0) for m in amat]
    p2 = [jnp.where(before, -m[:c, c:], 0.0) for m in amat]
    tinv = [eye + q for q in p2]
    for _ in range(int(math.log2(c)) - 1):
        p2 = [_bdot(q, q) for q in p2]
        tinv = [tinv[h] + _bdot(tinv[h], p2[h]) for h in heads]
    mnv = [_bdot(jnp.concatenate([mk[h], nk[h]], axis=0), vh[h]) for h in heads]
    wu = [_bdot(tinv[h], jnp.concatenate([kkd[h], mnv[h][:c]], axis=1)) for h in heads]
    nwu = [_bdot(nb[h], wu[h]) for h in heads]
    pd = [tdot(wu[h], bpp[h]) for h in heads]
    vk = [tdot(vh[h], kpp_all[:, hsl[h]]) for h in heads]
    s_in = [s_scr[h] for h in heads]
    ys = [mnv[h][c:] - nwu[h][:, n:] + _bdot_nt(rd[h] - nwu[h][:, :n], s_in[h]) for h in heads]
    for h in heads:
        s_scr[h] = s_in[h] * dg[:, hsl[h]] - _bdot(s_in[h], pd[h][:n]) + (vk[h] - pd[h][n:])
    bonus = [jnp.sum(rk[:, hsl[h]], axis=-1, keepdims=True) * vh[h] for h in heads]

    y_ref[...] = jnp.concatenate(ys, axis=1)
    bonus_ref[...] = jnp.concatenate(bonus, axis=1)

    @pl.when(i == pl.num_programs(1) - 1)
    def _():
        sl_ref[...] = s_scr[...]


def _rwkv_direction(pr, s0, p, rev):
    bsz, seqlen, width = pr.shape
    n_heads = s0.shape[1]
    hw = n_heads * R_HEADSIZE
    c = R_CHUNK
    nblk = seqlen // c
    sm_col = width // 512 - 1
    blk = lambda i: _scan_block(i, nblk, rev)
    halo = lambda i: _halo_block(i, nblk, c, rev)
    vec = lambda w: pl.BlockSpec((1, w), lambda b, i: (0, 0))
    lora = p['w2'].shape[0]
    return pl.pallas_call(
        functools.partial(_rwkv_kernel, rev=rev, n_heads=n_heads, lora=lora),
        grid=(bsz, nblk),
        in_specs=[pl.BlockSpec((None, c, 3 * hw), lambda b, i: (b, blk(i), 1)),
                  pl.BlockSpec((None, c, 512), lambda b, i: (b, blk(i), sm_col)),
                  pl.BlockSpec((None, HALO, 3 * hw), lambda b, i: (b, halo(i), 1)),
                  pl.BlockSpec((None, HALO, 512), lambda b, i: (b, halo(i), sm_col)),
                  vec(3 * hw), vec(512), vec(hw),
                  pl.BlockSpec((lora, hw), lambda b, i: (0, 0)),
                  vec(hw),
                  pl.BlockSpec((lora, hw), lambda b, i: (0, 0)),
                  vec(hw), vec(hw), vec(hw),
                  pl.BlockSpec((None, n_heads, R_HEADSIZE, R_HEADSIZE), lambda b, i: (b, 0, 0, 0))],
        out_specs=[pl.BlockSpec((None, c, hw), lambda b, i: (b, blk(i), 0)),
                   pl.BlockSpec((None, c, hw), lambda b, i: (b, blk(i), 0)),
                   pl.BlockSpec((None, n_heads, R_HEADSIZE, R_HEADSIZE), lambda b, i: (b, 0, 0, 0))],
        out_shape=[jax.ShapeDtypeStruct((bsz, seqlen, hw), F32),
                   jax.ShapeDtypeStruct((bsz, seqlen, hw), F32),
                   jax.ShapeDtypeStruct(s0.shape, F32)],
        scratch_shapes=[pltpu.VMEM(s0.shape[1:], F32),
                        pltpu.VMEM((c + HALO, 3 * hw), F32),
                        pltpu.VMEM((c + HALO, 512), F32)],
        compiler_params=_cparams("arbitrary", "arbitrary"),
    )(pr, pr, pr, pr, p['mu_rkv'], p['mu_sm'], p['w0'], p['w2'], p['a0'], p['a2'],
      p['kk'], p['ka'], p['rk'], s0)


def _expand_cols(x, e_bf16):
    hi, mid, lo = _split3(x)
    dot = lambda p: jnp.dot(p, e_bf16, preferred_element_type=F32)
    return dot(hi) + (dot(mid) + dot(lo))


def _causal_conv(scr_ref, w_ref, b_ref, rows, taps, rev):
    acc = b_ref[...] + w_ref[taps - 1:taps, :] * _lagged(scr_ref, 0, rows, rev)
    for kk in range(taps - 1):
        acc = acc + w_ref[kk:kk + 1, :] * _lagged(scr_ref, taps - 1 - kk, rows, rev)
    return acc


def _ssd_kernel(xbc_ref, halo_ref, sm_ref, dtt_ref, cw_ref, cb_ref, dtb_row_ref, dtb_col_ref,
                alog_row_ref, alog_col_ref, dskip_ref, e64_ref, e128_ref, h0_ref,
                y_ref, hl_ref, h_scr, x_scr, *, rev, n_heads):
    i = pl.program_id(1)
    c = M_CHUNK
    p = M_HEADDIM
    ns = M_STATE
    mix = n_heads * p
    hpg = n_heads // M_GROUPS

    @pl.when(i == 0)
    def _():
        h_scr[...] = h0_ref[...]

    keep = (i > 0).astype(F32)
    _stage_with_halo(x_scr, xbc_ref, halo_ref, keep, c, rev)
    xbc = _silu(_causal_conv(x_scr, cw_ref, cb_ref, c, M_CONV, rev))
    xs = xbc[:, :mix]
    bm = [xbc[:, mix + g * ns:mix + (g + 1) * ns] for g in range(M_GROUPS)]
    cm = [xbc[:, mix + (M_GROUPS + g) * ns:mix + (M_GROUPS + g + 1) * ns] for g in range(M_GROUPS)]

    ti = lax.broadcasted_iota(jnp.int32, (c, c), 0)
    tj = lax.broadcasted_iota(jnp.int32, (c, c), 1)
    upto = (tj >= ti) if rev else (tj <= ti)
    upto_t = (ti >= tj) if rev else (ti <= tj)
    end = 0 if rev else c - 1

    dt = _softplus(sm_ref[:, 480:480 + n_heads] + dtb_row_ref[...])
    a_cs = _tri_cumsum(upto.astype(BF16), dt * -jnp.exp(alog_row_ref[...]))
    dt_t = _softplus(dtt_ref[...] + dtb_col_ref[...])
    a_cs_t = _expand_cols(dt_t * -jnp.exp(alog_col_ref[...]), upto_t.astype(BF16))
    dte = jnp.exp(a_cs[end:end + 1, :] - a_cs)
    e64 = e64_ref[...]
    xdt = xs * _expand_cols(dt, e64)
    xdd_t = jnp.transpose(xs * _expand_cols(dt * dte, e64))
    ea_full = _expand_cols(jnp.exp(a_cs), e64)
    a_cs_b = _expand_cols(a_cs, e128_ref[...])

    cb = [_bdot_nt(cm[g], bm[g]) for g in range(M_GROUPS)]
    heads = range(n_heads)
    h_in = [h_scr[h] for h in heads]
    ydiag = []
    yoff = []
    for h in heads:
        g = h // hpg
        a_col = a_cs_b[:, h * 128:(h + 1) * 128]
        decay = jnp.where(upto, jnp.exp(jnp.minimum(a_col - a_cs_t[h:h + 1, :], 0.0)), 0.0)
        ydiag.append(_bdot(cb[g] * decay, xdt[:, h * p:(h + 1) * p]))
        yoff.append(_bdot_nt(cm[g], h_in[h]))
        h_scr[h] = (h_in[h] * jnp.exp(a_col[end:end + 1, :])
                    + _bdot(xdd_t[h * p:(h + 1) * p, :], bm[g]))
    y_ref[...] = (jnp.concatenate(ydiag, axis=1) + jnp.concatenate(yoff, axis=1) * ea_full
                  + xs * dskip_ref[...])

    @pl.when(i == pl.num_programs(1) - 1)
    def _():
        hl_ref[...] = h_scr[...]


def _ssd_direction(pr, dt_t, h0, p, rev):
    bsz, seqlen, width = pr.shape
    n_heads = h0.shape[1]
    mix = n_heads * M_HEADDIM
    xw = mix + 2 * M_GROUPS * M_STATE
    c = M_CHUNK
    nblk = seqlen // c
    sm_col = width // 512 - 1
    blk = lambda i: _scan_block(i, nblk, rev)
    halo = lambda i: _halo_block(i, nblk, c, rev)
    full = lambda a: pl.BlockSpec(a.shape, lambda b, i: (0,) * a.ndim)
    consts = [p['conv_w'], p['conv_b'], p['dtb_row'], p['dtb_col'], p['alog_row'], p['alog_col'],
              p['dskip'], p['e64'], p['e128']]
    state_spec = pl.BlockSpec((None,) + h0.shape[1:], lambda b, i: (b, 0, 0, 0))
    return pl.pallas_call(
        functools.partial(_ssd_kernel, rev=rev, n_heads=n_heads),
        grid=(bsz, nblk),
        in_specs=[pl.BlockSpec((None, c, xw), lambda b, i: (b, blk(i), 0)),
                  pl.BlockSpec((None, HALO, xw), lambda b, i: (b, halo(i), 0)),
                  pl.BlockSpec((None, c, 512), lambda b, i: (b, blk(i), sm_col)),
                  pl.BlockSpec((None, n_heads, c), lambda b, i: (b, 0, blk(i)))]
                 + [full(a) for a in consts] + [state_spec],
        out_specs=[pl.BlockSpec((None, c, mix), lambda b, i: (b, blk(i), 0)), state_spec],
        out_shape=[jax.ShapeDtypeStruct((bsz, seqlen, mix), F32),
                   jax.ShapeDtypeStruct(h0.shape, F32)],
        scratch_shapes=[pltpu.VMEM(h0.shape[1:], F32), pltpu.VMEM((c + HALO, xw), F32)],
        compiler_params=_cparams("arbitrary", "arbitrary"),
    )(pr, pr, pr, dt_t, *consts, h0)


def _rglru_kernel(x_ref, halo_ref, cw_ref, cb_ref, wa_ref, ba_ref, wx_ref, bx_ref, lam_ref, h0_ref,
                  o_ref, hl_ref, h_scr, x_scr, a_scr, u_scr, *, rev, tb):
    i = pl.program_id(1)

    @pl.when(i == 0)
    def _():
        h_scr[...] = h0_ref[...]

    keep = (i > 0).astype(F32)
    _stage_with_halo(x_scr, x_ref, halo_ref, keep, tb, rev)
    xc = _causal_conv(x_scr, cw_ref, cb_ref, tb, C_CONV, rev)
    blk = xc.shape[1] // C_HEADS
    xh = [xc[:, h * blk:(h + 1) * blk] for h in range(C_HEADS)]
    gate_r = jax.nn.sigmoid(
        jnp.concatenate([_bdot(xh[h], wa_ref[h]) for h in range(C_HEADS)], axis=1) + ba_ref[...])
    gate_i = jax.nn.sigmoid(
        jnp.concatenate([_bdot(xh[h], wx_ref[h]) for h in range(C_HEADS)], axis=1) + bx_ref[...])
    log_a = -RG_C * gate_r * _softplus(-lam_ref[...])
    a = jnp.exp(log_a)
    mult = jnp.sqrt(jnp.maximum(-jnp.tanh(log_a) * (a * a + 1.0), 0.0))
    a_scr[...] = a
    u_scr[...] = mult * gate_i * xc

    def step(t, h):
        tt = (tb - 1 - t) if rev else t
        h = a_scr[pl.ds(tt, 1), :] * h + u_scr[pl.ds(tt, 1), :]
        o_ref[pl.ds(tt, 1), :] = h
        return h

    h_scr[0:1, :] = lax.fori_loop(0, tb, step, h_scr[0:1, :], unroll=8)

    @pl.when(i == pl.num_programs(1) - 1)
    def _():
        hl_ref[...] = h_scr[...]


def _rglru_direction(xv, h0, p, rev, tb):
    bsz, rows, wtot = xv.shape
    width = h0.shape[2]
    ncol = wtot // (2 * width)
    nrb = rows // tb
    total = ncol * nrb
    per = tb // HALO

    def pos(i):
        s = _scan_block(i, total, rev)
        return s // nrb, s % nrb

    def x_map(b, i):
        col, q = pos(i)
        return b, q, 2 * col + 1

    def halo_map(b, i):
        s = _scan_block(i, total, rev)
        sp = jnp.minimum(s + 1, total - 1) if rev else jnp.maximum(s - 1, 0)
        col, q = sp // nrb, sp % nrb
        return b, (q * per if rev else (q + 1) * per - 1), 2 * col + 1

    def o_map(b, i):
        col, q = pos(i)
        return b, q, col

    full = lambda a: pl.BlockSpec(a.shape, lambda b, i: (0,) * a.ndim)
    consts = [p['conv_w'], p['conv_b'], p['wa'], p['ba'], p['wx'], p['bx'], p['lam']]
    state_spec = pl.BlockSpec((None, HALO, width), lambda b, i: (b, 0, 0))
    return pl.pallas_call(
        functools.partial(_rglru_kernel, rev=rev, tb=tb),
        grid=(bsz, total),
        in_specs=[pl.BlockSpec((None, tb, width), x_map), pl.BlockSpec((None, HALO, width), halo_map)]
                 + [full(a) for a in consts] + [state_spec],
        out_specs=[pl.BlockSpec((None, tb, width), o_map), state_spec],
        out_shape=[jax.ShapeDtypeStruct((bsz, rows, ncol * width), F32),
                   jax.ShapeDtypeStruct(h0.shape, F32)],
        scratch_shapes=[pltpu.VMEM((HALO, width), F32), pltpu.VMEM((tb + HALO, width), F32),
                        pltpu.VMEM((tb, width), F32), pltpu.VMEM((tb, width), F32)],
        compiler_params=_cparams("arbitrary", "arbitrary"),
    )(xv, xv, *consts, h0)


def _readout_ab_kernel(z_ref, sm_ref, ymf_ref, ymb_ref, yrf_ref, yrb_ref, bof_ref, bob_ref,
                       nw_ref, lw_ref, lb_ref, g2_ref, o_ref, *, n_heads):
    t = (ymf_ref[...] + ymb_ref[...]) * _silu(z_ref[...])
    gw = t.shape[1] // M_GROUPS
    a_out = []
    for g in range(M_GROUPS):
        tg = t[:, g * gw:(g + 1) * gw]
        a_out.append(tg * lax.rsqrt(jnp.mean(tg * tg, axis=-1, keepdims=True) + M_NORM_EPS))
    a_out = jnp.concatenate(a_out, axis=1) * nw_ref[...]

    yr = yrf_ref[...] + yrb_ref[...]
    n = R_HEADSIZE
    yn = []
    for h in range(n_heads):
        yh = yr[:, h * n:(h + 1) * n]
        dlt = yh - jnp.mean(yh, axis=-1, keepdims=True)
        yn.append(dlt * lax.rsqrt(jnp.mean(dlt * dlt, axis=-1, keepdims=True) + R_LN_EPS))
    yn = jnp.concatenate(yn, axis=1) * lw_ref[...] + lb_ref[...]
    gate = _bdot(jax.nn.sigmoid(sm_ref[:, 0:g2_ref.shape[0]]), g2_ref[...])
    b_out = (yn + bof_ref[...] + bob_ref[...]) * gate
    o_ref[...] = jnp.concatenate([a_out, b_out], axis=1).astype(o_ref.dtype)


def _readout_ab(pr, ym, yr, bo, ro, n_heads):
    bsz, seqlen, width = pr.shape
    m = bsz * seqlen
    mix = ym[0].shape[2]
    tm = _row_tile(m, 256)
    flat = lambda t: t.reshape(m, t.shape[2])
    rowblk = lambda w, col: pl.BlockSpec((tm, w), lambda i: (i, col))
    full = lambda a: pl.BlockSpec(a.shape, lambda i: (0,) * a.ndim)
    consts = [ro['norm_w'], ro['lnx_w'], ro['lnx_b'], ro['g2']]
    return pl.pallas_call(
        functools.partial(_readout_ab_kernel, n_heads=n_heads),
        grid=(m // tm,),
        in_specs=[rowblk(mix, 2), rowblk(512, width // 512 - 1)] + [rowblk(mix, 0)] * 6
                 + [full(a) for a in consts],
        out_specs=rowblk(2 * mix, 0),
        out_shape=jax.ShapeDtypeStruct((m, 2 * mix), BF16),
        compiler_params=_cparams("arbitrary"),
    )(flat(pr), flat(pr), flat(ym[0]), flat(ym[1]), flat(yr[0]), flat(yr[1]), flat(bo[0]),
      flat(bo[1]), *consts)


def _readout_c_kernel(gy_ref, hf_ref, hb_ref, o_ref):
    g = gy_ref[...]
    gelu = 0.5 * g * (1.0 + jnp.tanh(math.sqrt(2.0 / math.pi) * (g + 0.044715 * (g * g * g))))
    o_ref[...] = ((hf_ref[...] + hb_ref[...]) * gelu).astype(o_ref.dtype)


def _readout_c(p, hf, hb):
    m, cw = hf.shape
    tm = _row_tile(m, 512)
    spec = pl.BlockSpec((tm, cw), lambda i: (i, 0))
    return pl.pallas_call(
        _readout_c_kernel,
        grid=(m // tm,),
        in_specs=[spec, spec, spec],
        out_specs=spec,
        out_shape=jax.ShapeDtypeStruct((m, cw), BF16),
        compiler_params=_cparams("arbitrary"),
    )(p, hf, hb)


def kernel(x, c, ctx, c_ctx, ada_w, ada_b, norm1_w, norm2_w, ffn_w_gate, ffn_w_up, ffn_w_down, final_norm_w, ab_w_in, ab_w_out, m_conv_w, m_conv_b, m_dt_bias, m_a_log, m_d, m_norm_w, r_mu, r_w0, r_w2, r_a0, r_a2, r_kk, r_ka, r_rk, r_g2, r_lnx_w, r_lnx_b, c_w_in, c_w_out, c_conv_w, c_conv_b, c_wa, c_ba, c_wx, c_bx, c_lambda):
    bsz, seqlen, d = x.shape
    ctx_len = ctx.shape[1]
    depth = ada_w.shape[0]
    n_lat, n_ctx = bsz * seqlen, bsz * ctx_len
    row = lambda t: t.reshape(1, -1)

    mod = _adaln(jnp.concatenate([c_ctx[None, :], c], axis=0), ada_w, ada_b)

    xl = x.reshape(n_lat, d)
    xc = ctx.reshape(n_ctx, d)
    mix_a = m_norm_w.shape[1]
    m_heads = m_dt_bias.shape[2]
    r_heads = r_rk.shape[2]
    lora = r_w2.shape[2]
    e64 = jnp.repeat(jnp.eye(m_heads, dtype=BF16), M_HEADDIM, axis=1)
    e128 = jnp.repeat(jnp.eye(m_heads, dtype=BF16), 128, axis=1)

    for layer in range(depth):
        with_ctx = layer < depth - 1
        part = lambda k: mod[layer, :, k * d:(k + 1) * d]
        lat_vec = lambda k: part(k)[1:1 + bsz, None, :]
        ctx_vec = lambda k: part(k)[0:1, None, :]
        nw1, nw2 = norm1_w[layer], norm2_w[layer]

        if layer % 2 == 0:
            e = layer // 2
            w = ab_w_in[e]
            o1 = mix_a
            o2 = o1 + mix_a + 2 * M_GROUPS * M_STATE
            o3 = o2 + m_heads
            o4 = o3 + 3 * r_heads * R_HEADSIZE + 2 * lora
            rkv_w = 3 * r_heads * R_HEADSIZE
            zeros = lambda n: jnp.zeros((d, n), w.dtype)
            w_in = jnp.concatenate(
                [w[:, o1:o2], w[:, :o1], w[:, o3:o3 + rkv_w], w[:, o4:],
                 w[:, o3 + rkv_w:o3 + rkv_w + lora], zeros(128 - lora),
                 w[:, o3 + rkv_w + lora:o4], w[:, o2:o3], zeros(128 - lora - m_heads)],
                axis=1).astype(BF16)
            width = w_in.shape[1]
            w_out = ab_w_out[e].astype(BF16)
            g2w = r_g2[e].astype(BF16)

            def project(xs, sh, sc, rows_per_vec, seg_len):
                pr = _norm_mm(xs, nw1, sh, sc, [w_in], rows_per_vec, F32)
                pr = pr.reshape(bsz, seg_len, width)
                dt_t = jnp.swapaxes(pr[:, :, width - 32:width - 32 + m_heads], 1, 2)
                return pr, dt_t

            pr_l, dtt_l = project(xl, lat_vec(0), lat_vec(1), seqlen, seqlen)
            pr_c, dtt_c = project(xc, ctx_vec(0), ctx_vec(1), n_ctx, ctx_len)

            ym_l = ym_c = yr_l = yr_c = bo_l = bo_c = None
            for dr in range(2):
                rev = dr == 1
                mp = dict(conv_w=m_conv_w[e, dr], conv_b=row(m_conv_b[e, dr]),
                          dtb_row=row(m_dt_bias[e, dr]), dtb_col=m_dt_bias[e, dr].reshape(-1, 1),
                          alog_row=row(m_a_log[e, dr]), alog_col=m_a_log[e, dr].reshape(-1, 1),
                          dskip=row(jnp.repeat(m_d[e, dr], M_HEADDIM)), e64=e64, e128=e128)
                h0 = jnp.zeros((bsz, m_heads, M_HEADDIM, M_STATE), F32)
                y_c, h_c = _ssd_direction(pr_c, dtt_c, h0, mp, rev)
                y_l, _ = _ssd_direction(pr_l, dtt_l, h_c, mp, rev)
                mu = r_mu[e, dr]
                mu_sm = (jnp.zeros((512,), F32).at[256:256 + lora].set(mu[rkv_w:rkv_w + lora])
                         .at[384:384 + lora].set(mu[rkv_w + lora:]))
                rp = dict(mu_rkv=row(mu[:rkv_w]), mu_sm=row(mu_sm), w0=row(r_w0[e, dr]),
                          w2=r_w2[e, dr].astype(BF16), a0=row(r_a0[e, dr]), a2=r_a2[e, dr].astype(BF16),
                          kk=row(r_kk[e, dr]), ka=row(r_ka[e, dr]), rk=row(r_rk[e, dr]))
                s0 = jnp.zeros((bsz, r_heads, R_HEADSIZE, R_HEADSIZE), F32)
                v_c, b_c, s_c = _rwkv_direction(pr_c, s0, rp, rev)
                v_l, b_l, _ = _rwkv_direction(pr_l, s_c, rp, rev)
                if dr == 0:
                    ym_l, ym_c, yr_l, yr_c, bo_l, bo_c = y_l, y_c, v_l, v_c, b_l, b_c
                else:
                    ym_l, yr_l, bo_l = (ym_l, y_l), (yr_l, v_l), (bo_l, b_l)
                    ym_c, yr_c, bo_c = (ym_c, y_c), (yr_c, v_c), (bo_c, b_c)

            ro = dict(norm_w=row(m_norm_w[e]), lnx_w=row(r_lnx_w[e]), lnx_b=row(r_lnx_b[e]), g2=g2w)
            act_l = _readout_ab(pr_l, ym_l, yr_l, bo_l, ro, r_heads)
            xl = _mm_res(act_l, w_out, xl, lat_vec(2), seqlen)
            if with_ctx:
                act_c = _readout_ab(pr_c, ym_c, yr_c, bo_c, ro, r_heads)
                xc = _mm_res(act_c, w_out, xc, ctx_vec(2), n_ctx)
        else:
            o = layer // 2
            w_in = c_w_in[o].astype(BF16)
            w_out = c_w_out[o].astype(BF16)
            cw = w_out.shape[0]
            rows = seqlen // GRID_W
            p_l = _norm_mm(xl, nw1, lat_vec(0), lat_vec(1), [w_in], seqlen, F32)
            p_c = _norm_mm(xc, nw1, ctx_vec(0), ctx_vec(1), [w_in], n_ctx, F32)
            hs_l, hs_c = [], []
            for dr in range(2):
                rev = dr == 1
                cp = dict(conv_w=c_conv_w[o, dr], conv_b=row(c_conv_b[o, dr]),
                          wa=c_wa[o, dr].astype(BF16), ba=row(c_ba[o, dr]),
                          wx=c_wx[o, dr].astype(BF16), bx=row(c_bx[o, dr]), lam=row(c_lambda[o, dr]))
                h0 = jnp.zeros((bsz, HALO, cw), F32)
                h_c, s_c = _rglru_direction(p_c.reshape(bsz, ctx_len, 2 * cw), h0, cp, rev,
                                            _row_tile(ctx_len, 128))
                h_l, _ = _rglru_direction(p_l.reshape(bsz, rows, GRID_W * 2 * cw), s_c, cp, rev, rows)
                hs_l.append(h_l.reshape(n_lat, cw))
                hs_c.append(h_c.reshape(n_ctx, cw))
            act_l = _readout_c(p_l, hs_l[0], hs_l[1])
            xl = _mm_res(act_l, w_out, xl, lat_vec(2), seqlen)
            if with_ctx:
                act_c = _readout_c(p_c, hs_c[0], hs_c[1])
                xc = _mm_res(act_c, w_out, xc, ctx_vec(2), n_ctx)

        wg, wu, wd = (ffn_w_gate[layer].astype(BF16), ffn_w_up[layer].astype(BF16),
                      ffn_w_down[layer].astype(BF16))
        act = _norm_mm(xl, nw2, lat_vec(3), lat_vec(4), [wg, wu], seqlen, BF16)
        xl = _mm_res(act, wd, xl, lat_vec(5), seqlen)
        if with_ctx:
            act = _norm_mm(xc, nw2, ctx_vec(3), ctx_vec(4), [wg, wu], n_ctx, BF16)
            xc = _mm_res(act, wd, xc, ctx_vec(5), n_ctx)

    return _final_norm(xl, final_norm_w).reshape(bsz, seqlen, d)
```

```python
import functools
import math

import jax
import jax.numpy as jnp
from jax import lax
from jax.experimental import pallas as pl
from jax.experimental.pallas import tpu as pltpu

F32 = jnp.float32
BF16 = jnp.bfloat16

NORM_EPS = 1e-6
GRID_W = 64
M_HEADDIM = 64
M_GROUPS = 4
M_STATE = 128
M_CONV = 4
M_CHUNK = 128
M_NORM_EPS = 1e-5
R_HEADSIZE = 64
R_CHUNK = 64
R_GROUP = 4
R_LN_EPS = 64e-5
C_HEADS = 8
C_CONV = 4
RG_C = 8.0

VMEM_LIMIT_BYTES = 56 * 1024 * 1024
HALO = 8


def _cparams(*sem):
    return pltpu.CompilerParams(dimension_semantics=sem, vmem_limit_bytes=VMEM_LIMIT_BYTES)


def _bdot(a, b):
    return jnp.dot(a.astype(BF16), b.astype(BF16), preferred_element_type=F32)


def _bdot_nt(a, b):
    return lax.dot_general(a.astype(BF16), b.astype(BF16), (((1,), (1,)), ((), ())),
                           preferred_element_type=F32)


def _split3(x):
    hi = x.astype(BF16)
    r1 = x - hi.astype(F32)
    mid = r1.astype(BF16)
    lo = (r1 - mid.astype(F32)).astype(BF16)
    return hi, mid, lo


def _tri_cumsum(tri_bf16, x, parts=3):
    hi, mid, lo = _split3(x)
    dot = lambda p: jnp.dot(tri_bf16, p, preferred_element_type=F32)
    if parts == 2:
        return dot(hi) + dot(mid)
    return dot(hi) + (dot(mid) + dot(lo))


def _silu(x):
    return x * jax.nn.sigmoid(x)


def _softplus(x):
    return jnp.maximum(x, 0.0) + jnp.log1p(jnp.exp(-jnp.abs(x)))


def _adaln_kernel(cb_ref, w_ref, b_ref, o_ref, *, n_vec, tn):
    rows = []
    for v in range(n_vec):
        cols = []
        for j in range(tn // 128):
            wj = w_ref[:, j * 128:(j + 1) * 128]
            cols.append(jnp.sum(wj * cb_ref[v], axis=0, keepdims=True))
        rows.append(jnp.concatenate(cols, axis=1))
    pad = jnp.zeros((HALO - n_vec, tn), F32)
    o_ref[...] = jnp.concatenate(rows + [pad], axis=0) + b_ref[...]


def _adaln(cond, ada_w, ada_b):
    depth, d, n6 = ada_w.shape
    n_vec = cond.shape[0]
    tn = 1024
    cb = jnp.broadcast_to(_silu(cond)[:, :, None], (n_vec, d, 128))
    return pl.pallas_call(
        functools.partial(_adaln_kernel, n_vec=n_vec, tn=tn),
        grid=(depth, n6 // tn),
        in_specs=[pl.BlockSpec((n_vec, d, 128), lambda l, j: (0, 0, 0)),
                  pl.BlockSpec((None, d, tn), lambda l, j: (l, 0, j)),
                  pl.BlockSpec((None, 1, tn), lambda l, j: (l, 0, j))],
        out_specs=pl.BlockSpec((None, HALO, tn), lambda l, j: (l, 0, j)),
        out_shape=jax.ShapeDtypeStruct((depth, HALO, n6), F32),
        compiler_params=_cparams("arbitrary", "arbitrary"),
    )(cb, ada_w, ada_b.reshape(depth, 1, n6))


def _modulated(x_ref, nw_ref, sh_ref, sc_ref):
    x = x_ref[...]
    y = x * lax.rsqrt(jnp.mean(x * x, axis=-1, keepdims=True) + NORM_EPS)
    return (y * nw_ref[...]) * (1.0 + sc_ref[...]) + sh_ref[...]


def _norm_mm_kernel(x_ref, nw_ref, sh_ref, sc_ref, w_ref, o_ref, h_ref):
    @pl.when(pl.program_id(1) == 0)
    def _():
        h_ref[...] = _modulated(x_ref, nw_ref, sh_ref, sc_ref).astype(BF16)

    o_ref[...] = jnp.dot(h_ref[...], w_ref[...], preferred_element_type=F32).astype(o_ref.dtype)


def _norm_swiglu_kernel(x_ref, nw_ref, sh_ref, sc_ref, wg_ref, wu_ref, o_ref, h_ref):
    @pl.when(pl.program_id(1) == 0)
    def _():
        h_ref[...] = _modulated(x_ref, nw_ref, sh_ref, sc_ref).astype(BF16)

    h = h_ref[...]
    g = jnp.dot(h, wg_ref[...], preferred_element_type=F32)
    u = jnp.dot(h, wu_ref[...], preferred_element_type=F32)
    o_ref[...] = (_silu(g) * u).astype(o_ref.dtype)


def _row_tile(m, cap):
    t = cap
    while m % t:
        t //= 2
    return t


def _norm_mm(x, nw, shift, scale, weights, rows_per_vec, out_dtype, tm_cap=1024, tn=512):
    m, d = x.shape
    n = weights[0].shape[1]
    tm = _row_tile(rows_per_vec, tm_cap)
    vec_map = lambda i, j: ((i * tm) // rows_per_vec, 0, 0)
    kern = _norm_mm_kernel if len(weights) == 1 else _norm_swiglu_kernel
    return pl.pallas_call(
        kern,
        grid=(m // tm, n // tn),
        in_specs=[pl.BlockSpec((tm, d), lambda i, j: (i, 0)),
                  pl.BlockSpec((1, d), lambda i, j: (0, 0)),
                  pl.BlockSpec((None, 1, d), vec_map),
                  pl.BlockSpec((None, 1, d), vec_map)]
                 + [pl.BlockSpec((d, tn), lambda i, j: (0, j)) for _ in weights],
        out_specs=pl.BlockSpec((tm, tn), lambda i, j: (i, j)),
        out_shape=jax.ShapeDtypeStruct((m, n), out_dtype),
        scratch_shapes=[pltpu.VMEM((tm, d), BF16)],
        compiler_params=_cparams("arbitrary", "arbitrary"),
    )(x, nw.reshape(1, d), shift, scale, *weights)


def _mm_res_kernel(a_ref, w_ref, res_ref, g_ref, o_ref):
    acc = jnp.dot(a_ref[...], w_ref[...], preferred_element_type=F32)
    o_ref[...] = res_ref[...] + g_ref[...] * acc


def _mm_res(a, w, res, gate, rows_per_vec, tm_cap=1024, tn=512):
    m, k = a.shape
    n = w.shape[1]
    tm = _row_tile(rows_per_vec, tm_cap)
    return pl.pallas_call(
        _mm_res_kernel,
        grid=(m // tm, n // tn),
        in_specs=[pl.BlockSpec((tm, k), lambda i, j: (i, 0)),
                  pl.BlockSpec((k, tn), lambda i, j: (0, j)),
                  pl.BlockSpec((tm, tn), lambda i, j: (i, j)),
                  pl.BlockSpec((None, 1, tn), lambda i, j: ((i * tm) // rows_per_vec, 0, j))],
        out_specs=pl.BlockSpec((tm, tn), lambda i, j: (i, j)),
        out_shape=jax.ShapeDtypeStruct((m, n), F32),
        compiler_params=_cparams("arbitrary", "arbitrary"),
    )(a, w, res, gate)


def _final_norm_kernel(x_ref, w_ref, o_ref):
    x = x_ref[...]
    o_ref[...] = x * lax.rsqrt(jnp.mean(x * x, axis=-1, keepdims=True) + NORM_EPS) * w_ref[...]


def _final_norm(x, w):
    m, d = x.shape
    tm = _row_tile(m, 512)
    return pl.pallas_call(
        _final_norm_kernel,
        grid=(m // tm,),
        in_specs=[pl.BlockSpec((tm, d), lambda i: (i, 0)), pl.BlockSpec((1, d), lambda i: (0, 0))],
        out_specs=pl.BlockSpec((tm, d), lambda i: (i, 0)),
        out_shape=jax.ShapeDtypeStruct((m, d), F32),
        compiler_params=_cparams("arbitrary"),
    )(x, w.reshape(1, d))


def _scan_block(i, n_blocks, rev):
    return (n_blocks - 1 - i) if rev else i


def _halo_block(i, n_blocks, rows, rev):
    per = rows // HALO
    if rev:
        return jnp.minimum((_scan_block(i, n_blocks, True) + 1) * per, n_blocks * per - 1)
    return jnp.maximum(i * per - 1, 0)


def _stage_with_halo(scr_ref, x_ref, halo_ref, keep, rows, rev):
    if rev:
        scr_ref[0:rows, :] = x_ref[...]
        scr_ref[rows:rows + HALO, :] = halo_ref[...] * keep
    else:
        scr_ref[0:HALO, :] = halo_ref[...] * keep
        scr_ref[HALO:HALO + rows, :] = x_ref[...]


def _lagged(scr_ref, lag, rows, rev):
    start = lag if rev else HALO - lag
    return scr_ref[start:start + rows, :]


def _rwkv_kernel(rkv_ref, sm_ref, rkv_h_ref, sm_h_ref, mu_rkv_ref, mu_sm_ref, w0_ref, w2_ref,
                 a0_ref, a2_ref, kkw_ref, kaw_ref, rkw_ref, s0_ref,
                 y_ref, bonus_ref, sl_ref, s_scr, rkv_scr, sm_scr, *, rev, n_heads, lora):
    i = pl.program_id(1)
    c = R_CHUNK
    n = R_HEADSIZE
    hw = n_heads * n

    @pl.when(i == 0)
    def _():
        s_scr[...] = s0_ref[...]

    keep = (i > 0).astype(F32)
    _stage_with_halo(rkv_scr, rkv_ref, rkv_h_ref, keep, c, rev)
    _stage_with_halo(sm_scr, sm_ref, sm_h_ref, keep, c, rev)
    x = rkv_ref[...]
    f = x + (_lagged(rkv_scr, 1, c, rev) - x) * mu_rkv_ref[...]
    xs = sm_ref[...]
    fs = xs + (_lagged(sm_scr, 1, c, rev) - xs) * mu_sm_ref[...]
    r = f[:, :hw]
    k = f[:, hw:2 * hw]
    v = f[:, 2 * hw:3 * hw]
    wl = fs[:, 256:256 + lora]
    al = fs[:, 384:384 + lora]

    ww = w0_ref[...] + _bdot(jnp.tanh(wl), w2_ref[...])
    lw = -math.exp(-0.5) * jax.nn.sigmoid(ww)
    a = jax.nn.sigmoid(a0_ref[...] + _bdot(al, a2_ref[...]))
    kkr = k * kkw_ref[...]
    k2 = k * (1.0 + (a - 1.0) * kaw_ref[...])
    rk = r * k2 * rkw_ref[...]

    ti = lax.broadcasted_iota(jnp.int32, (c, c), 0)
    tj = lax.broadcasted_iota(jnp.int32, (c, c), 1)
    upto = (tj >= ti) if rev else (tj <= ti)
    g = _tri_cumsum(upto.astype(BF16), lw, parts=2)
    gtot = g[0:1, :] if rev else g[c - 1:c, :]
    eg = jnp.exp(g)
    eneg = jnp.exp(-g)
    egp = jnp.exp(g - lw)
    etail = jnp.exp(gtot - g)
    dg = jnp.exp(gtot)
    rd_all = r * eg
    kp_all = k2 * eneg
    kpp_all = k2 * etail

    gw = R_GROUP * n
    groups = range(hw // gw)
    gsl = [slice(q * gw, (q + 1) * gw) for q in groups]
    shift = int(math.log2(n))
    bi = lax.broadcasted_iota(jnp.int32, (gw, gw), 0) >> shift
    bj = lax.broadcasted_iota(jnp.int32, (gw, gw), 1) >> shift
    same_head = bi == bj
    ones_bd = same_head.astype(BF16)
    lane_head = lax.broadcasted_iota(jnp.int32, (1, 128), 1) >> shift
    keep_lanes = [(lane_head == j).astype(BF16) for j in range(128 // n)]

    def bdiag(t):
        tb = t.astype(BF16)
        zero = jnp.zeros((c, 128), BF16)
        rows = []
        for h in range(R_GROUP):
            lt, j = divmod(h * n, 128)
            piece = tb[:, lt * 128:(lt + 1) * 128] * keep_lanes[j // n]
            rows.append(jnp.concatenate([piece if u == lt else zero for u in range(gw // 128)], axis=1))
        return jnp.concatenate(rows, axis=0)

    def diag_blocks(full):
        masked = jnp.where(same_head, full, 0.0)
        out = masked[0:n]
        for h in range(1, R_GROUP):
            out = out + masked[h * n:(h + 1) * n]
        return out

    def head_sums(t):
        hi, mid, _ = _split3(t)
        dot = lambda p: jnp.dot(p, ones_bd, preferred_element_type=F32)
        return dot(hi) + dot(mid)

    tdot = lambda p, q: lax.dot_general(p.astype(BF16), q.astype(BF16), (((0,), (0,)), ((), ())),
                                        preferred_element_type=F32)
    gi = lax.broadcasted_iota(jnp.int32, (c, gw), 0)
    gj = lax.broadcasted_iota(jnp.int32, (c, gw), 1) & (c - 1)
    before_g = (gj > gi) if rev else (gj < gi)
    upto_g = (gj >= gi) if rev else (gj <= gi)
    eye_g = (gi == gj).astype(F32)

    kk = [kkr[:, s] * lax.rsqrt(jnp.maximum(head_sums(kkr[:, s] * kkr[:, s]), 1e-24)) for s in gsl]
    bh = [kk[q] * a[:, gsl[q]] for q in groups]
    kkd = [kk[q] * egp[:, gsl[q]] for q in groups]
    bp = [bh[q] * eneg[:, gsl[q]] for q in groups]
    bpp = [bh[q] * etail[:, gsl[q]] for q in groups]
    rd = [rd_all[:, s] for s in gsl]
    vq = [v[:, s] for s in gsl]
    amat = [_bdot_nt(jnp.concatenate([kkd[q], rd[q]], axis=0),
                     jnp.concatenate([bdiag(kp_all[:, gsl[q]]), bdiag(bp[q])], axis=0))
            for q in groups]
    mk = [jnp.where(before_g, m[:c, :gw], 0.0) for m in amat]
    nk = [jnp.where(upto_g, m[c:, :gw], 0.0) for m in amat]
    nb = [jnp.where(upto_g, m[c:, gw:], 0.0) for m in amat]
    p2 = [jnp.where(before_g, -m[:c, gw:], 0.0) for m in amat]
    tinv = [eye_g + t for t in p2]
    p2 = [_bdot(t, bdiag(t)) for t in p2]
    for _ in range(int(math.log2(c)) - 2):
        sq = [_bdot(p2[q], jnp.concatenate([bdiag(tinv[q]), bdiag(p2[q])], axis=1)) for q in groups]
        tinv = [tinv[q] + sq[q][:, :gw] for q in groups]
        p2 = [t[:, gw:] for t in sq]
    tinv = [tinv[q] + _bdot(p2[q], bdiag(tinv[q])) for q in groups]
    mnv = [_bdot(jnp.concatenate([mk[q], nk[q]], axis=0), bdiag(vq[q])) for q in groups]
    wu = [_bdot(tinv[q], jnp.concatenate([bdiag(kkd[q]), bdiag(mnv[q][:c])], axis=1))
          for q in groups]
    nwu = [_bdot(nb[q], jnp.concatenate([bdiag(wu[q][:, :gw]), bdiag(wu[q][:, gw:])], axis=1))
           for q in groups]
    wub = [tdot(wu[q], bpp[q]) for q in groups]
    vk = [diag_blocks(tdot(vq[q], kpp_all[:, gsl[q]])) for q in groups]
    s_in = [s_scr[q] for q in groups]
    for q in groups:
        qt = rd[q] - nwu[q][:, :gw]
        y_ref[:, gsl[q]] = mnv[q][c:] - nwu[q][:, gw:] + _bdot_nt(qt, bdiag(s_in[q]))
        pmat = diag_blocks(wub[q][:gw])
        dmat = vk[q] - diag_blocks(wub[q][gw:])
        s_scr[q] = s_in[q] * dg[:, gsl[q]] - _bdot(s_in[q], bdiag(pmat)) + dmat
        bonus_ref[:, gsl[q]] = head_sums(rk[:, gsl[q]]) * vq[q]

    @pl.when(i == pl.num_programs(1) - 1)
    def _():
        sl_ref[...] = s_scr[...]


def _rwkv_direction(pr, s0, p, rev):
    bsz, seqlen, width = pr.shape
    hw = s0.shape[1] * s0.shape[3]
    n_heads = hw // R_HEADSIZE
    c = R_CHUNK
    nblk = seqlen // c
    sm_col = width // 512 - 1
    blk = lambda i: _scan_block(i, nblk, rev)
    halo = lambda i: _halo_block(i, nblk, c, rev)
    vec = lambda w: pl.BlockSpec((1, w), lambda b, i: (0, 0))
    lora = p['w2'].shape[0]
    state_spec = pl.BlockSpec((None,) + s0.shape[1:], lambda b, i: (b, 0, 0, 0))
    return pl.pallas_call(
        functools.partial(_rwkv_kernel, rev=rev, n_heads=n_heads, lora=lora),
        grid=(bsz, nblk),
        in_specs=[pl.BlockSpec((None, c, 3 * hw), lambda b, i: (b, blk(i), 1)),
                  pl.BlockSpec((None, c, 512), lambda b, i: (b, blk(i), sm_col)),
                  pl.BlockSpec((None, HALO, 3 * hw), lambda b, i: (b, halo(i), 1)),
                  pl.BlockSpec((None, HALO, 512), lambda b, i: (b, halo(i), sm_col)),
                  vec(3 * hw), vec(512), vec(hw),
                  pl.BlockSpec((lora, hw), lambda b, i: (0, 0)),
                  vec(hw),
                  pl.BlockSpec((lora, hw), lambda b, i: (0, 0)),
                  vec(hw), vec(hw), vec(hw),
                  state_spec],
        out_specs=[pl.BlockSpec((None, c, hw), lambda b, i: (b, blk(i), 0)),
                   pl.BlockSpec((None, c, hw), lambda b, i: (b, blk(i), 0)),
                   state_spec],
        out_shape=[jax.ShapeDtypeStruct((bsz, seqlen, hw), F32),
                   jax.ShapeDtypeStruct((bsz, seqlen, hw), F32),
                   jax.ShapeDtypeStruct(s0.shape, F32)],
        scratch_shapes=[pltpu.VMEM(s0.shape[1:], F32),
                        pltpu.VMEM((c + HALO, 3 * hw), F32),
                        pltpu.VMEM((c + HALO, 512), F32)],
        compiler_params=_cparams("arbitrary", "arbitrary"),
    )(pr, pr, pr, pr, p['mu_rkv'], p['mu_sm'], p['w0'], p['w2'], p['a0'], p['a2'],
      p['kk'], p['ka'], p['rk'], s0)


def _expand_cols(x, e_bf16):
    hi, mid, lo = _split3(x)
    dot = lambda p: jnp.dot(p, e_bf16, preferred_element_type=F32)
    return dot(hi) + (dot(mid) + dot(lo))


def _causal_conv(scr_ref, w_ref, b_ref, rows, taps, rev):
    acc = b_ref[...] + w_ref[taps - 1:taps, :] * _lagged(scr_ref, 0, rows, rev)
    for kk in range(taps - 1):
        acc = acc + w_ref[kk:kk + 1, :] * _lagged(scr_ref, taps - 1 - kk, rows, rev)
    return acc


def _ssd_kernel(xbc_ref, halo_ref, sm_ref, dtt_ref, cw_ref, cb_ref, dtb_row_ref, dtb_col_ref,
                alog_row_ref, alog_col_ref, dskip_ref, e64_ref, e128_ref, h0_ref,
                y_ref, hl_ref, h_scr, x_scr, *, rev, n_heads):
    i = pl.program_id(1)
    c = M_CHUNK
    p = M_HEADDIM
    ns = M_STATE
    mix = n_heads * p
    hpg = n_heads // M_GROUPS

    @pl.when(i == 0)
    def _():
        h_scr[...] = h0_ref[...]

    keep = (i > 0).astype(F32)
    _stage_with_halo(x_scr, xbc_ref, halo_ref, keep, c, rev)
    xbc = _silu(_causal_conv(x_scr, cw_ref, cb_ref, c, M_CONV, rev))
    xs = xbc[:, :mix]
    bm = [xbc[:, mix + g * ns:mix + (g + 1) * ns] for g in range(M_GROUPS)]
    cm = [xbc[:, mix + (M_GROUPS + g) * ns:mix + (M_GROUPS + g + 1) * ns] for g in range(M_GROUPS)]

    ti = lax.broadcasted_iota(jnp.int32, (c, c), 0)
    tj = lax.broadcasted_iota(jnp.int32, (c, c), 1)
    upto = (tj >= ti) if rev else (tj <= ti)
    upto_t = (ti >= tj) if rev else (ti <= tj)
    end = 0 if rev else c - 1

    dt = _softplus(sm_ref[:, 480:480 + n_heads] + dtb_row_ref[...])
    a_cs = _tri_cumsum(upto.astype(BF16), dt * -jnp.exp(alog_row_ref[...]))
    dt_t = _softplus(dtt_ref[...] + dtb_col_ref[...])
    a_cs_t = _expand_cols(dt_t * -jnp.exp(alog_col_ref[...]), upto_t.astype(BF16))
    dte = jnp.exp(a_cs[end:end + 1, :] - a_cs)
    e64 = e64_ref[...]
    xdt = xs * _expand_cols(dt, e64)
    xdd_t = jnp.transpose(xs * _expand_cols(dt * dte, e64))
    ea_full = _expand_cols(jnp.exp(a_cs), e64)
    a_cs_b = _expand_cols(a_cs, e128_ref[...])

    cb = [_bdot_nt(cm[g], bm[g]) for g in range(M_GROUPS)]
    heads = range(n_heads)
    h_in = [h_scr[h] for h in heads]
    ydiag = []
    yoff = []
    for h in heads:
        g = h // hpg
        a_col = a_cs_b[:, h * 128:(h + 1) * 128]
        decay = jnp.where(upto, jnp.exp(jnp.minimum(a_col - a_cs_t[h:h + 1, :], 0.0)), 0.0)
        ydiag.append(_bdot(cb[g] * decay, xdt[:, h * p:(h + 1) * p]))
        yoff.append(_bdot_nt(cm[g], h_in[h]))
        h_scr[h] = (h_in[h] * jnp.exp(a_col[end:end + 1, :])
                    + _bdot(xdd_t[h * p:(h + 1) * p, :], bm[g]))
    y_ref[...] = (jnp.concatenate(ydiag, axis=1) + jnp.concatenate(yoff, axis=1) * ea_full
                  + xs * dskip_ref[...])

    @pl.when(i == pl.num_programs(1) - 1)
    def _():
        hl_ref[...] = h_scr[...]


def _ssd_direction(pr, dt_t, h0, p, rev):
    bsz, seqlen, width = pr.shape
    n_heads = h0.shape[1]
    mix = n_heads * M_HEADDIM
    xw = mix + 2 * M_GROUPS * M_STATE
    c = M_CHUNK
    nblk = seqlen // c
    sm_col = width // 512 - 1
    blk = lambda i: _scan_block(i, nblk, rev)
    halo = lambda i: _halo_block(i, nblk, c, rev)
    full = lambda a: pl.BlockSpec(a.shape, lambda b, i: (0,) * a.ndim)
    consts = [p['conv_w'], p['conv_b'], p['dtb_row'], p['dtb_col'], p['alog_row'], p['alog_col'],
              p['dskip'], p['e64'], p['e128']]
    state_spec = pl.BlockSpec((None,) + h0.shape[1:], lambda b, i: (b, 0, 0, 0))
    return pl.pallas_call(
        functools.partial(_ssd_kernel, rev=rev, n_heads=n_heads),
        grid=(bsz, nblk),
        in_specs=[pl.BlockSpec((None, c, xw), lambda b, i: (b, blk(i), 0)),
                  pl.BlockSpec((None, HALO, xw), lambda b, i: (b, halo(i), 0)),
                  pl.BlockSpec((None, c, 512), lambda b, i: (b, blk(i), sm_col)),
                  pl.BlockSpec((None, n_heads, c), lambda b, i: (b, 0, blk(i)))]
                 + [full(a) for a in consts] + [state_spec],
        out_specs=[pl.BlockSpec((None, c, mix), lambda b, i: (b, blk(i), 0)), state_spec],
        out_shape=[jax.ShapeDtypeStruct((bsz, seqlen, mix), F32),
                   jax.ShapeDtypeStruct(h0.shape, F32)],
        scratch_shapes=[pltpu.VMEM(h0.shape[1:], F32), pltpu.VMEM((c + HALO, xw), F32)],
        compiler_params=_cparams("arbitrary", "arbitrary"),
    )(pr, pr, pr, dt_t, *consts, h0)


def _rglru_kernel(x_ref, halo_ref, cw_ref, cb_ref, wa_ref, ba_ref, wx_ref, bx_ref, lam_ref, h0_ref,
                  o_ref, hl_ref, h_scr, x_scr, a_scr, u_scr, *, rev, tb):
    i = pl.program_id(1)

    @pl.when(i == 0)
    def _():
        h_scr[...] = h0_ref[...]

    keep = (i > 0).astype(F32)
    _stage_with_halo(x_scr, x_ref, halo_ref, keep, tb, rev)
    xc = _causal_conv(x_scr, cw_ref, cb_ref, tb, C_CONV, rev)
    blk = xc.shape[1] // C_HEADS
    xh = [xc[:, h * blk:(h + 1) * blk] for h in range(C_HEADS)]
    gate_r = jax.nn.sigmoid(
        jnp.concatenate([_bdot(xh[h], wa_ref[h]) for h in range(C_HEADS)], axis=1) + ba_ref[...])
    gate_i = jax.nn.sigmoid(
        jnp.concatenate([_bdot(xh[h], wx_ref[h]) for h in range(C_HEADS)], axis=1) + bx_ref[...])
    log_a = -RG_C * gate_r * _softplus(-lam_ref[...])
    a = jnp.exp(log_a)
    mult = jnp.sqrt(jnp.maximum(-jnp.tanh(log_a) * (a * a + 1.0), 0.0))
    a_scr[...] = a
    u_scr[...] = mult * gate_i * xc

    def step(t, h):
        tt = (tb - 1 - t) if rev else t
        h = a_scr[pl.ds(tt, 1), :] * h + u_scr[pl.ds(tt, 1), :]
        o_ref[pl.ds(tt, 1), :] = h
        return h

    h_scr[0:1, :] = lax.fori_loop(0, tb, step, h_scr[0:1, :], unroll=8)

    @pl.when(i == pl.num_programs(1) - 1)
    def _():
        hl_ref[...] = h_scr[...]


def _rglru_direction(xv, h0, p, rev, tb):
    bsz, rows, wtot = xv.shape
    width = h0.shape[2]
    ncol = wtot // (2 * width)
    nrb = rows // tb
    total = ncol * nrb
    per = tb // HALO

    def pos(i):
        s = _scan_block(i, total, rev)
        return s // nrb, s % nrb

    def x_map(b, i):
        col, q = pos(i)
        return b, q, 2 * col + 1

    def halo_map(b, i):
        s = _scan_block(i, total, rev)
        sp = jnp.minimum(s + 1, total - 1) if rev else jnp.maximum(s - 1, 0)
        col, q = sp // nrb, sp % nrb
        return b, (q * per if rev else (q + 1) * per - 1), 2 * col + 1

    def o_map(b, i):
        col, q = pos(i)
        return b, q, col

    full = lambda a: pl.BlockSpec(a.shape, lambda b, i: (0,) * a.ndim)
    consts = [p['conv_w'], p['conv_b'], p['wa'], p['ba'], p['wx'], p['bx'], p['lam']]
    state_spec = pl.BlockSpec((None, HALO, width), lambda b, i: (b, 0, 0))
    return pl.pallas_call(
        functools.partial(_rglru_kernel, rev=rev, tb=tb),
        grid=(bsz, total),
        in_specs=[pl.BlockSpec((None, tb, width), x_map), pl.BlockSpec((None, HALO, width), halo_map)]
                 + [full(a) for a in consts] + [state_spec],
        out_specs=[pl.BlockSpec((None, tb, width), o_map), state_spec],
        out_shape=[jax.ShapeDtypeStruct((bsz, rows, ncol * width), F32),
                   jax.ShapeDtypeStruct(h0.shape, F32)],
        scratch_shapes=[pltpu.VMEM((HALO, width), F32), pltpu.VMEM((tb + HALO, width), F32),
                        pltpu.VMEM((tb, width), F32), pltpu.VMEM((tb, width), F32)],
        compiler_params=_cparams("arbitrary", "arbitrary"),
    )(xv, xv, *consts, h0)


def _readout_ab_kernel(z_ref, sm_ref, ymf_ref, ymb_ref, yrf_ref, yrb_ref, bof_ref, bob_ref,
                       nw_ref, lw_ref, lb_ref, g2_ref, o_ref, *, n_heads):
    t = (ymf_ref[...] + ymb_ref[...]) * _silu(z_ref[...])
    gw = t.shape[1] // M_GROUPS
    a_out = []
    for g in range(M_GROUPS):
        tg = t[:, g * gw:(g + 1) * gw]
        a_out.append(tg * lax.rsqrt(jnp.mean(tg * tg, axis=-1, keepdims=True) + M_NORM_EPS))
    a_out = jnp.concatenate(a_out, axis=1) * nw_ref[...]

    yr = yrf_ref[...] + yrb_ref[...]
    n = R_HEADSIZE
    yn = []
    for h in range(n_heads):
        yh = yr[:, h * n:(h + 1) * n]
        dlt = yh - jnp.mean(yh, axis=-1, keepdims=True)
        yn.append(dlt * lax.rsqrt(jnp.mean(dlt * dlt, axis=-1, keepdims=True) + R_LN_EPS))
    yn = jnp.concatenate(yn, axis=1) * lw_ref[...] + lb_ref[...]
    gate = _bdot(jax.nn.sigmoid(sm_ref[:, 0:g2_ref.shape[0]]), g2_ref[...])
    b_out = (yn + bof_ref[...] + bob_ref[...]) * gate
    o_ref[...] = jnp.concatenate([a_out, b_out], axis=1).astype(o_ref.dtype)


def _readout_ab(pr, ym, yr, bo, ro, n_heads):
    bsz, seqlen, width = pr.shape
    m = bsz * seqlen
    mix = ym[0].shape[2]
    tm = _row_tile(m, 256)
    flat = lambda t: t.reshape(m, t.shape[2])
    rowblk = lambda w, col: pl.BlockSpec((tm, w), lambda i: (i, col))
    full = lambda a: pl.BlockSpec(a.shape, lambda i: (0,) * a.ndim)
    consts = [ro['norm_w'], ro['lnx_w'], ro['lnx_b'], ro['g2']]
    return pl.pallas_call(
        functools.partial(_readout_ab_kernel, n_heads=n_heads),
        grid=(m // tm,),
        in_specs=[rowblk(mix, 2), rowblk(512, width // 512 - 1)] + [rowblk(mix, 0)] * 6
                 + [full(a) for a in consts],
        out_specs=rowblk(2 * mix, 0),
        out_shape=jax.ShapeDtypeStruct((m, 2 * mix), BF16),
        compiler_params=_cparams("arbitrary"),
    )(flat(pr), flat(pr), flat(ym[0]), flat(ym[1]), flat(yr[0]), flat(yr[1]), flat(bo[0]),
      flat(bo[1]), *consts)


def _readout_c_kernel(gy_ref, hf_ref, hb_ref, o_ref):
    g = gy_ref[...]
    gelu = 0.5 * g * (1.0 + jnp.tanh(math.sqrt(2.0 / math.pi) * (g + 0.044715 * (g * g * g))))
    o_ref[...] = ((hf_ref[...] + hb_ref[...]) * gelu).astype(o_ref.dtype)


def _readout_c(p, hf, hb):
    m, cw = hf.shape
    tm = _row_tile(m, 512)
    spec = pl.BlockSpec((tm, cw), lambda i: (i, 0))
    return pl.pallas_call(
        _readout_c_kernel,
        grid=(m // tm,),
        in_specs=[spec, spec, spec],
        out_specs=spec,
        out_shape=jax.ShapeDtypeStruct((m, cw), BF16),
        compiler_params=_cparams("arbitrary"),
    )(p, hf, hb)


def kernel(x, c, ctx, c_ctx, ada_w, ada_b, norm1_w, norm2_w, ffn_w_gate, ffn_w_up, ffn_w_down, final_norm_w, ab_w_in, ab_w_out, m_conv_w, m_conv_b, m_dt_bias, m_a_log, m_d, m_norm_w, r_mu, r_w0, r_w2, r_a0, r_a2, r_kk, r_ka, r_rk, r_g2, r_lnx_w, r_lnx_b, c_w_in, c_w_out, c_conv_w, c_conv_b, c_wa, c_ba, c_wx, c_bx, c_lambda):
    bsz, seqlen, d = x.shape
    ctx_len = ctx.shape[1]
    depth = ada_w.shape[0]
    n_lat, n_ctx = bsz * seqlen, bsz * ctx_len
    row = lambda t: t.reshape(1, -1)

    mod = _adaln(jnp.concatenate([c_ctx[None, :], c], axis=0), ada_w, ada_b)

    xl = x.reshape(n_lat, d)
    xc = ctx.reshape(n_ctx, d)
    mix_a = m_norm_w.shape[1]
    m_heads = m_dt_bias.shape[2]
    r_heads = r_rk.shape[2]
    lora = r_w2.shape[2]
    e64 = jnp.repeat(jnp.eye(m_heads, dtype=BF16), M_HEADDIM, axis=1)
    e128 = jnp.repeat(jnp.eye(m_heads, dtype=BF16), 128, axis=1)

    for layer in range(depth):
        with_ctx = layer < depth - 1
        part = lambda k: mod[layer, :, k * d:(k + 1) * d]
        lat_vec = lambda k: part(k)[1:1 + bsz, None, :]
        ctx_vec = lambda k: part(k)[0:1, None, :]
        nw1, nw2 = norm1_w[layer], norm2_w[layer]

        if layer % 2 == 0:
            e = layer // 2
            w = ab_w_in[e]
            o1 = mix_a
            o2 = o1 + mix_a + 2 * M_GROUPS * M_STATE
            o3 = o2 + m_heads
            o4 = o3 + 3 * r_heads * R_HEADSIZE + 2 * lora
            rkv_w = 3 * r_heads * R_HEADSIZE
            zeros = lambda n: jnp.zeros((d, n), w.dtype)
            w_in = jnp.concatenate(
                [w[:, o1:o2], w[:, :o1], w[:, o3:o3 + rkv_w], w[:, o4:],
                 w[:, o3 + rkv_w:o3 + rkv_w + lora], zeros(128 - lora),
                 w[:, o3 + rkv_w + lora:o4], w[:, o2:o3], zeros(128 - lora - m_heads)],
                axis=1).astype(BF16)
            width = w_in.shape[1]
            w_out = ab_w_out[e].astype(BF16)
            g2w = r_g2[e].astype(BF16)

            def project(xs, sh, sc, rows_per_vec, seg_len):
                pr = _norm_mm(xs, nw1, sh, sc, [w_in], rows_per_vec, F32)
                pr = pr.reshape(bsz, seg_len, width)
                dt_t = jnp.swapaxes(pr[:, :, width - 32:width - 32 + m_heads], 1, 2)
                return pr, dt_t

            pr_l, dtt_l = project(xl, lat_vec(0), lat_vec(1), seqlen, seqlen)
            pr_c, dtt_c = project(xc, ctx_vec(0), ctx_vec(1), n_ctx, ctx_len)

            ym_l = ym_c = yr_l = yr_c = bo_l = bo_c = None
            for dr in range(2):
                rev = dr == 1
                mp = dict(conv_w=m_conv_w[e, dr], conv_b=row(m_conv_b[e, dr]),
                          dtb_row=row(m_dt_bias[e, dr]), dtb_col=m_dt_bias[e, dr].reshape(-1, 1),
                          alog_row=row(m_a_log[e, dr]), alog_col=m_a_log[e, dr].reshape(-1, 1),
                          dskip=row(jnp.repeat(m_d[e, dr], M_HEADDIM)), e64=e64, e128=e128)
                h0 = jnp.zeros((bsz, m_heads, M_HEADDIM, M_STATE), F32)
                y_c, h_c = _ssd_direction(pr_c, dtt_c, h0, mp, rev)
                y_l, _ = _ssd_direction(pr_l, dtt_l, h_c, mp, rev)
                mu = r_mu[e, dr]
                mu_sm = (jnp.zeros((512,), F32).at[256:256 + lora].set(mu[rkv_w:rkv_w + lora])
                         .at[384:384 + lora].set(mu[rkv_w + lora:]))
                rp = dict(mu_rkv=row(mu[:rkv_w]), mu_sm=row(mu_sm), w0=row(r_w0[e, dr]),
                          w2=r_w2[e, dr].astype(BF16), a0=row(r_a0[e, dr]), a2=r_a2[e, dr].astype(BF16),
                          kk=row(r_kk[e, dr]), ka=row(r_ka[e, dr]), rk=row(r_rk[e, dr]))
                s0 = jnp.zeros((bsz, r_heads // R_GROUP, R_HEADSIZE, R_GROUP * R_HEADSIZE), F32)
                v_c, b_c, s_c = _rwkv_direction(pr_c, s0, rp, rev)
                v_l, b_l, _ = _rwkv_direction(pr_l, s_c, rp, rev)
                if dr == 0:
                    ym_l, ym_c, yr_l, yr_c, bo_l, bo_c = y_l, y_c, v_l, v_c, b_l, b_c
                else:
                    ym_l, yr_l, bo_l = (ym_l, y_l), (yr_l, v_l), (bo_l, b_l)
                    ym_c, yr_c, bo_c = (ym_c, y_c), (yr_c, v_c), (bo_c, b_c)

            ro = dict(norm_w=row(m_norm_w[e]), lnx_w=row(r_lnx_w[e]), lnx_b=row(r_lnx_b[e]), g2=g2w)
            act_l = _readout_ab(pr_l, ym_l, yr_l, bo_l, ro, r_heads)
            xl = _mm_res(act_l, w_out, xl, lat_vec(2), seqlen)
            if with_ctx:
                act_c = _readout_ab(pr_c, ym_c, yr_c, bo_c, ro, r_heads)
                xc = _mm_res(act_c, w_out, xc, ctx_vec(2), n_ctx)
        else:
            o = layer // 2
            w_in = c_w_in[o].astype(BF16)
            w_out = c_w_out[o].astype(BF16)
            cw = w_out.shape[0]
            rows = seqlen // GRID_W
            p_l = _norm_mm(xl, nw1, lat_vec(0), lat_vec(1), [w_in], seqlen, F32)
            p_c = _norm_mm(xc, nw1, ctx_vec(0), ctx_vec(1), [w_in], n_ctx, F32)
            hs_l, hs_c = [], []
            for dr in range(2):
                rev = dr == 1
                cp = dict(conv_w=c_conv_w[o, dr], conv_b=row(c_conv_b[o, dr]),
                          wa=c_wa[o, dr].astype(BF16), ba=row(c_ba[o, dr]),
                          wx=c_wx[o, dr].astype(BF16), bx=row(c_bx[o, dr]), lam=row(c_lambda[o, dr]))
                h0 = jnp.zeros((bsz, HALO, cw), F32)
                h_c, s_c = _rglru_direction(p_c.reshape(bsz, ctx_len, 2 * cw), h0, cp, rev,
                                            _row_tile(ctx_len, 128))
                h_l, _ = _rglru_direction(p_l.reshape(bsz, rows, GRID_W * 2 * cw), s_c, cp, rev, rows)
                hs_l.append(h_l.reshape(n_lat, cw))
                hs_c.append(h_c.reshape(n_ctx, cw))
            act_l = _readout_c(p_l, hs_l[0], hs_l[1])
            xl = _mm_res(act_l, w_out, xl, lat_vec(2), seqlen)
            if with_ctx:
                act_c = _readout_c(p_c, hs_c[0], hs_c[1])
                xc = _mm_res(act_c, w_out, xc, ctx_vec(2), n_ctx)

        wg, wu, wd = (ffn_w_gate[layer].astype(BF16), ffn_w_up[layer].astype(BF16),
                      ffn_w_down[layer].astype(BF16))
        act = _norm_mm(xl, nw2, lat_vec(3), lat_vec(4), [wg, wu], seqlen, BF16)
        xl = _mm_res(act, wd, xl, lat_vec(5), seqlen)
        if with_ctx:
            act = _norm_mm(xc, nw2, ctx_vec(3), ctx_vec(4), [wg, wu], n_ctx, BF16)
            xc = _mm_res(act, wd, xc, ctx_vec(5), n_ctx)

    return _final_norm(xl, final_norm_w).reshape(bsz, seqlen, d)
```

```python
import functools
import math

import jax
import jax.numpy as jnp
from jax import lax
from jax.experimental import pallas as pl
from jax.experimental.pallas import tpu as pltpu

F32 = jnp.float32
BF16 = jnp.bfloat16

NORM_EPS = 1e-6
GRID_W = 64
M_HEADDIM = 64
M_GROUPS = 4
M_STATE = 128
M_CONV = 4
M_CHUNK = 128
M_NORM_EPS = 1e-5
R_HEADSIZE = 64
R_CHUNK = 64
R_GROUP = 4
R_LN_EPS = 64e-5
C_HEADS = 8
C_CONV = 4
RG_C = 8.0

VMEM_LIMIT_BYTES = 56 * 1024 * 1024
HALO = 8


def _cparams(*sem):
    return pltpu.CompilerParams(dimension_semantics=sem, vmem_limit_bytes=VMEM_LIMIT_BYTES)


def _bdot(a, b):
    return jnp.dot(a.astype(BF16), b.astype(BF16), preferred_element_type=F32)


def _bdot_nt(a, b):
    return lax.dot_general(a.astype(BF16), b.astype(BF16), (((1,), (1,)), ((), ())),
                           preferred_element_type=F32)


def _split3(x):
    hi = x.astype(BF16)
    r1 = x - hi.astype(F32)
    mid = r1.astype(BF16)
    lo = (r1 - mid.astype(F32)).astype(BF16)
    return hi, mid, lo


def _tri_cumsum(tri_bf16, x, parts=3):
    hi, mid, lo = _split3(x)
    dot = lambda p: jnp.dot(tri_bf16, p, preferred_element_type=F32)
    if parts == 2:
        return dot(hi) + dot(mid)
    return dot(hi) + (dot(mid) + dot(lo))


def _silu(x):
    return x * jax.nn.sigmoid(x)


def _softplus(x):
    return jnp.maximum(x, 0.0) + jnp.log1p(jnp.exp(-jnp.abs(x)))


def _adaln_kernel(cb_ref, w_ref, b_ref, o_ref, *, n_vec, tn):
    rows = []
    for v in range(n_vec):
        cols = []
        for j in range(tn // 128):
            wj = w_ref[:, j * 128:(j + 1) * 128]
            cols.append(jnp.sum(wj * cb_ref[v], axis=0, keepdims=True))
        rows.append(jnp.concatenate(cols, axis=1))
    pad = jnp.zeros((HALO - n_vec, tn), F32)
    o_ref[...] = jnp.concatenate(rows + [pad], axis=0) + b_ref[...]


def _adaln(cond, ada_w, ada_b):
    depth, d, n6 = ada_w.shape
    n_vec = cond.shape[0]
    tn = 1024
    cb = jnp.broadcast_to(_silu(cond)[:, :, None], (n_vec, d, 128))
    return pl.pallas_call(
        functools.partial(_adaln_kernel, n_vec=n_vec, tn=tn),
        grid=(depth, n6 // tn),
        in_specs=[pl.BlockSpec((n_vec, d, 128), lambda l, j: (0, 0, 0)),
                  pl.BlockSpec((None, d, tn), lambda l, j: (l, 0, j)),
                  pl.BlockSpec((None, 1, tn), lambda l, j: (l, 0, j))],
        out_specs=pl.BlockSpec((None, HALO, tn), lambda l, j: (l, 0, j)),
        out_shape=jax.ShapeDtypeStruct((depth, HALO, n6), F32),
        compiler_params=_cparams("arbitrary", "arbitrary"),
    )(cb, ada_w, ada_b.reshape(depth, 1, n6))


def _modulated(x_ref, nw_ref, sh_ref, sc_ref):
    x = x_ref[...]
    y = x * lax.rsqrt(jnp.mean(x * x, axis=-1, keepdims=True) + NORM_EPS)
    return (y * nw_ref[...]) * (1.0 + sc_ref[...]) + sh_ref[...]


def _norm_mm_kernel(x_ref, nw_ref, sh_ref, sc_ref, w_ref, o_ref, h_ref):
    @pl.when(pl.program_id(1) == 0)
    def _():
        h_ref[...] = _modulated(x_ref, nw_ref, sh_ref, sc_ref).astype(BF16)

    o_ref[...] = jnp.dot(h_ref[...], w_ref[...], preferred_element_type=F32).astype(o_ref.dtype)


def _norm_swiglu_kernel(x_ref, nw_ref, sh_ref, sc_ref, wg_ref, wu_ref, o_ref, h_ref):
    @pl.when(pl.program_id(1) == 0)
    def _():
        h_ref[...] = _modulated(x_ref, nw_ref, sh_ref, sc_ref).astype(BF16)

    h = h_ref[...]
    g = jnp.dot(h, wg_ref[...], preferred_element_type=F32)
    u = jnp.dot(h, wu_ref[...], preferred_element_type=F32)
    o_ref[...] = (_silu(g) * u).astype(o_ref.dtype)


def _row_tile(m, cap):
    t = cap
    while m % t:
        t //= 2
    return t


def _norm_mm(x, nw, shift, scale, weights, rows_per_vec, out_dtype, tm_cap=1024, tn=512):
    m, d = x.shape
    n = weights[0].shape[1]
    tm = _row_tile(rows_per_vec, tm_cap)
    vec_map = lambda i, j: ((i * tm) // rows_per_vec, 0, 0)
    kern = _norm_mm_kernel if len(weights) == 1 else _norm_swiglu_kernel
    return pl.pallas_call(
        kern,
        grid=(m // tm, n // tn),
        in_specs=[pl.BlockSpec((tm, d), lambda i, j: (i, 0)),
                  pl.BlockSpec((1, d), lambda i, j: (0, 0)),
                  pl.BlockSpec((None, 1, d), vec_map),
                  pl.BlockSpec((None, 1, d), vec_map)]
                 + [pl.BlockSpec((d, tn), lambda i, j: (0, j)) for _ in weights],
        out_specs=pl.BlockSpec((tm, tn), lambda i, j: (i, j)),
        out_shape=jax.ShapeDtypeStruct((m, n), out_dtype),
        scratch_shapes=[pltpu.VMEM((tm, d), BF16)],
        compiler_params=_cparams("arbitrary", "arbitrary"),
    )(x, nw.reshape(1, d), shift, scale, *weights)


def _mm_res_kernel(a_ref, w_ref, res_ref, g_ref, o_ref):
    acc = jnp.dot(a_ref[...], w_ref[...], preferred_element_type=F32)
    o_ref[...] = res_ref[...] + g_ref[...] * acc


def _mm_res(a, w, res, gate, rows_per_vec, tm_cap=1024, tn=512):
    m, k = a.shape
    n = w.shape[1]
    tm = _row_tile(rows_per_vec, tm_cap)
    return pl.pallas_call(
        _mm_res_kernel,
        grid=(m // tm, n // tn),
        in_specs=[pl.BlockSpec((tm, k), lambda i, j: (i, 0)),
                  pl.BlockSpec((k, tn), lambda i, j: (0, j)),
                  pl.BlockSpec((tm, tn), lambda i, j: (i, j)),
                  pl.BlockSpec((None, 1, tn), lambda i, j: ((i * tm) // rows_per_vec, 0, j))],
        out_specs=pl.BlockSpec((tm, tn), lambda i, j: (i, j)),
        out_shape=jax.ShapeDtypeStruct((m, n), F32),
        compiler_params=_cparams("arbitrary", "arbitrary"),
    )(a, w, res, gate)


def _final_norm_kernel(x_ref, w_ref, o_ref):
    x = x_ref[...]
    o_ref[...] = x * lax.rsqrt(jnp.mean(x * x, axis=-1, keepdims=True) + NORM_EPS) * w_ref[...]


def _final_norm(x, w):
    m, d = x.shape
    tm = _row_tile(m, 512)
    return pl.pallas_call(
        _final_norm_kernel,
        grid=(m // tm,),
        in_specs=[pl.BlockSpec((tm, d), lambda i: (i, 0)), pl.BlockSpec((1, d), lambda i: (0, 0))],
        out_specs=pl.BlockSpec((tm, d), lambda i: (i, 0)),
        out_shape=jax.ShapeDtypeStruct((m, d), F32),
        compiler_params=_cparams("arbitrary"),
    )(x, w.reshape(1, d))


def _scan_block(i, n_blocks, rev):
    return (n_blocks - 1 - i) if rev else i


def _halo_block(i, n_blocks, rows, rev):
    per = rows // HALO
    if rev:
        return jnp.minimum((_scan_block(i, n_blocks, True) + 1) * per, n_blocks * per - 1)
    return jnp.maximum(i * per - 1, 0)


def _stage_with_halo(scr_ref, x_ref, halo_ref, keep, rows, rev):
    if rev:
        scr_ref[0:rows, :] = x_ref[...]
        scr_ref[rows:rows + HALO, :] = halo_ref[...] * keep
    else:
        scr_ref[0:HALO, :] = halo_ref[...] * keep
        scr_ref[HALO:HALO + rows, :] = x_ref[...]


def _lagged(scr_ref, lag, rows, rev):
    start = lag if rev else HALO - lag
    return scr_ref[start:start + rows, :]


def _rwkv_kernel(rkv_ref, sm_ref, rkv_h_ref, sm_h_ref, mu_rkv_ref, mu_sm_ref, w0_ref, w2_ref,
                 a0_ref, a2_ref, kkw_ref, kaw_ref, rkw_ref, s0_ref,
                 y_ref, bonus_ref, sl_ref, s_scr, rkv_scr, sm_scr, *, rev, n_heads, lora):
    i = pl.program_id(1)
    c = R_CHUNK
    n = R_HEADSIZE
    hw = n_heads * n

    @pl.when(i == 0)
    def _():
        s_scr[...] = s0_ref[...]

    keep = (i > 0).astype(F32)
    _stage_with_halo(rkv_scr, rkv_ref, rkv_h_ref, keep, c, rev)
    _stage_with_halo(sm_scr, sm_ref, sm_h_ref, keep, c, rev)
    x = rkv_ref[...]
    f = x + (_lagged(rkv_scr, 1, c, rev) - x) * mu_rkv_ref[...]
    xs = sm_ref[...]
    fs = xs + (_lagged(sm_scr, 1, c, rev) - xs) * mu_sm_ref[...]
    r = f[:, :hw]
    k = f[:, hw:2 * hw]
    v = f[:, 2 * hw:3 * hw]
    wl = fs[:, 256:256 + lora]
    al = fs[:, 384:384 + lora]

    ww = w0_ref[...] + _bdot(jnp.tanh(wl), w2_ref[...])
    lw = -math.exp(-0.5) * jax.nn.sigmoid(ww)
    a = jax.nn.sigmoid(a0_ref[...] + _bdot(al, a2_ref[...]))
    kkr = k * kkw_ref[...]
    k2 = k * (1.0 + (a - 1.0) * kaw_ref[...])
    rk = r * k2 * rkw_ref[...]

    ti = lax.broadcasted_iota(jnp.int32, (c, c), 0)
    tj = lax.broadcasted_iota(jnp.int32, (c, c), 1)
    upto = (tj >= ti) if rev else (tj <= ti)
    g = _tri_cumsum(upto.astype(BF16), lw, parts=2)
    gtot = g[0:1, :] if rev else g[c - 1:c, :]
    eg = jnp.exp(g)
    eneg = jnp.exp(-g)
    egp = jnp.exp(g - lw)
    etail = jnp.exp(gtot - g)
    dg = jnp.exp(gtot)
    rd_all = r * eg
    kp_all = k2 * eneg
    kpp_all = k2 * etail

    gw = R_GROUP * n
    groups = range(hw // gw)
    gsl = [slice(q * gw, (q + 1) * gw) for q in groups]
    shift = int(math.log2(n))
    bi = lax.broadcasted_iota(jnp.int32, (gw, gw), 0) >> shift
    bj = lax.broadcasted_iota(jnp.int32, (gw, gw), 1) >> shift
    same_head = bi == bj
    ones_bd = same_head.astype(BF16)
    lane_head = lax.broadcasted_iota(jnp.int32, (1, 128), 1) >> shift
    keep_lanes = [(lane_head == j).astype(BF16) for j in range(128 // n)]

    def bdiag(t):
        tb = t.astype(BF16)
        zero = jnp.zeros((c, 128), BF16)
        rows = []
        for h in range(R_GROUP):
            lt, j = divmod(h * n, 128)
            piece = tb[:, lt * 128:(lt + 1) * 128] * keep_lanes[j // n]
            rows.append(jnp.concatenate([piece if u == lt else zero for u in range(gw // 128)], axis=1))
        return jnp.concatenate(rows, axis=0)

    def diag_blocks(full):
        masked = jnp.where(same_head, full, 0.0)
        out = masked[0:n]
        for h in range(1, R_GROUP):
            out = out + masked[h * n:(h + 1) * n]
        return out

    def head_sums(t):
        hi, mid, _ = _split3(t)
        dot = lambda p: jnp.dot(p, ones_bd, preferred_element_type=F32)
        return dot(hi) + dot(mid)

    tdot = lambda p, q: lax.dot_general(p.astype(BF16), q.astype(BF16), (((0,), (0,)), ((), ())),
                                        preferred_element_type=F32)
    gi = lax.broadcasted_iota(jnp.int32, (c, gw), 0)
    gj = lax.broadcasted_iota(jnp.int32, (c, gw), 1) & (c - 1)
    before_g = (gj > gi) if rev else (gj < gi)
    upto_g = (gj >= gi) if rev else (gj <= gi)
    eye_g = (gi == gj).astype(F32)

    kk = [kkr[:, s] * lax.rsqrt(jnp.maximum(head_sums(kkr[:, s] * kkr[:, s]), 1e-24)) for s in gsl]
    bh = [kk[q] * a[:, gsl[q]] for q in groups]
    kkd = [kk[q] * egp[:, gsl[q]] for q in groups]
    bp = [bh[q] * eneg[:, gsl[q]] for q in groups]
    bpp = [bh[q] * etail[:, gsl[q]] for q in groups]
    rd = [rd_all[:, s] for s in gsl]
    vq = [v[:, s] for s in gsl]
    amat = [_bdot_nt(jnp.concatenate([kkd[q], rd[q]], axis=0),
                     jnp.concatenate([bdiag(kp_all[:, gsl[q]]), bdiag(bp[q])], axis=0))
            for q in groups]
    mk = [jnp.where(before_g, m[:c, :gw], 0.0) for m in amat]
    nk = [jnp.where(upto_g, m[c:, :gw], 0.0) for m in amat]
    nb = [jnp.where(upto_g, m[c:, gw:], 0.0) for m in amat]
    p2 = [jnp.where(before_g, -m[:c, gw:], 0.0) for m in amat]
    tinv = [eye_g + t for t in p2]
    p2 = [_bdot(t, bdiag(t)) for t in p2]
    for _ in range(int(math.log2(c)) - 2):
        sq = [_bdot(p2[q], jnp.concatenate([bdiag(tinv[q]), bdiag(p2[q])], axis=1)) for q in groups]
        tinv = [tinv[q] + sq[q][:, :gw] for q in groups]
        p2 = [t[:, gw:] for t in sq]
    tinv = [tinv[q] + _bdot(p2[q], bdiag(tinv[q])) for q in groups]
    mnv = [_bdot(jnp.concatenate([mk[q], nk[q]], axis=0), bdiag(vq[q])) for q in groups]
    wu = [_bdot(tinv[q], jnp.concatenate([bdiag(kkd[q]), bdiag(mnv[q][:c])], axis=1))
          for q in groups]
    nwu = [_bdot(nb[q], jnp.concatenate([bdiag(wu[q][:, :gw]), bdiag(wu[q][:, gw:])], axis=1))
           for q in groups]
    wub = [tdot(wu[q], bpp[q]) for q in groups]
    vk = [diag_blocks(tdot(vq[q], kpp_all[:, gsl[q]])) for q in groups]
    s_in = [s_scr[q] for q in groups]
    for q in groups:
        qt = rd[q] - nwu[q][:, :gw]
        y_ref[:, gsl[q]] = mnv[q][c:] - nwu[q][:, gw:] + _bdot_nt(qt, bdiag(s_in[q]))
        pmat = diag_blocks(wub[q][:gw])
        dmat = vk[q] - diag_blocks(wub[q][gw:])
        s_scr[q] = s_in[q] * dg[:, gsl[q]] - _bdot(s_in[q], bdiag(pmat)) + dmat
        bonus_ref[:, gsl[q]] = head_sums(rk[:, gsl[q]]) * vq[q]

    @pl.when(i == pl.num_programs(1) - 1)
    def _():
        sl_ref[...] = s_scr[...]


def _rwkv_direction(pr, s0, p, rev):
    bsz, seqlen, width = pr.shape
    hw = s0.shape[1] * s0.shape[3]
    n_heads = hw // R_HEADSIZE
    c = R_CHUNK
    nblk = seqlen // c
    sm_col = width // 512 - 1
    blk = lambda i: _scan_block(i, nblk, rev)
    halo = lambda i: _halo_block(i, nblk, c, rev)
    vec = lambda w: pl.BlockSpec((1, w), lambda b, i: (0, 0))
    lora = p['w2'].shape[0]
    state_spec = pl.BlockSpec((None,) + s0.shape[1:], lambda b, i: (b, 0, 0, 0))
    return pl.pallas_call(
        functools.partial(_rwkv_kernel, rev=rev, n_heads=n_heads, lora=lora),
        grid=(bsz, nblk),
        in_specs=[pl.BlockSpec((None, c, 3 * hw), lambda b, i: (b, blk(i), 1)),
                  pl.BlockSpec((None, c, 512), lambda b, i: (b, blk(i), sm_col)),
                  pl.BlockSpec((None, HALO, 3 * hw), lambda b, i: (b, halo(i), 1)),
                  pl.BlockSpec((None, HALO, 512), lambda b, i: (b, halo(i), sm_col)),
                  vec(3 * hw), vec(512), vec(hw),
                  pl.BlockSpec((lora, hw), lambda b, i: (0, 0)),
                  vec(hw),
                  pl.BlockSpec((lora, hw), lambda b, i: (0, 0)),
                  vec(hw), vec(hw), vec(hw),
                  state_spec],
        out_specs=[pl.BlockSpec((None, c, hw), lambda b, i: (b, blk(i), 0)),
                   pl.BlockSpec((None, c, hw), lambda b, i: (b, blk(i), 0)),
                   state_spec],
        out_shape=[jax.ShapeDtypeStruct((bsz, seqlen, hw), F32),
                   jax.ShapeDtypeStruct((bsz, seqlen, hw), F32),
                   jax.ShapeDtypeStruct(s0.shape, F32)],
        scratch_shapes=[pltpu.VMEM(s0.shape[1:], F32),
                        pltpu.VMEM((c + HALO, 3 * hw), F32),
                        pltpu.VMEM((c + HALO, 512), F32)],
        compiler_params=_cparams("arbitrary", "arbitrary"),
    )(pr, pr, pr, pr, p['mu_rkv'], p['mu_sm'], p['w0'], p['w2'], p['a0'], p['a2'],
      p['kk'], p['ka'], p['rk'], s0)


def _expand_cols(x, e_bf16):
    hi, mid, lo = _split3(x)
    dot = lambda p: jnp.dot(p, e_bf16, preferred_element_type=F32)
    return dot(hi) + (dot(mid) + dot(lo))


def _causal_conv(scr_ref, w_ref, b_ref, rows, taps, rev):
    acc = b_ref[...] + w_ref[taps - 1:taps, :] * _lagged(scr_ref, 0, rows, rev)
    for kk in range(taps - 1):
        acc = acc + w_ref[kk:kk + 1, :] * _lagged(scr_ref, taps - 1 - kk, rows, rev)
    return acc


def _ssd_kernel(xbc_ref, halo_ref, sm_ref, dtt_ref, cw_ref, cb_ref, dtb_row_ref, dtb_col_ref,
                alog_row_ref, alog_col_ref, dskip_ref, e64_ref, e128_ref, h0_ref,
                y_ref, hl_ref, h_scr, x_scr, *, rev, n_heads):
    i = pl.program_id(1)
    c = M_CHUNK
    p = M_HEADDIM
    ns = M_STATE
    mix = n_heads * p
    hpg = n_heads // M_GROUPS

    @pl.when(i == 0)
    def _():
        h_scr[...] = h0_ref[...]

    keep = (i > 0).astype(F32)
    _stage_with_halo(x_scr, xbc_ref, halo_ref, keep, c, rev)
    xbc = _silu(_causal_conv(x_scr, cw_ref, cb_ref, c, M_CONV, rev))
    xs = xbc[:, :mix]
    bm = [xbc[:, mix + g * ns:mix + (g + 1) * ns] for g in range(M_GROUPS)]
    cm = [xbc[:, mix + (M_GROUPS + g) * ns:mix + (M_GROUPS + g + 1) * ns] for g in range(M_GROUPS)]

    ti = lax.broadcasted_iota(jnp.int32, (c, c), 0)
    tj = lax.broadcasted_iota(jnp.int32, (c, c), 1)
    upto = (tj >= ti) if rev else (tj <= ti)
    upto_t = (ti >= tj) if rev else (ti <= tj)
    end = 0 if rev else c - 1

    dt = _softplus(sm_ref[:, 480:480 + n_heads] + dtb_row_ref[...])
    a_cs = _tri_cumsum(upto.astype(BF16), dt * -jnp.exp(alog_row_ref[...]))
    dt_t = _softplus(dtt_ref[...] + dtb_col_ref[...])
    a_cs_t = _expand_cols(dt_t * -jnp.exp(alog_col_ref[...]), upto_t.astype(BF16))
    dte = jnp.exp(a_cs[end:end + 1, :] - a_cs)
    e64 = e64_ref[...]
    xdt = xs * _expand_cols(dt, e64)
    xdd_t = jnp.transpose(xs * _expand_cols(dt * dte, e64))
    ea_full = _expand_cols(jnp.exp(a_cs), e64)
    a_cs_b = _expand_cols(a_cs, e128_ref[...])

    cb = [_bdot_nt(cm[g], bm[g]) for g in range(M_GROUPS)]
    heads = range(n_heads)
    h_in = [h_scr[h] for h in heads]
    ydiag = []
    yoff = []
    for h in heads:
        g = h // hpg
        a_col = a_cs_b[:, h * 128:(h + 1) * 128]
        decay = jnp.where(upto, jnp.exp(jnp.minimum(a_col - a_cs_t[h:h + 1, :], 0.0)), 0.0)
        ydiag.append(_bdot(cb[g] * decay, xdt[:, h * p:(h + 1) * p]))
        yoff.append(_bdot_nt(cm[g], h_in[h]))
        h_scr[h] = (h_in[h] * jnp.exp(a_col[end:end + 1, :])
                    + _bdot(xdd_t[h * p:(h + 1) * p, :], bm[g]))
    y_ref[...] = (jnp.concatenate(ydiag, axis=1) + jnp.concatenate(yoff, axis=1) * ea_full
                  + xs * dskip_ref[...])

    @pl.when(i == pl.num_programs(1) - 1)
    def _():
        hl_ref[...] = h_scr[...]


def _ssd_direction(pr, dt_t, h0, p, rev):
    bsz, seqlen, width = pr.shape
    n_heads = h0.shape[1]
    mix = n_heads * M_HEADDIM
    xw = mix + 2 * M_GROUPS * M_STATE
    c = M_CHUNK
    nblk = seqlen // c
    sm_col = width // 512 - 1
    blk = lambda i: _scan_block(i, nblk, rev)
    halo = lambda i: _halo_block(i, nblk, c, rev)
    full = lambda a: pl.BlockSpec(a.shape, lambda b, i: (0,) * a.ndim)
    consts = [p['conv_w'], p['conv_b'], p['dtb_row'], p['dtb_col'], p['alog_row'], p['alog_col'],
              p['dskip'], p['e64'], p['e128']]
    state_spec = pl.BlockSpec((None,) + h0.shape[1:], lambda b, i: (b, 0, 0, 0))
    return pl.pallas_call(
        functools.partial(_ssd_kernel, rev=rev, n_heads=n_heads),
        grid=(bsz, nblk),
        in_specs=[pl.BlockSpec((None, c, xw), lambda b, i: (b, blk(i), 0)),
                  pl.BlockSpec((None, HALO, xw), lambda b, i: (b, halo(i), 0)),
                  pl.BlockSpec((None, c, 512), lambda b, i: (b, blk(i), sm_col)),
                  pl.BlockSpec((None, n_heads, c), lambda b, i: (b, 0, blk(i)))]
                 + [full(a) for a in consts] + [state_spec],
        out_specs=[pl.BlockSpec((None, c, mix), lambda b, i: (b, blk(i), 0)), state_spec],
        out_shape=[jax.ShapeDtypeStruct((bsz, seqlen, mix), F32),
                   jax.ShapeDtypeStruct(h0.shape, F32)],
        scratch_shapes=[pltpu.VMEM(h0.shape[1:], F32), pltpu.VMEM((c + HALO, xw), F32)],
        compiler_params=_cparams("arbitrary", "arbitrary"),
    )(pr, pr, pr, dt_t, *consts, h0)


def _rglru_kernel(x_ref, halo_ref, cw_ref, cb_ref, wa_ref, ba_ref, wx_ref, bx_ref, lam_ref, h0_ref,
                  o_ref, hl_ref, h_scr, x_scr, a_scr, u_scr, *, rev, tb):
    i = pl.program_id(1)

    @pl.when(i == 0)
    def _():
        h_scr[...] = h0_ref[...]

    keep = (i > 0).astype(F32)
    _stage_with_halo(x_scr, x_ref, halo_ref, keep, tb, rev)
    xc = _causal_conv(x_scr, cw_ref, cb_ref, tb, C_CONV, rev)
    a_scr[...], u_scr[...] = _rglru_gates(xc, wa_ref, ba_ref, wx_ref, bx_ref, lam_ref)

    def step(t, h):
        tt = (tb - 1 - t) if rev else t
        h = a_scr[pl.ds(tt, 1), :] * h + u_scr[pl.ds(tt, 1), :]
        o_ref[pl.ds(tt, 1), :] = h
        return h

    h_scr[0:1, :] = lax.fori_loop(0, tb, step, h_scr[0:1, :], unroll=8)

    @pl.when(i == pl.num_programs(1) - 1)
    def _():
        hl_ref[...] = h_scr[...]


def _rglru_direction(xv, h0, p, rev, tb):
    bsz, rows, wtot = xv.shape
    width = h0.shape[2]
    ncol = wtot // (2 * width)
    nrb = rows // tb
    total = ncol * nrb
    per = tb // HALO

    def pos(i):
        s = _scan_block(i, total, rev)
        return s // nrb, s % nrb

    def x_map(b, i):
        col, q = pos(i)
        return b, q, 2 * col + 1

    def halo_map(b, i):
        s = _scan_block(i, total, rev)
        sp = jnp.minimum(s + 1, total - 1) if rev else jnp.maximum(s - 1, 0)
        col, q = sp // nrb, sp % nrb
        return b, (q * per if rev else (q + 1) * per - 1), 2 * col + 1

    def o_map(b, i):
        col, q = pos(i)
        return b, q, col

    full = lambda a: pl.BlockSpec(a.shape, lambda b, i: (0,) * a.ndim)
    consts = [p['conv_w'], p['conv_b'], p['wa'], p['ba'], p['wx'], p['bx'], p['lam']]
    state_spec = pl.BlockSpec((None, HALO, width), lambda b, i: (b, 0, 0))
    return pl.pallas_call(
        functools.partial(_rglru_kernel, rev=rev, tb=tb),
        grid=(bsz, total),
        in_specs=[pl.BlockSpec((None, tb, width), x_map), pl.BlockSpec((None, HALO, width), halo_map)]
                 + [full(a) for a in consts] + [state_spec],
        out_specs=[pl.BlockSpec((None, tb, width), o_map), state_spec],
        out_shape=[jax.ShapeDtypeStruct((bsz, rows, ncol * width), F32),
                   jax.ShapeDtypeStruct(h0.shape, F32)],
        scratch_shapes=[pltpu.VMEM((HALO, width), F32), pltpu.VMEM((tb + HALO, width), F32),
                        pltpu.VMEM((tb, width), F32), pltpu.VMEM((tb, width), F32)],
        compiler_params=_cparams("arbitrary", "arbitrary"),
    )(xv, xv, *consts, h0)


def _rglru_gates(xc, wa_ref, ba_ref, wx_ref, bx_ref, lam_ref):
    blk = xc.shape[1] // C_HEADS
    xh = [xc[:, h * blk:(h + 1) * blk] for h in range(C_HEADS)]
    gate_r = jax.nn.sigmoid(
        jnp.concatenate([_bdot(xh[h], wa_ref[h]) for h in range(C_HEADS)], axis=1) + ba_ref[...])
    gate_i = jax.nn.sigmoid(
        jnp.concatenate([_bdot(xh[h], wx_ref[h]) for h in range(C_HEADS)], axis=1) + bx_ref[...])
    log_a = -RG_C * gate_r * _softplus(-lam_ref[...])
    a = jnp.exp(log_a)
    one_m_a2 = jnp.maximum(-jnp.tanh(log_a) * (a * a + 1.0), 0.0)
    mult = jnp.where(one_m_a2 > 0.0, one_m_a2 * lax.rsqrt(one_m_a2), 0.0)
    return a, mult * gate_i * xc


def _rglru_grid_kernel(x_ref, e0_ref, e1_ref, e2_ref, cw_ref, cb_ref, wa_ref, ba_ref, wx_ref,
                       bx_ref, lam_ref, h0_ref, hloc_ref, aprod_ref, cin_ref, hl_ref,
                       ring_scr, h_scr, ap_scr, stage_scr, *, rev, ncol):
    i = pl.program_id(1)

    def from_previous_column(e_ref):
        _stage_with_halo(stage_scr, e_ref, jnp.zeros((HALO, e_ref.shape[1]), F32), 0.0, ncol, rev)
        return _lagged(stage_scr, 1, ncol, rev)

    @pl.when(i == 0)
    def _():
        ring_scr[0] = from_previous_column(e2_ref)
        ring_scr[1] = from_previous_column(e1_ref)
        ring_scr[2] = from_previous_column(e0_ref)
        h_scr[...] = jnp.zeros(h_scr.shape, F32)
        ap_scr[...] = jnp.ones(ap_scr.shape, F32)

    x0 = x_ref[...]
    lag1, lag2, lag3 = ring_scr[0], ring_scr[1], ring_scr[2]
    xc = (cb_ref[...] + cw_ref[3:4, :] * x0 + cw_ref[2:3, :] * lag1
          + cw_ref[1:2, :] * lag2 + cw_ref[0:1, :] * lag3)
    ring_scr[2] = lag2
    ring_scr[1] = lag1
    ring_scr[0] = x0
    a, u = _rglru_gates(xc, wa_ref, ba_ref, wx_ref, bx_ref, lam_ref)
    h = a * h_scr[...] + u
    ap = a * ap_scr[...]
    h_scr[...] = h
    ap_scr[...] = ap
    hloc_ref[...] = h
    aprod_ref[...] = ap

    @pl.when(i == pl.num_programs(1) - 1)
    def _():
        def step(t, carry):
            col = (ncol - 1 - t) if rev else t
            cin_ref[pl.ds(col, 1), :] = carry
            return h_scr[pl.ds(col, 1), :] + ap_scr[pl.ds(col, 1), :] * carry

        last = lax.fori_loop(0, ncol, step, h0_ref[0:1, :], unroll=8)
        hl_ref[...] = jnp.zeros(hl_ref.shape, F32)
        hl_ref[0:1, :] = last


def _rglru_grid_direction(p3, h0, p, rev):
    bsz, seqlen, _ = p3.shape
    width = h0.shape[2]
    ncol = GRID_W
    rows = seqlen // ncol
    blk = lambda i: _scan_block(i, rows, rev)
    edge = lambda k: pl.BlockSpec((None, ncol, width),
                                  lambda b, i: (b, (2 - k) if rev else (rows - 3 + k), 1))
    full = lambda a: pl.BlockSpec(a.shape, lambda b, i: (0,) * a.ndim)
    consts = [p['conv_w'], p['conv_b'], p['wa'], p['ba'], p['wx'], p['bx'], p['lam']]
    state_spec = pl.BlockSpec((None, HALO, width), lambda b, i: (b, 0, 0))
    row_spec = pl.BlockSpec((None, ncol, width), lambda b, i: (b, blk(i), 0))
    return pl.pallas_call(
        functools.partial(_rglru_grid_kernel, rev=rev, ncol=ncol),
        grid=(bsz, rows),
        in_specs=[pl.BlockSpec((None, ncol, width), lambda b, i: (b, blk(i), 1)),
                  edge(0), edge(1), edge(2)] + [full(a) for a in consts] + [state_spec],
        out_specs=[row_spec, row_spec,
                   pl.BlockSpec((None, ncol, width), lambda b, i: (b, 0, 0)), state_spec],
        out_shape=[jax.ShapeDtypeStruct((bsz, seqlen, width), F32),
                   jax.ShapeDtypeStruct((bsz, seqlen, width), F32),
                   jax.ShapeDtypeStruct((bsz, ncol, width), F32),
                   jax.ShapeDtypeStruct(h0.shape, F32)],
        scratch_shapes=[pltpu.VMEM((3, ncol, width), F32), pltpu.VMEM((ncol, width), F32),
                        pltpu.VMEM((ncol, width), F32), pltpu.VMEM((ncol + HALO, width), F32)],
        compiler_params=_cparams("arbitrary", "arbitrary"),
    )(p3, p3, p3, p3, *consts, h0)


def _readout_ab_kernel(z_ref, sm_ref, ymf_ref, ymb_ref, yrf_ref, yrb_ref, bof_ref, bob_ref,
                       nw_ref, lw_ref, lb_ref, g2_ref, o_ref, *, n_heads):
    t = (ymf_ref[...] + ymb_ref[...]) * _silu(z_ref[...])
    gw = t.shape[1] // M_GROUPS
    a_out = []
    for g in range(M_GROUPS):
        tg = t[:, g * gw:(g + 1) * gw]
        a_out.append(tg * lax.rsqrt(jnp.mean(tg * tg, axis=-1, keepdims=True) + M_NORM_EPS))
    a_out = jnp.concatenate(a_out, axis=1) * nw_ref[...]

    yr = yrf_ref[...] + yrb_ref[...]
    n = R_HEADSIZE
    yn = []
    for h in range(n_heads):
        yh = yr[:, h * n:(h + 1) * n]
        dlt = yh - jnp.mean(yh, axis=-1, keepdims=True)
        yn.append(dlt * lax.rsqrt(jnp.mean(dlt * dlt, axis=-1, keepdims=True) + R_LN_EPS))
    yn = jnp.concatenate(yn, axis=1) * lw_ref[...] + lb_ref[...]
    gate = _bdot(jax.nn.sigmoid(sm_ref[:, 0:g2_ref.shape[0]]), g2_ref[...])
    b_out = (yn + bof_ref[...] + bob_ref[...]) * gate
    o_ref[...] = jnp.concatenate([a_out, b_out], axis=1).astype(o_ref.dtype)


def _readout_ab(pr, ym, yr, bo, ro, n_heads):
    bsz, seqlen, width = pr.shape
    m = bsz * seqlen
    mix = ym[0].shape[2]
    tm = _row_tile(m, 256)
    flat = lambda t: t.reshape(m, t.shape[2])
    rowblk = lambda w, col: pl.BlockSpec((tm, w), lambda i: (i, col))
    full = lambda a: pl.BlockSpec(a.shape, lambda i: (0,) * a.ndim)
    consts = [ro['norm_w'], ro['lnx_w'], ro['lnx_b'], ro['g2']]
    return pl.pallas_call(
        functools.partial(_readout_ab_kernel, n_heads=n_heads),
        grid=(m // tm,),
        in_specs=[rowblk(mix, 2), rowblk(512, width // 512 - 1)] + [rowblk(mix, 0)] * 6
                 + [full(a) for a in consts],
        out_specs=rowblk(2 * mix, 0),
        out_shape=jax.ShapeDtypeStruct((m, 2 * mix), BF16),
        compiler_params=_cparams("arbitrary"),
    )(flat(pr), flat(pr), flat(ym[0]), flat(ym[1]), flat(yr[0]), flat(yr[1]), flat(bo[0]),
      flat(bo[1]), *consts)


def _gelu_tanh(g):
    return 0.5 * g * (1.0 + jnp.tanh(math.sqrt(2.0 / math.pi) * (g + 0.044715 * (g * g * g))))


def _readout_c_kernel(gy_ref, hf_ref, hb_ref, o_ref):
    o_ref[...] = ((hf_ref[...] + hb_ref[...]) * _gelu_tanh(gy_ref[...])).astype(o_ref.dtype)


def _readout_grid_kernel(gy_ref, hf_ref, af_ref, cf_ref, hb_ref, ab_ref, cb_ref, o_ref, *, ncol):
    for j in range(o_ref.shape[0] // ncol):
        rs = slice(j * ncol, (j + 1) * ncol)
        h = (hf_ref[rs, :] + af_ref[rs, :] * cf_ref[...]) + (hb_ref[rs, :] + ab_ref[rs, :] * cb_ref[...])
        o_ref[rs, :] = (h * _gelu_tanh(gy_ref[rs, :])).astype(o_ref.dtype)


def _readout_grid(p3, fwd, bwd):
    bsz, seqlen, _ = p3.shape
    cw = fwd[0].shape[2]
    ncol = fwd[2].shape[1]
    tm = _row_tile(seqlen, 8 * ncol)
    spec = pl.BlockSpec((None, tm, cw), lambda b, i: (b, i, 0))
    cspec = pl.BlockSpec((None, ncol, cw), lambda b, i: (b, 0, 0))
    return pl.pallas_call(
        functools.partial(_readout_grid_kernel, ncol=ncol),
        grid=(bsz, seqlen // tm),
        in_specs=[spec, spec, spec, cspec, spec, spec, cspec],
        out_specs=spec,
        out_shape=jax.ShapeDtypeStruct((bsz, seqlen, cw), BF16),
        compiler_params=_cparams("arbitrary", "arbitrary"),
    )(p3, *fwd, *bwd)


def _readout_c(p, hf, hb):
    m, cw = hf.shape
    tm = _row_tile(m, 512)
    spec = pl.BlockSpec((tm, cw), lambda i: (i, 0))
    return pl.pallas_call(
        _readout_c_kernel,
        grid=(m // tm,),
        in_specs=[spec, spec, spec],
        out_specs=spec,
        out_shape=jax.ShapeDtypeStruct((m, cw), BF16),
        compiler_params=_cparams("arbitrary"),
    )(p, hf, hb)


def kernel(x, c, ctx, c_ctx, ada_w, ada_b, norm1_w, norm2_w, ffn_w_gate, ffn_w_up, ffn_w_down, final_norm_w, ab_w_in, ab_w_out, m_conv_w, m_conv_b, m_dt_bias, m_a_log, m_d, m_norm_w, r_mu, r_w0, r_w2, r_a0, r_a2, r_kk, r_ka, r_rk, r_g2, r_lnx_w, r_lnx_b, c_w_in, c_w_out, c_conv_w, c_conv_b, c_wa, c_ba, c_wx, c_bx, c_lambda):
    bsz, seqlen, d = x.shape
    ctx_len = ctx.shape[1]
    depth = ada_w.shape[0]
    n_lat, n_ctx = bsz * seqlen, bsz * ctx_len
    row = lambda t: t.reshape(1, -1)

    mod = _adaln(jnp.concatenate([c_ctx[None, :], c], axis=0), ada_w, ada_b)

    xl = x.reshape(n_lat, d)
    xc = ctx.reshape(n_ctx, d)
    mix_a = m_norm_w.shape[1]
    m_heads = m_dt_bias.shape[2]
    r_heads = r_rk.shape[2]
    lora = r_w2.shape[2]
    e64 = jnp.repeat(jnp.eye(m_heads, dtype=BF16), M_HEADDIM, axis=1)
    e128 = jnp.repeat(jnp.eye(m_heads, dtype=BF16), 128, axis=1)

    for layer in range(depth):
        with_ctx = layer < depth - 1
        part = lambda k: mod[layer, :, k * d:(k + 1) * d]
        lat_vec = lambda k: part(k)[1:1 + bsz, None, :]
        ctx_vec = lambda k: part(k)[0:1, None, :]
        nw1, nw2 = norm1_w[layer], norm2_w[layer]

        if layer % 2 == 0:
            e = layer // 2
            w = ab_w_in[e]
            o1 = mix_a
            o2 = o1 + mix_a + 2 * M_GROUPS * M_STATE
            o3 = o2 + m_heads
            o4 = o3 + 3 * r_heads * R_HEADSIZE + 2 * lora
            rkv_w = 3 * r_heads * R_HEADSIZE
            zeros = lambda n: jnp.zeros((d, n), w.dtype)
            w_in = jnp.concatenate(
                [w[:, o1:o2], w[:, :o1], w[:, o3:o3 + rkv_w], w[:, o4:],
                 w[:, o3 + rkv_w:o3 + rkv_w + lora], zeros(128 - lora),
                 w[:, o3 + rkv_w + lora:o4], w[:, o2:o3], zeros(128 - lora - m_heads)],
                axis=1).astype(BF16)
            width = w_in.shape[1]
            w_out = ab_w_out[e].astype(BF16)
            g2w = r_g2[e].astype(BF16)

            def project(xs, sh, sc, rows_per_vec, seg_len):
                pr = _norm_mm(xs, nw1, sh, sc, [w_in], rows_per_vec, F32)
                pr = pr.reshape(bsz, seg_len, width)
                dt_t = jnp.swapaxes(pr[:, :, width - 32:width - 32 + m_heads], 1, 2)
                return pr, dt_t

            pr_l, dtt_l = project(xl, lat_vec(0), lat_vec(1), seqlen, seqlen)
            pr_c, dtt_c = project(xc, ctx_vec(0), ctx_vec(1), n_ctx, ctx_len)

            ym_l = ym_c = yr_l = yr_c = bo_l = bo_c = None
            for dr in range(2):
                rev = dr == 1
                mp = dict(conv_w=m_conv_w[e, dr], conv_b=row(m_conv_b[e, dr]),
                          dtb_row=row(m_dt_bias[e, dr]), dtb_col=m_dt_bias[e, dr].reshape(-1, 1),
                          alog_row=row(m_a_log[e, dr]), alog_col=m_a_log[e, dr].reshape(-1, 1),
                          dskip=row(jnp.repeat(m_d[e, dr], M_HEADDIM)), e64=e64, e128=e128)
                h0 = jnp.zeros((bsz, m_heads, M_HEADDIM, M_STATE), F32)
                y_c, h_c = _ssd_direction(pr_c, dtt_c, h0, mp, rev)
                y_l, _ = _ssd_direction(pr_l, dtt_l, h_c, mp, rev)
                mu = r_mu[e, dr]
                mu_sm = (jnp.zeros((512,), F32).at[256:256 + lora].set(mu[rkv_w:rkv_w + lora])
                         .at[384:384 + lora].set(mu[rkv_w + lora:]))
                rp = dict(mu_rkv=row(mu[:rkv_w]), mu_sm=row(mu_sm), w0=row(r_w0[e, dr]),
                          w2=r_w2[e, dr].astype(BF16), a0=row(r_a0[e, dr]), a2=r_a2[e, dr].astype(BF16),
                          kk=row(r_kk[e, dr]), ka=row(r_ka[e, dr]), rk=row(r_rk[e, dr]))
                s0 = jnp.zeros((bsz, r_heads // R_GROUP, R_HEADSIZE, R_GROUP * R_HEADSIZE), F32)
                v_c, b_c, s_c = _rwkv_direction(pr_c, s0, rp, rev)
                v_l, b_l, _ = _rwkv_direction(pr_l, s_c, rp, rev)
                if dr == 0:
                    ym_l, ym_c, yr_l, yr_c, bo_l, bo_c = y_l, y_c, v_l, v_c, b_l, b_c
                else:
                    ym_l, yr_l, bo_l = (ym_l, y_l), (yr_l, v_l), (bo_l, b_l)
                    ym_c, yr_c, bo_c = (ym_c, y_c), (yr_c, v_c), (bo_c, b_c)

            ro = dict(norm_w=row(m_norm_w[e]), lnx_w=row(r_lnx_w[e]), lnx_b=row(r_lnx_b[e]), g2=g2w)
            act_l = _readout_ab(pr_l, ym_l, yr_l, bo_l, ro, r_heads)
            xl = _mm_res(act_l, w_out, xl, lat_vec(2), seqlen)
            if with_ctx:
                act_c = _readout_ab(pr_c, ym_c, yr_c, bo_c, ro, r_heads)
                xc = _mm_res(act_c, w_out, xc, ctx_vec(2), n_ctx)
        else:
            o = layer // 2
            w_in = c_w_in[o].astype(BF16)
            w_out = c_w_out[o].astype(BF16)
            cw = w_out.shape[0]
            p_l = _norm_mm(xl, nw1, lat_vec(0), lat_vec(1), [w_in], seqlen, F32)
            p_c = _norm_mm(xc, nw1, ctx_vec(0), ctx_vec(1), [w_in], n_ctx, F32)
            p3_l = p_l.reshape(bsz, seqlen, 2 * cw)
            hs_l, hs_c = [], []
            for dr in range(2):
                rev = dr == 1
                cp = dict(conv_w=c_conv_w[o, dr], conv_b=row(c_conv_b[o, dr]),
                          wa=c_wa[o, dr].astype(BF16), ba=row(c_ba[o, dr]),
                          wx=c_wx[o, dr].astype(BF16), bx=row(c_bx[o, dr]), lam=row(c_lambda[o, dr]))
                h0 = jnp.zeros((bsz, HALO, cw), F32)
                h_c, s_c = _rglru_direction(p_c.reshape(bsz, ctx_len, 2 * cw), h0, cp, rev,
                                            _row_tile(ctx_len, 128))
                hs_l.append(_rglru_grid_direction(p3_l, s_c, cp, rev)[:3])
                hs_c.append(h_c.reshape(n_ctx, cw))
            act_l = _readout_grid(p3_l, hs_l[0], hs_l[1]).reshape(n_lat, cw)
            xl = _mm_res(act_l, w_out, xl, lat_vec(2), seqlen)
            if with_ctx:
                act_c = _readout_c(p_c, hs_c[0], hs_c[1])
                xc = _mm_res(act_c, w_out, xc, ctx_vec(2), n_ctx)

        wg, wu, wd = (ffn_w_gate[layer].astype(BF16), ffn_w_up[layer].astype(BF16),
                      ffn_w_down[layer].astype(BF16))
        act = _norm_mm(xl, nw2, lat_vec(3), lat_vec(4), [wg, wu], seqlen, BF16)
        xl = _mm_res(act, wd, xl, lat_vec(5), seqlen)
        if with_ctx:
            act = _norm_mm(xc, nw2, ctx_vec(3), ctx_vec(4), [wg, wu], n_ctx, BF16)
            xc = _mm_res(act, wd, xc, ctx_vec(5), n_ctx)

    return _final_norm(xl, final_norm_w).reshape(bsz, seqlen, d)
```

```python
import functools
import math

import jax
import jax.numpy as jnp
from jax import lax
from jax.experimental import pallas as pl
from jax.experimental.pallas import tpu as pltpu

F32 = jnp.float32
BF16 = jnp.bfloat16

NORM_EPS = 1e-6
GRID_W = 64
M_HEADDIM = 64
M_GROUPS = 4
M_STATE = 128
M_CONV = 4
M_CHUNK = 128
M_NORM_EPS = 1e-5
R_HEADSIZE = 64
R_CHUNK = 64
R_GROUP = 4
R_SUB = 4
R_LN_EPS = 64e-5
C_HEADS = 8
C_CONV = 4
RG_C = 8.0

VMEM_LIMIT_BYTES = 56 * 1024 * 1024
HALO = 8


def _cparams(*sem):
    return pltpu.CompilerParams(dimension_semantics=sem, vmem_limit_bytes=VMEM_LIMIT_BYTES)


def _bdot(a, b):
    return jnp.dot(a.astype(BF16), b.astype(BF16), preferred_element_type=F32)


def _bdot_nt(a, b):
    return lax.dot_general(a.astype(BF16), b.astype(BF16), (((1,), (1,)), ((), ())),
                           preferred_element_type=F32)


def _split3(x):
    hi = x.astype(BF16)
    r1 = x - hi.astype(F32)
    mid = r1.astype(BF16)
    lo = (r1 - mid.astype(F32)).astype(BF16)
    return hi, mid, lo


def _tri_cumsum(tri_bf16, x, parts=3):
    hi, mid, lo = _split3(x)
    dot = lambda p: jnp.dot(tri_bf16, p, preferred_element_type=F32)
    if parts == 2:
        return dot(hi) + dot(mid)
    return dot(hi) + (dot(mid) + dot(lo))


def _silu(x):
    return x * jax.nn.sigmoid(x)


def _softplus(x):
    return jnp.maximum(x, 0.0) + jnp.log1p(jnp.exp(-jnp.abs(x)))


def _adaln_kernel(cb_ref, w_ref, b_ref, o_ref, *, n_vec, tn):
    rows = []
    for v in range(n_vec):
        cols = []
        for j in range(tn // 128):
            wj = w_ref[:, j * 128:(j + 1) * 128]
            cols.append(jnp.sum(wj * cb_ref[v], axis=0, keepdims=True))
        rows.append(jnp.concatenate(cols, axis=1))
    pad = jnp.zeros((HALO - n_vec, tn), F32)
    o_ref[...] = jnp.concatenate(rows + [pad], axis=0) + b_ref[...]


def _adaln(cond, ada_w, ada_b):
    depth, d, n6 = ada_w.shape
    n_vec = cond.shape[0]
    tn = 1024
    cb = jnp.broadcast_to(_silu(cond)[:, :, None], (n_vec, d, 128))
    return pl.pallas_call(
        functools.partial(_adaln_kernel, n_vec=n_vec, tn=tn),
        grid=(depth, n6 // tn),
        in_specs=[pl.BlockSpec((n_vec, d, 128), lambda l, j: (0, 0, 0)),
                  pl.BlockSpec((None, d, tn), lambda l, j: (l, 0, j)),
                  pl.BlockSpec((None, 1, tn), lambda l, j: (l, 0, j))],
        out_specs=pl.BlockSpec((None, HALO, tn), lambda l, j: (l, 0, j)),
        out_shape=jax.ShapeDtypeStruct((depth, HALO, n6), F32),
        compiler_params=_cparams("arbitrary", "arbitrary"),
    )(cb, ada_w, ada_b.reshape(depth, 1, n6))


def _modulated(x_ref, nw_ref, sh_ref, sc_ref):
    x = x_ref[...]
    y = x * lax.rsqrt(jnp.mean(x * x, axis=-1, keepdims=True) + NORM_EPS)
    return (y * nw_ref[...]) * (1.0 + sc_ref[...]) + sh_ref[...]


def _norm_mm_kernel(x_ref, nw_ref, sh_ref, sc_ref, w_ref, o_ref, h_ref):
    @pl.when(pl.program_id(1) == 0)
    def _():
        h_ref[...] = _modulated(x_ref, nw_ref, sh_ref, sc_ref).astype(BF16)

    o_ref[...] = jnp.dot(h_ref[...], w_ref[...], preferred_element_type=F32).astype(o_ref.dtype)


def _norm_swiglu_kernel(x_ref, nw_ref, sh_ref, sc_ref, wg_ref, wu_ref, o_ref, h_ref):
    @pl.when(pl.program_id(1) == 0)
    def _():
        h_ref[...] = _modulated(x_ref, nw_ref, sh_ref, sc_ref).astype(BF16)

    h = h_ref[...]
    g = jnp.dot(h, wg_ref[...], preferred_element_type=F32)
    u = jnp.dot(h, wu_ref[...], preferred_element_type=F32)
    o_ref[...] = (_silu(g) * u).astype(o_ref.dtype)


def _row_tile(m, cap):
    t = cap
    while m % t:
        t //= 2
    return t


def _norm_mm(x, nw, shift, scale, weights, rows_per_vec, out_dtype, tm_cap=1024, tn=512):
    m, d = x.shape
    n = weights[0].shape[1]
    tm = _row_tile(rows_per_vec, tm_cap)
    vec_map = lambda i, j: ((i * tm) // rows_per_vec, 0, 0)
    kern = _norm_mm_kernel if len(weights) == 1 else _norm_swiglu_kernel
    return pl.pallas_call(
        kern,
        grid=(m // tm, n // tn),
        in_specs=[pl.BlockSpec((tm, d), lambda i, j: (i, 0)),
                  pl.BlockSpec((1, d), lambda i, j: (0, 0)),
                  pl.BlockSpec((None, 1, d), vec_map),
                  pl.BlockSpec((None, 1, d), vec_map)]
                 + [pl.BlockSpec((d, tn), lambda i, j: (0, j)) for _ in weights],
        out_specs=pl.BlockSpec((tm, tn), lambda i, j: (i, j)),
        out_shape=jax.ShapeDtypeStruct((m, n), out_dtype),
        scratch_shapes=[pltpu.VMEM((tm, d), BF16)],
        compiler_params=_cparams("arbitrary", "arbitrary"),
    )(x, nw.reshape(1, d), shift, scale, *weights)


def _mm_res_kernel(a_ref, w_ref, res_ref, g_ref, o_ref):
    acc = jnp.dot(a_ref[...], w_ref[...], preferred_element_type=F32)
    o_ref[...] = res_ref[...] + g_ref[...] * acc


def _mm_res(a, w, res, gate, rows_per_vec, tm_cap=1024, tn=512):
    m, k = a.shape
    n = w.shape[1]
    tm = _row_tile(rows_per_vec, tm_cap)
    return pl.pallas_call(
        _mm_res_kernel,
        grid=(m // tm, n // tn),
        in_specs=[pl.BlockSpec((tm, k), lambda i, j: (i, 0)),
                  pl.BlockSpec((k, tn), lambda i, j: (0, j)),
                  pl.BlockSpec((tm, tn), lambda i, j: (i, j)),
                  pl.BlockSpec((None, 1, tn), lambda i, j: ((i * tm) // rows_per_vec, 0, j))],
        out_specs=pl.BlockSpec((tm, tn), lambda i, j: (i, j)),
        out_shape=jax.ShapeDtypeStruct((m, n), F32),
        compiler_params=_cparams("arbitrary", "arbitrary"),
    )(a, w, res, gate)


def _final_norm_kernel(x_ref, w_ref, o_ref):
    x = x_ref[...]
    o_ref[...] = x * lax.rsqrt(jnp.mean(x * x, axis=-1, keepdims=True) + NORM_EPS) * w_ref[...]


def _final_norm(x, w):
    m, d = x.shape
    tm = _row_tile(m, 512)
    return pl.pallas_call(
        _final_norm_kernel,
        grid=(m // tm,),
        in_specs=[pl.BlockSpec((tm, d), lambda i: (i, 0)), pl.BlockSpec((1, d), lambda i: (0, 0))],
        out_specs=pl.BlockSpec((tm, d), lambda i: (i, 0)),
        out_shape=jax.ShapeDtypeStruct((m, d), F32),
        compiler_params=_cparams("arbitrary"),
    )(x, w.reshape(1, d))


def _scan_block(i, n_blocks, rev):
    return (n_blocks - 1 - i) if rev else i


def _halo_block(i, n_blocks, rows, rev):
    per = rows // HALO
    if rev:
        return jnp.minimum((_scan_block(i, n_blocks, True) + 1) * per, n_blocks * per - 1)
    return jnp.maximum(i * per - 1, 0)


def _stage_with_halo(scr_ref, x_ref, halo_ref, keep, rows, rev):
    if rev:
        scr_ref[0:rows, :] = x_ref[...]
        scr_ref[rows:rows + HALO, :] = halo_ref[...] * keep
    else:
        scr_ref[0:HALO, :] = halo_ref[...] * keep
        scr_ref[HALO:HALO + rows, :] = x_ref[...]


def _lagged(scr_ref, lag, rows, rev):
    start = lag if rev else HALO - lag
    return scr_ref[start:start + rows, :]


def _rwkv_kernel(rkv_ref, sm_ref, rkv_h_ref, sm_h_ref, mu_rkv_ref, mu_sm_ref, w0_ref, w2_ref,
                 a0_ref, a2_ref, kkw_ref, kaw_ref, rkw_ref, s0_ref,
                 y_ref, bonus_ref, sl_ref, s_scr, rkv_scr, sm_scr, *, rev, n_heads, lora, nsub):
    i = pl.program_id(1)
    c = R_CHUNK
    rows = nsub * c
    n = R_HEADSIZE
    hw = n_heads * n

    @pl.when(i == 0)
    def _():
        s_scr[...] = s0_ref[...]

    keep = (i > 0).astype(F32)
    _stage_with_halo(rkv_scr, rkv_ref, rkv_h_ref, keep, rows, rev)
    _stage_with_halo(sm_scr, sm_ref, sm_h_ref, keep, rows, rev)
    x = rkv_ref[...]
    f = x + (_lagged(rkv_scr, 1, rows, rev) - x) * mu_rkv_ref[...]
    xs = sm_ref[...]
    fs = xs + (_lagged(sm_scr, 1, rows, rev) - xs) * mu_sm_ref[...]
    r = f[:, :hw]
    k = f[:, hw:2 * hw]
    v = f[:, 2 * hw:3 * hw]
    wl = fs[:, 256:256 + lora]
    al = fs[:, 384:384 + lora]

    ww = w0_ref[...] + _bdot(jnp.tanh(wl), w2_ref[...])
    lw = -math.exp(-0.5) * jax.nn.sigmoid(ww)
    a = jax.nn.sigmoid(a0_ref[...] + _bdot(al, a2_ref[...]))
    kkr = k * kkw_ref[...]
    k2 = k * (1.0 + (a - 1.0) * kaw_ref[...])
    rk = r * k2 * rkw_ref[...]

    cshift = int(math.log2(c))
    ti = lax.broadcasted_iota(jnp.int32, (rows, rows), 0)
    tj = lax.broadcasted_iota(jnp.int32, (rows, rows), 1)
    same_chunk = (ti >> cshift) == (tj >> cshift)
    upto = same_chunk & ((tj >= ti) if rev else (tj <= ti))
    g = _tri_cumsum(upto.astype(BF16), lw, parts=2)
    rsl = [slice(k * c, (k + 1) * c) for k in range(nsub)]
    gtot = [g[k * c:k * c + 1, :] if rev else g[(k + 1) * c - 1:(k + 1) * c, :] for k in range(nsub)]
    eg = jnp.exp(g)
    eneg = jnp.exp(-g)
    egp = jnp.exp(g - lw)
    etail = jnp.exp(jnp.concatenate([jnp.broadcast_to(t, (c, hw)) for t in gtot], axis=0) - g)
    dg = [jnp.exp(t) for t in gtot]
    rd_all = r * eg
    kp_all = k2 * eneg
    kpp_all = k2 * etail

    gw = R_GROUP * n
    groups = range(hw // gw)
    gsl = [slice(q * gw, (q + 1) * gw) for q in groups]
    shift = int(math.log2(n))
    bi = lax.broadcasted_iota(jnp.int32, (gw, gw), 0) >> shift
    bj = lax.broadcasted_iota(jnp.int32, (gw, gw), 1) >> shift
    same_head = bi == bj
    ones_bd = same_head.astype(BF16)
    lane_head = lax.broadcasted_iota(jnp.int32, (1, 128), 1) >> shift
    keep_lanes = [(lane_head == j).astype(BF16) for j in range(128 // n)]

    def bdiag(t):
        tb = t.astype(BF16)
        zero = jnp.zeros((c, 128), BF16)
        rows = []
        for h in range(R_GROUP):
            lt, j = divmod(h * n, 128)
            piece = tb[:, lt * 128:(lt + 1) * 128] * keep_lanes[j // n]
            rows.append(jnp.concatenate([piece if u == lt else zero for u in range(gw // 128)], axis=1))
        return jnp.concatenate(rows, axis=0)

    def diag_blocks(full):
        masked = jnp.where(same_head, full, 0.0)
        out = masked[0:n]
        for h in range(1, R_GROUP):
            out = out + masked[h * n:(h + 1) * n]
        return out

    def head_sums(t):
        hi, mid, _ = _split3(t)
        dot = lambda p: jnp.dot(p, ones_bd, preferred_element_type=F32)
        return dot(hi) + dot(mid)

    tdot = lambda p, q: lax.dot_general(p.astype(BF16), q.astype(BF16), (((0,), (0,)), ((), ())),
                                        preferred_element_type=F32)
    gi = lax.broadcasted_iota(jnp.int32, (c, gw), 0)
    gj = lax.broadcasted_iota(jnp.int32, (c, gw), 1) & (c - 1)
    before_g = (gj > gi) if rev else (gj < gi)
    upto_g = (gj >= gi) if rev else (gj <= gi)
    eye_g = (gi == gj).astype(F32)

    kk_all = jnp.concatenate(
        [kkr[:, s] * lax.rsqrt(jnp.maximum(head_sums(kkr[:, s] * kkr[:, s]), 1e-24)) for s in gsl], axis=1)
    bh_all = kk_all * a
    kkd_all = kk_all * egp
    bp_all = bh_all * eneg
    bpp_all = bh_all * etail
    bonus_ref[...] = jnp.concatenate([head_sums(rk[:, s]) for s in gsl], axis=1) * v

    units = [(k, q) for k in range(nsub) for q in groups]
    un = range(len(units))
    cut = lambda t: [t[rsl[k], gsl[q]] for k, q in units]
    kkd, bp, bpp, rd, vq, kp, kpp = (cut(t) for t in (kkd_all, bp_all, bpp_all, rd_all, v, kp_all, kpp_all))
    amat = [_bdot_nt(jnp.concatenate([kkd[u], rd[u]], axis=0),
                     jnp.concatenate([bdiag(kp[u]), bdiag(bp[u])], axis=0)) for u in un]
    mk = [jnp.where(before_g, m[:c, :gw], 0.0) for m in amat]
    nk = [jnp.where(upto_g, m[c:, :gw], 0.0) for m in amat]
    nb = [jnp.where(upto_g, m[c:, gw:], 0.0) for m in amat]
    p2 = [jnp.where(before_g, -m[:c, gw:], 0.0) for m in amat]
    tinv = [eye_g + t for t in p2]
    p2 = [_bdot(t, bdiag(t)) for t in p2]
    for _ in range(cshift - 2):
        sq = [_bdot(jnp.concatenate([tinv[u], p2[u]], axis=0), bdiag(p2[u])) for u in un]
        tinv = [tinv[u] + sq[u][:c] for u in un]
        p2 = [t[c:] for t in sq]
    tinv = [tinv[u] + _bdot(tinv[u], bdiag(p2[u])) for u in un]
    mnv = [_bdot(jnp.concatenate([mk[u], nk[u]], axis=0), bdiag(vq[u])) for u in un]
    wu = [_bdot(tinv[u], jnp.concatenate([bdiag(kkd[u]), bdiag(mnv[u][:c])], axis=1))
          for u in un]
    nwu = [_bdot(nb[u], jnp.concatenate([bdiag(wu[u][:, :gw]), bdiag(wu[u][:, gw:])], axis=1))
           for u in un]
    wub = [tdot(wu[u], bpp[u]) for u in un]
    pmat = [bdiag(diag_blocks(t[:gw])) for t in wub]
    dmat = [diag_blocks(tdot(vq[u], kpp[u])) - diag_blocks(wub[u][gw:]) for u in un]
    qt = [rd[u] - nwu[u][:, :gw] for u in un]
    y0 = [mnv[u][c:] - nwu[u][:, gw:] for u in un]
    state = [s_scr[q] for q in groups]
    for k in (reversed(range(nsub)) if rev else range(nsub)):
        for q in groups:
            u = k * len(groups) + q
            y_ref[rsl[k], gsl[q]] = y0[u] + _bdot_nt(qt[u], bdiag(state[q]))
            state[q] = state[q] * dg[k][:, gsl[q]] - _bdot(state[q], pmat[u]) + dmat[u]
    for q in groups:
        s_scr[q] = state[q]

    @pl.when(i == pl.num_programs(1) - 1)
    def _():
        sl_ref[...] = s_scr[...]


def _rwkv_direction(pr, s0, p, rev):
    bsz, seqlen, width = pr.shape
    hw = s0.shape[1] * s0.shape[3]
    n_heads = hw // R_HEADSIZE
    nsub = R_SUB if seqlen % (R_SUB * R_CHUNK) == 0 else 1
    c = nsub * R_CHUNK
    nblk = seqlen // c
    sm_col = width // 512 - 1
    blk = lambda i: _scan_block(i, nblk, rev)
    halo = lambda i: _halo_block(i, nblk, c, rev)
    vec = lambda w: pl.BlockSpec((1, w), lambda b, i: (0, 0))
    lora = p['w2'].shape[0]
    state_spec = pl.BlockSpec((None,) + s0.shape[1:], lambda b, i: (b, 0, 0, 0))
    return pl.pallas_call(
        functools.partial(_rwkv_kernel, rev=rev, n_heads=n_heads, lora=lora, nsub=nsub),
        grid=(bsz, nblk),
        in_specs=[pl.BlockSpec((None, c, 3 * hw), lambda b, i: (b, blk(i), 1)),
                  pl.BlockSpec((None, c, 512), lambda b, i: (b, blk(i), sm_col)),
                  pl.BlockSpec((None, HALO, 3 * hw), lambda b, i: (b, halo(i), 1)),
                  pl.BlockSpec((None, HALO, 512), lambda b, i: (b, halo(i), sm_col)),
                  vec(3 * hw), vec(512), vec(hw),
                  pl.BlockSpec((lora, hw), lambda b, i: (0, 0)),
                  vec(hw),
                  pl.BlockSpec((lora, hw), lambda b, i: (0, 0)),
                  vec(hw), vec(hw), vec(hw),
                  state_spec],
        out_specs=[pl.BlockSpec((None, c, hw), lambda b, i: (b, blk(i), 0)),
                   pl.BlockSpec((None, c, hw), lambda b, i: (b, blk(i), 0)),
                   state_spec],
        out_shape=[jax.ShapeDtypeStruct((bsz, seqlen, hw), F32),
                   jax.ShapeDtypeStruct((bsz, seqlen, hw), F32),
                   jax.ShapeDtypeStruct(s0.shape, F32)],
        scratch_shapes=[pltpu.VMEM(s0.shape[1:], F32),
                        pltpu.VMEM((c + HALO, 3 * hw), F32),
                        pltpu.VMEM((c + HALO, 512), F32)],
        compiler_params=_cparams("arbitrary", "arbitrary"),
    )(pr, pr, pr, pr, p['mu_rkv'], p['mu_sm'], p['w0'], p['w2'], p['a0'], p['a2'],
      p['kk'], p['ka'], p['rk'], s0)


def _expand_cols(x, e_bf16):
    hi, mid, lo = _split3(x)
    dot = lambda p: jnp.dot(p, e_bf16, preferred_element_type=F32)
    return dot(hi) + (dot(mid) + dot(lo))


def _causal_conv(scr_ref, w_ref, b_ref, rows, taps, rev):
    acc = b_ref[...] + w_ref[taps - 1:taps, :] * _lagged(scr_ref, 0, rows, rev)
    for kk in range(taps - 1):
        acc = acc + w_ref[kk:kk + 1, :] * _lagged(scr_ref, taps - 1 - kk, rows, rev)
    return acc


def _ssd_kernel(xbc_ref, halo_ref, sm_ref, dtt_ref, cw_ref, cb_ref, dtb_row_ref, dtb_col_ref,
                alog_row_ref, alog_col_ref, dskip_ref, e64_ref, e128_ref, h0_ref,
                y_ref, hl_ref, h_scr, x_scr, *, rev, n_heads):
    i = pl.program_id(1)
    c = M_CHUNK
    p = M_HEADDIM
    ns = M_STATE
    mix = n_heads * p
    hpg = n_heads // M_GROUPS

    @pl.when(i == 0)
    def _():
        h_scr[...] = h0_ref[...]

    keep = (i > 0).astype(F32)
    _stage_with_halo(x_scr, xbc_ref, halo_ref, keep, c, rev)
    xbc = _silu(_causal_conv(x_scr, cw_ref, cb_ref, c, M_CONV, rev))
    xs = xbc[:, :mix]
    bm = [xbc[:, mix + g * ns:mix + (g + 1) * ns] for g in range(M_GROUPS)]
    cm = [xbc[:, mix + (M_GROUPS + g) * ns:mix + (M_GROUPS + g + 1) * ns] for g in range(M_GROUPS)]

    ti = lax.broadcasted_iota(jnp.int32, (c, c), 0)
    tj = lax.broadcasted_iota(jnp.int32, (c, c), 1)
    upto = (tj >= ti) if rev else (tj <= ti)
    upto_t = (ti >= tj) if rev else (ti <= tj)
    end = 0 if rev else c - 1

    dt = _softplus(sm_ref[:, 480:480 + n_heads] + dtb_row_ref[...])
    a_cs = _tri_cumsum(upto.astype(BF16), dt * -jnp.exp(alog_row_ref[...]))
    dt_t = _softplus(dtt_ref[...] + dtb_col_ref[...])
    a_cs_t = _expand_cols(dt_t * -jnp.exp(alog_col_ref[...]), upto_t.astype(BF16))
    dte = jnp.exp(a_cs[end:end + 1, :] - a_cs)
    e64 = e64_ref[...]
    xdt = xs * _expand_cols(dt, e64)
    xdd_t = jnp.transpose(xs * _expand_cols(dt * dte, e64))
    ea_full = _expand_cols(jnp.exp(a_cs), e64)
    a_cs_b = _expand_cols(a_cs, e128_ref[...])

    cb = [_bdot_nt(cm[g], bm[g]) for g in range(M_GROUPS)]
    heads = range(n_heads)
    h_in = [h_scr[h] for h in heads]
    ydiag = []
    yoff = []
    for h in heads:
        g = h // hpg
        a_col = a_cs_b[:, h * 128:(h + 1) * 128]
        decay = jnp.where(upto, jnp.exp(jnp.minimum(a_col - a_cs_t[h:h + 1, :], 0.0)), 0.0)
        ydiag.append(_bdot(cb[g] * decay, xdt[:, h * p:(h + 1) * p]))
        yoff.append(_bdot_nt(cm[g], h_in[h]))
        h_scr[h] = (h_in[h] * jnp.exp(a_col[end:end + 1, :])
                    + _bdot(xdd_t[h * p:(h + 1) * p, :], bm[g]))
    y_ref[...] = (jnp.concatenate(ydiag, axis=1) + jnp.concatenate(yoff, axis=1) * ea_full
                  + xs * dskip_ref[...])

    @pl.when(i == pl.num_programs(1) - 1)
    def _():
        hl_ref[...] = h_scr[...]


def _ssd_direction(pr, dt_t, h0, p, rev):
    bsz, seqlen, width = pr.shape
    n_heads = h0.shape[1]
    mix = n_heads * M_HEADDIM
    xw = mix + 2 * M_GROUPS * M_STATE
    c = M_CHUNK
    nblk = seqlen // c
    sm_col = width // 512 - 1
    blk = lambda i: _scan_block(i, nblk, rev)
    halo = lambda i: _halo_block(i, nblk, c, rev)
    full = lambda a: pl.BlockSpec(a.shape, lambda b, i: (0,) * a.ndim)
    consts = [p['conv_w'], p['conv_b'], p['dtb_row'], p['dtb_col'], p['alog_row'], p['alog_col'],
              p['dskip'], p['e64'], p['e128']]
    state_spec = pl.BlockSpec((None,) + h0.shape[1:], lambda b, i: (b, 0, 0, 0))
    return pl.pallas_call(
        functools.partial(_ssd_kernel, rev=rev, n_heads=n_heads),
        grid=(bsz, nblk),
        in_specs=[pl.BlockSpec((None, c, xw), lambda b, i: (b, blk(i), 0)),
                  pl.BlockSpec((None, HALO, xw), lambda b, i: (b, halo(i), 0)),
                  pl.BlockSpec((None, c, 512), lambda b, i: (b, blk(i), sm_col)),
                  pl.BlockSpec((None, n_heads, c), lambda b, i: (b, 0, blk(i)))]
                 + [full(a) for a in consts] + [state_spec],
        out_specs=[pl.BlockSpec((None, c, mix), lambda b, i: (b, blk(i), 0)), state_spec],
        out_shape=[jax.ShapeDtypeStruct((bsz, seqlen, mix), F32),
                   jax.ShapeDtypeStruct(h0.shape, F32)],
        scratch_shapes=[pltpu.VMEM(h0.shape[1:], F32), pltpu.VMEM((c + HALO, xw), F32)],
        compiler_params=_cparams("arbitrary", "arbitrary"),
    )(pr, pr, pr, dt_t, *consts, h0)


def _rglru_kernel(x_ref, halo_ref, cw_ref, cb_ref, wa_ref, ba_ref, wx_ref, bx_ref, lam_ref, h0_ref,
                  o_ref, hl_ref, h_scr, x_scr, a_scr, u_scr, *, rev, tb):
    i = pl.program_id(1)

    @pl.when(i == 0)
    def _():
        h_scr[...] = h0_ref[...]

    keep = (i > 0).astype(F32)
    _stage_with_halo(x_scr, x_ref, halo_ref, keep, tb, rev)
    xc = _causal_conv(x_scr, cw_ref, cb_ref, tb, C_CONV, rev)
    a_scr[...], u_scr[...] = _rglru_gates(xc, wa_ref, ba_ref, wx_ref, bx_ref, lam_ref)

    def step(t, h):
        tt = (tb - 1 - t) if rev else t
        h = a_scr[pl.ds(tt, 1), :] * h + u_scr[pl.ds(tt, 1), :]
        o_ref[pl.ds(tt, 1), :] = h
        return h

    h_scr[0:1, :] = lax.fori_loop(0, tb, step, h_scr[0:1, :], unroll=8)

    @pl.when(i == pl.num_programs(1) - 1)
    def _():
        hl_ref[...] = h_scr[...]


def _rglru_direction(xv, h0, p, rev, tb):
    bsz, rows, wtot = xv.shape
    width = h0.shape[2]
    ncol = wtot // (2 * width)
    nrb = rows // tb
    total = ncol * nrb
    per = tb // HALO

    def pos(i):
        s = _scan_block(i, total, rev)
        return s // nrb, s % nrb

    def x_map(b, i):
        col, q = pos(i)
        return b, q, 2 * col + 1

    def halo_map(b, i):
        s = _scan_block(i, total, rev)
        sp = jnp.minimum(s + 1, total - 1) if rev else jnp.maximum(s - 1, 0)
        col, q = sp // nrb, sp % nrb
        return b, (q * per if rev else (q + 1) * per - 1), 2 * col + 1

    def o_map(b, i):
        col, q = pos(i)
        return b, q, col

    full = lambda a: pl.BlockSpec(a.shape, lambda b, i: (0,) * a.ndim)
    consts = [p['conv_w'], p['conv_b'], p['wa'], p['ba'], p['wx'], p['bx'], p['lam']]
    state_spec = pl.BlockSpec((None, HALO, width), lambda b, i: (b, 0, 0))
    return pl.pallas_call(
        functools.partial(_rglru_kernel, rev=rev, tb=tb),
        grid=(bsz, total),
        in_specs=[pl.BlockSpec((None, tb, width), x_map), pl.BlockSpec((None, HALO, width), halo_map)]
                 + [full(a) for a in consts] + [state_spec],
        out_specs=[pl.BlockSpec((None, tb, width), o_map), state_spec],
        out_shape=[jax.ShapeDtypeStruct((bsz, rows, ncol * width), F32),
                   jax.ShapeDtypeStruct(h0.shape, F32)],
        scratch_shapes=[pltpu.VMEM((HALO, width), F32), pltpu.VMEM((tb + HALO, width), F32),
                        pltpu.VMEM((tb, width), F32), pltpu.VMEM((tb, width), F32)],
        compiler_params=_cparams("arbitrary", "arbitrary"),
    )(xv, xv, *consts, h0)


def _rglru_gates(xc, wa_ref, ba_ref, wx_ref, bx_ref, lam_ref):
    blk = xc.shape[1] // C_HEADS
    xh = [xc[:, h * blk:(h + 1) * blk] for h in range(C_HEADS)]
    gate_r = jax.nn.sigmoid(
        jnp.concatenate([_bdot(xh[h], wa_ref[h]) for h in range(C_HEADS)], axis=1) + ba_ref[...])
    gate_i = jax.nn.sigmoid(
        jnp.concatenate([_bdot(xh[h], wx_ref[h]) for h in range(C_HEADS)], axis=1) + bx_ref[...])
    log_a = -RG_C * gate_r * _softplus(-lam_ref[...])
    a = jnp.exp(log_a)
    one_m_a2 = jnp.maximum(-jnp.tanh(log_a) * (a * a + 1.0), 0.0)
    mult = jnp.where(one_m_a2 > 0.0, one_m_a2 * lax.rsqrt(one_m_a2), 0.0)
    return a, mult * gate_i * xc


def _rglru_grid_kernel(x_ref, e0_ref, e1_ref, e2_ref, cw_ref, cb_ref, wa_ref, ba_ref, wx_ref,
                       bx_ref, lam_ref, h0_ref, hloc_ref, aprod_ref, cin_ref, hl_ref,
                       ring_scr, h_scr, ap_scr, stage_scr, *, rev, ncol):
    i = pl.program_id(1)

    def from_previous_column(e_ref):
        _stage_with_halo(stage_scr, e_ref, jnp.zeros((HALO, e_ref.shape[1]), F32), 0.0, ncol, rev)
        return _lagged(stage_scr, 1, ncol, rev)

    @pl.when(i == 0)
    def _():
        ring_scr[0] = from_previous_column(e2_ref)
        ring_scr[1] = from_previous_column(e1_ref)
        ring_scr[2] = from_previous_column(e0_ref)
        h_scr[...] = jnp.zeros(h_scr.shape, F32)
        ap_scr[...] = jnp.ones(ap_scr.shape, F32)

    x0 = x_ref[...]
    lag1, lag2, lag3 = ring_scr[0], ring_scr[1], ring_scr[2]
    xc = (cb_ref[...] + cw_ref[3:4, :] * x0 + cw_ref[2:3, :] * lag1
          + cw_ref[1:2, :] * lag2 + cw_ref[0:1, :] * lag3)
    ring_scr[2] = lag2
    ring_scr[1] = lag1
    ring_scr[0] = x0
    a, u = _rglru_gates(xc, wa_ref, ba_ref, wx_ref, bx_ref, lam_ref)
    h = a * h_scr[...] + u
    ap = a * ap_scr[...]
    h_scr[...] = h
    ap_scr[...] = ap
    hloc_ref[...] = h
    aprod_ref[...] = ap

    @pl.when(i == pl.num_programs(1) - 1)
    def _():
        def step(t, carry):
            col = (ncol - 1 - t) if rev else t
            cin_ref[pl.ds(col, 1), :] = carry
            return h_scr[pl.ds(col, 1), :] + ap_scr[pl.ds(col, 1), :] * carry

        last = lax.fori_loop(0, ncol, step, h0_ref[0:1, :], unroll=8)
        hl_ref[...] = jnp.zeros(hl_ref.shape, F32)
        hl_ref[0:1, :] = last


def _rglru_grid_direction(p3, h0, p, rev):
    bsz, seqlen, _ = p3.shape
    width = h0.shape[2]
    ncol = GRID_W
    rows = seqlen // ncol
    blk = lambda i: _scan_block(i, rows, rev)
    edge = lambda k: pl.BlockSpec((None, ncol, width),
                                  lambda b, i: (b, (2 - k) if rev else (rows - 3 + k), 1))
    full = lambda a: pl.BlockSpec(a.shape, lambda b, i: (0,) * a.ndim)
    consts = [p['conv_w'], p['conv_b'], p['wa'], p['ba'], p['wx'], p['bx'], p['lam']]
    state_spec = pl.BlockSpec((None, HALO, width), lambda b, i: (b, 0, 0))
    row_spec = pl.BlockSpec((None, ncol, width), lambda b, i: (b, blk(i), 0))
    return pl.pallas_call(
        functools.partial(_rglru_grid_kernel, rev=rev, ncol=ncol),
        grid=(bsz, rows),
        in_specs=[pl.BlockSpec((None, ncol, width), lambda b, i: (b, blk(i), 1)),
                  edge(0), edge(1), edge(2)] + [full(a) for a in consts] + [state_spec],
        out_specs=[row_spec, row_spec,
                   pl.BlockSpec((None, ncol, width), lambda b, i: (b, 0, 0)), state_spec],
        out_shape=[jax.ShapeDtypeStruct((bsz, seqlen, width), F32),
                   jax.ShapeDtypeStruct((bsz, seqlen, width), F32),
                   jax.ShapeDtypeStruct((bsz, ncol, width), F32),
                   jax.ShapeDtypeStruct(h0.shape, F32)],
        scratch_shapes=[pltpu.VMEM((3, ncol, width), F32), pltpu.VMEM((ncol, width), F32),
                        pltpu.VMEM((ncol, width), F32), pltpu.VMEM((ncol + HALO, width), F32)],
        compiler_params=_cparams("arbitrary", "arbitrary"),
    )(p3, p3, p3, p3, *consts, h0)


def _readout_ab_kernel(z_ref, sm_ref, ymf_ref, ymb_ref, yrf_ref, yrb_ref, bof_ref, bob_ref,
                       nw_ref, lw_ref, lb_ref, g2_ref, o_ref, *, n_heads):
    t = (ymf_ref[...] + ymb_ref[...]) * _silu(z_ref[...])
    gw = t.shape[1] // M_GROUPS
    a_out = []
    for g in range(M_GROUPS):
        tg = t[:, g * gw:(g + 1) * gw]
        a_out.append(tg * lax.rsqrt(jnp.mean(tg * tg, axis=-1, keepdims=True) + M_NORM_EPS))
    a_out = jnp.concatenate(a_out, axis=1) * nw_ref[...]

    yr = yrf_ref[...] + yrb_ref[...]
    n = R_HEADSIZE
    yn = []
    for h in range(n_heads):
        yh = yr[:, h * n:(h + 1) * n]
        dlt = yh - jnp.mean(yh, axis=-1, keepdims=True)
        yn.append(dlt * lax.rsqrt(jnp.mean(dlt * dlt, axis=-1, keepdims=True) + R_LN_EPS))
    yn = jnp.concatenate(yn, axis=1) * lw_ref[...] + lb_ref[...]
    gate = _bdot(jax.nn.sigmoid(sm_ref[:, 0:g2_ref.shape[0]]), g2_ref[...])
    b_out = (yn + bof_ref[...] + bob_ref[...]) * gate
    o_ref[...] = jnp.concatenate([a_out, b_out], axis=1).astype(o_ref.dtype)


def _readout_ab(pr, ym, yr, bo, ro, n_heads):
    bsz, seqlen, width = pr.shape
    m = bsz * seqlen
    mix = ym[0].shape[2]
    tm = _row_tile(m, 256)
    flat = lambda t: t.reshape(m, t.shape[2])
    rowblk = lambda w, col: pl.BlockSpec((tm, w), lambda i: (i, col))
    full = lambda a: pl.BlockSpec(a.shape, lambda i: (0,) * a.ndim)
    consts = [ro['norm_w'], ro['lnx_w'], ro['lnx_b'], ro['g2']]
    return pl.pallas_call(
        functools.partial(_readout_ab_kernel, n_heads=n_heads),
        grid=(m // tm,),
        in_specs=[rowblk(mix, 2), rowblk(512, width // 512 - 1)] + [rowblk(mix, 0)] * 6
                 + [full(a) for a in consts],
        out_specs=rowblk(2 * mix, 0),
        out_shape=jax.ShapeDtypeStruct((m, 2 * mix), BF16),
        compiler_params=_cparams("arbitrary"),
    )(flat(pr), flat(pr), flat(ym[0]), flat(ym[1]), flat(yr[0]), flat(yr[1]), flat(bo[0]),
      flat(bo[1]), *consts)


def _gelu_tanh(g):
    return 0.5 * g * (1.0 + jnp.tanh(math.sqrt(2.0 / math.pi) * (g + 0.044715 * (g * g * g))))


def _readout_c_kernel(gy_ref, hf_ref, hb_ref, o_ref):
    o_ref[...] = ((hf_ref[...] + hb_ref[...]) * _gelu_tanh(gy_ref[...])).astype(o_ref.dtype)


def _readout_grid_kernel(gy_ref, hf_ref, af_ref, cf_ref, hb_ref, ab_ref, cb_ref, o_ref, *, ncol):
    for j in range(o_ref.shape[0] // ncol):
        rs = slice(j * ncol, (j + 1) * ncol)
        h = (hf_ref[rs, :] + af_ref[rs, :] * cf_ref[...]) + (hb_ref[rs, :] + ab_ref[rs, :] * cb_ref[...])
        o_ref[rs, :] = (h * _gelu_tanh(gy_ref[rs, :])).astype(o_ref.dtype)


def _readout_grid(p3, fwd, bwd):
    bsz, seqlen, _ = p3.shape
    cw = fwd[0].shape[2]
    ncol = fwd[2].shape[1]
    tm = _row_tile(seqlen, 8 * ncol)
    spec = pl.BlockSpec((None, tm, cw), lambda b, i: (b, i, 0))
    cspec = pl.BlockSpec((None, ncol, cw), lambda b, i: (b, 0, 0))
    return pl.pallas_call(
        functools.partial(_readout_grid_kernel, ncol=ncol),
        grid=(bsz, seqlen // tm),
        in_specs=[spec, spec, spec, cspec, spec, spec, cspec],
        out_specs=spec,
        out_shape=jax.ShapeDtypeStruct((bsz, seqlen, cw), BF16),
        compiler_params=_cparams("arbitrary", "arbitrary"),
    )(p3, *fwd, *bwd)


def _readout_c(p, hf, hb):
    m, cw = hf.shape
    tm = _row_tile(m, 512)
    spec = pl.BlockSpec((tm, cw), lambda i: (i, 0))
    return pl.pallas_call(
        _readout_c_kernel,
        grid=(m // tm,),
        in_specs=[spec, spec, spec],
        out_specs=spec,
        out_shape=jax.ShapeDtypeStruct((m, cw), BF16),
        compiler_params=_cparams("arbitrary"),
    )(p, hf, hb)


def kernel(x, c, ctx, c_ctx, ada_w, ada_b, norm1_w, norm2_w, ffn_w_gate, ffn_w_up, ffn_w_down, final_norm_w, ab_w_in, ab_w_out, m_conv_w, m_conv_b, m_dt_bias, m_a_log, m_d, m_norm_w, r_mu, r_w0, r_w2, r_a0, r_a2, r_kk, r_ka, r_rk, r_g2, r_lnx_w, r_lnx_b, c_w_in, c_w_out, c_conv_w, c_conv_b, c_wa, c_ba, c_wx, c_bx, c_lambda):
    bsz, seqlen, d = x.shape
    ctx_len = ctx.shape[1]
    depth = ada_w.shape[0]
    n_lat, n_ctx = bsz * seqlen, bsz * ctx_len
    row = lambda t: t.reshape(1, -1)

    mod = _adaln(jnp.concatenate([c_ctx[None, :], c], axis=0), ada_w, ada_b)

    xl = x.reshape(n_lat, d)
    xc = ctx.reshape(n_ctx, d)
    mix_a = m_norm_w.shape[1]
    m_heads = m_dt_bias.shape[2]
    r_heads = r_rk.shape[2]
    lora = r_w2.shape[2]
    e64 = jnp.repeat(jnp.eye(m_heads, dtype=BF16), M_HEADDIM, axis=1)
    e128 = jnp.repeat(jnp.eye(m_heads, dtype=BF16), 128, axis=1)

    for layer in range(depth):
        with_ctx = layer < depth - 1
        part = lambda k: mod[layer, :, k * d:(k + 1) * d]
        lat_vec = lambda k: part(k)[1:1 + bsz, None, :]
        ctx_vec = lambda k: part(k)[0:1, None, :]
        nw1, nw2 = norm1_w[layer], norm2_w[layer]

        if layer % 2 == 0:
            e = layer // 2
            w = ab_w_in[e]
            o1 = mix_a
            o2 = o1 + mix_a + 2 * M_GROUPS * M_STATE
            o3 = o2 + m_heads
            o4 = o3 + 3 * r_heads * R_HEADSIZE + 2 * lora
            rkv_w = 3 * r_heads * R_HEADSIZE
            zeros = lambda n: jnp.zeros((d, n), w.dtype)
            w_in = jnp.concatenate(
                [w[:, o1:o2], w[:, :o1], w[:, o3:o3 + rkv_w], w[:, o4:],
                 w[:, o3 + rkv_w:o3 + rkv_w + lora], zeros(128 - lora),
                 w[:, o3 + rkv_w + lora:o4], w[:, o2:o3], zeros(128 - lora - m_heads)],
                axis=1).astype(BF16)
            width = w_in.shape[1]
            w_out = ab_w_out[e].astype(BF16)
            g2w = r_g2[e].astype(BF16)

            def project(xs, sh, sc, rows_per_vec, seg_len):
                pr = _norm_mm(xs, nw1, sh, sc, [w_in], rows_per_vec, F32)
                pr = pr.reshape(bsz, seg_len, width)
                dt_t = jnp.swapaxes(pr[:, :, width - 32:width - 32 + m_heads], 1, 2)
                return pr, dt_t

            pr_l, dtt_l = project(xl, lat_vec(0), lat_vec(1), seqlen, seqlen)
            pr_c, dtt_c = project(xc, ctx_vec(0), ctx_vec(1), n_ctx, ctx_len)

            ym_l = ym_c = yr_l = yr_c = bo_l = bo_c = None
            for dr in range(2):
                rev = dr == 1
                mp = dict(conv_w=m_conv_w[e, dr], conv_b=row(m_conv_b[e, dr]),
                          dtb_row=row(m_dt_bias[e, dr]), dtb_col=m_dt_bias[e, dr].reshape(-1, 1),
                          alog_row=row(m_a_log[e, dr]), alog_col=m_a_log[e, dr].reshape(-1, 1),
                          dskip=row(jnp.repeat(m_d[e, dr], M_HEADDIM)), e64=e64, e128=e128)
                h0 = jnp.zeros((bsz, m_heads, M_HEADDIM, M_STATE), F32)
                y_c, h_c = _ssd_direction(pr_c, dtt_c, h0, mp, rev)
                y_l, _ = _ssd_direction(pr_l, dtt_l, h_c, mp, rev)
                mu = r_mu[e, dr]
                mu_sm = (jnp.zeros((512,), F32).at[256:256 + lora].set(mu[rkv_w:rkv_w + lora])
                         .at[384:384 + lora].set(mu[rkv_w + lora:]))
                rp = dict(mu_rkv=row(mu[:rkv_w]), mu_sm=row(mu_sm), w0=row(r_w0[e, dr]),
                          w2=r_w2[e, dr].astype(BF16), a0=row(r_a0[e, dr]), a2=r_a2[e, dr].astype(BF16),
                          kk=row(r_kk[e, dr]), ka=row(r_ka[e, dr]), rk=row(r_rk[e, dr]))
                s0 = jnp.zeros((bsz, r_heads // R_GROUP, R_HEADSIZE, R_GROUP * R_HEADSIZE), F32)
                v_c, b_c, s_c = _rwkv_direction(pr_c, s0, rp, rev)
                v_l, b_l, _ = _rwkv_direction(pr_l, s_c, rp, rev)
                if dr == 0:
                    ym_l, ym_c, yr_l, yr_c, bo_l, bo_c = y_l, y_c, v_l, v_c, b_l, b_c
                else:
                    ym_l, yr_l, bo_l = (ym_l, y_l), (yr_l, v_l), (bo_l, b_l)
                    ym_c, yr_c, bo_c = (ym_c, y_c), (yr_c, v_c), (bo_c, b_c)

            ro = dict(norm_w=row(m_norm_w[e]), lnx_w=row(r_lnx_w[e]), lnx_b=row(r_lnx_b[e]), g2=g2w)
            act_l = _readout_ab(pr_l, ym_l, yr_l, bo_l, ro, r_heads)
            xl = _mm_res(act_l, w_out, xl, lat_vec(2), seqlen)
            if with_ctx:
                act_c = _readout_ab(pr_c, ym_c, yr_c, bo_c, ro, r_heads)
                xc = _mm_res(act_c, w_out, xc, ctx_vec(2), n_ctx)
        else:
            o = layer // 2
            w_in = c_w_in[o].astype(BF16)
            w_out = c_w_out[o].astype(BF16)
            cw = w_out.shape[0]
            p_l = _norm_mm(xl, nw1, lat_vec(0), lat_vec(1), [w_in], seqlen, F32)
            p_c = _norm_mm(xc, nw1, ctx_vec(0), ctx_vec(1), [w_in], n_ctx, F32)
            p3_l = p_l.reshape(bsz, seqlen, 2 * cw)
            hs_l, hs_c = [], []
            for dr in range(2):
                rev = dr == 1
                cp = dict(conv_w=c_conv_w[o, dr], conv_b=row(c_conv_b[o, dr]),
                          wa=c_wa[o, dr].astype(BF16), ba=row(c_ba[o, dr]),
                          wx=c_wx[o, dr].astype(BF16), bx=row(c_bx[o, dr]), lam=row(c_lambda[o, dr]))
                h0 = jnp.zeros((bsz, HALO, cw), F32)
                h_c, s_c = _rglru_direction(p_c.reshape(bsz, ctx_len, 2 * cw), h0, cp, rev,
                                            _row_tile(ctx_len, 128))
                hs_l.append(_rglru_grid_direction(p3_l, s_c, cp, rev)[:3])
                hs_c.append(h_c.reshape(n_ctx, cw))
            act_l = _readout_grid(p3_l, hs_l[0], hs_l[1]).reshape(n_lat, cw)
            xl = _mm_res(act_l, w_out, xl, lat_vec(2), seqlen)
            if with_ctx:
                act_c = _readout_c(p_c, hs_c[0], hs_c[1])
                xc = _mm_res(act_c, w_out, xc, ctx_vec(2), n_ctx)

        wg, wu, wd = (ffn_w_gate[layer].astype(BF16), ffn_w_up[layer].astype(BF16),
                      ffn_w_down[layer].astype(BF16))
        act = _norm_mm(xl, nw2, lat_vec(3), lat_vec(4), [wg, wu], seqlen, BF16)
        xl = _mm_res(act, wd, xl, lat_vec(5), seqlen)
        if with_ctx:
            act = _norm_mm(xc, nw2, ctx_vec(3), ctx_vec(4), [wg, wu], n_ctx, BF16)
            xc = _mm_res(act, wd, xc, ctx_vec(5), n_ctx)

    return _final_norm(xl, final_norm_w).reshape(bsz, seqlen, d)
```

```python
import functools
import math

import jax
import jax.numpy as jnp
from jax import lax
from jax.experimental import pallas as pl
from jax.experimental.pallas import tpu as pltpu

F32 = jnp.float32
BF16 = jnp.bfloat16

NORM_EPS = 1e-6
GRID_W = 64
M_HEADDIM = 64
M_GROUPS = 4
M_STATE = 128
M_CONV = 4
M_CHUNK = 128
M_NORM_EPS = 1e-5
R_HEADSIZE = 64
R_CHUNK = 64
R_GROUP = 4
R_SUB = 4
R_LN_EPS = 64e-5
C_HEADS = 8
C_CONV = 4
RG_C = 8.0

VMEM_LIMIT_BYTES = 56 * 1024 * 1024
HALO = 8


def _cparams(*sem):
    return pltpu.CompilerParams(dimension_semantics=sem, vmem_limit_bytes=VMEM_LIMIT_BYTES)


def _bdot(a, b):
    return jnp.dot(a.astype(BF16), b.astype(BF16), preferred_element_type=F32)


def _bdot_nt(a, b):
    return lax.dot_general(a.astype(BF16), b.astype(BF16), (((1,), (1,)), ((), ())),
                           preferred_element_type=F32)


def _split3(x):
    hi = x.astype(BF16)
    r1 = x - hi.astype(F32)
    mid = r1.astype(BF16)
    lo = (r1 - mid.astype(F32)).astype(BF16)
    return hi, mid, lo


def _tri_cumsum(tri_bf16, x, parts=3):
    hi, mid, lo = _split3(x)
    dot = lambda p: jnp.dot(tri_bf16, p, preferred_element_type=F32)
    if parts == 2:
        return dot(hi) + dot(mid)
    return dot(hi) + (dot(mid) + dot(lo))


def _silu(x):
    return x * jax.nn.sigmoid(x)


def _softplus(x):
    return jnp.maximum(x, 0.0) + jnp.log1p(jnp.exp(-jnp.abs(x)))


def _adaln_kernel(cb_ref, w_ref, b_ref, o_ref, *, n_vec, tn):
    rows = []
    for v in range(n_vec):
        cols = []
        for j in range(tn // 128):
            wj = w_ref[:, j * 128:(j + 1) * 128]
            cols.append(jnp.sum(wj * cb_ref[v], axis=0, keepdims=True))
        rows.append(jnp.concatenate(cols, axis=1))
    pad = jnp.zeros((HALO - n_vec, tn), F32)
    o_ref[...] = jnp.concatenate(rows + [pad], axis=0) + b_ref[...]


def _adaln(cond, ada_w, ada_b):
    depth, d, n6 = ada_w.shape
    n_vec = cond.shape[0]
    tn = 1024
    cb = jnp.broadcast_to(_silu(cond)[:, :, None], (n_vec, d, 128))
    return pl.pallas_call(
        functools.partial(_adaln_kernel, n_vec=n_vec, tn=tn),
        grid=(depth, n6 // tn),
        in_specs=[pl.BlockSpec((n_vec, d, 128), lambda l, j: (0, 0, 0)),
                  pl.BlockSpec((None, d, tn), lambda l, j: (l, 0, j)),
                  pl.BlockSpec((None, 1, tn), lambda l, j: (l, 0, j))],
        out_specs=pl.BlockSpec((None, HALO, tn), lambda l, j: (l, 0, j)),
        out_shape=jax.ShapeDtypeStruct((depth, HALO, n6), F32),
        compiler_params=_cparams("arbitrary", "arbitrary"),
    )(cb, ada_w, ada_b.reshape(depth, 1, n6))


def _modulated(x_ref, nw_ref, sh_ref, sc_ref):
    x = x_ref[...]
    y = x * lax.rsqrt(jnp.mean(x * x, axis=-1, keepdims=True) + NORM_EPS)
    return (y * nw_ref[...]) * (1.0 + sc_ref[...]) + sh_ref[...]


def _norm_mm_kernel(x_ref, nw_ref, sh_ref, sc_ref, w_ref, o_ref, h_ref):
    @pl.when(pl.program_id(1) == 0)
    def _():
        h_ref[...] = _modulated(x_ref, nw_ref, sh_ref, sc_ref).astype(BF16)

    o_ref[...] = jnp.dot(h_ref[...], w_ref[...], preferred_element_type=F32).astype(o_ref.dtype)


def _norm_swiglu_kernel(x_ref, nw_ref, sh_ref, sc_ref, wg_ref, wu_ref, o_ref, h_ref):
    @pl.when(pl.program_id(1) == 0)
    def _():
        h_ref[...] = _modulated(x_ref, nw_ref, sh_ref, sc_ref).astype(BF16)

    h = h_ref[...]
    g = jnp.dot(h, wg_ref[...], preferred_element_type=F32)
    u = jnp.dot(h, wu_ref[...], preferred_element_type=F32)
    o_ref[...] = (_silu(g) * u).astype(o_ref.dtype)


def _row_tile(m, cap):
    t = cap
    while m % t:
        t //= 2
    return t


def _norm_mm(x, nw, shift, scale, weights, rows_per_vec, out_dtype, tm_cap=1024, tn=512):
    m, d = x.shape
    n = weights[0].shape[1]
    tm = _row_tile(rows_per_vec, tm_cap)
    vec_map = lambda i, j: ((i * tm) // rows_per_vec, 0, 0)
    kern = _norm_mm_kernel if len(weights) == 1 else _norm_swiglu_kernel
    return pl.pallas_call(
        kern,
        grid=(m // tm, n // tn),
        in_specs=[pl.BlockSpec((tm, d), lambda i, j: (i, 0)),
                  pl.BlockSpec((1, d), lambda i, j: (0, 0)),
                  pl.BlockSpec((None, 1, d), vec_map),
                  pl.BlockSpec((None, 1, d), vec_map)]
                 + [pl.BlockSpec((d, tn), lambda i, j: (0, j)) for _ in weights],
        out_specs=pl.BlockSpec((tm, tn), lambda i, j: (i, j)),
        out_shape=jax.ShapeDtypeStruct((m, n), out_dtype),
        scratch_shapes=[pltpu.VMEM((tm, d), BF16)],
        compiler_params=_cparams("arbitrary", "arbitrary"),
    )(x, nw.reshape(1, d), shift, scale, *weights)


def _mm_res_kernel(a_ref, w_ref, res_ref, g_ref, o_ref):
    acc = jnp.dot(a_ref[...], w_ref[...], preferred_element_type=F32)
    o_ref[...] = res_ref[...] + g_ref[...] * acc


def _mm_res(a, w, res, gate, rows_per_vec, tm_cap=1024, tn=512):
    m, k = a.shape
    n = w.shape[1]
    tm = _row_tile(rows_per_vec, tm_cap)
    return pl.pallas_call(
        _mm_res_kernel,
        grid=(m // tm, n // tn),
        in_specs=[pl.BlockSpec((tm, k), lambda i, j: (i, 0)),
                  pl.BlockSpec((k, tn), lambda i, j: (0, j)),
                  pl.BlockSpec((tm, tn), lambda i, j: (i, j)),
                  pl.BlockSpec((None, 1, tn), lambda i, j: ((i * tm) // rows_per_vec, 0, j))],
        out_specs=pl.BlockSpec((tm, tn), lambda i, j: (i, j)),
        out_shape=jax.ShapeDtypeStruct((m, n), F32),
        compiler_params=_cparams("arbitrary", "arbitrary"),
    )(a, w, res, gate)


def _final_norm_kernel(x_ref, w_ref, o_ref):
    x = x_ref[...]
    o_ref[...] = x * lax.rsqrt(jnp.mean(x * x, axis=-1, keepdims=True) + NORM_EPS) * w_ref[...]


def _final_norm(x, w):
    m, d = x.shape
    tm = _row_tile(m, 512)
    return pl.pallas_call(
        _final_norm_kernel,
        grid=(m // tm,),
        in_specs=[pl.BlockSpec((tm, d), lambda i: (i, 0)), pl.BlockSpec((1, d), lambda i: (0, 0))],
        out_specs=pl.BlockSpec((tm, d), lambda i: (i, 0)),
        out_shape=jax.ShapeDtypeStruct((m, d), F32),
        compiler_params=_cparams("arbitrary"),
    )(x, w.reshape(1, d))


def _scan_block(i, n_blocks, rev):
    return (n_blocks - 1 - i) if rev else i


def _halo_block(i, n_blocks, rows, rev):
    per = rows // HALO
    if rev:
        return jnp.minimum((_scan_block(i, n_blocks, True) + 1) * per, n_blocks * per - 1)
    return jnp.maximum(i * per - 1, 0)


def _stage_with_halo(scr_ref, x_ref, halo_ref, keep, rows, rev):
    if rev:
        scr_ref[0:rows, :] = x_ref[...]
        scr_ref[rows:rows + HALO, :] = halo_ref[...] * keep
    else:
        scr_ref[0:HALO, :] = halo_ref[...] * keep
        scr_ref[HALO:HALO + rows, :] = x_ref[...]


def _lagged(scr_ref, lag, rows, rev):
    start = lag if rev else HALO - lag
    return scr_ref[start:start + rows, :]


def _rwkv_kernel(rkv_ref, sm_ref, rkv_h_ref, sm_h_ref, mu_rkv_ref, mu_sm_ref, w0_ref, w2_ref,
                 a0_ref, a2_ref, kkw_ref, kaw_ref, rkw_ref, s0_ref,
                 y_ref, bonus_ref, sl_ref, s_scr, rkv_scr, sm_scr, *, rev, n_heads, lora, nsub):
    i = pl.program_id(1)
    c = R_CHUNK
    rows = nsub * c
    n = R_HEADSIZE
    hw = n_heads * n

    @pl.when(i == 0)
    def _():
        s_scr[...] = s0_ref[...]

    keep = (i > 0).astype(F32)
    _stage_with_halo(rkv_scr, rkv_ref, rkv_h_ref, keep, rows, rev)
    _stage_with_halo(sm_scr, sm_ref, sm_h_ref, keep, rows, rev)
    x = rkv_ref[...]
    f = x + (_lagged(rkv_scr, 1, rows, rev) - x) * mu_rkv_ref[...]
    xs = sm_ref[...]
    fs = xs + (_lagged(sm_scr, 1, rows, rev) - xs) * mu_sm_ref[...]
    r = f[:, :hw]
    k = f[:, hw:2 * hw]
    v = f[:, 2 * hw:3 * hw]
    wl = fs[:, 256:256 + lora]
    al = fs[:, 384:384 + lora]

    ww = w0_ref[...] + _bdot(jnp.tanh(wl), w2_ref[...])
    lw = -math.exp(-0.5) * jax.nn.sigmoid(ww)
    a = jax.nn.sigmoid(a0_ref[...] + _bdot(al, a2_ref[...]))
    kkr = k * kkw_ref[...]
    k2 = k * (1.0 + (a - 1.0) * kaw_ref[...])
    rk = r * k2 * rkw_ref[...]

    cshift = int(math.log2(c))
    ti = lax.broadcasted_iota(jnp.int32, (rows, rows), 0)
    tj = lax.broadcasted_iota(jnp.int32, (rows, rows), 1)
    same_chunk = (ti >> cshift) == (tj >> cshift)
    upto = same_chunk & ((tj >= ti) if rev else (tj <= ti))
    g = _tri_cumsum(upto.astype(BF16), lw, parts=2)
    rsl = [slice(k * c, (k + 1) * c) for k in range(nsub)]
    gtot = [g[k * c:k * c + 1, :] if rev else g[(k + 1) * c - 1:(k + 1) * c, :] for k in range(nsub)]
    eg = jnp.exp(g)
    eneg = jnp.exp(-g)
    egp = jnp.exp(g - lw)
    etail = jnp.exp(jnp.concatenate([jnp.broadcast_to(t, (c, hw)) for t in gtot], axis=0) - g)
    dg = [jnp.exp(t) for t in gtot]
    rd_all = r * eg
    kp_all = k2 * eneg
    kpp_all = k2 * etail

    gw = R_GROUP * n
    groups = range(hw // gw)
    gsl = [slice(q * gw, (q + 1) * gw) for q in groups]
    shift = int(math.log2(n))
    bi = lax.broadcasted_iota(jnp.int32, (gw, gw), 0) >> shift
    bj = lax.broadcasted_iota(jnp.int32, (gw, gw), 1) >> shift
    same_head = bi == bj
    ones_bd = same_head.astype(BF16)
    lane_head = lax.broadcasted_iota(jnp.int32, (1, 128), 1) >> shift
    keep_f32 = [(lane_head == j).astype(F32) for j in range(128 // n)]
    keep_lanes = [t.astype(BF16) for t in keep_f32]

    def bdiag(t):
        tb = t.astype(BF16)
        zero = jnp.zeros((c, 128), BF16)
        rows = []
        for h in range(R_GROUP):
            lt, j = divmod(h * n, 128)
            piece = tb[:, lt * 128:(lt + 1) * 128] * keep_lanes[j // n]
            rows.append(jnp.concatenate([piece if u == lt else zero for u in range(gw // 128)], axis=1))
        return jnp.concatenate(rows, axis=0)

    def diag_blocks(full):
        tiles = []
        for lt in range(gw // 128):
            acc = None
            for j in range(128 // n):
                h = lt * (128 // n) + j
                part = full[h * n:(h + 1) * n, lt * 128:(lt + 1) * 128] * keep_f32[j]
                acc = part if acc is None else acc + part
            tiles.append(acc)
        return jnp.concatenate(tiles, axis=1)

    def head_sums(t):
        hi, mid, _ = _split3(t)
        dot = lambda p: jnp.dot(p, ones_bd, preferred_element_type=F32)
        return dot(hi) + dot(mid)

    tdot = lambda p, q: lax.dot_general(p.astype(BF16), q.astype(BF16), (((0,), (0,)), ((), ())),
                                        preferred_element_type=F32)
    gi = lax.broadcasted_iota(jnp.int32, (c, gw), 0)
    gj = lax.broadcasted_iota(jnp.int32, (c, gw), 1) & (c - 1)
    before_g = (gj > gi) if rev else (gj < gi)
    upto_g = (gj >= gi) if rev else (gj <= gi)
    eye_g = (gi == gj).astype(F32)

    kk_all = jnp.concatenate(
        [kkr[:, s] * lax.rsqrt(jnp.maximum(head_sums(kkr[:, s] * kkr[:, s]), 1e-24)) for s in gsl], axis=1)
    bh_all = kk_all * a
    kkd_all = kk_all * egp
    bp_all = bh_all * eneg
    bpp_all = bh_all * etail
    bonus_ref[...] = jnp.concatenate([head_sums(rk[:, s]) for s in gsl], axis=1) * v

    units = [(k, q) for k in range(nsub) for q in groups]
    un = range(len(units))
    cut = lambda t: [t[rsl[k], gsl[q]] for k, q in units]
    kkd, bp, bpp, rd, vq, kp, kpp = (cut(t) for t in (kkd_all, bp_all, bpp_all, rd_all, v, kp_all, kpp_all))
    amat = [_bdot_nt(jnp.concatenate([kkd[u], rd[u]], axis=0),
                     jnp.concatenate([bdiag(kp[u]), bdiag(bp[u])], axis=0)) for u in un]
    mk = [jnp.where(before_g, m[:c, :gw], 0.0) for m in amat]
    nk = [jnp.where(upto_g, m[c:, :gw], 0.0) for m in amat]
    nb = [jnp.where(upto_g, m[c:, gw:], 0.0) for m in amat]
    p2 = [jnp.where(before_g, -m[:c, gw:], 0.0) for m in amat]
    tinv = [eye_g + t for t in p2]
    p2 = [_bdot(t, bdiag(t)) for t in p2]
    for _ in range(cshift - 2):
        sq = [_bdot(jnp.concatenate([tinv[u], p2[u]], axis=0), bdiag(p2[u])) for u in un]
        tinv = [tinv[u] + sq[u][:c] for u in un]
        p2 = [t[c:] for t in sq]
    tinv = [tinv[u] + _bdot(tinv[u], bdiag(p2[u])) for u in un]
    mnv = [_bdot(jnp.concatenate([mk[u], nk[u]], axis=0), bdiag(vq[u])) for u in un]
    wu = [_bdot(tinv[u], jnp.concatenate([bdiag(kkd[u]), bdiag(mnv[u][:c])], axis=1))
          for u in un]
    nwu = [_bdot(nb[u], jnp.concatenate([bdiag(wu[u][:, :gw]), bdiag(wu[u][:, gw:])], axis=1))
           for u in un]
    wub = [tdot(wu[u], bpp[u]) for u in un]
    pmat = [bdiag(diag_blocks(t[:gw])) for t in wub]
    dmat = [diag_blocks(tdot(vq[u], kpp[u])) - diag_blocks(wub[u][gw:]) for u in un]
    qt = [rd[u] - nwu[u][:, :gw] for u in un]
    y0 = [mnv[u][c:] - nwu[u][:, gw:] for u in un]
    state = [s_scr[q] for q in groups]
    for k in (reversed(range(nsub)) if rev else range(nsub)):
        for q in groups:
            u = k * len(groups) + q
            y_ref[rsl[k], gsl[q]] = y0[u] + _bdot_nt(qt[u], bdiag(state[q]))
            state[q] = state[q] * dg[k][:, gsl[q]] - _bdot(state[q], pmat[u]) + dmat[u]
    for q in groups:
        s_scr[q] = state[q]

    @pl.when(i == pl.num_programs(1) - 1)
    def _():
        sl_ref[...] = s_scr[...]


def _rwkv_direction(pr, s0, p, rev):
    bsz, seqlen, width = pr.shape
    hw = s0.shape[1] * s0.shape[3]
    n_heads = hw // R_HEADSIZE
    nsub = R_SUB if seqlen % (R_SUB * R_CHUNK) == 0 else 1
    c = nsub * R_CHUNK
    nblk = seqlen // c
    sm_col = width // 512 - 1
    blk = lambda i: _scan_block(i, nblk, rev)
    halo = lambda i: _halo_block(i, nblk, c, rev)
    vec = lambda w: pl.BlockSpec((1, w), lambda b, i: (0, 0))
    lora = p['w2'].shape[0]
    state_spec = pl.BlockSpec((None,) + s0.shape[1:], lambda b, i: (b, 0, 0, 0))
    return pl.pallas_call(
        functools.partial(_rwkv_kernel, rev=rev, n_heads=n_heads, lora=lora, nsub=nsub),
        grid=(bsz, nblk),
        in_specs=[pl.BlockSpec((None, c, 3 * hw), lambda b, i: (b, blk(i), 1)),
                  pl.BlockSpec((None, c, 512), lambda b, i: (b, blk(i), sm_col)),
                  pl.BlockSpec((None, HALO, 3 * hw), lambda b, i: (b, halo(i), 1)),
                  pl.BlockSpec((None, HALO, 512), lambda b, i: (b, halo(i), sm_col)),
                  vec(3 * hw), vec(512), vec(hw),
                  pl.BlockSpec((lora, hw), lambda b, i: (0, 0)),
                  vec(hw),
                  pl.BlockSpec((lora, hw), lambda b, i: (0, 0)),
                  vec(hw), vec(hw), vec(hw),
                  state_spec],
        out_specs=[pl.BlockSpec((None, c, hw), lambda b, i: (b, blk(i), 0)),
                   pl.BlockSpec((None, c, hw), lambda b, i: (b, blk(i), 0)),
                   state_spec],
        out_shape=[jax.ShapeDtypeStruct((bsz, seqlen, hw), F32),
                   jax.ShapeDtypeStruct((bsz, seqlen, hw), F32),
                   jax.ShapeDtypeStruct(s0.shape, F32)],
        scratch_shapes=[pltpu.VMEM(s0.shape[1:], F32),
                        pltpu.VMEM((c + HALO, 3 * hw), F32),
                        pltpu.VMEM((c + HALO, 512), F32)],
        compiler_params=_cparams("arbitrary", "arbitrary"),
    )(pr, pr, pr, pr, p['mu_rkv'], p['mu_sm'], p['w0'], p['w2'], p['a0'], p['a2'],
      p['kk'], p['ka'], p['rk'], s0)


def _expand_cols(x, e_bf16):
    hi, mid, lo = _split3(x)
    dot = lambda p: jnp.dot(p, e_bf16, preferred_element_type=F32)
    return dot(hi) + (dot(mid) + dot(lo))


def _causal_conv(scr_ref, w_ref, b_ref, rows, taps, rev):
    acc = b_ref[...] + w_ref[taps - 1:taps, :] * _lagged(scr_ref, 0, rows, rev)
    for kk in range(taps - 1):
        acc = acc + w_ref[kk:kk + 1, :] * _lagged(scr_ref, taps - 1 - kk, rows, rev)
    return acc


def _ssd_kernel(xbc_ref, halo_ref, sm_ref, dtt_ref, cw_ref, cb_ref, dtb_row_ref, dtb_col_ref,
                alog_row_ref, alog_col_ref, dskip_ref, e64_ref, hmask_ref, h0_ref,
                y_ref, hl_ref, h_scr, x_scr, *, rev, n_heads):
    i = pl.program_id(1)
    c = M_CHUNK
    p = M_HEADDIM
    ns = M_STATE
    mix = n_heads * p
    hpg = n_heads // M_GROUPS
    gw = hpg * p

    @pl.when(i == 0)
    def _():
        h_scr[...] = h0_ref[...]

    keep = (i > 0).astype(F32)
    _stage_with_halo(x_scr, xbc_ref, halo_ref, keep, c, rev)
    xbc = _silu(_causal_conv(x_scr, cw_ref, cb_ref, c, M_CONV, rev))
    xs = xbc[:, :mix]
    bm = [xbc[:, mix + g * ns:mix + (g + 1) * ns] for g in range(M_GROUPS)]
    cm = [xbc[:, mix + (M_GROUPS + g) * ns:mix + (M_GROUPS + g + 1) * ns] for g in range(M_GROUPS)]

    ti = lax.broadcasted_iota(jnp.int32, (c, c), 0)
    tj = lax.broadcasted_iota(jnp.int32, (c, c), 1)
    upto = (tj >= ti) if rev else (tj <= ti)
    upto_t = (ti >= tj) if rev else (ti <= tj)
    end = 0 if rev else c - 1

    dt = _softplus(sm_ref[:, 480:480 + n_heads] + dtb_row_ref[...])
    a_cs = _tri_cumsum(upto.astype(BF16), dt * -jnp.exp(alog_row_ref[...]))
    dt_t = _softplus(dtt_ref[...] + dtb_col_ref[...])
    a_cs_t = _expand_cols(dt_t * -jnp.exp(alog_col_ref[...]), upto_t.astype(BF16))
    a_end_t = a_cs_t[:, end:end + 1]
    w_t = dt_t * jnp.exp(a_end_t - a_cs_t)
    grow_t = jnp.broadcast_to(jnp.exp(a_end_t), (n_heads, ns))

    parts = [t.astype(F32) for t in _split3(a_cs)]
    ones = jnp.ones((c, n_heads), F32)
    lhs = jnp.concatenate(parts + [ones] * 3, axis=1).astype(BF16)
    hmask = hmask_ref[...]
    parts_t = jnp.concatenate(list(_split3(a_cs_t)), axis=0)
    rhs = jnp.concatenate([hmask, -jnp.concatenate([parts_t] * n_heads, axis=1) * hmask], axis=0)
    seg = jnp.dot(lhs, rhs, preferred_element_type=F32)
    ea_full = jnp.exp(jnp.dot(lhs[:, :3 * n_heads], e64_ref[...], preferred_element_type=F32))

    lane_head = lax.broadcasted_iota(jnp.int32, (1, gw), 1) >> int(math.log2(p))
    keep_lanes = [(lane_head == j).astype(BF16) for j in range(hpg)]
    xs_t = jnp.transpose(xs)
    y_diag, y_off = [], []
    for g in range(M_GROUPS):
        heads = range(g * hpg, (g + 1) * hpg)
        cb = _bdot_nt(cm[g], bm[g])
        scores = []
        for h in heads:
            decay = jnp.where(upto, jnp.exp(jnp.minimum(seg[:, h * c:(h + 1) * c], 0.0)), 0.0)
            scores.append((cb * decay * dt_t[h:h + 1, :]).astype(BF16))
        xg = xs[:, g * gw:(g + 1) * gw].astype(BF16)
        x_bd = jnp.concatenate([xg * keep_lanes[j] for j in range(hpg)], axis=0)
        y_diag.append(jnp.dot(jnp.concatenate(scores, axis=1), x_bd, preferred_element_type=F32))
        h_in = h_scr[g]
        y_off.append(_bdot_nt(cm[g], h_in))
        wrows = jnp.concatenate([jnp.broadcast_to(w_t[h:h + 1, :], (p, c)) for h in heads], axis=0)
        grow = jnp.concatenate([jnp.broadcast_to(grow_t[h:h + 1, :], (p, ns)) for h in heads], axis=0)
        h_scr[g] = h_in * grow + _bdot(xs_t[g * gw:(g + 1) * gw, :] * wrows, bm[g])
    y_ref[...] = (jnp.concatenate(y_diag, axis=1) + jnp.concatenate(y_off, axis=1) * ea_full
                  + xs * dskip_ref[...])

    @pl.when(i == pl.num_programs(1) - 1)
    def _():
        hl_ref[...] = h_scr[...]


def _ssd_direction(pr, dt_t, h0, p, rev):
    bsz, seqlen, width = pr.shape
    n_heads = dt_t.shape[1]
    mix = n_heads * M_HEADDIM
    xw = mix + 2 * M_GROUPS * M_STATE
    c = M_CHUNK
    nblk = seqlen // c
    sm_col = width // 512 - 1
    blk = lambda i: _scan_block(i, nblk, rev)
    halo = lambda i: _halo_block(i, nblk, c, rev)
    full = lambda a: pl.BlockSpec(a.shape, lambda b, i: (0,) * a.ndim)
    consts = [p['conv_w'], p['conv_b'], p['dtb_row'], p['dtb_col'], p['alog_row'], p['alog_col'],
              p['dskip'], p['e64'], p['hmask']]
    state_spec = pl.BlockSpec((None,) + h0.shape[1:], lambda b, i: (b, 0, 0, 0))
    return pl.pallas_call(
        functools.partial(_ssd_kernel, rev=rev, n_heads=n_heads),
        grid=(bsz, nblk),
        in_specs=[pl.BlockSpec((None, c, xw), lambda b, i: (b, blk(i), 0)),
                  pl.BlockSpec((None, HALO, xw), lambda b, i: (b, halo(i), 0)),
                  pl.BlockSpec((None, c, 512), lambda b, i: (b, blk(i), sm_col)),
                  pl.BlockSpec((None, n_heads, c), lambda b, i: (b, 0, blk(i)))]
                 + [full(a) for a in consts] + [state_spec],
        out_specs=[pl.BlockSpec((None, c, mix), lambda b, i: (b, blk(i), 0)), state_spec],
        out_shape=[jax.ShapeDtypeStruct((bsz, seqlen, mix), F32),
                   jax.ShapeDtypeStruct(h0.shape, F32)],
        scratch_shapes=[pltpu.VMEM(h0.shape[1:], F32), pltpu.VMEM((c + HALO, xw), F32)],
        compiler_params=_cparams("arbitrary", "arbitrary"),
    )(pr, pr, pr, dt_t, *consts, h0)


def _rglru_kernel(x_ref, halo_ref, cw_ref, cb_ref, wa_ref, ba_ref, wx_ref, bx_ref, lam_ref, h0_ref,
                  o_ref, hl_ref, h_scr, x_scr, a_scr, u_scr, *, rev, tb):
    i = pl.program_id(1)

    @pl.when(i == 0)
    def _():
        h_scr[...] = h0_ref[...]

    keep = (i > 0).astype(F32)
    _stage_with_halo(x_scr, x_ref, halo_ref, keep, tb, rev)
    xc = _causal_conv(x_scr, cw_ref, cb_ref, tb, C_CONV, rev)
    a_scr[...], u_scr[...] = _rglru_gates(xc, wa_ref, ba_ref, wx_ref, bx_ref, lam_ref)

    def step(t, h):
        tt = (tb - 1 - t) if rev else t
        h = a_scr[pl.ds(tt, 1), :] * h + u_scr[pl.ds(tt, 1), :]
        o_ref[pl.ds(tt, 1), :] = h
        return h

    h_scr[0:1, :] = lax.fori_loop(0, tb, step, h_scr[0:1, :], unroll=8)

    @pl.when(i == pl.num_programs(1) - 1)
    def _():
        hl_ref[...] = h_scr[...]


def _rglru_direction(xv, h0, p, rev, tb):
    bsz, rows, wtot = xv.shape
    width = h0.shape[2]
    ncol = wtot // (2 * width)
    nrb = rows // tb
    total = ncol * nrb
    per = tb // HALO

    def pos(i):
        s = _scan_block(i, total, rev)
        return s // nrb, s % nrb

    def x_map(b, i):
        col, q = pos(i)
        return b, q, 2 * col + 1

    def halo_map(b, i):
        s = _scan_block(i, total, rev)
        sp = jnp.minimum(s + 1, total - 1) if rev else jnp.maximum(s - 1, 0)
        col, q = sp // nrb, sp % nrb
        return b, (q * per if rev else (q + 1) * per - 1), 2 * col + 1

    def o_map(b, i):
        col, q = pos(i)
        return b, q, col

    full = lambda a: pl.BlockSpec(a.shape, lambda b, i: (0,) * a.ndim)
    consts = [p['conv_w'], p['conv_b'], p['wa'], p['ba'], p['wx'], p['bx'], p['lam']]
    state_spec = pl.BlockSpec((None, HALO, width), lambda b, i: (b, 0, 0))
    return pl.pallas_call(
        functools.partial(_rglru_kernel, rev=rev, tb=tb),
        grid=(bsz, total),
        in_specs=[pl.BlockSpec((None, tb, width), x_map), pl.BlockSpec((None, HALO, width), halo_map)]
                 + [full(a) for a in consts] + [state_spec],
        out_specs=[pl.BlockSpec((None, tb, width), o_map), state_spec],
        out_shape=[jax.ShapeDtypeStruct((bsz, rows, ncol * width), F32),
                   jax.ShapeDtypeStruct(h0.shape, F32)],
        scratch_shapes=[pltpu.VMEM((HALO, width), F32), pltpu.VMEM((tb + HALO, width), F32),
                        pltpu.VMEM((tb, width), F32), pltpu.VMEM((tb, width), F32)],
        compiler_params=_cparams("arbitrary", "arbitrary"),
    )(xv, xv, *consts, h0)


def _rglru_gates(xc, wa_ref, ba_ref, wx_ref, bx_ref, lam_ref):
    blk = xc.shape[1] // C_HEADS
    xh = [xc[:, h * blk:(h + 1) * blk] for h in range(C_HEADS)]
    gate_r = jax.nn.sigmoid(
        jnp.concatenate([_bdot(xh[h], wa_ref[h]) for h in range(C_HEADS)], axis=1) + ba_ref[...])
    gate_i = jax.nn.sigmoid(
        jnp.concatenate([_bdot(xh[h], wx_ref[h]) for h in range(C_HEADS)], axis=1) + bx_ref[...])
    log_a = -RG_C * gate_r * _softplus(-lam_ref[...])
    a = jnp.exp(log_a)
    one_m_a2 = jnp.maximum(-jnp.tanh(log_a) * (a * a + 1.0), 0.0)
    mult = jnp.where(one_m_a2 > 0.0, one_m_a2 * lax.rsqrt(one_m_a2), 0.0)
    return a, mult * gate_i * xc


def _rglru_grid_kernel(x_ref, e0_ref, e1_ref, e2_ref, cw_ref, cb_ref, wa_ref, ba_ref, wx_ref,
                       bx_ref, lam_ref, h0_ref, hloc_ref, aprod_ref, cin_ref, hl_ref,
                       ring_scr, h_scr, ap_scr, stage_scr, *, rev, ncol):
    i = pl.program_id(1)

    def from_previous_column(e_ref):
        _stage_with_halo(stage_scr, e_ref, jnp.zeros((HALO, e_ref.shape[1]), F32), 0.0, ncol, rev)
        return _lagged(stage_scr, 1, ncol, rev)

    @pl.when(i == 0)
    def _():
        ring_scr[0] = from_previous_column(e2_ref)
        ring_scr[1] = from_previous_column(e1_ref)
        ring_scr[2] = from_previous_column(e0_ref)
        h_scr[...] = jnp.zeros(h_scr.shape, F32)
        ap_scr[...] = jnp.ones(ap_scr.shape, F32)

    x0 = x_ref[...]
    lag1, lag2, lag3 = ring_scr[0], ring_scr[1], ring_scr[2]
    xc = (cb_ref[...] + cw_ref[3:4, :] * x0 + cw_ref[2:3, :] * lag1
          + cw_ref[1:2, :] * lag2 + cw_ref[0:1, :] * lag3)
    ring_scr[2] = lag2
    ring_scr[1] = lag1
    ring_scr[0] = x0
    a, u = _rglru_gates(xc, wa_ref, ba_ref, wx_ref, bx_ref, lam_ref)
    h = a * h_scr[...] + u
    ap = a * ap_scr[...]
    h_scr[...] = h
    ap_scr[...] = ap
    hloc_ref[...] = h
    aprod_ref[...] = ap

    @pl.when(i == pl.num_programs(1) - 1)
    def _():
        def step(t, carry):
            col = (ncol - 1 - t) if rev else t
            cin_ref[pl.ds(col, 1), :] = carry
            return h_scr[pl.ds(col, 1), :] + ap_scr[pl.ds(col, 1), :] * carry

        last = lax.fori_loop(0, ncol, step, h0_ref[0:1, :], unroll=8)
        hl_ref[...] = jnp.zeros(hl_ref.shape, F32)
        hl_ref[0:1, :] = last


def _rglru_grid_direction(p3, h0, p, rev):
    bsz, seqlen, _ = p3.shape
    width = h0.shape[2]
    ncol = GRID_W
    rows = seqlen // ncol
    blk = lambda i: _scan_block(i, rows, rev)
    edge = lambda k: pl.BlockSpec((None, ncol, width),
                                  lambda b, i: (b, (2 - k) if rev else (rows - 3 + k), 1))
    full = lambda a: pl.BlockSpec(a.shape, lambda b, i: (0,) * a.ndim)
    consts = [p['conv_w'], p['conv_b'], p['wa'], p['ba'], p['wx'], p['bx'], p['lam']]
    state_spec = pl.BlockSpec((None, HALO, width), lambda b, i: (b, 0, 0))
    row_spec = pl.BlockSpec((None, ncol, width), lambda b, i: (b, blk(i), 0))
    return pl.pallas_call(
        functools.partial(_rglru_grid_kernel, rev=rev, ncol=ncol),
        grid=(bsz, rows),
        in_specs=[pl.BlockSpec((None, ncol, width), lambda b, i: (b, blk(i), 1)),
                  edge(0), edge(1), edge(2)] + [full(a) for a in consts] + [state_spec],
        out_specs=[row_spec, row_spec,
                   pl.BlockSpec((None, ncol, width), lambda b, i: (b, 0, 0)), state_spec],
        out_shape=[jax.ShapeDtypeStruct((bsz, seqlen, width), F32),
                   jax.ShapeDtypeStruct((bsz, seqlen, width), F32),
                   jax.ShapeDtypeStruct((bsz, ncol, width), F32),
                   jax.ShapeDtypeStruct(h0.shape, F32)],
        scratch_shapes=[pltpu.VMEM((3, ncol, width), F32), pltpu.VMEM((ncol, width), F32),
                        pltpu.VMEM((ncol, width), F32), pltpu.VMEM((ncol + HALO, width), F32)],
        compiler_params=_cparams("arbitrary", "arbitrary"),
    )(p3, p3, p3, p3, *consts, h0)


def _readout_ab_kernel(z_ref, sm_ref, ymf_ref, ymb_ref, yrf_ref, yrb_ref, bof_ref, bob_ref,
                       nw_ref, lw_ref, lb_ref, g2_ref, o_ref, *, n_heads):
    t = (ymf_ref[...] + ymb_ref[...]) * _silu(z_ref[...])
    gw = t.shape[1] // M_GROUPS
    a_out = []
    for g in range(M_GROUPS):
        tg = t[:, g * gw:(g + 1) * gw]
        a_out.append(tg * lax.rsqrt(jnp.mean(tg * tg, axis=-1, keepdims=True) + M_NORM_EPS))
    a_out = jnp.concatenate(a_out, axis=1) * nw_ref[...]

    yr = yrf_ref[...] + yrb_ref[...]
    n = R_HEADSIZE
    yn = []
    for h in range(n_heads):
        yh = yr[:, h * n:(h + 1) * n]
        dlt = yh - jnp.mean(yh, axis=-1, keepdims=True)
        yn.append(dlt * lax.rsqrt(jnp.mean(dlt * dlt, axis=-1, keepdims=True) + R_LN_EPS))
    yn = jnp.concatenate(yn, axis=1) * lw_ref[...] + lb_ref[...]
    gate = _bdot(jax.nn.sigmoid(sm_ref[:, 0:g2_ref.shape[0]]), g2_ref[...])
    b_out = (yn + bof_ref[...] + bob_ref[...]) * gate
    o_ref[...] = jnp.concatenate([a_out, b_out], axis=1).astype(o_ref.dtype)


def _readout_ab(pr, ym, yr, bo, ro, n_heads):
    bsz, seqlen, width = pr.shape
    m = bsz * seqlen
    mix = ym[0].shape[2]
    tm = _row_tile(m, 256)
    flat = lambda t: t.reshape(m, t.shape[2])
    rowblk = lambda w, col: pl.BlockSpec((tm, w), lambda i: (i, col))
    full = lambda a: pl.BlockSpec(a.shape, lambda i: (0,) * a.ndim)
    consts = [ro['norm_w'], ro['lnx_w'], ro['lnx_b'], ro['g2']]
    return pl.pallas_call(
        functools.partial(_readout_ab_kernel, n_heads=n_heads),
        grid=(m // tm,),
        in_specs=[rowblk(mix, 2), rowblk(512, width // 512 - 1)] + [rowblk(mix, 0)] * 6
                 + [full(a) for a in consts],
        out_specs=rowblk(2 * mix, 0),
        out_shape=jax.ShapeDtypeStruct((m, 2 * mix), BF16),
        compiler_params=_cparams("arbitrary"),
    )(flat(pr), flat(pr), flat(ym[0]), flat(ym[1]), flat(yr[0]), flat(yr[1]), flat(bo[0]),
      flat(bo[1]), *consts)


def _gelu_tanh(g):
    return 0.5 * g * (1.0 + jnp.tanh(math.sqrt(2.0 / math.pi) * (g + 0.044715 * (g * g * g))))


def _readout_c_kernel(gy_ref, hf_ref, hb_ref, o_ref):
    o_ref[...] = ((hf_ref[...] + hb_ref[...]) * _gelu_tanh(gy_ref[...])).astype(o_ref.dtype)


def _readout_grid_kernel(gy_ref, hf_ref, af_ref, cf_ref, hb_ref, ab_ref, cb_ref, o_ref, *, ncol):
    for j in range(o_ref.shape[0] // ncol):
        rs = slice(j * ncol, (j + 1) * ncol)
        h = (hf_ref[rs, :] + af_ref[rs, :] * cf_ref[...]) + (hb_ref[rs, :] + ab_ref[rs, :] * cb_ref[...])
        o_ref[rs, :] = (h * _gelu_tanh(gy_ref[rs, :])).astype(o_ref.dtype)


def _readout_grid(p3, fwd, bwd):
    bsz, seqlen, _ = p3.shape
    cw = fwd[0].shape[2]
    ncol = fwd[2].shape[1]
    tm = _row_tile(seqlen, 8 * ncol)
    spec = pl.BlockSpec((None, tm, cw), lambda b, i: (b, i, 0))
    cspec = pl.BlockSpec((None, ncol, cw), lambda b, i: (b, 0, 0))
    return pl.pallas_call(
        functools.partial(_readout_grid_kernel, ncol=ncol),
        grid=(bsz, seqlen // tm),
        in_specs=[spec, spec, spec, cspec, spec, spec, cspec],
        out_specs=spec,
        out_shape=jax.ShapeDtypeStruct((bsz, seqlen, cw), BF16),
        compiler_params=_cparams("arbitrary", "arbitrary"),
    )(p3, *fwd, *bwd)


def _readout_c(p, hf, hb):
    m, cw = hf.shape
    tm = _row_tile(m, 512)
    spec = pl.BlockSpec((tm, cw), lambda i: (i, 0))
    return pl.pallas_call(
        _readout_c_kernel,
        grid=(m // tm,),
        in_specs=[spec, spec, spec],
        out_specs=spec,
        out_shape=jax.ShapeDtypeStruct((m, cw), BF16),
        compiler_params=_cparams("arbitrary"),
    )(p, hf, hb)


def kernel(x, c, ctx, c_ctx, ada_w, ada_b, norm1_w, norm2_w, ffn_w_gate, ffn_w_up, ffn_w_down, final_norm_w, ab_w_in, ab_w_out, m_conv_w, m_conv_b, m_dt_bias, m_a_log, m_d, m_norm_w, r_mu, r_w0, r_w2, r_a0, r_a2, r_kk, r_ka, r_rk, r_g2, r_lnx_w, r_lnx_b, c_w_in, c_w_out, c_conv_w, c_conv_b, c_wa, c_ba, c_wx, c_bx, c_lambda):
    bsz, seqlen, d = x.shape
    ctx_len = ctx.shape[1]
    depth = ada_w.shape[0]
    n_lat, n_ctx = bsz * seqlen, bsz * ctx_len
    row = lambda t: t.reshape(1, -1)

    mod = _adaln(jnp.concatenate([c_ctx[None, :], c], axis=0), ada_w, ada_b)

    xl = x.reshape(n_lat, d)
    xc = ctx.reshape(n_ctx, d)
    mix_a = m_norm_w.shape[1]
    m_heads = m_dt_bias.shape[2]
    r_heads = r_rk.shape[2]
    lora = r_w2.shape[2]
    e64 = jnp.tile(jnp.repeat(jnp.eye(m_heads, dtype=BF16), M_HEADDIM, axis=1), (3, 1))
    hmask = jnp.tile(jnp.repeat(jnp.eye(m_heads, dtype=BF16), M_CHUNK, axis=1), (3, 1))

    for layer in range(depth):
        with_ctx = layer < depth - 1
        part = lambda k: mod[layer, :, k * d:(k + 1) * d]
        lat_vec = lambda k: part(k)[1:1 + bsz, None, :]
        ctx_vec = lambda k: part(k)[0:1, None, :]
        nw1, nw2 = norm1_w[layer], norm2_w[layer]

        if layer % 2 == 0:
            e = layer // 2
            w = ab_w_in[e]
            o1 = mix_a
            o2 = o1 + mix_a + 2 * M_GROUPS * M_STATE
            o3 = o2 + m_heads
            o4 = o3 + 3 * r_heads * R_HEADSIZE + 2 * lora
            rkv_w = 3 * r_heads * R_HEADSIZE
            zeros = lambda n: jnp.zeros((d, n), w.dtype)
            w_in = jnp.concatenate(
                [w[:, o1:o2], w[:, :o1], w[:, o3:o3 + rkv_w], w[:, o4:],
                 w[:, o3 + rkv_w:o3 + rkv_w + lora], zeros(128 - lora),
                 w[:, o3 + rkv_w + lora:o4], w[:, o2:o3], zeros(128 - lora - m_heads)],
                axis=1).astype(BF16)
            width = w_in.shape[1]
            w_out = ab_w_out[e].astype(BF16)
            g2w = r_g2[e].astype(BF16)

            def project(xs, sh, sc, rows_per_vec, seg_len):
                pr = _norm_mm(xs, nw1, sh, sc, [w_in], rows_per_vec, F32)
                pr = pr.reshape(bsz, seg_len, width)
                dt_t = jnp.swapaxes(pr[:, :, width - 32:width - 32 + m_heads], 1, 2)
                return pr, dt_t

            pr_l, dtt_l = project(xl, lat_vec(0), lat_vec(1), seqlen, seqlen)
            pr_c, dtt_c = project(xc, ctx_vec(0), ctx_vec(1), n_ctx, ctx_len)

            ym_l = ym_c = yr_l = yr_c = bo_l = bo_c = None
            for dr in range(2):
                rev = dr == 1
                mp = dict(conv_w=m_conv_w[e, dr], conv_b=row(m_conv_b[e, dr]),
                          dtb_row=row(m_dt_bias[e, dr]), dtb_col=m_dt_bias[e, dr].reshape(-1, 1),
                          alog_row=row(m_a_log[e, dr]), alog_col=m_a_log[e, dr].reshape(-1, 1),
                          dskip=row(jnp.repeat(m_d[e, dr], M_HEADDIM)), e64=e64, hmask=hmask)
                h0 = jnp.zeros((bsz, M_GROUPS, m_heads // M_GROUPS * M_HEADDIM, M_STATE), F32)
                y_c, h_c = _ssd_direction(pr_c, dtt_c, h0, mp, rev)
                y_l, _ = _ssd_direction(pr_l, dtt_l, h_c, mp, rev)
                mu = r_mu[e, dr]
                mu_sm = (jnp.zeros((512,), F32).at[256:256 + lora].set(mu[rkv_w:rkv_w + lora])
                         .at[384:384 + lora].set(mu[rkv_w + lora:]))
                rp = dict(mu_rkv=row(mu[:rkv_w]), mu_sm=row(mu_sm), w0=row(r_w0[e, dr]),
                          w2=r_w2[e, dr].astype(BF16), a0=row(r_a0[e, dr]), a2=r_a2[e, dr].astype(BF16),
                          kk=row(r_kk[e, dr]), ka=row(r_ka[e, dr]), rk=row(r_rk[e, dr]))
                s0 = jnp.zeros((bsz, r_heads // R_GROUP, R_HEADSIZE, R_GROUP * R_HEADSIZE), F32)
                v_c, b_c, s_c = _rwkv_direction(pr_c, s0, rp, rev)
                v_l, b_l, _ = _rwkv_direction(pr_l, s_c, rp, rev)
                if dr == 0:
                    ym_l, ym_c, yr_l, yr_c, bo_l, bo_c = y_l, y_c, v_l, v_c, b_l, b_c
                else:
                    ym_l, yr_l, bo_l = (ym_l, y_l), (yr_l, v_l), (bo_l, b_l)
                    ym_c, yr_c, bo_c = (ym_c, y_c), (yr_c, v_c), (bo_c, b_c)

            ro = dict(norm_w=row(m_norm_w[e]), lnx_w=row(r_lnx_w[e]), lnx_b=row(r_lnx_b[e]), g2=g2w)
            act_l = _readout_ab(pr_l, ym_l, yr_l, bo_l, ro, r_heads)
            xl = _mm_res(act_l, w_out, xl, lat_vec(2), seqlen)
            if with_ctx:
                act_c = _readout_ab(pr_c, ym_c, yr_c, bo_c, ro, r_heads)
                xc = _mm_res(act_c, w_out, xc, ctx_vec(2), n_ctx)
        else:
            o = layer // 2
            w_in = c_w_in[o].astype(BF16)
            w_out = c_w_out[o].astype(BF16)
            cw = w_out.shape[0]
            p_l = _norm_mm(xl, nw1, lat_vec(0), lat_vec(1), [w_in], seqlen, F32)
            p_c = _norm_mm(xc, nw1, ctx_vec(0), ctx_vec(1), [w_in], n_ctx, F32)
            p3_l = p_l.reshape(bsz, seqlen, 2 * cw)
            hs_l, hs_c = [], []
            for dr in range(2):
                rev = dr == 1
                cp = dict(conv_w=c_conv_w[o, dr], conv_b=row(c_conv_b[o, dr]),
                          wa=c_wa[o, dr].astype(BF16), ba=row(c_ba[o, dr]),
                          wx=c_wx[o, dr].astype(BF16), bx=row(c_bx[o, dr]), lam=row(c_lambda[o, dr]))
                h0 = jnp.zeros((bsz, HALO, cw), F32)
                h_c, s_c = _rglru_direction(p_c.reshape(bsz, ctx_len, 2 * cw), h0, cp, rev,
                                            _row_tile(ctx_len, 128))
                hs_l.append(_rglru_grid_direction(p3_l, s_c, cp, rev)[:3])
                hs_c.append(h_c.reshape(n_ctx, cw))
            act_l = _readout_grid(p3_l, hs_l[0], hs_l[1]).reshape(n_lat, cw)
            xl = _mm_res(act_l, w_out, xl, lat_vec(2), seqlen)
            if with_ctx:
                act_c = _readout_c(p_c, hs_c[0], hs_c[1])
                xc = _mm_res(act_c, w_out, xc, ctx_vec(2), n_ctx)

        wg, wu, wd = (ffn_w_gate[layer].astype(BF16), ffn_w_up[layer].astype(BF16),
                      ffn_w_down[layer].astype(BF16))
        act = _norm_mm(xl, nw2, lat_vec(3), lat_vec(4), [wg, wu], seqlen, BF16)
        xl = _mm_res(act, wd, xl, lat_vec(5), seqlen)
        if with_ctx:
            act = _norm_mm(xc, nw2, ctx_vec(3), ctx_vec(4), [wg, wu], n_ctx, BF16)
            xc = _mm_res(act, wd, xc, ctx_vec(5), n_ctx)

    return _final_norm(xl, final_norm_w).reshape(bsz, seqlen, d)
```

```python
import functools
import math

import jax
import jax.numpy as jnp
from jax import lax
from jax.experimental import pallas as pl
from jax.experimental.pallas import tpu as pltpu

F32 = jnp.float32
BF16 = jnp.bfloat16

NORM_EPS = 1e-6
GRID_W = 64
M_HEADDIM = 64
M_GROUPS = 4
M_STATE = 128
M_CONV = 4
M_CHUNK = 128
M_NORM_EPS = 1e-5
R_HEADSIZE = 64
R_CHUNK = 64
R_GROUP = 4
R_SUB = 4
R_LN_EPS = 64e-5
C_HEADS = 8
C_CONV = 4
RG_C = 8.0

SCAN_OUT_DTYPE = BF16
VMEM_LIMIT_BYTES = 56 * 1024 * 1024
HALO = 8


def _cparams(*sem):
    return pltpu.CompilerParams(dimension_semantics=sem, vmem_limit_bytes=VMEM_LIMIT_BYTES)


def _bdot(a, b):
    return jnp.dot(a.astype(BF16), b.astype(BF16), preferred_element_type=F32)


def _bdot_nt(a, b):
    return lax.dot_general(a.astype(BF16), b.astype(BF16), (((1,), (1,)), ((), ())),
                           preferred_element_type=F32)


def _split3(x):
    hi = x.astype(BF16)
    r1 = x - hi.astype(F32)
    mid = r1.astype(BF16)
    lo = (r1 - mid.astype(F32)).astype(BF16)
    return hi, mid, lo


def _tri_cumsum(tri_bf16, x, parts=3):
    hi, mid, lo = _split3(x)
    dot = lambda p: jnp.dot(tri_bf16, p, preferred_element_type=F32)
    if parts == 2:
        return dot(hi) + dot(mid)
    return dot(hi) + (dot(mid) + dot(lo))


def _silu(x):
    return x * jax.nn.sigmoid(x)


def _softplus(x):
    return jnp.maximum(x, 0.0) + jnp.log1p(jnp.exp(-jnp.abs(x)))


def _adaln_kernel(cb_ref, w_ref, b_ref, o_ref, *, n_vec, tn):
    rows = []
    for v in range(n_vec):
        cols = []
        for j in range(tn // 128):
            wj = w_ref[:, j * 128:(j + 1) * 128]
            cols.append(jnp.sum(wj * cb_ref[v], axis=0, keepdims=True))
        rows.append(jnp.concatenate(cols, axis=1))
    pad = jnp.zeros((HALO - n_vec, tn), F32)
    o_ref[...] = jnp.concatenate(rows + [pad], axis=0) + b_ref[...]


def _adaln(cond, ada_w, ada_b):
    depth, d, n6 = ada_w.shape
    n_vec = cond.shape[0]
    tn = 1024
    cb = jnp.broadcast_to(_silu(cond)[:, :, None], (n_vec, d, 128))
    return pl.pallas_call(
        functools.partial(_adaln_kernel, n_vec=n_vec, tn=tn),
        grid=(depth, n6 // tn),
        in_specs=[pl.BlockSpec((n_vec, d, 128), lambda l, j: (0, 0, 0)),
                  pl.BlockSpec((None, d, tn), lambda l, j: (l, 0, j)),
                  pl.BlockSpec((None, 1, tn), lambda l, j: (l, 0, j))],
        out_specs=pl.BlockSpec((None, HALO, tn), lambda l, j: (l, 0, j)),
        out_shape=jax.ShapeDtypeStruct((depth, HALO, n6), F32),
        compiler_params=_cparams("arbitrary", "arbitrary"),
    )(cb, ada_w, ada_b.reshape(depth, 1, n6))


def _modulated(x_ref, nw_ref, sh_ref, sc_ref):
    x = x_ref[...]
    y = x * lax.rsqrt(jnp.mean(x * x, axis=-1, keepdims=True) + NORM_EPS)
    return (y * nw_ref[...]) * (1.0 + sc_ref[...]) + sh_ref[...]


def _norm_mm_kernel(x_ref, nw_ref, sh_ref, sc_ref, w_ref, o_ref, h_ref):
    @pl.when(pl.program_id(1) == 0)
    def _():
        h_ref[...] = _modulated(x_ref, nw_ref, sh_ref, sc_ref).astype(BF16)

    o_ref[...] = jnp.dot(h_ref[...], w_ref[...], preferred_element_type=F32).astype(o_ref.dtype)


def _norm_swiglu_kernel(x_ref, nw_ref, sh_ref, sc_ref, wg_ref, wu_ref, o_ref, h_ref):
    @pl.when(pl.program_id(1) == 0)
    def _():
        h_ref[...] = _modulated(x_ref, nw_ref, sh_ref, sc_ref).astype(BF16)

    h = h_ref[...]
    g = jnp.dot(h, wg_ref[...], preferred_element_type=F32)
    u = jnp.dot(h, wu_ref[...], preferred_element_type=F32)
    o_ref[...] = (_silu(g) * u).astype(o_ref.dtype)


def _row_tile(m, cap):
    t = cap
    while m % t:
        t //= 2
    return t


def _norm_mm(x, nw, shift, scale, weights, rows_per_vec, out_dtype, tm_cap=1024, tn=512):
    m, d = x.shape
    n = weights[0].shape[1]
    tm = _row_tile(rows_per_vec, tm_cap)
    vec_map = lambda i, j: ((i * tm) // rows_per_vec, 0, 0)
    kern = _norm_mm_kernel if len(weights) == 1 else _norm_swiglu_kernel
    return pl.pallas_call(
        kern,
        grid=(m // tm, n // tn),
        in_specs=[pl.BlockSpec((tm, d), lambda i, j: (i, 0)),
                  pl.BlockSpec((1, d), lambda i, j: (0, 0)),
                  pl.BlockSpec((None, 1, d), vec_map),
                  pl.BlockSpec((None, 1, d), vec_map)]
                 + [pl.BlockSpec((d, tn), lambda i, j: (0, j)) for _ in weights],
        out_specs=pl.BlockSpec((tm, tn), lambda i, j: (i, j)),
        out_shape=jax.ShapeDtypeStruct((m, n), out_dtype),
        scratch_shapes=[pltpu.VMEM((tm, d), BF16)],
        compiler_params=_cparams("arbitrary", "arbitrary"),
    )(x, nw.reshape(1, d), shift, scale, *weights)


def _mm_res_kernel(a_ref, w_ref, res_ref, g_ref, o_ref):
    acc = jnp.dot(a_ref[...], w_ref[...], preferred_element_type=F32)
    o_ref[...] = res_ref[...] + g_ref[...] * acc


def _mm_res(a, w, res, gate, rows_per_vec, tm_cap=1024, tn=512):
    m, k = a.shape
    n = w.shape[1]
    tm = _row_tile(rows_per_vec, tm_cap)
    return pl.pallas_call(
        _mm_res_kernel,
        grid=(m // tm, n // tn),
        in_specs=[pl.BlockSpec((tm, k), lambda i, j: (i, 0)),
                  pl.BlockSpec((k, tn), lambda i, j: (0, j)),
                  pl.BlockSpec((tm, tn), lambda i, j: (i, j)),
                  pl.BlockSpec((None, 1, tn), lambda i, j: ((i * tm) // rows_per_vec, 0, j))],
        out_specs=pl.BlockSpec((tm, tn), lambda i, j: (i, j)),
        out_shape=jax.ShapeDtypeStruct((m, n), F32),
        compiler_params=_cparams("arbitrary", "arbitrary"),
    )(a, w, res, gate)


def _final_norm_kernel(x_ref, w_ref, o_ref):
    x = x_ref[...]
    o_ref[...] = x * lax.rsqrt(jnp.mean(x * x, axis=-1, keepdims=True) + NORM_EPS) * w_ref[...]


def _final_norm(x, w):
    m, d = x.shape
    tm = _row_tile(m, 512)
    return pl.pallas_call(
        _final_norm_kernel,
        grid=(m // tm,),
        in_specs=[pl.BlockSpec((tm, d), lambda i: (i, 0)), pl.BlockSpec((1, d), lambda i: (0, 0))],
        out_specs=pl.BlockSpec((tm, d), lambda i: (i, 0)),
        out_shape=jax.ShapeDtypeStruct((m, d), F32),
        compiler_params=_cparams("arbitrary"),
    )(x, w.reshape(1, d))


def _scan_block(i, n_blocks, rev):
    return (n_blocks - 1 - i) if rev else i


def _halo_block(i, n_blocks, rows, rev):
    per = rows // HALO
    if rev:
        return jnp.minimum((_scan_block(i, n_blocks, True) + 1) * per, n_blocks * per - 1)
    return jnp.maximum(i * per - 1, 0)


def _stage_with_halo(scr_ref, x_ref, halo_ref, keep, rows, rev):
    if rev:
        scr_ref[0:rows, :] = x_ref[...]
        scr_ref[rows:rows + HALO, :] = halo_ref[...] * keep
    else:
        scr_ref[0:HALO, :] = halo_ref[...] * keep
        scr_ref[HALO:HALO + rows, :] = x_ref[...]


def _lagged(scr_ref, lag, rows, rev):
    start = lag if rev else HALO - lag
    return scr_ref[start:start + rows, :]


def _rwkv_kernel(rkv_ref, sm_ref, rkv_h_ref, sm_h_ref, mu_rkv_ref, mu_sm_ref, w0_ref, w2_ref,
                 a0_ref, a2_ref, kkw_ref, kaw_ref, rkw_ref, s0_ref,
                 y_ref, bonus_ref, sl_ref, s_scr, rkv_scr, sm_scr, *, rev, n_heads, lora, nsub):
    i = pl.program_id(1)
    c = R_CHUNK
    rows = nsub * c
    n = R_HEADSIZE
    hw = n_heads * n

    @pl.when(i == 0)
    def _():
        s_scr[...] = s0_ref[...]

    keep = (i > 0).astype(F32)
    _stage_with_halo(rkv_scr, rkv_ref, rkv_h_ref, keep, rows, rev)
    _stage_with_halo(sm_scr, sm_ref, sm_h_ref, keep, rows, rev)
    x = rkv_ref[...]
    f = x + (_lagged(rkv_scr, 1, rows, rev) - x) * mu_rkv_ref[...]
    xs = sm_ref[...]
    fs = xs + (_lagged(sm_scr, 1, rows, rev) - xs) * mu_sm_ref[...]
    r = f[:, :hw]
    k = f[:, hw:2 * hw]
    v = f[:, 2 * hw:3 * hw]
    wl = fs[:, 256:256 + lora]
    al = fs[:, 384:384 + lora]

    ww = w0_ref[...] + _bdot(jnp.tanh(wl), w2_ref[...])
    lw = -math.exp(-0.5) * jax.nn.sigmoid(ww)
    a = jax.nn.sigmoid(a0_ref[...] + _bdot(al, a2_ref[...]))
    kkr = k * kkw_ref[...]
    k2 = k * (1.0 + (a - 1.0) * kaw_ref[...])
    rk = r * k2 * rkw_ref[...]

    cshift = int(math.log2(c))
    ti = lax.broadcasted_iota(jnp.int32, (rows, rows), 0)
    tj = lax.broadcasted_iota(jnp.int32, (rows, rows), 1)
    same_chunk = (ti >> cshift) == (tj >> cshift)
    upto = same_chunk & ((tj >= ti) if rev else (tj <= ti))
    g = _tri_cumsum(upto.astype(BF16), lw, parts=2)
    rsl = [slice(k * c, (k + 1) * c) for k in range(nsub)]
    gtot = [g[k * c:k * c + 1, :] if rev else g[(k + 1) * c - 1:(k + 1) * c, :] for k in range(nsub)]
    eg = jnp.exp(g)
    eneg = jnp.exp(-g)
    egp = jnp.exp(g - lw)
    etail = jnp.exp(jnp.concatenate([jnp.broadcast_to(t, (c, hw)) for t in gtot], axis=0) - g)
    dg = [jnp.exp(t) for t in gtot]
    rd_all = r * eg
    kp_all = k2 * eneg
    kpp_all = k2 * etail

    gw = R_GROUP * n
    groups = range(hw // gw)
    gsl = [slice(q * gw, (q + 1) * gw) for q in groups]
    shift = int(math.log2(n))
    bi = lax.broadcasted_iota(jnp.int32, (gw, gw), 0) >> shift
    bj = lax.broadcasted_iota(jnp.int32, (gw, gw), 1) >> shift
    same_head = bi == bj
    ones_bd = same_head.astype(BF16)
    lane_head = lax.broadcasted_iota(jnp.int32, (1, 128), 1) >> shift
    keep_f32 = [(lane_head == j).astype(F32) for j in range(128 // n)]
    keep_lanes = [t.astype(BF16) for t in keep_f32]

    def bdiag(t):
        tb = t.astype(BF16)
        zero = jnp.zeros((c, 128), BF16)
        rows = []
        for h in range(R_GROUP):
            lt, j = divmod(h * n, 128)
            piece = tb[:, lt * 128:(lt + 1) * 128] * keep_lanes[j // n]
            rows.append(jnp.concatenate([piece if u == lt else zero for u in range(gw // 128)], axis=1))
        return jnp.concatenate(rows, axis=0)

    def diag_blocks(full):
        tiles = []
        for lt in range(gw // 128):
            acc = None
            for j in range(128 // n):
                h = lt * (128 // n) + j
                part = full[h * n:(h + 1) * n, lt * 128:(lt + 1) * 128] * keep_f32[j]
                acc = part if acc is None else acc + part
            tiles.append(acc)
        return jnp.concatenate(tiles, axis=1)

    def head_sums(t):
        hi, mid, _ = _split3(t)
        dot = lambda p: jnp.dot(p, ones_bd, preferred_element_type=F32)
        return dot(hi) + dot(mid)

    tdot = lambda p, q: lax.dot_general(p.astype(BF16), q.astype(BF16), (((0,), (0,)), ((), ())),
                                        preferred_element_type=F32)
    gi = lax.broadcasted_iota(jnp.int32, (c, gw), 0)
    gj = lax.broadcasted_iota(jnp.int32, (c, gw), 1) & (c - 1)
    before_g = (gj > gi) if rev else (gj < gi)
    upto_g = (gj >= gi) if rev else (gj <= gi)
    eye_g = (gi == gj).astype(F32)

    kk_all = jnp.concatenate(
        [kkr[:, s] * lax.rsqrt(jnp.maximum(head_sums(kkr[:, s] * kkr[:, s]), 1e-24)) for s in gsl], axis=1)
    bh_all = kk_all * a
    kkd_all = kk_all * egp
    bp_all = bh_all * eneg
    bpp_all = bh_all * etail
    bonus_ref[...] = (jnp.concatenate([head_sums(rk[:, s]) for s in gsl], axis=1) * v
                      ).astype(bonus_ref.dtype)

    units = [(k, q) for k in range(nsub) for q in groups]
    un = range(len(units))
    cut = lambda t: [t[rsl[k], gsl[q]] for k, q in units]
    kkd, bp, bpp, rd, vq, kp, kpp = (cut(t) for t in (kkd_all, bp_all, bpp_all, rd_all, v, kp_all, kpp_all))
    amat = [_bdot_nt(jnp.concatenate([kkd[u], rd[u]], axis=0),
                     jnp.concatenate([bdiag(kp[u]), bdiag(bp[u])], axis=0)) for u in un]
    mk = [jnp.where(before_g, m[:c, :gw], 0.0) for m in amat]
    nk = [jnp.where(upto_g, m[c:, :gw], 0.0) for m in amat]
    nb = [jnp.where(upto_g, m[c:, gw:], 0.0) for m in amat]
    p2 = [jnp.where(before_g, -m[:c, gw:], 0.0) for m in amat]
    tinv = [eye_g + t for t in p2]
    p2 = [_bdot(t, bdiag(t)) for t in p2]
    for _ in range(cshift - 2):
        sq = [_bdot(jnp.concatenate([tinv[u], p2[u]], axis=0), bdiag(p2[u])) for u in un]
        tinv = [tinv[u] + sq[u][:c] for u in un]
        p2 = [t[c:] for t in sq]
    tinv = [tinv[u] + _bdot(tinv[u], bdiag(p2[u])) for u in un]
    mnv = [_bdot(jnp.concatenate([mk[u], nk[u]], axis=0), bdiag(vq[u])) for u in un]
    wu = [_bdot(tinv[u], jnp.concatenate([bdiag(kkd[u]), bdiag(mnv[u][:c])], axis=1))
          for u in un]
    nwu = [_bdot(nb[u], jnp.concatenate([bdiag(wu[u][:, :gw]), bdiag(wu[u][:, gw:])], axis=1))
           for u in un]
    wub = [tdot(wu[u], bpp[u]) for u in un]
    pmat = [bdiag(diag_blocks(t[:gw])) for t in wub]
    dmat = [diag_blocks(tdot(vq[u], kpp[u])) - diag_blocks(wub[u][gw:]) for u in un]
    qt = [rd[u] - nwu[u][:, :gw] for u in un]
    y0 = [mnv[u][c:] - nwu[u][:, gw:] for u in un]
    state = [s_scr[q] for q in groups]
    for k in (reversed(range(nsub)) if rev else range(nsub)):
        for q in groups:
            u = k * len(groups) + q
            y_ref[rsl[k], gsl[q]] = (y0[u] + _bdot_nt(qt[u], bdiag(state[q]))).astype(y_ref.dtype)
            state[q] = state[q] * dg[k][:, gsl[q]] - _bdot(state[q], pmat[u]) + dmat[u]
    for q in groups:
        s_scr[q] = state[q]

    @pl.when(i == pl.num_programs(1) - 1)
    def _():
        sl_ref[...] = s_scr[...]


def _rwkv_direction(pr, s0, p, rev):
    bsz, seqlen, width = pr.shape
    hw = s0.shape[1] * s0.shape[3]
    n_heads = hw // R_HEADSIZE
    nsub = R_SUB if seqlen % (R_SUB * R_CHUNK) == 0 else 1
    c = nsub * R_CHUNK
    nblk = seqlen // c
    sm_col = width // 512 - 1
    blk = lambda i: _scan_block(i, nblk, rev)
    halo = lambda i: _halo_block(i, nblk, c, rev)
    vec = lambda w: pl.BlockSpec((1, w), lambda b, i: (0, 0))
    lora = p['w2'].shape[0]
    state_spec = pl.BlockSpec((None,) + s0.shape[1:], lambda b, i: (b, 0, 0, 0))
    return pl.pallas_call(
        functools.partial(_rwkv_kernel, rev=rev, n_heads=n_heads, lora=lora, nsub=nsub),
        grid=(bsz, nblk),
        in_specs=[pl.BlockSpec((None, c, 3 * hw), lambda b, i: (b, blk(i), 1)),
                  pl.BlockSpec((None, c, 512), lambda b, i: (b, blk(i), sm_col)),
                  pl.BlockSpec((None, HALO, 3 * hw), lambda b, i: (b, halo(i), 1)),
                  pl.BlockSpec((None, HALO, 512), lambda b, i: (b, halo(i), sm_col)),
                  vec(3 * hw), vec(512), vec(hw),
                  pl.BlockSpec((lora, hw), lambda b, i: (0, 0)),
                  vec(hw),
                  pl.BlockSpec((lora, hw), lambda b, i: (0, 0)),
                  vec(hw), vec(hw), vec(hw),
                  state_spec],
        out_specs=[pl.BlockSpec((None, c, hw), lambda b, i: (b, blk(i), 0)),
                   pl.BlockSpec((None, c, hw), lambda b, i: (b, blk(i), 0)),
                   state_spec],
        out_shape=[jax.ShapeDtypeStruct((bsz, seqlen, hw), SCAN_OUT_DTYPE),
                   jax.ShapeDtypeStruct((bsz, seqlen, hw), SCAN_OUT_DTYPE),
                   jax.ShapeDtypeStruct(s0.shape, F32)],
        scratch_shapes=[pltpu.VMEM(s0.shape[1:], F32),
                        pltpu.VMEM((c + HALO, 3 * hw), F32),
                        pltpu.VMEM((c + HALO, 512), F32)],
        compiler_params=_cparams("arbitrary", "arbitrary"),
    )(pr, pr, pr, pr, p['mu_rkv'], p['mu_sm'], p['w0'], p['w2'], p['a0'], p['a2'],
      p['kk'], p['ka'], p['rk'], s0)


def _expand_cols(x, e_bf16):
    hi, mid, lo = _split3(x)
    dot = lambda p: jnp.dot(p, e_bf16, preferred_element_type=F32)
    return dot(hi) + (dot(mid) + dot(lo))


def _causal_conv(scr_ref, w_ref, b_ref, rows, taps, rev):
    acc = b_ref[...] + w_ref[taps - 1:taps, :] * _lagged(scr_ref, 0, rows, rev)
    for kk in range(taps - 1):
        acc = acc + w_ref[kk:kk + 1, :] * _lagged(scr_ref, taps - 1 - kk, rows, rev)
    return acc


def _ssd_kernel(xbc_ref, halo_ref, sm_ref, dtt_ref, cw_ref, cb_ref, dtb_row_ref, dtb_col_ref,
                alog_row_ref, alog_col_ref, dskip_ref, e64_ref, hmask_ref, h0_ref,
                y_ref, hl_ref, h_scr, x_scr, *, rev, n_heads):
    i = pl.program_id(1)
    c = M_CHUNK
    p = M_HEADDIM
    ns = M_STATE
    mix = n_heads * p
    hpg = n_heads // M_GROUPS
    gw = hpg * p

    @pl.when(i == 0)
    def _():
        h_scr[...] = h0_ref[...]

    keep = (i > 0).astype(F32)
    _stage_with_halo(x_scr, xbc_ref, halo_ref, keep, c, rev)
    xbc = _silu(_causal_conv(x_scr, cw_ref, cb_ref, c, M_CONV, rev))
    xs = xbc[:, :mix]
    bm = [xbc[:, mix + g * ns:mix + (g + 1) * ns] for g in range(M_GROUPS)]
    cm = [xbc[:, mix + (M_GROUPS + g) * ns:mix + (M_GROUPS + g + 1) * ns] for g in range(M_GROUPS)]

    ti = lax.broadcasted_iota(jnp.int32, (c, c), 0)
    tj = lax.broadcasted_iota(jnp.int32, (c, c), 1)
    upto = (tj >= ti) if rev else (tj <= ti)
    upto_t = (ti >= tj) if rev else (ti <= tj)
    end = 0 if rev else c - 1

    dt = _softplus(sm_ref[:, 480:480 + n_heads] + dtb_row_ref[...])
    a_cs = _tri_cumsum(upto.astype(BF16), dt * -jnp.exp(alog_row_ref[...]))
    dt_t = _softplus(dtt_ref[...] + dtb_col_ref[...])
    a_cs_t = _expand_cols(dt_t * -jnp.exp(alog_col_ref[...]), upto_t.astype(BF16))
    a_end_t = a_cs_t[:, end:end + 1]
    w_t = dt_t * jnp.exp(a_end_t - a_cs_t)
    grow_t = jnp.broadcast_to(jnp.exp(a_end_t), (n_heads, ns))

    parts = [t.astype(F32) for t in _split3(a_cs)]
    ones = jnp.ones((c, n_heads), F32)
    lhs = jnp.concatenate(parts + [ones] * 3, axis=1).astype(BF16)
    hmask = hmask_ref[...]
    parts_t = jnp.concatenate(list(_split3(a_cs_t)), axis=0)
    rhs = jnp.concatenate([hmask, -jnp.concatenate([parts_t] * n_heads, axis=1) * hmask], axis=0)
    seg = jnp.dot(lhs, rhs, preferred_element_type=F32)
    ea_full = jnp.exp(jnp.dot(lhs[:, :3 * n_heads], e64_ref[...], preferred_element_type=F32))

    lane_head = lax.broadcasted_iota(jnp.int32, (1, gw), 1) >> int(math.log2(p))
    keep_lanes = [(lane_head == j).astype(BF16) for j in range(hpg)]
    xs_t = jnp.transpose(xs)
    y_diag, y_off = [], []
    for g in range(M_GROUPS):
        heads = range(g * hpg, (g + 1) * hpg)
        cb = _bdot_nt(cm[g], bm[g])
        scores = []
        for h in heads:
            decay = jnp.where(upto, jnp.exp(jnp.minimum(seg[:, h * c:(h + 1) * c], 0.0)), 0.0)
            scores.append((cb * decay * dt_t[h:h + 1, :]).astype(BF16))
        xg = xs[:, g * gw:(g + 1) * gw].astype(BF16)
        x_bd = jnp.concatenate([xg * keep_lanes[j] for j in range(hpg)], axis=0)
        y_diag.append(jnp.dot(jnp.concatenate(scores, axis=1), x_bd, preferred_element_type=F32))
        h_in = h_scr[g]
        y_off.append(_bdot_nt(cm[g], h_in))
        wrows = jnp.concatenate([jnp.broadcast_to(w_t[h:h + 1, :], (p, c)) for h in heads], axis=0)
        grow = jnp.concatenate([jnp.broadcast_to(grow_t[h:h + 1, :], (p, ns)) for h in heads], axis=0)
        h_scr[g] = h_in * grow + _bdot(xs_t[g * gw:(g + 1) * gw, :] * wrows, bm[g])
    y_ref[...] = (jnp.concatenate(y_diag, axis=1) + jnp.concatenate(y_off, axis=1) * ea_full
                  + xs * dskip_ref[...]).astype(y_ref.dtype)

    @pl.when(i == pl.num_programs(1) - 1)
    def _():
        hl_ref[...] = h_scr[...]


def _ssd_direction(pr, dt_t, h0, p, rev):
    bsz, seqlen, width = pr.shape
    n_heads = dt_t.shape[1]
    mix = n_heads * M_HEADDIM
    xw = mix + 2 * M_GROUPS * M_STATE
    c = M_CHUNK
    nblk = seqlen // c
    sm_col = width // 512 - 1
    blk = lambda i: _scan_block(i, nblk, rev)
    halo = lambda i: _halo_block(i, nblk, c, rev)
    full = lambda a: pl.BlockSpec(a.shape, lambda b, i: (0,) * a.ndim)
    consts = [p['conv_w'], p['conv_b'], p['dtb_row'], p['dtb_col'], p['alog_row'], p['alog_col'],
              p['dskip'], p['e64'], p['hmask']]
    state_spec = pl.BlockSpec((None,) + h0.shape[1:], lambda b, i: (b, 0, 0, 0))
    return pl.pallas_call(
        functools.partial(_ssd_kernel, rev=rev, n_heads=n_heads),
        grid=(bsz, nblk),
        in_specs=[pl.BlockSpec((None, c, xw), lambda b, i: (b, blk(i), 0)),
                  pl.BlockSpec((None, HALO, xw), lambda b, i: (b, halo(i), 0)),
                  pl.BlockSpec((None, c, 512), lambda b, i: (b, blk(i), sm_col)),
                  pl.BlockSpec((None, n_heads, c), lambda b, i: (b, 0, blk(i)))]
                 + [full(a) for a in consts] + [state_spec],
        out_specs=[pl.BlockSpec((None, c, mix), lambda b, i: (b, blk(i), 0)), state_spec],
        out_shape=[jax.ShapeDtypeStruct((bsz, seqlen, mix), SCAN_OUT_DTYPE),
                   jax.ShapeDtypeStruct(h0.shape, F32)],
        scratch_shapes=[pltpu.VMEM(h0.shape[1:], F32), pltpu.VMEM((c + HALO, xw), F32)],
        compiler_params=_cparams("arbitrary", "arbitrary"),
    )(pr, pr, pr, dt_t, *consts, h0)


def _rglru_kernel(x_ref, halo_ref, cw_ref, cb_ref, wa_ref, ba_ref, wx_ref, bx_ref, lam_ref, h0_ref,
                  o_ref, hl_ref, h_scr, x_scr, a_scr, u_scr, *, rev, tb):
    i = pl.program_id(1)

    @pl.when(i == 0)
    def _():
        h_scr[...] = h0_ref[...]

    keep = (i > 0).astype(F32)
    _stage_with_halo(x_scr, x_ref, halo_ref, keep, tb, rev)
    xc = _causal_conv(x_scr, cw_ref, cb_ref, tb, C_CONV, rev)
    a_scr[...], u_scr[...] = _rglru_gates(xc, wa_ref, ba_ref, wx_ref, bx_ref, lam_ref)

    def step(t, h):
        tt = (tb - 1 - t) if rev else t
        h = a_scr[pl.ds(tt, 1), :] * h + u_scr[pl.ds(tt, 1), :]
        o_ref[pl.ds(tt, 1), :] = h
        return h

    h_scr[0:1, :] = lax.fori_loop(0, tb, step, h_scr[0:1, :], unroll=8)

    @pl.when(i == pl.num_programs(1) - 1)
    def _():
        hl_ref[...] = h_scr[...]


def _rglru_direction(xv, h0, p, rev, tb):
    bsz, rows, wtot = xv.shape
    width = h0.shape[2]
    ncol = wtot // (2 * width)
    nrb = rows // tb
    total = ncol * nrb
    per = tb // HALO

    def pos(i):
        s = _scan_block(i, total, rev)
        return s // nrb, s % nrb

    def x_map(b, i):
        col, q = pos(i)
        return b, q, 2 * col + 1

    def halo_map(b, i):
        s = _scan_block(i, total, rev)
        sp = jnp.minimum(s + 1, total - 1) if rev else jnp.maximum(s - 1, 0)
        col, q = sp // nrb, sp % nrb
        return b, (q * per if rev else (q + 1) * per - 1), 2 * col + 1

    def o_map(b, i):
        col, q = pos(i)
        return b, q, col

    full = lambda a: pl.BlockSpec(a.shape, lambda b, i: (0,) * a.ndim)
    consts = [p['conv_w'], p['conv_b'], p['wa'], p['ba'], p['wx'], p['bx'], p['lam']]
    state_spec = pl.BlockSpec((None, HALO, width), lambda b, i: (b, 0, 0))
    return pl.pallas_call(
        functools.partial(_rglru_kernel, rev=rev, tb=tb),
        grid=(bsz, total),
        in_specs=[pl.BlockSpec((None, tb, width), x_map), pl.BlockSpec((None, HALO, width), halo_map)]
                 + [full(a) for a in consts] + [state_spec],
        out_specs=[pl.BlockSpec((None, tb, width), o_map), state_spec],
        out_shape=[jax.ShapeDtypeStruct((bsz, rows, ncol * width), F32),
                   jax.ShapeDtypeStruct(h0.shape, F32)],
        scratch_shapes=[pltpu.VMEM((HALO, width), F32), pltpu.VMEM((tb + HALO, width), F32),
                        pltpu.VMEM((tb, width), F32), pltpu.VMEM((tb, width), F32)],
        compiler_params=_cparams("arbitrary", "arbitrary"),
    )(xv, xv, *consts, h0)


def _rglru_gates(xc, wa_ref, ba_ref, wx_ref, bx_ref, lam_ref):
    blk = xc.shape[1] // C_HEADS
    xh = [xc[:, h * blk:(h + 1) * blk] for h in range(C_HEADS)]
    gate_r = jax.nn.sigmoid(
        jnp.concatenate([_bdot(xh[h], wa_ref[h]) for h in range(C_HEADS)], axis=1) + ba_ref[...])
    gate_i = jax.nn.sigmoid(
        jnp.concatenate([_bdot(xh[h], wx_ref[h]) for h in range(C_HEADS)], axis=1) + bx_ref[...])
    log_a = -RG_C * gate_r * _softplus(-lam_ref[...])
    a = jnp.exp(log_a)
    one_m_a2 = jnp.maximum(-jnp.tanh(log_a) * (a * a + 1.0), 0.0)
    mult = jnp.where(one_m_a2 > 0.0, one_m_a2 * lax.rsqrt(one_m_a2), 0.0)
    return a, mult * gate_i * xc


def _rglru_grid_kernel(x_ref, e0_ref, e1_ref, e2_ref, cw_ref, cb_ref, wa_ref, ba_ref, wx_ref,
                       bx_ref, lam_ref, h0_ref, hloc_ref, aprod_ref, cin_ref, hl_ref,
                       ring_scr, h_scr, ap_scr, stage_scr, *, rev, ncol):
    i = pl.program_id(1)

    def from_previous_column(e_ref):
        _stage_with_halo(stage_scr, e_ref, jnp.zeros((HALO, e_ref.shape[1]), F32), 0.0, ncol, rev)
        return _lagged(stage_scr, 1, ncol, rev)

    @pl.when(i == 0)
    def _():
        ring_scr[0] = from_previous_column(e2_ref)
        ring_scr[1] = from_previous_column(e1_ref)
        ring_scr[2] = from_previous_column(e0_ref)
        h_scr[...] = jnp.zeros(h_scr.shape, F32)
        ap_scr[...] = jnp.ones(ap_scr.shape, F32)

    x0 = x_ref[...]
    lag1, lag2, lag3 = ring_scr[0], ring_scr[1], ring_scr[2]
    xc = (cb_ref[...] + cw_ref[3:4, :] * x0 + cw_ref[2:3, :] * lag1
          + cw_ref[1:2, :] * lag2 + cw_ref[0:1, :] * lag3)
    ring_scr[2] = lag2
    ring_scr[1] = lag1
    ring_scr[0] = x0
    a, u = _rglru_gates(xc, wa_ref, ba_ref, wx_ref, bx_ref, lam_ref)
    h = a * h_scr[...] + u
    ap = a * ap_scr[...]
    h_scr[...] = h
    ap_scr[...] = ap
    hloc_ref[...] = h.astype(hloc_ref.dtype)
    aprod_ref[...] = ap.astype(aprod_ref.dtype)

    @pl.when(i == pl.num_programs(1) - 1)
    def _():
        def step(t, carry):
            col = (ncol - 1 - t) if rev else t
            cin_ref[pl.ds(col, 1), :] = carry
            return h_scr[pl.ds(col, 1), :] + ap_scr[pl.ds(col, 1), :] * carry

        last = lax.fori_loop(0, ncol, step, h0_ref[0:1, :], unroll=8)
        hl_ref[...] = jnp.zeros(hl_ref.shape, F32)
        hl_ref[0:1, :] = last


def _rglru_grid_direction(p3, h0, p, rev):
    bsz, seqlen, _ = p3.shape
    width = h0.shape[2]
    ncol = GRID_W
    rows = seqlen // ncol
    blk = lambda i: _scan_block(i, rows, rev)
    edge = lambda k: pl.BlockSpec((None, ncol, width),
                                  lambda b, i: (b, (2 - k) if rev else (rows - 3 + k), 1))
    full = lambda a: pl.BlockSpec(a.shape, lambda b, i: (0,) * a.ndim)
    consts = [p['conv_w'], p['conv_b'], p['wa'], p['ba'], p['wx'], p['bx'], p['lam']]
    state_spec = pl.BlockSpec((None, HALO, width), lambda b, i: (b, 0, 0))
    row_spec = pl.BlockSpec((None, ncol, width), lambda b, i: (b, blk(i), 0))
    return pl.pallas_call(
        functools.partial(_rglru_grid_kernel, rev=rev, ncol=ncol),
        grid=(bsz, rows),
        in_specs=[pl.BlockSpec((None, ncol, width), lambda b, i: (b, blk(i), 1)),
                  edge(0), edge(1), edge(2)] + [full(a) for a in consts] + [state_spec],
        out_specs=[row_spec, row_spec,
                   pl.BlockSpec((None, ncol, width), lambda b, i: (b, 0, 0)), state_spec],
        out_shape=[jax.ShapeDtypeStruct((bsz, seqlen, width), SCAN_OUT_DTYPE),
                   jax.ShapeDtypeStruct((bsz, seqlen, width), SCAN_OUT_DTYPE),
                   jax.ShapeDtypeStruct((bsz, ncol, width), F32),
                   jax.ShapeDtypeStruct(h0.shape, F32)],
        scratch_shapes=[pltpu.VMEM((3, ncol, width), F32), pltpu.VMEM((ncol, width), F32),
                        pltpu.VMEM((ncol, width), F32), pltpu.VMEM((ncol + HALO, width), F32)],
        compiler_params=_cparams("arbitrary", "arbitrary"),
    )(p3, p3, p3, p3, *consts, h0)


def _readout_ab_kernel(z_ref, sm_ref, ymf_ref, ymb_ref, yrf_ref, yrb_ref, bof_ref, bob_ref,
                       nw_ref, lw_ref, lb_ref, g2_ref, o_ref, *, n_heads):
    f32 = lambda ref: ref[...].astype(F32)
    t = (f32(ymf_ref) + f32(ymb_ref)) * _silu(z_ref[...])
    gw = t.shape[1] // M_GROUPS
    a_out = []
    for g in range(M_GROUPS):
        tg = t[:, g * gw:(g + 1) * gw]
        a_out.append(tg * lax.rsqrt(jnp.mean(tg * tg, axis=-1, keepdims=True) + M_NORM_EPS))
    a_out = jnp.concatenate(a_out, axis=1) * nw_ref[...]

    yr = f32(yrf_ref) + f32(yrb_ref)
    n = R_HEADSIZE
    yn = []
    for h in range(n_heads):
        yh = yr[:, h * n:(h + 1) * n]
        dlt = yh - jnp.mean(yh, axis=-1, keepdims=True)
        yn.append(dlt * lax.rsqrt(jnp.mean(dlt * dlt, axis=-1, keepdims=True) + R_LN_EPS))
    yn = jnp.concatenate(yn, axis=1) * lw_ref[...] + lb_ref[...]
    gate = _bdot(jax.nn.sigmoid(sm_ref[:, 0:g2_ref.shape[0]]), g2_ref[...])
    b_out = (yn + f32(bof_ref) + f32(bob_ref)) * gate
    o_ref[...] = jnp.concatenate([a_out, b_out], axis=1).astype(o_ref.dtype)


def _readout_ab(pr, ym, yr, bo, ro, n_heads):
    bsz, seqlen, width = pr.shape
    m = bsz * seqlen
    mix = ym[0].shape[2]
    tm = _row_tile(m, 256)
    flat = lambda t: t.reshape(m, t.shape[2])
    rowblk = lambda w, col: pl.BlockSpec((tm, w), lambda i: (i, col))
    full = lambda a: pl.BlockSpec(a.shape, lambda i: (0,) * a.ndim)
    consts = [ro['norm_w'], ro['lnx_w'], ro['lnx_b'], ro['g2']]
    return pl.pallas_call(
        functools.partial(_readout_ab_kernel, n_heads=n_heads),
        grid=(m // tm,),
        in_specs=[rowblk(mix, 2), rowblk(512, width // 512 - 1)] + [rowblk(mix, 0)] * 6
                 + [full(a) for a in consts],
        out_specs=rowblk(2 * mix, 0),
        out_shape=jax.ShapeDtypeStruct((m, 2 * mix), BF16),
        compiler_params=_cparams("arbitrary"),
    )(flat(pr), flat(pr), flat(ym[0]), flat(ym[1]), flat(yr[0]), flat(yr[1]), flat(bo[0]),
      flat(bo[1]), *consts)


def _gelu_tanh(g):
    return 0.5 * g * (1.0 + jnp.tanh(math.sqrt(2.0 / math.pi) * (g + 0.044715 * (g * g * g))))


def _readout_c_kernel(gy_ref, hf_ref, hb_ref, o_ref):
    o_ref[...] = ((hf_ref[...] + hb_ref[...]) * _gelu_tanh(gy_ref[...])).astype(o_ref.dtype)


def _readout_grid_kernel(gy_ref, hf_ref, af_ref, cf_ref, hb_ref, ab_ref, cb_ref, o_ref, *, ncol):
    for j in range(o_ref.shape[0] // ncol):
        rs = slice(j * ncol, (j + 1) * ncol)
        f32 = lambda ref: ref[rs, :].astype(F32)
        h = (f32(hf_ref) + f32(af_ref) * cf_ref[...]) + (f32(hb_ref) + f32(ab_ref) * cb_ref[...])
        o_ref[rs, :] = (h * _gelu_tanh(gy_ref[rs, :])).astype(o_ref.dtype)


def _readout_grid(p3, fwd, bwd):
    bsz, seqlen, _ = p3.shape
    cw = fwd[0].shape[2]
    ncol = fwd[2].shape[1]
    tm = _row_tile(seqlen, 8 * ncol)
    spec = pl.BlockSpec((None, tm, cw), lambda b, i: (b, i, 0))
    cspec = pl.BlockSpec((None, ncol, cw), lambda b, i: (b, 0, 0))
    return pl.pallas_call(
        functools.partial(_readout_grid_kernel, ncol=ncol),
        grid=(bsz, seqlen // tm),
        in_specs=[spec, spec, spec, cspec, spec, spec, cspec],
        out_specs=spec,
        out_shape=jax.ShapeDtypeStruct((bsz, seqlen, cw), BF16),
        compiler_params=_cparams("arbitrary", "arbitrary"),
    )(p3, *fwd, *bwd)


def _readout_c(p, hf, hb):
    m, cw = hf.shape
    tm = _row_tile(m, 512)
    spec = pl.BlockSpec((tm, cw), lambda i: (i, 0))
    return pl.pallas_call(
        _readout_c_kernel,
        grid=(m // tm,),
        in_specs=[spec, spec, spec],
        out_specs=spec,
        out_shape=jax.ShapeDtypeStruct((m, cw), BF16),
        compiler_params=_cparams("arbitrary"),
    )(p, hf, hb)


def kernel(x, c, ctx, c_ctx, ada_w, ada_b, norm1_w, norm2_w, ffn_w_gate, ffn_w_up, ffn_w_down, final_norm_w, ab_w_in, ab_w_out, m_conv_w, m_conv_b, m_dt_bias, m_a_log, m_d, m_norm_w, r_mu, r_w0, r_w2, r_a0, r_a2, r_kk, r_ka, r_rk, r_g2, r_lnx_w, r_lnx_b, c_w_in, c_w_out, c_conv_w, c_conv_b, c_wa, c_ba, c_wx, c_bx, c_lambda):
    bsz, seqlen, d = x.shape
    ctx_len = ctx.shape[1]
    depth = ada_w.shape[0]
    n_lat, n_ctx = bsz * seqlen, bsz * ctx_len
    row = lambda t: t.reshape(1, -1)

    mod = _adaln(jnp.concatenate([c_ctx[None, :], c], axis=0), ada_w, ada_b)

    xl = x.reshape(n_lat, d)
    xc = ctx.reshape(n_ctx, d)
    mix_a = m_norm_w.shape[1]
    m_heads = m_dt_bias.shape[2]
    r_heads = r_rk.shape[2]
    lora = r_w2.shape[2]
    e64 = jnp.tile(jnp.repeat(jnp.eye(m_heads, dtype=BF16), M_HEADDIM, axis=1), (3, 1))
    hmask = jnp.tile(jnp.repeat(jnp.eye(m_heads, dtype=BF16), M_CHUNK, axis=1), (3, 1))

    for layer in range(depth):
        with_ctx = layer < depth - 1
        part = lambda k: mod[layer, :, k * d:(k + 1) * d]
        lat_vec = lambda k: part(k)[1:1 + bsz, None, :]
        ctx_vec = lambda k: part(k)[0:1, None, :]
        nw1, nw2 = norm1_w[layer], norm2_w[layer]

        if layer % 2 == 0:
            e = layer // 2
            w = ab_w_in[e]
            o1 = mix_a
            o2 = o1 + mix_a + 2 * M_GROUPS * M_STATE
            o3 = o2 + m_heads
            o4 = o3 + 3 * r_heads * R_HEADSIZE + 2 * lora
            rkv_w = 3 * r_heads * R_HEADSIZE
            zeros = lambda n: jnp.zeros((d, n), w.dtype)
            w_in = jnp.concatenate(
                [w[:, o1:o2], w[:, :o1], w[:, o3:o3 + rkv_w], w[:, o4:],
                 w[:, o3 + rkv_w:o3 + rkv_w + lora], zeros(128 - lora),
                 w[:, o3 + rkv_w + lora:o4], w[:, o2:o3], zeros(128 - lora - m_heads)],
                axis=1).astype(BF16)
            width = w_in.shape[1]
            w_out = ab_w_out[e].astype(BF16)
            g2w = r_g2[e].astype(BF16)

            def project(xs, sh, sc, rows_per_vec, seg_len):
                pr = _norm_mm(xs, nw1, sh, sc, [w_in], rows_per_vec, F32)
                pr = pr.reshape(bsz, seg_len, width)
                dt_t = jnp.swapaxes(pr[:, :, width - 32:width - 32 + m_heads], 1, 2)
                return pr, dt_t

            pr_l, dtt_l = project(xl, lat_vec(0), lat_vec(1), seqlen, seqlen)
            pr_c, dtt_c = project(xc, ctx_vec(0), ctx_vec(1), n_ctx, ctx_len)

            ym_l = ym_c = yr_l = yr_c = bo_l = bo_c = None
            for dr in range(2):
                rev = dr == 1
                mp = dict(conv_w=m_conv_w[e, dr], conv_b=row(m_conv_b[e, dr]),
                          dtb_row=row(m_dt_bias[e, dr]), dtb_col=m_dt_bias[e, dr].reshape(-1, 1),
                          alog_row=row(m_a_log[e, dr]), alog_col=m_a_log[e, dr].reshape(-1, 1),
                          dskip=row(jnp.repeat(m_d[e, dr], M_HEADDIM)), e64=e64, hmask=hmask)
                h0 = jnp.zeros((bsz, M_GROUPS, m_heads // M_GROUPS * M_HEADDIM, M_STATE), F32)
                y_c, h_c = _ssd_direction(pr_c, dtt_c, h0, mp, rev)
                y_l, _ = _ssd_direction(pr_l, dtt_l, h_c, mp, rev)
                mu = r_mu[e, dr]
                mu_sm = (jnp.zeros((512,), F32).at[256:256 + lora].set(mu[rkv_w:rkv_w + lora])
                         .at[384:384 + lora].set(mu[rkv_w + lora:]))
                rp = dict(mu_rkv=row(mu[:rkv_w]), mu_sm=row(mu_sm), w0=row(r_w0[e, dr]),
                          w2=r_w2[e, dr].astype(BF16), a0=row(r_a0[e, dr]), a2=r_a2[e, dr].astype(BF16),
                          kk=row(r_kk[e, dr]), ka=row(r_ka[e, dr]), rk=row(r_rk[e, dr]))
                s0 = jnp.zeros((bsz, r_heads // R_GROUP, R_HEADSIZE, R_GROUP * R_HEADSIZE), F32)
                v_c, b_c, s_c = _rwkv_direction(pr_c, s0, rp, rev)
                v_l, b_l, _ = _rwkv_direction(pr_l, s_c, rp, rev)
                if dr == 0:
                    ym_l, ym_c, yr_l, yr_c, bo_l, bo_c = y_l, y_c, v_l, v_c, b_l, b_c
                else:
                    ym_l, yr_l, bo_l = (ym_l, y_l), (yr_l, v_l), (bo_l, b_l)
                    ym_c, yr_c, bo_c = (ym_c, y_c), (yr_c, v_c), (bo_c, b_c)

            ro = dict(norm_w=row(m_norm_w[e]), lnx_w=row(r_lnx_w[e]), lnx_b=row(r_lnx_b[e]), g2=g2w)
            act_l = _readout_ab(pr_l, ym_l, yr_l, bo_l, ro, r_heads)
            xl = _mm_res(act_l, w_out, xl, lat_vec(2), seqlen)
            if with_ctx:
                act_c = _readout_ab(pr_c, ym_c, yr_c, bo_c, ro, r_heads)
                xc = _mm_res(act_c, w_out, xc, ctx_vec(2), n_ctx)
        else:
            o = layer // 2
            w_in = c_w_in[o].astype(BF16)
            w_out = c_w_out[o].astype(BF16)
            cw = w_out.shape[0]
            p_l = _norm_mm(xl, nw1, lat_vec(0), lat_vec(1), [w_in], seqlen, F32)
            p_c = _norm_mm(xc, nw1, ctx_vec(0), ctx_vec(1), [w_in], n_ctx, F32)
            p3_l = p_l.reshape(bsz, seqlen, 2 * cw)
            hs_l, hs_c = [], []
            for dr in range(2):
                rev = dr == 1
                cp = dict(conv_w=c_conv_w[o, dr], conv_b=row(c_conv_b[o, dr]),
                          wa=c_wa[o, dr].astype(BF16), ba=row(c_ba[o, dr]),
                          wx=c_wx[o, dr].astype(BF16), bx=row(c_bx[o, dr]), lam=row(c_lambda[o, dr]))
                h0 = jnp.zeros((bsz, HALO, cw), F32)
                h_c, s_c = _rglru_direction(p_c.reshape(bsz, ctx_len, 2 * cw), h0, cp, rev,
                                            _row_tile(ctx_len, 128))
                hs_l.append(_rglru_grid_direction(p3_l, s_c, cp, rev)[:3])
                hs_c.append(h_c.reshape(n_ctx, cw))
            act_l = _readout_grid(p3_l, hs_l[0], hs_l[1]).reshape(n_lat, cw)
            xl = _mm_res(act_l, w_out, xl, lat_vec(2), seqlen)
            if with_ctx:
                act_c = _readout_c(p_c, hs_c[0], hs_c[1])
                xc = _mm_res(act_c, w_out, xc, ctx_vec(2), n_ctx)

        wg, wu, wd = (ffn_w_gate[layer].astype(BF16), ffn_w_up[layer].astype(BF16),
                      ffn_w_down[layer].astype(BF16))
        act = _norm_mm(xl, nw2, lat_vec(3), lat_vec(4), [wg, wu], seqlen, BF16)
        xl = _mm_res(act, wd, xl, lat_vec(5), seqlen)
        if with_ctx:
            act = _norm_mm(xc, nw2, ctx_vec(3), ctx_vec(4), [wg, wu], n_ctx, BF16)
            xc = _mm_res(act, wd, xc, ctx_vec(5), n_ctx)

    return _final_norm(xl, final_norm_w).reshape(bsz, seqlen, d)
```

```python
import functools
import math

import jax
import jax.numpy as jnp
from jax import lax
from jax.experimental import pallas as pl
from jax.experimental.pallas import tpu as pltpu

F32 = jnp.float32
BF16 = jnp.bfloat16

NORM_EPS = 1e-6
GRID_W = 64
M_HEADDIM = 64
M_GROUPS = 4
M_STATE = 128
M_CONV = 4
M_CHUNK = 128
M_NORM_EPS = 1e-5
R_HEADSIZE = 64
R_CHUNK = 64
R_GROUP = 4
R_SUB = 4
R_LN_EPS = 64e-5
C_HEADS = 8
C_CONV = 4
C_ROWS_PER_STEP = 2
RG_C = 8.0

SCAN_OUT_DTYPE = BF16
VMEM_LIMIT_BYTES = 56 * 1024 * 1024
HALO = 8


def _cparams(*sem):
    return pltpu.CompilerParams(dimension_semantics=sem, vmem_limit_bytes=VMEM_LIMIT_BYTES)


def _bdot(a, b):
    return jnp.dot(a.astype(BF16), b.astype(BF16), preferred_element_type=F32)


def _bdot_nt(a, b):
    return lax.dot_general(a.astype(BF16), b.astype(BF16), (((1,), (1,)), ((), ())),
                           preferred_element_type=F32)


def _split3(x):
    hi = x.astype(BF16)
    r1 = x - hi.astype(F32)
    mid = r1.astype(BF16)
    lo = (r1 - mid.astype(F32)).astype(BF16)
    return hi, mid, lo


def _tri_cumsum(tri_bf16, x, parts=3):
    hi, mid, lo = _split3(x)
    dot = lambda p: jnp.dot(tri_bf16, p, preferred_element_type=F32)
    if parts == 2:
        return dot(hi) + dot(mid)
    return dot(hi) + (dot(mid) + dot(lo))


def _silu(x):
    return x * jax.nn.sigmoid(x)


def _softplus(x):
    return jnp.maximum(x, 0.0) + jnp.log1p(jnp.exp(-jnp.abs(x)))


def _adaln_kernel(cb_ref, w_ref, b_ref, o_ref, *, n_vec, tn):
    rows = []
    for v in range(n_vec):
        cols = []
        for j in range(tn // 128):
            wj = w_ref[:, j * 128:(j + 1) * 128]
            cols.append(jnp.sum(wj * cb_ref[v], axis=0, keepdims=True))
        rows.append(jnp.concatenate(cols, axis=1))
    pad = jnp.zeros((HALO - n_vec, tn), F32)
    o_ref[...] = jnp.concatenate(rows + [pad], axis=0) + b_ref[...]


def _adaln(cond, ada_w, ada_b):
    depth, d, n6 = ada_w.shape
    n_vec = cond.shape[0]
    tn = 1024
    cb = jnp.broadcast_to(_silu(cond)[:, :, None], (n_vec, d, 128))
    return pl.pallas_call(
        functools.partial(_adaln_kernel, n_vec=n_vec, tn=tn),
        grid=(depth, n6 // tn),
        in_specs=[pl.BlockSpec((n_vec, d, 128), lambda l, j: (0, 0, 0)),
                  pl.BlockSpec((None, d, tn), lambda l, j: (l, 0, j)),
                  pl.BlockSpec((None, 1, tn), lambda l, j: (l, 0, j))],
        out_specs=pl.BlockSpec((None, HALO, tn), lambda l, j: (l, 0, j)),
        out_shape=jax.ShapeDtypeStruct((depth, HALO, n6), F32),
        compiler_params=_cparams("arbitrary", "arbitrary"),
    )(cb, ada_w, ada_b.reshape(depth, 1, n6))


def _modulated(x_ref, nw_ref, sh_ref, sc_ref):
    x = x_ref[...]
    y = x * lax.rsqrt(jnp.mean(x * x, axis=-1, keepdims=True) + NORM_EPS)
    return (y * nw_ref[...]) * (1.0 + sc_ref[...]) + sh_ref[...]


def _norm_mm_kernel(x_ref, nw_ref, sh_ref, sc_ref, w_ref, o_ref, h_ref):
    @pl.when(pl.program_id(1) == 0)
    def _():
        h_ref[...] = _modulated(x_ref, nw_ref, sh_ref, sc_ref).astype(BF16)

    o_ref[...] = jnp.dot(h_ref[...], w_ref[...].astype(BF16),
                         preferred_element_type=F32).astype(o_ref.dtype)


def _norm_swiglu_kernel(x_ref, nw_ref, sh_ref, sc_ref, wg_ref, wu_ref, o_ref, h_ref):
    @pl.when(pl.program_id(1) == 0)
    def _():
        h_ref[...] = _modulated(x_ref, nw_ref, sh_ref, sc_ref).astype(BF16)

    h = h_ref[...]
    g = jnp.dot(h, wg_ref[...].astype(BF16), preferred_element_type=F32)
    u = jnp.dot(h, wu_ref[...].astype(BF16), preferred_element_type=F32)
    o_ref[...] = (_silu(g) * u).astype(o_ref.dtype)


def _row_tile(m, cap):
    t = cap
    while m % t:
        t //= 2
    return t


def _norm_mm(x, nw, shift, scale, weights, rows_per_vec, out_dtype, tm_cap=1024, tn=512):
    m, d = x.shape
    n = weights[0].shape[1]
    tm = _row_tile(rows_per_vec, tm_cap)
    vec_map = lambda i, j: ((i * tm) // rows_per_vec, 0, 0)
    kern = _norm_mm_kernel if len(weights) == 1 else _norm_swiglu_kernel
    return pl.pallas_call(
        kern,
        grid=(m // tm, n // tn),
        in_specs=[pl.BlockSpec((tm, d), lambda i, j: (i, 0)),
                  pl.BlockSpec((1, d), lambda i, j: (0, 0)),
                  pl.BlockSpec((None, 1, d), vec_map),
                  pl.BlockSpec((None, 1, d), vec_map)]
                 + [pl.BlockSpec((d, tn), lambda i, j: (0, j)) for _ in weights],
        out_specs=pl.BlockSpec((tm, tn), lambda i, j: (i, j)),
        out_shape=jax.ShapeDtypeStruct((m, n), out_dtype),
        scratch_shapes=[pltpu.VMEM((tm, d), BF16)],
        compiler_params=_cparams("arbitrary", "arbitrary"),
    )(x, nw.reshape(1, d), shift, scale, *weights)


def _mm_res_kernel(a_ref, w_ref, res_ref, g_ref, o_ref):
    acc = jnp.dot(a_ref[...], w_ref[...], preferred_element_type=F32)
    o_ref[...] = res_ref[...] + g_ref[...] * acc


def _mm_res(a, w, res, gate, rows_per_vec, tm_cap=1024, tn=512):
    m, k = a.shape
    n = w.shape[1]
    tm = _row_tile(rows_per_vec, tm_cap)
    return pl.pallas_call(
        _mm_res_kernel,
        grid=(m // tm, n // tn),
        in_specs=[pl.BlockSpec((tm, k), lambda i, j: (i, 0)),
                  pl.BlockSpec((k, tn), lambda i, j: (0, j)),
                  pl.BlockSpec((tm, tn), lambda i, j: (i, j)),
                  pl.BlockSpec((None, 1, tn), lambda i, j: ((i * tm) // rows_per_vec, 0, j))],
        out_specs=pl.BlockSpec((tm, tn), lambda i, j: (i, j)),
        out_shape=jax.ShapeDtypeStruct((m, n), F32),
        compiler_params=_cparams("arbitrary", "arbitrary"),
    )(a, w, res, gate)


def _final_norm_kernel(x_ref, w_ref, o_ref):
    x = x_ref[...]
    o_ref[...] = x * lax.rsqrt(jnp.mean(x * x, axis=-1, keepdims=True) + NORM_EPS) * w_ref[...]


def _final_norm(x, w):
    m, d = x.shape
    tm = _row_tile(m, 512)
    return pl.pallas_call(
        _final_norm_kernel,
        grid=(m // tm,),
        in_specs=[pl.BlockSpec((tm, d), lambda i: (i, 0)), pl.BlockSpec((1, d), lambda i: (0, 0))],
        out_specs=pl.BlockSpec((tm, d), lambda i: (i, 0)),
        out_shape=jax.ShapeDtypeStruct((m, d), F32),
        compiler_params=_cparams("arbitrary"),
    )(x, w.reshape(1, d))


def _scan_block(i, n_blocks, rev):
    return (n_blocks - 1 - i) if rev else i


def _halo_block(i, n_blocks, rows, rev):
    per = rows // HALO
    if rev:
        return jnp.minimum((_scan_block(i, n_blocks, True) + 1) * per, n_blocks * per - 1)
    return jnp.maximum(i * per - 1, 0)


def _stage_with_halo(scr_ref, x_ref, halo_ref, keep, rows, rev):
    if rev:
        scr_ref[0:rows, :] = x_ref[...]
        scr_ref[rows:rows + HALO, :] = halo_ref[...] * keep
    else:
        scr_ref[0:HALO, :] = halo_ref[...] * keep
        scr_ref[HALO:HALO + rows, :] = x_ref[...]


def _lagged(scr_ref, lag, rows, rev):
    start = lag if rev else HALO - lag
    return scr_ref[start:start + rows, :]


def _rwkv_kernel(rkv_ref, sm_ref, rkv_h_ref, sm_h_ref, mu_rkv_ref, mu_sm_ref, w0_ref, w2_ref,
                 a0_ref, a2_ref, kkw_ref, kaw_ref, rkw_ref, s0_ref,
                 y_ref, bonus_ref, sl_ref, s_scr, rkv_scr, sm_scr, *, rev, n_heads, lora, nsub):
    i = pl.program_id(1)
    c = R_CHUNK
    rows = nsub * c
    n = R_HEADSIZE
    hw = n_heads * n

    @pl.when(i == 0)
    def _():
        s_scr[...] = s0_ref[...]

    keep = (i > 0).astype(F32)
    _stage_with_halo(rkv_scr, rkv_ref, rkv_h_ref, keep, rows, rev)
    _stage_with_halo(sm_scr, sm_ref, sm_h_ref, keep, rows, rev)
    x = rkv_ref[...]
    f = x + (_lagged(rkv_scr, 1, rows, rev) - x) * mu_rkv_ref[...]
    xs = sm_ref[...]
    fs = xs + (_lagged(sm_scr, 1, rows, rev) - xs) * mu_sm_ref[...]
    r = f[:, :hw]
    k = f[:, hw:2 * hw]
    v = f[:, 2 * hw:3 * hw]
    wl = fs[:, 256:256 + lora]
    al = fs[:, 384:384 + lora]

    ww = w0_ref[...] + _bdot(jnp.tanh(wl), w2_ref[...])
    lw = -math.exp(-0.5) * jax.nn.sigmoid(ww)
    a = jax.nn.sigmoid(a0_ref[...] + _bdot(al, a2_ref[...]))
    kkr = k * kkw_ref[...]
    k2 = k * (1.0 + (a - 1.0) * kaw_ref[...])
    rk = r * k2 * rkw_ref[...]

    cshift = int(math.log2(c))
    ti = lax.broadcasted_iota(jnp.int32, (rows, rows), 0)
    tj = lax.broadcasted_iota(jnp.int32, (rows, rows), 1)
    same_chunk = (ti >> cshift) == (tj >> cshift)
    upto = same_chunk & ((tj >= ti) if rev else (tj <= ti))
    g = _tri_cumsum(upto.astype(BF16), lw, parts=2)
    rsl = [slice(k * c, (k + 1) * c) for k in range(nsub)]
    gtot = [g[k * c:k * c + 1, :] if rev else g[(k + 1) * c - 1:(k + 1) * c, :] for k in range(nsub)]
    eg = jnp.exp(g)
    eneg = jnp.exp(-g)
    egp = jnp.exp(g - lw)
    etail = jnp.exp(jnp.concatenate([jnp.broadcast_to(t, (c, hw)) for t in gtot], axis=0) - g)
    dg = [jnp.exp(t) for t in gtot]
    rd_all = r * eg
    kp_all = k2 * eneg
    kpp_all = k2 * etail

    gw = R_GROUP * n
    groups = range(hw // gw)
    gsl = [slice(q * gw, (q + 1) * gw) for q in groups]
    shift = int(math.log2(n))
    bi = lax.broadcasted_iota(jnp.int32, (gw, gw), 0) >> shift
    bj = lax.broadcasted_iota(jnp.int32, (gw, gw), 1) >> shift
    same_head = bi == bj
    ones_bd = same_head.astype(BF16)
    lane_head = lax.broadcasted_iota(jnp.int32, (1, 128), 1) >> shift
    keep_f32 = [(lane_head == j).astype(F32) for j in range(128 // n)]
    keep_lanes = [t.astype(BF16) for t in keep_f32]

    def bdiag(t):
        tb = t.astype(BF16)
        zero = jnp.zeros((c, 128), BF16)
        rows = []
        for h in range(R_GROUP):
            lt, j = divmod(h * n, 128)
            piece = tb[:, lt * 128:(lt + 1) * 128] * keep_lanes[j // n]
            rows.append(jnp.concatenate([piece if u == lt else zero for u in range(gw // 128)], axis=1))
        return jnp.concatenate(rows, axis=0)

    def diag_blocks(full):
        tiles = []
        for lt in range(gw // 128):
            acc = None
            for j in range(128 // n):
                h = lt * (128 // n) + j
                part = full[h * n:(h + 1) * n, lt * 128:(lt + 1) * 128] * keep_f32[j]
                acc = part if acc is None else acc + part
            tiles.append(acc)
        return jnp.concatenate(tiles, axis=1)

    def head_sums(t):
        hi, mid, _ = _split3(t)
        dot = lambda p: jnp.dot(p, ones_bd, preferred_element_type=F32)
        return dot(hi) + dot(mid)

    tdot = lambda p, q: lax.dot_general(p.astype(BF16), q.astype(BF16), (((0,), (0,)), ((), ())),
                                        preferred_element_type=F32)
    gi = lax.broadcasted_iota(jnp.int32, (c, gw), 0)
    gj = lax.broadcasted_iota(jnp.int32, (c, gw), 1) & (c - 1)
    before_g = (gj > gi) if rev else (gj < gi)
    upto_g = (gj >= gi) if rev else (gj <= gi)
    eye_g = (gi == gj).astype(F32)

    kk_all = jnp.concatenate(
        [kkr[:, s] * lax.rsqrt(jnp.maximum(head_sums(kkr[:, s] * kkr[:, s]), 1e-24)) for s in gsl], axis=1)
    bh_all = kk_all * a
    kkd_all = kk_all * egp
    bp_all = bh_all * eneg
    bpp_all = bh_all * etail
    bonus_ref[...] = (jnp.concatenate([head_sums(rk[:, s]) for s in gsl], axis=1) * v
                      ).astype(bonus_ref.dtype)

    units = [(k, q) for k in range(nsub) for q in groups]
    un = range(len(units))
    cut = lambda t: [t[rsl[k], gsl[q]] for k, q in units]
    kkd, bp, bpp, rd, vq, kp, kpp = (cut(t) for t in (kkd_all, bp_all, bpp_all, rd_all, v, kp_all, kpp_all))
    amat = [_bdot_nt(jnp.concatenate([kkd[u], rd[u]], axis=0),
                     jnp.concatenate([bdiag(kp[u]), bdiag(bp[u])], axis=0)) for u in un]
    mk = [jnp.where(before_g, m[:c, :gw], 0.0) for m in amat]
    nk = [jnp.where(upto_g, m[c:, :gw], 0.0) for m in amat]
    nb = [jnp.where(upto_g, m[c:, gw:], 0.0) for m in amat]
    p2 = [jnp.where(before_g, -m[:c, gw:], 0.0) for m in amat]
    tinv = [eye_g + t for t in p2]
    p2 = [_bdot(t, bdiag(t)) for t in p2]
    for _ in range(cshift - 2):
        sq = [_bdot(jnp.concatenate([tinv[u], p2[u]], axis=0), bdiag(p2[u])) for u in un]
        tinv = [tinv[u] + sq[u][:c] for u in un]
        p2 = [t[c:] for t in sq]
    tinv = [tinv[u] + _bdot(tinv[u], bdiag(p2[u])) for u in un]
    mnv = [_bdot(jnp.concatenate([mk[u], nk[u]], axis=0), bdiag(vq[u])) for u in un]
    wu = [_bdot(tinv[u], jnp.concatenate([bdiag(kkd[u]), bdiag(mnv[u][:c])], axis=1))
          for u in un]
    nwu = [_bdot(nb[u], jnp.concatenate([bdiag(wu[u][:, :gw]), bdiag(wu[u][:, gw:])], axis=1))
           for u in un]
    wub = [tdot(wu[u], bpp[u]) for u in un]
    pmat = [bdiag(diag_blocks(t[:gw])) for t in wub]
    dmat = [diag_blocks(tdot(vq[u], kpp[u])) - diag_blocks(wub[u][gw:]) for u in un]
    qt = [rd[u] - nwu[u][:, :gw] for u in un]
    y0 = [mnv[u][c:] - nwu[u][:, gw:] for u in un]
    state = [s_scr[q] for q in groups]
    for k in (reversed(range(nsub)) if rev else range(nsub)):
        for q in groups:
            u = k * len(groups) + q
            y_ref[rsl[k], gsl[q]] = (y0[u] + _bdot_nt(qt[u], bdiag(state[q]))).astype(y_ref.dtype)
            state[q] = state[q] * dg[k][:, gsl[q]] - _bdot(state[q], pmat[u]) + dmat[u]
    for q in groups:
        s_scr[q] = state[q]

    @pl.when(i == pl.num_programs(1) - 1)
    def _():
        sl_ref[...] = s_scr[...]


def _rwkv_direction(pr, s0, p, rev):
    bsz, seqlen, width = pr.shape
    hw = s0.shape[1] * s0.shape[3]
    n_heads = hw // R_HEADSIZE
    nsub = R_SUB if seqlen % (R_SUB * R_CHUNK) == 0 else 1
    c = nsub * R_CHUNK
    nblk = seqlen // c
    sm_col = width // 512 - 1
    blk = lambda i: _scan_block(i, nblk, rev)
    halo = lambda i: _halo_block(i, nblk, c, rev)
    vec = lambda w: pl.BlockSpec((1, w), lambda b, i: (0, 0))
    lora = p['w2'].shape[0]
    state_spec = pl.BlockSpec((None,) + s0.shape[1:], lambda b, i: (b, 0, 0, 0))
    return pl.pallas_call(
        functools.partial(_rwkv_kernel, rev=rev, n_heads=n_heads, lora=lora, nsub=nsub),
        grid=(bsz, nblk),
        in_specs=[pl.BlockSpec((None, c, 3 * hw), lambda b, i: (b, blk(i), 1)),
                  pl.BlockSpec((None, c, 512), lambda b, i: (b, blk(i), sm_col)),
                  pl.BlockSpec((None, HALO, 3 * hw), lambda b, i: (b, halo(i), 1)),
                  pl.BlockSpec((None, HALO, 512), lambda b, i: (b, halo(i), sm_col)),
                  vec(3 * hw), vec(512), vec(hw),
                  pl.BlockSpec((lora, hw), lambda b, i: (0, 0)),
                  vec(hw),
                  pl.BlockSpec((lora, hw), lambda b, i: (0, 0)),
                  vec(hw), vec(hw), vec(hw),
                  state_spec],
        out_specs=[pl.BlockSpec((None, c, hw), lambda b, i: (b, blk(i), 0)),
                   pl.BlockSpec((None, c, hw), lambda b, i: (b, blk(i), 0)),
                   state_spec],
        out_shape=[jax.ShapeDtypeStruct((bsz, seqlen, hw), SCAN_OUT_DTYPE),
                   jax.ShapeDtypeStruct((bsz, seqlen, hw), SCAN_OUT_DTYPE),
                   jax.ShapeDtypeStruct(s0.shape, F32)],
        scratch_shapes=[pltpu.VMEM(s0.shape[1:], F32),
                        pltpu.VMEM((c + HALO, 3 * hw), F32),
                        pltpu.VMEM((c + HALO, 512), F32)],
        compiler_params=_cparams("arbitrary", "arbitrary"),
    )(pr, pr, pr, pr, p['mu_rkv'], p['mu_sm'], p['w0'], p['w2'], p['a0'], p['a2'],
      p['kk'], p['ka'], p['rk'], s0)


def _expand_cols(x, e_bf16):
    hi, mid, lo = _split3(x)
    dot = lambda p: jnp.dot(p, e_bf16, preferred_element_type=F32)
    return dot(hi) + (dot(mid) + dot(lo))


def _causal_conv(scr_ref, w_ref, b_ref, rows, taps, rev):
    acc = b_ref[...] + w_ref[taps - 1:taps, :] * _lagged(scr_ref, 0, rows, rev)
    for kk in range(taps - 1):
        acc = acc + w_ref[kk:kk + 1, :] * _lagged(scr_ref, taps - 1 - kk, rows, rev)
    return acc


def _ssd_kernel(xbc_ref, halo_ref, sm_ref, dtt_ref, cw_ref, cb_ref, dtb_row_ref, dtb_col_ref,
                alog_row_ref, alog_col_ref, dskip_ref, e64_ref, hmask_ref, h0_ref,
                y_ref, hl_ref, h_scr, x_scr, *, rev, n_heads):
    i = pl.program_id(1)
    c = M_CHUNK
    p = M_HEADDIM
    ns = M_STATE
    mix = n_heads * p
    hpg = n_heads // M_GROUPS
    gw = hpg * p

    @pl.when(i == 0)
    def _():
        h_scr[...] = h0_ref[...]

    keep = (i > 0).astype(F32)
    _stage_with_halo(x_scr, xbc_ref, halo_ref, keep, c, rev)
    xbc = _silu(_causal_conv(x_scr, cw_ref, cb_ref, c, M_CONV, rev))
    xs = xbc[:, :mix]
    bm = [xbc[:, mix + g * ns:mix + (g + 1) * ns] for g in range(M_GROUPS)]
    cm = [xbc[:, mix + (M_GROUPS + g) * ns:mix + (M_GROUPS + g + 1) * ns] for g in range(M_GROUPS)]

    ti = lax.broadcasted_iota(jnp.int32, (c, c), 0)
    tj = lax.broadcasted_iota(jnp.int32, (c, c), 1)
    upto = (tj >= ti) if rev else (tj <= ti)
    upto_t = (ti >= tj) if rev else (ti <= tj)
    end = 0 if rev else c - 1

    dt = _softplus(sm_ref[:, 480:480 + n_heads] + dtb_row_ref[...])
    a_cs = _tri_cumsum(upto.astype(BF16), dt * -jnp.exp(alog_row_ref[...]))
    dt_t = _softplus(dtt_ref[...] + dtb_col_ref[...])
    a_cs_t = _expand_cols(dt_t * -jnp.exp(alog_col_ref[...]), upto_t.astype(BF16))
    a_end_t = a_cs_t[:, end:end + 1]
    w_t = dt_t * jnp.exp(a_end_t - a_cs_t)
    grow_t = jnp.broadcast_to(jnp.exp(a_end_t), (n_heads, ns))

    parts = [t.astype(F32) for t in _split3(a_cs)]
    ones = jnp.ones((c, n_heads), F32)
    lhs = jnp.concatenate(parts + [ones] * 3, axis=1).astype(BF16)
    hmask = hmask_ref[...]
    parts_t = jnp.concatenate(list(_split3(a_cs_t)), axis=0)
    rhs = jnp.concatenate([hmask, -jnp.concatenate([parts_t] * n_heads, axis=1) * hmask], axis=0)
    seg = jnp.dot(lhs, rhs, preferred_element_type=F32)
    ea_full = jnp.exp(jnp.dot(lhs[:, :3 * n_heads], e64_ref[...], preferred_element_type=F32))

    lane_head = lax.broadcasted_iota(jnp.int32, (1, gw), 1) >> int(math.log2(p))
    keep_lanes = [(lane_head == j).astype(BF16) for j in range(hpg)]
    xs_t = jnp.transpose(xs)
    y_diag, y_off = [], []
    for g in range(M_GROUPS):
        heads = range(g * hpg, (g + 1) * hpg)
        cb = _bdot_nt(cm[g], bm[g])
        scores = []
        for h in heads:
            decay = jnp.where(upto, jnp.exp(jnp.minimum(seg[:, h * c:(h + 1) * c], 0.0)), 0.0)
            scores.append((cb * decay * dt_t[h:h + 1, :]).astype(BF16))
        xg = xs[:, g * gw:(g + 1) * gw].astype(BF16)
        x_bd = jnp.concatenate([xg * keep_lanes[j] for j in range(hpg)], axis=0)
        y_diag.append(jnp.dot(jnp.concatenate(scores, axis=1), x_bd, preferred_element_type=F32))
        h_in = h_scr[g]
        y_off.append(_bdot_nt(cm[g], h_in))
        wrows = jnp.concatenate([jnp.broadcast_to(w_t[h:h + 1, :], (p, c)) for h in heads], axis=0)
        grow = jnp.concatenate([jnp.broadcast_to(grow_t[h:h + 1, :], (p, ns)) for h in heads], axis=0)
        h_scr[g] = h_in * grow + _bdot(xs_t[g * gw:(g + 1) * gw, :] * wrows, bm[g])
    y_ref[...] = (jnp.concatenate(y_diag, axis=1) + jnp.concatenate(y_off, axis=1) * ea_full
                  + xs * dskip_ref[...]).astype(y_ref.dtype)

    @pl.when(i == pl.num_programs(1) - 1)
    def _():
        hl_ref[...] = h_scr[...]


def _ssd_direction(pr, dt_t, h0, p, rev):
    bsz, seqlen, width = pr.shape
    n_heads = dt_t.shape[1]
    mix = n_heads * M_HEADDIM
    xw = mix + 2 * M_GROUPS * M_STATE
    c = M_CHUNK
    nblk = seqlen // c
    sm_col = width // 512 - 1
    blk = lambda i: _scan_block(i, nblk, rev)
    halo = lambda i: _halo_block(i, nblk, c, rev)
    full = lambda a: pl.BlockSpec(a.shape, lambda b, i: (0,) * a.ndim)
    consts = [p['conv_w'], p['conv_b'], p['dtb_row'], p['dtb_col'], p['alog_row'], p['alog_col'],
              p['dskip'], p['e64'], p['hmask']]
    state_spec = pl.BlockSpec((None,) + h0.shape[1:], lambda b, i: (b, 0, 0, 0))
    return pl.pallas_call(
        functools.partial(_ssd_kernel, rev=rev, n_heads=n_heads),
        grid=(bsz, nblk),
        in_specs=[pl.BlockSpec((None, c, xw), lambda b, i: (b, blk(i), 0)),
                  pl.BlockSpec((None, HALO, xw), lambda b, i: (b, halo(i), 0)),
                  pl.BlockSpec((None, c, 512), lambda b, i: (b, blk(i), sm_col)),
                  pl.BlockSpec((None, n_heads, c), lambda b, i: (b, 0, blk(i)))]
                 + [full(a) for a in consts] + [state_spec],
        out_specs=[pl.BlockSpec((None, c, mix), lambda b, i: (b, blk(i), 0)), state_spec],
        out_shape=[jax.ShapeDtypeStruct((bsz, seqlen, mix), SCAN_OUT_DTYPE),
                   jax.ShapeDtypeStruct(h0.shape, F32)],
        scratch_shapes=[pltpu.VMEM(h0.shape[1:], F32), pltpu.VMEM((c + HALO, xw), F32)],
        compiler_params=_cparams("arbitrary", "arbitrary"),
    )(pr, pr, pr, dt_t, *consts, h0)


def _rglru_kernel(x_ref, halo_ref, cw_ref, cb_ref, wa_ref, ba_ref, wx_ref, bx_ref, lam_ref, h0_ref,
                  o_ref, hl_ref, h_scr, x_scr, a_scr, u_scr, *, rev, tb):
    i = pl.program_id(1)

    @pl.when(i == 0)
    def _():
        h_scr[...] = h0_ref[...]

    keep = (i > 0).astype(F32)
    _stage_with_halo(x_scr, x_ref, halo_ref, keep, tb, rev)
    xc = _causal_conv(x_scr, cw_ref, cb_ref, tb, C_CONV, rev)
    a_scr[...], u_scr[...] = _rglru_gates(xc, wa_ref, ba_ref, wx_ref, bx_ref, lam_ref)

    def step(t, h):
        tt = (tb - 1 - t) if rev else t
        h = a_scr[pl.ds(tt, 1), :] * h + u_scr[pl.ds(tt, 1), :]
        o_ref[pl.ds(tt, 1), :] = h
        return h

    h_scr[0:1, :] = lax.fori_loop(0, tb, step, h_scr[0:1, :], unroll=8)

    @pl.when(i == pl.num_programs(1) - 1)
    def _():
        hl_ref[...] = h_scr[...]


def _rglru_direction(xv, h0, p, rev, tb):
    bsz, rows, wtot = xv.shape
    width = h0.shape[2]
    ncol = wtot // (2 * width)
    nrb = rows // tb
    total = ncol * nrb
    per = tb // HALO

    def pos(i):
        s = _scan_block(i, total, rev)
        return s // nrb, s % nrb

    def x_map(b, i):
        col, q = pos(i)
        return b, q, 2 * col + 1

    def halo_map(b, i):
        s = _scan_block(i, total, rev)
        sp = jnp.minimum(s + 1, total - 1) if rev else jnp.maximum(s - 1, 0)
        col, q = sp // nrb, sp % nrb
        return b, (q * per if rev else (q + 1) * per - 1), 2 * col + 1

    def o_map(b, i):
        col, q = pos(i)
        return b, q, col

    full = lambda a: pl.BlockSpec(a.shape, lambda b, i: (0,) * a.ndim)
    consts = [p['conv_w'], p['conv_b'], p['wa'], p['ba'], p['wx'], p['bx'], p['lam']]
    state_spec = pl.BlockSpec((None, HALO, width), lambda b, i: (b, 0, 0))
    return pl.pallas_call(
        functools.partial(_rglru_kernel, rev=rev, tb=tb),
        grid=(bsz, total),
        in_specs=[pl.BlockSpec((None, tb, width), x_map), pl.BlockSpec((None, HALO, width), halo_map)]
                 + [full(a) for a in consts] + [state_spec],
        out_specs=[pl.BlockSpec((None, tb, width), o_map), state_spec],
        out_shape=[jax.ShapeDtypeStruct((bsz, rows, ncol * width), F32),
                   jax.ShapeDtypeStruct(h0.shape, F32)],
        scratch_shapes=[pltpu.VMEM((HALO, width), F32), pltpu.VMEM((tb + HALO, width), F32),
                        pltpu.VMEM((tb, width), F32), pltpu.VMEM((tb, width), F32)],
        compiler_params=_cparams("arbitrary", "arbitrary"),
    )(xv, xv, *consts, h0)


def _rglru_gates(xc, wa_ref, ba_ref, wx_ref, bx_ref, lam_ref):
    blk = xc.shape[1] // C_HEADS
    xh = [xc[:, h * blk:(h + 1) * blk] for h in range(C_HEADS)]
    gate_r = jax.nn.sigmoid(
        jnp.concatenate([_bdot(xh[h], wa_ref[h]) for h in range(C_HEADS)], axis=1) + ba_ref[...])
    gate_i = jax.nn.sigmoid(
        jnp.concatenate([_bdot(xh[h], wx_ref[h]) for h in range(C_HEADS)], axis=1) + bx_ref[...])
    log_a = -RG_C * gate_r * _softplus(-lam_ref[...])
    a = jnp.exp(log_a)
    one_m_a2 = jnp.maximum(-jnp.tanh(log_a) * (a * a + 1.0), 0.0)
    mult = jnp.where(one_m_a2 > 0.0, one_m_a2 * lax.rsqrt(one_m_a2), 0.0)
    return a, mult * gate_i * xc


def _rglru_grid_kernel(x_ref, e0_ref, e1_ref, e2_ref, cw_ref, cb_ref, wa_ref, ba_ref, wx_ref,
                       bx_ref, lam_ref, h0_ref, hloc_ref, aprod_ref, cin_ref, hl_ref,
                       ring_scr, h_scr, ap_scr, stage_scr, *, rev, ncol):
    i = pl.program_id(1)

    def from_previous_column(e_ref):
        _stage_with_halo(stage_scr, e_ref, jnp.zeros((HALO, e_ref.shape[1]), F32), 0.0, ncol, rev)
        return _lagged(stage_scr, 1, ncol, rev)

    @pl.when(i == 0)
    def _():
        ring_scr[0] = from_previous_column(e2_ref)
        ring_scr[1] = from_previous_column(e1_ref)
        ring_scr[2] = from_previous_column(e0_ref)
        h_scr[...] = jnp.zeros(h_scr.shape, F32)
        ap_scr[...] = jnp.ones(ap_scr.shape, F32)

    nrow = x_ref.shape[0] // ncol
    order = list(reversed(range(nrow))) if rev else list(range(nrow))
    rsl = [slice(m * ncol, (m + 1) * ncol) for m in range(nrow)]
    hist = [ring_scr[2], ring_scr[1], ring_scr[0]] + [x_ref[rsl[m], :] for m in order]
    xc = [cb_ref[...] + cw_ref[3:4, :] * hist[t + 3] + cw_ref[2:3, :] * hist[t + 2]
          + cw_ref[1:2, :] * hist[t + 1] + cw_ref[0:1, :] * hist[t] for t in range(nrow)]
    for lag in range(3):
        ring_scr[lag] = hist[-1 - lag]
    a, u = _rglru_gates(jnp.concatenate(xc, axis=0), wa_ref, ba_ref, wx_ref, bx_ref, lam_ref)
    h, ap = h_scr[...], ap_scr[...]
    for t, m in enumerate(order):
        ts = slice(t * ncol, (t + 1) * ncol)
        h = a[ts, :] * h + u[ts, :]
        ap = a[ts, :] * ap
        hloc_ref[rsl[m], :] = h.astype(hloc_ref.dtype)
        aprod_ref[rsl[m], :] = ap.astype(aprod_ref.dtype)
    h_scr[...] = h
    ap_scr[...] = ap

    @pl.when(i == pl.num_programs(1) - 1)
    def _():
        def step(t, carry):
            col = (ncol - 1 - t) if rev else t
            cin_ref[pl.ds(col, 1), :] = carry
            return h_scr[pl.ds(col, 1), :] + ap_scr[pl.ds(col, 1), :] * carry

        last = lax.fori_loop(0, ncol, step, h0_ref[0:1, :], unroll=8)
        hl_ref[...] = jnp.zeros(hl_ref.shape, F32)
        hl_ref[0:1, :] = last


def _rglru_grid_direction(p3, h0, p, rev):
    bsz, seqlen, _ = p3.shape
    width = h0.shape[2]
    ncol = GRID_W
    rows = seqlen // ncol
    nrow = C_ROWS_PER_STEP if rows % C_ROWS_PER_STEP == 0 else 1
    nblk = rows // nrow
    blk = lambda i: _scan_block(i, nblk, rev)
    edge = lambda k: pl.BlockSpec((None, ncol, width),
                                  lambda b, i: (b, (2 - k) if rev else (rows - 3 + k), 1))
    full = lambda a: pl.BlockSpec(a.shape, lambda b, i: (0,) * a.ndim)
    consts = [p['conv_w'], p['conv_b'], p['wa'], p['ba'], p['wx'], p['bx'], p['lam']]
    state_spec = pl.BlockSpec((None, HALO, width), lambda b, i: (b, 0, 0))
    row_spec = pl.BlockSpec((None, nrow * ncol, width), lambda b, i: (b, blk(i), 0))
    return pl.pallas_call(
        functools.partial(_rglru_grid_kernel, rev=rev, ncol=ncol),
        grid=(bsz, nblk),
        in_specs=[pl.BlockSpec((None, nrow * ncol, width), lambda b, i: (b, blk(i), 1)),
                  edge(0), edge(1), edge(2)] + [full(a) for a in consts] + [state_spec],
        out_specs=[row_spec, row_spec,
                   pl.BlockSpec((None, ncol, width), lambda b, i: (b, 0, 0)), state_spec],
        out_shape=[jax.ShapeDtypeStruct((bsz, seqlen, width), SCAN_OUT_DTYPE),
                   jax.ShapeDtypeStruct((bsz, seqlen, width), SCAN_OUT_DTYPE),
                   jax.ShapeDtypeStruct((bsz, ncol, width), F32),
                   jax.ShapeDtypeStruct(h0.shape, F32)],
        scratch_shapes=[pltpu.VMEM((3, ncol, width), F32), pltpu.VMEM((ncol, width), F32),
                        pltpu.VMEM((ncol, width), F32), pltpu.VMEM((ncol + HALO, width), F32)],
        compiler_params=_cparams("arbitrary", "arbitrary"),
    )(p3, p3, p3, p3, *consts, h0)


def _readout_ab_kernel(z_ref, sm_ref, ymf_ref, ymb_ref, yrf_ref, yrb_ref, bof_ref, bob_ref,
                       nw_ref, lw_ref, lb_ref, g2_ref, o_ref, *, n_heads):
    f32 = lambda ref: ref[...].astype(F32)
    t = (f32(ymf_ref) + f32(ymb_ref)) * _silu(z_ref[...])
    gw = t.shape[1] // M_GROUPS
    a_out = []
    for g in range(M_GROUPS):
        tg = t[:, g * gw:(g + 1) * gw]
        a_out.append(tg * lax.rsqrt(jnp.mean(tg * tg, axis=-1, keepdims=True) + M_NORM_EPS))
    a_out = jnp.concatenate(a_out, axis=1) * nw_ref[...]

    yr = f32(yrf_ref) + f32(yrb_ref)
    n = R_HEADSIZE
    yn = []
    for h in range(n_heads):
        yh = yr[:, h * n:(h + 1) * n]
        dlt = yh - jnp.mean(yh, axis=-1, keepdims=True)
        yn.append(dlt * lax.rsqrt(jnp.mean(dlt * dlt, axis=-1, keepdims=True) + R_LN_EPS))
    yn = jnp.concatenate(yn, axis=1) * lw_ref[...] + lb_ref[...]
    gate = _bdot(jax.nn.sigmoid(sm_ref[:, 0:g2_ref.shape[0]]), g2_ref[...])
    b_out = (yn + f32(bof_ref) + f32(bob_ref)) * gate
    o_ref[...] = jnp.concatenate([a_out, b_out], axis=1).astype(o_ref.dtype)


def _readout_ab(pr, ym, yr, bo, ro, n_heads):
    bsz, seqlen, width = pr.shape
    m = bsz * seqlen
    mix = ym[0].shape[2]
    tm = _row_tile(m, 256)
    flat = lambda t: t.reshape(m, t.shape[2])
    rowblk = lambda w, col: pl.BlockSpec((tm, w), lambda i: (i, col))
    full = lambda a: pl.BlockSpec(a.shape, lambda i: (0,) * a.ndim)
    consts = [ro['norm_w'], ro['lnx_w'], ro['lnx_b'], ro['g2']]
    return pl.pallas_call(
        functools.partial(_readout_ab_kernel, n_heads=n_heads),
        grid=(m // tm,),
        in_specs=[rowblk(mix, 2), rowblk(512, width // 512 - 1)] + [rowblk(mix, 0)] * 6
                 + [full(a) for a in consts],
        out_specs=rowblk(2 * mix, 0),
        out_shape=jax.ShapeDtypeStruct((m, 2 * mix), BF16),
        compiler_params=_cparams("arbitrary"),
    )(flat(pr), flat(pr), flat(ym[0]), flat(ym[1]), flat(yr[0]), flat(yr[1]), flat(bo[0]),
      flat(bo[1]), *consts)


def _gelu_tanh(g):
    return 0.5 * g * (1.0 + jnp.tanh(math.sqrt(2.0 / math.pi) * (g + 0.044715 * (g * g * g))))


def _readout_c_kernel(gy_ref, hf_ref, hb_ref, o_ref):
    o_ref[...] = ((hf_ref[...] + hb_ref[...]) * _gelu_tanh(gy_ref[...])).astype(o_ref.dtype)


def _readout_grid_kernel(gy_ref, hf_ref, af_ref, cf_ref, hb_ref, ab_ref, cb_ref, o_ref, *, ncol):
    for j in range(o_ref.shape[0] // ncol):
        rs = slice(j * ncol, (j + 1) * ncol)
        f32 = lambda ref: ref[rs, :].astype(F32)
        h = (f32(hf_ref) + f32(af_ref) * cf_ref[...]) + (f32(hb_ref) + f32(ab_ref) * cb_ref[...])
        o_ref[rs, :] = (h * _gelu_tanh(gy_ref[rs, :])).astype(o_ref.dtype)


def _readout_grid(p3, fwd, bwd):
    bsz, seqlen, _ = p3.shape
    cw = fwd[0].shape[2]
    ncol = fwd[2].shape[1]
    tm = _row_tile(seqlen, 8 * ncol)
    spec = pl.BlockSpec((None, tm, cw), lambda b, i: (b, i, 0))
    cspec = pl.BlockSpec((None, ncol, cw), lambda b, i: (b, 0, 0))
    return pl.pallas_call(
        functools.partial(_readout_grid_kernel, ncol=ncol),
        grid=(bsz, seqlen // tm),
        in_specs=[spec, spec, spec, cspec, spec, spec, cspec],
        out_specs=spec,
        out_shape=jax.ShapeDtypeStruct((bsz, seqlen, cw), BF16),
        compiler_params=_cparams("arbitrary", "arbitrary"),
    )(p3, *fwd, *bwd)


def _readout_c(p, hf, hb):
    m, cw = hf.shape
    tm = _row_tile(m, 512)
    spec = pl.BlockSpec((tm, cw), lambda i: (i, 0))
    return pl.pallas_call(
        _readout_c_kernel,
        grid=(m // tm,),
        in_specs=[spec, spec, spec],
        out_specs=spec,
        out_shape=jax.ShapeDtypeStruct((m, cw), BF16),
        compiler_params=_cparams("arbitrary"),
    )(p, hf, hb)


def kernel(x, c, ctx, c_ctx, ada_w, ada_b, norm1_w, norm2_w, ffn_w_gate, ffn_w_up, ffn_w_down, final_norm_w, ab_w_in, ab_w_out, m_conv_w, m_conv_b, m_dt_bias, m_a_log, m_d, m_norm_w, r_mu, r_w0, r_w2, r_a0, r_a2, r_kk, r_ka, r_rk, r_g2, r_lnx_w, r_lnx_b, c_w_in, c_w_out, c_conv_w, c_conv_b, c_wa, c_ba, c_wx, c_bx, c_lambda):
    bsz, seqlen, d = x.shape
    ctx_len = ctx.shape[1]
    depth = ada_w.shape[0]
    n_lat, n_ctx = bsz * seqlen, bsz * ctx_len
    row = lambda t: t.reshape(1, -1)

    mod = _adaln(jnp.concatenate([c_ctx[None, :], c], axis=0), ada_w, ada_b)

    xl = x.reshape(n_lat, d)
    xc = ctx.reshape(n_ctx, d)
    mix_a = m_norm_w.shape[1]
    m_heads = m_dt_bias.shape[2]
    r_heads = r_rk.shape[2]
    lora = r_w2.shape[2]
    e64 = jnp.tile(jnp.repeat(jnp.eye(m_heads, dtype=BF16), M_HEADDIM, axis=1), (3, 1))
    hmask = jnp.tile(jnp.repeat(jnp.eye(m_heads, dtype=BF16), M_CHUNK, axis=1), (3, 1))

    for layer in range(depth):
        with_ctx = layer < depth - 1
        part = lambda k: mod[layer, :, k * d:(k + 1) * d]
        lat_vec = lambda k: part(k)[1:1 + bsz, None, :]
        ctx_vec = lambda k: part(k)[0:1, None, :]
        nw1, nw2 = norm1_w[layer], norm2_w[layer]

        if layer % 2 == 0:
            e = layer // 2
            w = ab_w_in[e].astype(BF16)
            o1 = mix_a
            o2 = o1 + mix_a + 2 * M_GROUPS * M_STATE
            o3 = o2 + m_heads
            o4 = o3 + 3 * r_heads * R_HEADSIZE + 2 * lora
            rkv_w = 3 * r_heads * R_HEADSIZE
            zeros = lambda n: jnp.zeros((d, n), w.dtype)
            w_in = jnp.concatenate(
                [w[:, o1:o2], w[:, :o1], w[:, o3:o3 + rkv_w], w[:, o4:],
                 w[:, o3 + rkv_w:o3 + rkv_w + lora], zeros(128 - lora),
                 w[:, o3 + rkv_w + lora:o4], w[:, o2:o3], zeros(128 - lora - m_heads)],
                axis=1)
            width = w_in.shape[1]
            w_out = ab_w_out[e].astype(BF16)
            g2w = r_g2[e].astype(BF16)

            def project(xs, sh, sc, rows_per_vec, seg_len):
                pr = _norm_mm(xs, nw1, sh, sc, [w_in], rows_per_vec, F32)
                pr = pr.reshape(bsz, seg_len, width)
                dt_t = jnp.swapaxes(pr[:, :, width - 32:width - 32 + m_heads], 1, 2)
                return pr, dt_t

            pr_l, dtt_l = project(xl, lat_vec(0), lat_vec(1), seqlen, seqlen)
            pr_c, dtt_c = project(xc, ctx_vec(0), ctx_vec(1), n_ctx, ctx_len)

            ym_l = ym_c = yr_l = yr_c = bo_l = bo_c = None
            for dr in range(2):
                rev = dr == 1
                mp = dict(conv_w=m_conv_w[e, dr], conv_b=row(m_conv_b[e, dr]),
                          dtb_row=row(m_dt_bias[e, dr]), dtb_col=m_dt_bias[e, dr].reshape(-1, 1),
                          alog_row=row(m_a_log[e, dr]), alog_col=m_a_log[e, dr].reshape(-1, 1),
                          dskip=row(jnp.repeat(m_d[e, dr], M_HEADDIM)), e64=e64, hmask=hmask)
                h0 = jnp.zeros((bsz, M_GROUPS, m_heads // M_GROUPS * M_HEADDIM, M_STATE), F32)
                y_c, h_c = _ssd_direction(pr_c, dtt_c, h0, mp, rev)
                y_l, _ = _ssd_direction(pr_l, dtt_l, h_c, mp, rev)
                mu = r_mu[e, dr]
                mu_sm = (jnp.zeros((512,), F32).at[256:256 + lora].set(mu[rkv_w:rkv_w + lora])
                         .at[384:384 + lora].set(mu[rkv_w + lora:]))
                rp = dict(mu_rkv=row(mu[:rkv_w]), mu_sm=row(mu_sm), w0=row(r_w0[e, dr]),
                          w2=r_w2[e, dr].astype(BF16), a0=row(r_a0[e, dr]), a2=r_a2[e, dr].astype(BF16),
                          kk=row(r_kk[e, dr]), ka=row(r_ka[e, dr]), rk=row(r_rk[e, dr]))
                s0 = jnp.zeros((bsz, r_heads // R_GROUP, R_HEADSIZE, R_GROUP * R_HEADSIZE), F32)
                v_c, b_c, s_c = _rwkv_direction(pr_c, s0, rp, rev)
                v_l, b_l, _ = _rwkv_direction(pr_l, s_c, rp, rev)
                if dr == 0:
                    ym_l, ym_c, yr_l, yr_c, bo_l, bo_c = y_l, y_c, v_l, v_c, b_l, b_c
                else:
                    ym_l, yr_l, bo_l = (ym_l, y_l), (yr_l, v_l), (bo_l, b_l)
                    ym_c, yr_c, bo_c = (ym_c, y_c), (yr_c, v_c), (bo_c, b_c)

            ro = dict(norm_w=row(m_norm_w[e]), lnx_w=row(r_lnx_w[e]), lnx_b=row(r_lnx_b[e]), g2=g2w)
            act_l = _readout_ab(pr_l, ym_l, yr_l, bo_l, ro, r_heads)
            xl = _mm_res(act_l, w_out, xl, lat_vec(2), seqlen)
            if with_ctx:
                act_c = _readout_ab(pr_c, ym_c, yr_c, bo_c, ro, r_heads)
                xc = _mm_res(act_c, w_out, xc, ctx_vec(2), n_ctx)
        else:
            o = layer // 2
            w_in = c_w_in[o]
            w_out = c_w_out[o].astype(BF16)
            cw = w_out.shape[0]
            p_l = _norm_mm(xl, nw1, lat_vec(0), lat_vec(1), [w_in], seqlen, F32)
            p_c = _norm_mm(xc, nw1, ctx_vec(0), ctx_vec(1), [w_in], n_ctx, F32)
            p3_l = p_l.reshape(bsz, seqlen, 2 * cw)
            hs_l, hs_c = [], []
            for dr in range(2):
                rev = dr == 1
                cp = dict(conv_w=c_conv_w[o, dr], conv_b=row(c_conv_b[o, dr]),
                          wa=c_wa[o, dr].astype(BF16), ba=row(c_ba[o, dr]),
                          wx=c_wx[o, dr].astype(BF16), bx=row(c_bx[o, dr]), lam=row(c_lambda[o, dr]))
                h0 = jnp.zeros((bsz, HALO, cw), F32)
                h_c, s_c = _rglru_direction(p_c.reshape(bsz, ctx_len, 2 * cw), h0, cp, rev,
                                            _row_tile(ctx_len, 128))
                hs_l.append(_rglru_grid_direction(p3_l, s_c, cp, rev)[:3])
                hs_c.append(h_c.reshape(n_ctx, cw))
            act_l = _readout_grid(p3_l, hs_l[0], hs_l[1]).reshape(n_lat, cw)
            xl = _mm_res(act_l, w_out, xl, lat_vec(2), seqlen)
            if with_ctx:
                act_c = _readout_c(p_c, hs_c[0], hs_c[1])
                xc = _mm_res(act_c, w_out, xc, ctx_vec(2), n_ctx)

        wg, wu, wd = ffn_w_gate[layer], ffn_w_up[layer], ffn_w_down[layer].astype(BF16)
        act = _norm_mm(xl, nw2, lat_vec(3), lat_vec(4), [wg, wu], seqlen, BF16)
        xl = _mm_res(act, wd, xl, lat_vec(5), seqlen)
        if with_ctx:
            act = _norm_mm(xc, nw2, ctx_vec(3), ctx_vec(4), [wg, wu], n_ctx, BF16)
            xc = _mm_res(act, wd, xc, ctx_vec(5), n_ctx)

    return _final_norm(xl, final_norm_w).reshape(bsz, seqlen, d)
```

```python
import functools
import math

import jax
import jax.numpy as jnp
from jax import lax
from jax.experimental import pallas as pl
from jax.experimental.pallas import tpu as pltpu

F32 = jnp.float32
BF16 = jnp.bfloat16

NORM_EPS = 1e-6
GRID_W = 64
M_HEADDIM = 64
M_GROUPS = 4
M_STATE = 128
M_CONV = 4
M_CHUNK = 128
M_NORM_EPS = 1e-5
R_HEADSIZE = 64
R_CHUNK = 64
R_GROUP = 4
R_SUB = 4
R_LN_EPS = 64e-5
C_HEADS = 8
C_CONV = 4
C_ROWS_PER_STEP = 2
RG_C = 8.0

SCAN_OUT_DTYPE = BF16
VMEM_LIMIT_BYTES = 56 * 1024 * 1024
HALO = 8


def _cparams(*sem):
    return pltpu.CompilerParams(dimension_semantics=sem, vmem_limit_bytes=VMEM_LIMIT_BYTES)


def _bdot(a, b):
    return jnp.dot(a.astype(BF16), b.astype(BF16), preferred_element_type=F32)


def _bdot_nt(a, b):
    return lax.dot_general(a.astype(BF16), b.astype(BF16), (((1,), (1,)), ((), ())),
                           preferred_element_type=F32)


def _split3(x):
    hi = x.astype(BF16)
    r1 = x - hi.astype(F32)
    mid = r1.astype(BF16)
    lo = (r1 - mid.astype(F32)).astype(BF16)
    return hi, mid, lo


def _tri_cumsum(tri_bf16, x, parts=3):
    hi, mid, lo = _split3(x)
    dot = lambda p: jnp.dot(tri_bf16, p, preferred_element_type=F32)
    if parts == 2:
        return dot(hi) + dot(mid)
    return dot(hi) + (dot(mid) + dot(lo))


def _silu(x):
    return x * jax.nn.sigmoid(x)


def _softplus(x):
    return jnp.maximum(x, 0.0) + jnp.log1p(jnp.exp(-jnp.abs(x)))


def _adaln_kernel(cb_ref, w_ref, b_ref, o_ref, *, n_vec, tn):
    rows = []
    for v in range(n_vec):
        cols = []
        for j in range(tn // 128):
            wj = w_ref[:, j * 128:(j + 1) * 128]
            cols.append(jnp.sum(wj * cb_ref[v], axis=0, keepdims=True))
        rows.append(jnp.concatenate(cols, axis=1))
    pad = jnp.zeros((HALO - n_vec, tn), F32)
    o_ref[...] = jnp.concatenate(rows + [pad], axis=0) + b_ref[...]


def _adaln(cond, ada_w, ada_b):
    depth, d, n6 = ada_w.shape
    n_vec = cond.shape[0]
    tn = 1024
    cb = jnp.broadcast_to(_silu(cond)[:, :, None], (n_vec, d, 128))
    return pl.pallas_call(
        functools.partial(_adaln_kernel, n_vec=n_vec, tn=tn),
        grid=(depth, n6 // tn),
        in_specs=[pl.BlockSpec((n_vec, d, 128), lambda l, j: (0, 0, 0)),
                  pl.BlockSpec((None, d, tn), lambda l, j: (l, 0, j)),
                  pl.BlockSpec((None, 1, tn), lambda l, j: (l, 0, j))],
        out_specs=pl.BlockSpec((None, HALO, tn), lambda l, j: (l, 0, j)),
        out_shape=jax.ShapeDtypeStruct((depth, HALO, n6), F32),
        compiler_params=_cparams("arbitrary", "arbitrary"),
    )(cb, ada_w, ada_b.reshape(depth, 1, n6))


def _modulated(x_ref, nw_ref, sh_ref, sc_ref):
    x = x_ref[...]
    y = x * lax.rsqrt(jnp.mean(x * x, axis=-1, keepdims=True) + NORM_EPS)
    return (y * nw_ref[...]) * (1.0 + sc_ref[...]) + sh_ref[...]


def _norm_mm_kernel(x_ref, nw_ref, sh_ref, sc_ref, w_ref, o_ref, h_ref):
    @pl.when(pl.program_id(1) == 0)
    def _():
        h_ref[...] = _modulated(x_ref, nw_ref, sh_ref, sc_ref).astype(BF16)

    o_ref[...] = jnp.dot(h_ref[...], w_ref[...].astype(BF16),
                         preferred_element_type=F32).astype(o_ref.dtype)


def _norm_swiglu_kernel(x_ref, nw_ref, sh_ref, sc_ref, wg_ref, wu_ref, o_ref, h_ref):
    @pl.when(pl.program_id(1) == 0)
    def _():
        h_ref[...] = _modulated(x_ref, nw_ref, sh_ref, sc_ref).astype(BF16)

    h = h_ref[...]
    g = jnp.dot(h, wg_ref[...].astype(BF16), preferred_element_type=F32)
    u = jnp.dot(h, wu_ref[...].astype(BF16), preferred_element_type=F32)
    o_ref[...] = (_silu(g) * u).astype(o_ref.dtype)


def _row_tile(m, cap):
    t = cap
    while m % t:
        t //= 2
    return t


def _norm_mm(x, nw, shift, scale, weights, rows_per_vec, out_dtype, tm_cap=1024, tn=512, layer=0):
    m, d = x.shape
    n = weights[0].shape[-1]
    tm = _row_tile(rows_per_vec, tm_cap)
    vec_map = lambda i, j: ((i * tm) // rows_per_vec, 0, 0)
    kern = _norm_mm_kernel if len(weights) == 1 else _norm_swiglu_kernel

    def w_spec(w):
        if w.ndim == 3:
            return pl.BlockSpec((None, d, tn), lambda i, j: (layer, 0, j))
        return pl.BlockSpec((d, tn), lambda i, j: (0, j))
    return pl.pallas_call(
        kern,
        grid=(m // tm, n // tn),
        in_specs=[pl.BlockSpec((tm, d), lambda i, j: (i, 0)),
                  pl.BlockSpec((1, d), lambda i, j: (0, 0)),
                  pl.BlockSpec((None, 1, d), vec_map),
                  pl.BlockSpec((None, 1, d), vec_map)]
                 + [w_spec(w) for w in weights],
        out_specs=pl.BlockSpec((tm, tn), lambda i, j: (i, j)),
        out_shape=jax.ShapeDtypeStruct((m, n), out_dtype),
        scratch_shapes=[pltpu.VMEM((tm, d), BF16)],
        compiler_params=_cparams("arbitrary", "arbitrary"),
    )(x, nw.reshape(1, d), shift, scale, *weights)


def _mm_res_kernel(a_ref, w_ref, res_ref, g_ref, o_ref):
    acc = jnp.dot(a_ref[...], w_ref[...], preferred_element_type=F32)
    o_ref[...] = res_ref[...] + g_ref[...] * acc


def _mm_res(a, w, res, gate, rows_per_vec, tm_cap=1024, tn=512):
    m, k = a.shape
    n = w.shape[1]
    tm = _row_tile(rows_per_vec, tm_cap)
    return pl.pallas_call(
        _mm_res_kernel,
        grid=(m // tm, n // tn),
        in_specs=[pl.BlockSpec((tm, k), lambda i, j: (i, 0)),
                  pl.BlockSpec((k, tn), lambda i, j: (0, j)),
                  pl.BlockSpec((tm, tn), lambda i, j: (i, j)),
                  pl.BlockSpec((None, 1, tn), lambda i, j: ((i * tm) // rows_per_vec, 0, j))],
        out_specs=pl.BlockSpec((tm, tn), lambda i, j: (i, j)),
        out_shape=jax.ShapeDtypeStruct((m, n), F32),
        compiler_params=_cparams("arbitrary", "arbitrary"),
    )(a, w, res, gate)


def _final_norm_kernel(x_ref, w_ref, o_ref):
    x = x_ref[...]
    o_ref[...] = x * lax.rsqrt(jnp.mean(x * x, axis=-1, keepdims=True) + NORM_EPS) * w_ref[...]


def _final_norm(x, w):
    m, d = x.shape
    tm = _row_tile(m, 512)
    return pl.pallas_call(
        _final_norm_kernel,
        grid=(m // tm,),
        in_specs=[pl.BlockSpec((tm, d), lambda i: (i, 0)), pl.BlockSpec((1, d), lambda i: (0, 0))],
        out_specs=pl.BlockSpec((tm, d), lambda i: (i, 0)),
        out_shape=jax.ShapeDtypeStruct((m, d), F32),
        compiler_params=_cparams("arbitrary"),
    )(x, w.reshape(1, d))


def _scan_block(i, n_blocks, rev):
    return (n_blocks - 1 - i) if rev else i


def _halo_block(i, n_blocks, rows, rev):
    per = rows // HALO
    if rev:
        return jnp.minimum((_scan_block(i, n_blocks, True) + 1) * per, n_blocks * per - 1)
    return jnp.maximum(i * per - 1, 0)


def _stage_with_halo(scr_ref, x_ref, halo_ref, keep, rows, rev):
    if rev:
        scr_ref[0:rows, :] = x_ref[...]
        scr_ref[rows:rows + HALO, :] = halo_ref[...] * keep
    else:
        scr_ref[0:HALO, :] = halo_ref[...] * keep
        scr_ref[HALO:HALO + rows, :] = x_ref[...]


def _lagged(scr_ref, lag, rows, rev):
    start = lag if rev else HALO - lag
    return scr_ref[start:start + rows, :]


def _rwkv_kernel(rkv_ref, sm_ref, rkv_h_ref, sm_h_ref, mu_rkv_ref, mu_sm_ref, w0_ref, w2_ref,
                 a0_ref, a2_ref, kkw_ref, kaw_ref, rkw_ref, s0_ref,
                 y_ref, bonus_ref, sl_ref, s_scr, rkv_scr, sm_scr, *, rev, n_heads, lora, nsub):
    i = pl.program_id(1)
    c = R_CHUNK
    rows = nsub * c
    n = R_HEADSIZE
    hw = n_heads * n

    @pl.when(i == 0)
    def _():
        s_scr[...] = s0_ref[...]

    keep = (i > 0).astype(F32)
    _stage_with_halo(rkv_scr, rkv_ref, rkv_h_ref, keep, rows, rev)
    _stage_with_halo(sm_scr, sm_ref, sm_h_ref, keep, rows, rev)
    x = rkv_ref[...]
    f = x + (_lagged(rkv_scr, 1, rows, rev) - x) * mu_rkv_ref[...]
    xs = sm_ref[...]
    fs = xs + (_lagged(sm_scr, 1, rows, rev) - xs) * mu_sm_ref[...]
    r = f[:, :hw]
    k = f[:, hw:2 * hw]
    v = f[:, 2 * hw:3 * hw]
    wl = fs[:, 256:256 + lora]
    al = fs[:, 384:384 + lora]

    ww = w0_ref[...] + _bdot(jnp.tanh(wl), w2_ref[...])
    lw = -math.exp(-0.5) * jax.nn.sigmoid(ww)
    a = jax.nn.sigmoid(a0_ref[...] + _bdot(al, a2_ref[...]))
    kkr = k * kkw_ref[...]
    k2 = k * (1.0 + (a - 1.0) * kaw_ref[...])
    rk = r * k2 * rkw_ref[...]

    cshift = int(math.log2(c))
    ti = lax.broadcasted_iota(jnp.int32, (rows, rows), 0)
    tj = lax.broadcasted_iota(jnp.int32, (rows, rows), 1)
    same_chunk = (ti >> cshift) == (tj >> cshift)
    upto = same_chunk & ((tj >= ti) if rev else (tj <= ti))
    g = _tri_cumsum(upto.astype(BF16), lw, parts=2)
    rsl = [slice(k * c, (k + 1) * c) for k in range(nsub)]
    gtot = [g[k * c:k * c + 1, :] if rev else g[(k + 1) * c - 1:(k + 1) * c, :] for k in range(nsub)]
    eg = jnp.exp(g)
    eneg = jnp.exp(-g)
    egp = jnp.exp(g - lw)
    etail = jnp.exp(jnp.concatenate([jnp.broadcast_to(t, (c, hw)) for t in gtot], axis=0) - g)
    dg = [jnp.exp(t) for t in gtot]
    rd_all = r * eg
    kp_all = k2 * eneg
    kpp_all = k2 * etail

    gw = R_GROUP * n
    groups = range(hw // gw)
    gsl = [slice(q * gw, (q + 1) * gw) for q in groups]
    shift = int(math.log2(n))
    bi = lax.broadcasted_iota(jnp.int32, (gw, gw), 0) >> shift
    bj = lax.broadcasted_iota(jnp.int32, (gw, gw), 1) >> shift
    same_head = bi == bj
    ones_bd = same_head.astype(BF16)
    lane_head = lax.broadcasted_iota(jnp.int32, (1, 128), 1) >> shift
    keep_f32 = [(lane_head == j).astype(F32) for j in range(128 // n)]
    keep_lanes = [t.astype(BF16) for t in keep_f32]

    def bdiag(t):
        tb = t.astype(BF16)
        zero = jnp.zeros((c, 128), BF16)
        rows = []
        for h in range(R_GROUP):
            lt, j = divmod(h * n, 128)
            piece = tb[:, lt * 128:(lt + 1) * 128] * keep_lanes[j // n]
            rows.append(jnp.concatenate([piece if u == lt else zero for u in range(gw // 128)], axis=1))
        return jnp.concatenate(rows, axis=0)

    def diag_blocks(full):
        tiles = []
        for lt in range(gw // 128):
            acc = None
            for j in range(128 // n):
                h = lt * (128 // n) + j
                part = full[h * n:(h + 1) * n, lt * 128:(lt + 1) * 128] * keep_f32[j]
                acc = part if acc is None else acc + part
            tiles.append(acc)
        return jnp.concatenate(tiles, axis=1)

    def head_sums(t):
        hi, mid, _ = _split3(t)
        dot = lambda p: jnp.dot(p, ones_bd, preferred_element_type=F32)
        return dot(hi) + dot(mid)

    tdot = lambda p, q: lax.dot_general(p.astype(BF16), q.astype(BF16), (((0,), (0,)), ((), ())),
                                        preferred_element_type=F32)
    gi = lax.broadcasted_iota(jnp.int32, (c, gw), 0)
    gj = lax.broadcasted_iota(jnp.int32, (c, gw), 1) & (c - 1)
    before_g = (gj > gi) if rev else (gj < gi)
    upto_g = (gj >= gi) if rev else (gj <= gi)
    eye_g = (gi == gj).astype(F32)

    kk_all = jnp.concatenate(
        [kkr[:, s] * lax.rsqrt(jnp.maximum(head_sums(kkr[:, s] * kkr[:, s]), 1e-24)) for s in gsl], axis=1)
    bh_all = kk_all * a
    kkd_all = kk_all * egp
    bp_all = bh_all * eneg
    bpp_all = bh_all * etail
    bonus_ref[...] = (jnp.concatenate([head_sums(rk[:, s]) for s in gsl], axis=1) * v
                      ).astype(bonus_ref.dtype)

    units = [(k, q) for k in range(nsub) for q in groups]
    un = range(len(units))
    cut = lambda t: [t[rsl[k], gsl[q]] for k, q in units]
    kkd, bp, bpp, rd, vq, kp, kpp = (cut(t) for t in (kkd_all, bp_all, bpp_all, rd_all, v, kp_all, kpp_all))
    amat = [_bdot_nt(jnp.concatenate([kkd[u], rd[u]], axis=0),
                     jnp.concatenate([bdiag(kp[u]), bdiag(bp[u])], axis=0)) for u in un]
    mk = [jnp.where(before_g, m[:c, :gw], 0.0) for m in amat]
    nk = [jnp.where(upto_g, m[c:, :gw], 0.0) for m in amat]
    nb = [jnp.where(upto_g, m[c:, gw:], 0.0) for m in amat]
    p2 = [jnp.where(before_g, -m[:c, gw:], 0.0) for m in amat]
    tinv = [eye_g + t for t in p2]
    p2 = [_bdot(t, bdiag(t)) for t in p2]
    for _ in range(cshift - 2):
        sq = [_bdot(jnp.concatenate([tinv[u], p2[u]], axis=0), bdiag(p2[u])) for u in un]
        tinv = [tinv[u] + sq[u][:c] for u in un]
        p2 = [t[c:] for t in sq]
    tinv = [tinv[u] + _bdot(tinv[u], bdiag(p2[u])) for u in un]
    mnv = [_bdot(jnp.concatenate([mk[u], nk[u]], axis=0), bdiag(vq[u])) for u in un]
    wu = [_bdot(tinv[u], jnp.concatenate([bdiag(kkd[u]), bdiag(mnv[u][:c])], axis=1))
          for u in un]
    nwu = [_bdot(nb[u], jnp.concatenate([bdiag(wu[u][:, :gw]), bdiag(wu[u][:, gw:])], axis=1))
           for u in un]
    wub = [tdot(wu[u], bpp[u]) for u in un]
    pmat = [bdiag(diag_blocks(t[:gw])) for t in wub]
    dmat = [diag_blocks(tdot(vq[u], kpp[u])) - diag_blocks(wub[u][gw:]) for u in un]
    qt = [rd[u] - nwu[u][:, :gw] for u in un]
    y0 = [mnv[u][c:] - nwu[u][:, gw:] for u in un]
    state = [s_scr[q] for q in groups]
    for k in (reversed(range(nsub)) if rev else range(nsub)):
        for q in groups:
            u = k * len(groups) + q
            y_ref[rsl[k], gsl[q]] = (y0[u] + _bdot_nt(qt[u], bdiag(state[q]))).astype(y_ref.dtype)
            state[q] = state[q] * dg[k][:, gsl[q]] - _bdot(state[q], pmat[u]) + dmat[u]
    for q in groups:
        s_scr[q] = state[q]

    @pl.when(i == pl.num_programs(1) - 1)
    def _():
        sl_ref[...] = s_scr[...]


def _rwkv_direction(pr, s0, p, rev):
    bsz, seqlen, width = pr.shape
    hw = s0.shape[1] * s0.shape[3]
    n_heads = hw // R_HEADSIZE
    nsub = R_SUB if seqlen % (R_SUB * R_CHUNK) == 0 else 1
    c = nsub * R_CHUNK
    nblk = seqlen // c
    sm_col = width // 512 - 1
    blk = lambda i: _scan_block(i, nblk, rev)
    halo = lambda i: _halo_block(i, nblk, c, rev)
    vec = lambda w: pl.BlockSpec((1, w), lambda b, i: (0, 0))
    lora = p['w2'].shape[0]
    state_spec = pl.BlockSpec((None,) + s0.shape[1:], lambda b, i: (b, 0, 0, 0))
    return pl.pallas_call(
        functools.partial(_rwkv_kernel, rev=rev, n_heads=n_heads, lora=lora, nsub=nsub),
        grid=(bsz, nblk),
        in_specs=[pl.BlockSpec((None, c, 3 * hw), lambda b, i: (b, blk(i), 1)),
                  pl.BlockSpec((None, c, 512), lambda b, i: (b, blk(i), sm_col)),
                  pl.BlockSpec((None, HALO, 3 * hw), lambda b, i: (b, halo(i), 1)),
                  pl.BlockSpec((None, HALO, 512), lambda b, i: (b, halo(i), sm_col)),
                  vec(3 * hw), vec(512), vec(hw),
                  pl.BlockSpec((lora, hw), lambda b, i: (0, 0)),
                  vec(hw),
                  pl.BlockSpec((lora, hw), lambda b, i: (0, 0)),
                  vec(hw), vec(hw), vec(hw),
                  state_spec],
        out_specs=[pl.BlockSpec((None, c, hw), lambda b, i: (b, blk(i), 0)),
                   pl.BlockSpec((None, c, hw), lambda b, i: (b, blk(i), 0)),
                   state_spec],
        out_shape=[jax.ShapeDtypeStruct((bsz, seqlen, hw), SCAN_OUT_DTYPE),
                   jax.ShapeDtypeStruct((bsz, seqlen, hw), SCAN_OUT_DTYPE),
                   jax.ShapeDtypeStruct(s0.shape, F32)],
        scratch_shapes=[pltpu.VMEM(s0.shape[1:], F32),
                        pltpu.VMEM((c + HALO, 3 * hw), F32),
                        pltpu.VMEM((c + HALO, 512), F32)],
        compiler_params=_cparams("arbitrary", "arbitrary"),
    )(pr, pr, pr, pr, p['mu_rkv'], p['mu_sm'], p['w0'], p['w2'], p['a0'], p['a2'],
      p['kk'], p['ka'], p['rk'], s0)


def _expand_cols(x, e_bf16):
    hi, mid, lo = _split3(x)
    dot = lambda p: jnp.dot(p, e_bf16, preferred_element_type=F32)
    return dot(hi) + (dot(mid) + dot(lo))


def _causal_conv(scr_ref, w_ref, b_ref, rows, taps, rev):
    acc = b_ref[...] + w_ref[taps - 1:taps, :] * _lagged(scr_ref, 0, rows, rev)
    for kk in range(taps - 1):
        acc = acc + w_ref[kk:kk + 1, :] * _lagged(scr_ref, taps - 1 - kk, rows, rev)
    return acc


def _ssd_kernel(xbc_ref, halo_ref, sm_ref, dtt_ref, cw_ref, cb_ref, dtb_row_ref, dtb_col_ref,
                alog_row_ref, alog_col_ref, dskip_ref, e64_ref, hmask_ref, h0_ref,
                y_ref, hl_ref, h_scr, x_scr, *, rev, n_heads):
    i = pl.program_id(1)
    c = M_CHUNK
    p = M_HEADDIM
    ns = M_STATE
    mix = n_heads * p
    hpg = n_heads // M_GROUPS
    gw = hpg * p

    @pl.when(i == 0)
    def _():
        h_scr[...] = h0_ref[...]

    keep = (i > 0).astype(F32)
    _stage_with_halo(x_scr, xbc_ref, halo_ref, keep, c, rev)
    xbc = _silu(_causal_conv(x_scr, cw_ref, cb_ref, c, M_CONV, rev))
    xs = xbc[:, :mix]
    bm = [xbc[:, mix + g * ns:mix + (g + 1) * ns] for g in range(M_GROUPS)]
    cm = [xbc[:, mix + (M_GROUPS + g) * ns:mix + (M_GROUPS + g + 1) * ns] for g in range(M_GROUPS)]

    ti = lax.broadcasted_iota(jnp.int32, (c, c), 0)
    tj = lax.broadcasted_iota(jnp.int32, (c, c), 1)
    upto = (tj >= ti) if rev else (tj <= ti)
    upto_t = (ti >= tj) if rev else (ti <= tj)
    end = 0 if rev else c - 1

    dt = _softplus(sm_ref[:, 480:480 + n_heads] + dtb_row_ref[...])
    a_cs = _tri_cumsum(upto.astype(BF16), dt * -jnp.exp(alog_row_ref[...]))
    dt_t = _softplus(dtt_ref[...] + dtb_col_ref[...])
    a_cs_t = _expand_cols(dt_t * -jnp.exp(alog_col_ref[...]), upto_t.astype(BF16))
    a_end_t = a_cs_t[:, end:end + 1]
    w_t = dt_t * jnp.exp(a_end_t - a_cs_t)
    grow_t = jnp.broadcast_to(jnp.exp(a_end_t), (n_heads, ns))

    parts = [t.astype(F32) for t in _split3(a_cs)]
    ones = jnp.ones((c, n_heads), F32)
    lhs = jnp.concatenate(parts + [ones] * 3, axis=1).astype(BF16)
    hmask = hmask_ref[...]
    parts_t = jnp.concatenate(list(_split3(a_cs_t)), axis=0)
    rhs = jnp.concatenate([hmask, -jnp.concatenate([parts_t] * n_heads, axis=1) * hmask], axis=0)
    seg = jnp.dot(lhs, rhs, preferred_element_type=F32)
    ea_full = jnp.exp(jnp.dot(lhs[:, :3 * n_heads], e64_ref[...], preferred_element_type=F32))

    lane_head = lax.broadcasted_iota(jnp.int32, (1, gw), 1) >> int(math.log2(p))
    keep_lanes = [(lane_head == j).astype(BF16) for j in range(hpg)]
    xs_t = jnp.transpose(xs)
    y_diag, y_off = [], []
    for g in range(M_GROUPS):
        heads = range(g * hpg, (g + 1) * hpg)
        cb = _bdot_nt(cm[g], bm[g])
        scores = []
        for h in heads:
            decay = jnp.where(upto, jnp.exp(jnp.minimum(seg[:, h * c:(h + 1) * c], 0.0)), 0.0)
            scores.append((cb * decay * dt_t[h:h + 1, :]).astype(BF16))
        xg = xs[:, g * gw:(g + 1) * gw].astype(BF16)
        x_bd = jnp.concatenate([xg * keep_lanes[j] for j in range(hpg)], axis=0)
        y_diag.append(jnp.dot(jnp.concatenate(scores, axis=1), x_bd, preferred_element_type=F32))
        h_in = h_scr[g]
        y_off.append(_bdot_nt(cm[g], h_in))
        wrows = jnp.concatenate([jnp.broadcast_to(w_t[h:h + 1, :], (p, c)) for h in heads], axis=0)
        grow = jnp.concatenate([jnp.broadcast_to(grow_t[h:h + 1, :], (p, ns)) for h in heads], axis=0)
        h_scr[g] = h_in * grow + _bdot(xs_t[g * gw:(g + 1) * gw, :] * wrows, bm[g])
    y_ref[...] = (jnp.concatenate(y_diag, axis=1) + jnp.concatenate(y_off, axis=1) * ea_full
                  + xs * dskip_ref[...]).astype(y_ref.dtype)

    @pl.when(i == pl.num_programs(1) - 1)
    def _():
        hl_ref[...] = h_scr[...]


def _ssd_direction(pr, dt_t, h0, p, rev):
    bsz, seqlen, width = pr.shape
    n_heads = dt_t.shape[1]
    mix = n_heads * M_HEADDIM
    xw = mix + 2 * M_GROUPS * M_STATE
    c = M_CHUNK
    nblk = seqlen // c
    sm_col = width // 512 - 1
    blk = lambda i: _scan_block(i, nblk, rev)
    halo = lambda i: _halo_block(i, nblk, c, rev)
    full = lambda a: pl.BlockSpec(a.shape, lambda b, i: (0,) * a.ndim)
    consts = [p['conv_w'], p['conv_b'], p['dtb_row'], p['dtb_col'], p['alog_row'], p['alog_col'],
              p['dskip'], p['e64'], p['hmask']]
    state_spec = pl.BlockSpec((None,) + h0.shape[1:], lambda b, i: (b, 0, 0, 0))
    return pl.pallas_call(
        functools.partial(_ssd_kernel, rev=rev, n_heads=n_heads),
        grid=(bsz, nblk),
        in_specs=[pl.BlockSpec((None, c, xw), lambda b, i: (b, blk(i), 0)),
                  pl.BlockSpec((None, HALO, xw), lambda b, i: (b, halo(i), 0)),
                  pl.BlockSpec((None, c, 512), lambda b, i: (b, blk(i), sm_col)),
                  pl.BlockSpec((None, n_heads, c), lambda b, i: (b, 0, blk(i)))]
                 + [full(a) for a in consts] + [state_spec],
        out_specs=[pl.BlockSpec((None, c, mix), lambda b, i: (b, blk(i), 0)), state_spec],
        out_shape=[jax.ShapeDtypeStruct((bsz, seqlen, mix), SCAN_OUT_DTYPE),
                   jax.ShapeDtypeStruct(h0.shape, F32)],
        scratch_shapes=[pltpu.VMEM(h0.shape[1:], F32), pltpu.VMEM((c + HALO, xw), F32)],
        compiler_params=_cparams("arbitrary", "arbitrary"),
    )(pr, pr, pr, dt_t, *consts, h0)


def _rglru_kernel(x_ref, halo_ref, cw_ref, cb_ref, wa_ref, ba_ref, wx_ref, bx_ref, lam_ref, h0_ref,
                  o_ref, hl_ref, h_scr, x_scr, a_scr, u_scr, *, rev, tb):
    i = pl.program_id(1)

    @pl.when(i == 0)
    def _():
        h_scr[...] = h0_ref[...]

    keep = (i > 0).astype(F32)
    _stage_with_halo(x_scr, x_ref, halo_ref, keep, tb, rev)
    xc = _causal_conv(x_scr, cw_ref, cb_ref, tb, C_CONV, rev)
    a_scr[...], u_scr[...] = _rglru_gates(xc, wa_ref, ba_ref, wx_ref, bx_ref, lam_ref)

    def step(t, h):
        tt = (tb - 1 - t) if rev else t
        h = a_scr[pl.ds(tt, 1), :] * h + u_scr[pl.ds(tt, 1), :]
        o_ref[pl.ds(tt, 1), :] = h
        return h

    h_scr[0:1, :] = lax.fori_loop(0, tb, step, h_scr[0:1, :], unroll=8)

    @pl.when(i == pl.num_programs(1) - 1)
    def _():
        hl_ref[...] = h_scr[...]


def _rglru_direction(xv, h0, p, rev, tb):
    bsz, rows, wtot = xv.shape
    width = h0.shape[2]
    ncol = wtot // (2 * width)
    nrb = rows // tb
    total = ncol * nrb
    per = tb // HALO

    def pos(i):
        s = _scan_block(i, total, rev)
        return s // nrb, s % nrb

    def x_map(b, i):
        col, q = pos(i)
        return b, q, 2 * col + 1

    def halo_map(b, i):
        s = _scan_block(i, total, rev)
        sp = jnp.minimum(s + 1, total - 1) if rev else jnp.maximum(s - 1, 0)
        col, q = sp // nrb, sp % nrb
        return b, (q * per if rev else (q + 1) * per - 1), 2 * col + 1

    def o_map(b, i):
        col, q = pos(i)
        return b, q, col

    full = lambda a: pl.BlockSpec(a.shape, lambda b, i: (0,) * a.ndim)
    consts = [p['conv_w'], p['conv_b'], p['wa'], p['ba'], p['wx'], p['bx'], p['lam']]
    state_spec = pl.BlockSpec((None, HALO, width), lambda b, i: (b, 0, 0))
    return pl.pallas_call(
        functools.partial(_rglru_kernel, rev=rev, tb=tb),
        grid=(bsz, total),
        in_specs=[pl.BlockSpec((None, tb, width), x_map), pl.BlockSpec((None, HALO, width), halo_map)]
                 + [full(a) for a in consts] + [state_spec],
        out_specs=[pl.BlockSpec((None, tb, width), o_map), state_spec],
        out_shape=[jax.ShapeDtypeStruct((bsz, rows, ncol * width), F32),
                   jax.ShapeDtypeStruct(h0.shape, F32)],
        scratch_shapes=[pltpu.VMEM((HALO, width), F32), pltpu.VMEM((tb + HALO, width), F32),
                        pltpu.VMEM((tb, width), F32), pltpu.VMEM((tb, width), F32)],
        compiler_params=_cparams("arbitrary", "arbitrary"),
    )(xv, xv, *consts, h0)


def _rglru_gates(xc, wa_ref, ba_ref, wx_ref, bx_ref, lam_ref):
    blk = xc.shape[1] // C_HEADS
    xh = [xc[:, h * blk:(h + 1) * blk] for h in range(C_HEADS)]
    gate_r = jax.nn.sigmoid(
        jnp.concatenate([_bdot(xh[h], wa_ref[h]) for h in range(C_HEADS)], axis=1) + ba_ref[...])
    gate_i = jax.nn.sigmoid(
        jnp.concatenate([_bdot(xh[h], wx_ref[h]) for h in range(C_HEADS)], axis=1) + bx_ref[...])
    log_a = -RG_C * gate_r * _softplus(-lam_ref[...])
    a = jnp.exp(log_a)
    one_m_a2 = jnp.maximum(-jnp.tanh(log_a) * (a * a + 1.0), 0.0)
    mult = jnp.where(one_m_a2 > 0.0, one_m_a2 * lax.rsqrt(one_m_a2), 0.0)
    return a, mult * gate_i * xc


def _rglru_grid_kernel(x_ref, e0_ref, e1_ref, e2_ref, cw_ref, cb_ref, wa_ref, ba_ref, wx_ref,
                       bx_ref, lam_ref, h0_ref, hloc_ref, aprod_ref, cin_ref, hl_ref,
                       ring_scr, h_scr, ap_scr, stage_scr, *, rev, ncol):
    i = pl.program_id(1)

    def from_previous_column(e_ref):
        _stage_with_halo(stage_scr, e_ref, jnp.zeros((HALO, e_ref.shape[1]), F32), 0.0, ncol, rev)
        return _lagged(stage_scr, 1, ncol, rev)

    @pl.when(i == 0)
    def _():
        ring_scr[0] = from_previous_column(e2_ref)
        ring_scr[1] = from_previous_column(e1_ref)
        ring_scr[2] = from_previous_column(e0_ref)
        h_scr[...] = jnp.zeros(h_scr.shape, F32)
        ap_scr[...] = jnp.ones(ap_scr.shape, F32)

    nrow = x_ref.shape[0] // ncol
    order = list(reversed(range(nrow))) if rev else list(range(nrow))
    rsl = [slice(m * ncol, (m + 1) * ncol) for m in range(nrow)]
    hist = [ring_scr[2], ring_scr[1], ring_scr[0]] + [x_ref[rsl[m], :] for m in order]
    xc = [cb_ref[...] + cw_ref[3:4, :] * hist[t + 3] + cw_ref[2:3, :] * hist[t + 2]
          + cw_ref[1:2, :] * hist[t + 1] + cw_ref[0:1, :] * hist[t] for t in range(nrow)]
    for lag in range(3):
        ring_scr[lag] = hist[-1 - lag]
    a, u = _rglru_gates(jnp.concatenate(xc, axis=0), wa_ref, ba_ref, wx_ref, bx_ref, lam_ref)
    h, ap = h_scr[...], ap_scr[...]
    for t, m in enumerate(order):
        ts = slice(t * ncol, (t + 1) * ncol)
        h = a[ts, :] * h + u[ts, :]
        ap = a[ts, :] * ap
        hloc_ref[rsl[m], :] = h.astype(hloc_ref.dtype)
        aprod_ref[rsl[m], :] = ap.astype(aprod_ref.dtype)
    h_scr[...] = h
    ap_scr[...] = ap

    @pl.when(i == pl.num_programs(1) - 1)
    def _():
        def step(t, carry):
            col = (ncol - 1 - t) if rev else t
            cin_ref[pl.ds(col, 1), :] = carry
            return h_scr[pl.ds(col, 1), :] + ap_scr[pl.ds(col, 1), :] * carry

        last = lax.fori_loop(0, ncol, step, h0_ref[0:1, :], unroll=8)
        hl_ref[...] = jnp.zeros(hl_ref.shape, F32)
        hl_ref[0:1, :] = last


def _rglru_grid_direction(p3, h0, p, rev):
    bsz, seqlen, _ = p3.shape
    width = h0.shape[2]
    ncol = GRID_W
    rows = seqlen // ncol
    nrow = C_ROWS_PER_STEP if rows % C_ROWS_PER_STEP == 0 else 1
    nblk = rows // nrow
    blk = lambda i: _scan_block(i, nblk, rev)
    edge = lambda k: pl.BlockSpec((None, ncol, width),
                                  lambda b, i: (b, (2 - k) if rev else (rows - 3 + k), 1))
    full = lambda a: pl.BlockSpec(a.shape, lambda b, i: (0,) * a.ndim)
    consts = [p['conv_w'], p['conv_b'], p['wa'], p['ba'], p['wx'], p['bx'], p['lam']]
    state_spec = pl.BlockSpec((None, HALO, width), lambda b, i: (b, 0, 0))
    row_spec = pl.BlockSpec((None, nrow * ncol, width), lambda b, i: (b, blk(i), 0))
    return pl.pallas_call(
        functools.partial(_rglru_grid_kernel, rev=rev, ncol=ncol),
        grid=(bsz, nblk),
        in_specs=[pl.BlockSpec((None, nrow * ncol, width), lambda b, i: (b, blk(i), 1)),
                  edge(0), edge(1), edge(2)] + [full(a) for a in consts] + [state_spec],
        out_specs=[row_spec, row_spec,
                   pl.BlockSpec((None, ncol, width), lambda b, i: (b, 0, 0)), state_spec],
        out_shape=[jax.ShapeDtypeStruct((bsz, seqlen, width), SCAN_OUT_DTYPE),
                   jax.ShapeDtypeStruct((bsz, seqlen, width), SCAN_OUT_DTYPE),
                   jax.ShapeDtypeStruct((bsz, ncol, width), F32),
                   jax.ShapeDtypeStruct(h0.shape, F32)],
        scratch_shapes=[pltpu.VMEM((3, ncol, width), F32), pltpu.VMEM((ncol, width), F32),
                        pltpu.VMEM((ncol, width), F32), pltpu.VMEM((ncol + HALO, width), F32)],
        compiler_params=_cparams("arbitrary", "arbitrary"),
    )(p3, p3, p3, p3, *consts, h0)


def _readout_ab_kernel(z_ref, sm_ref, ymf_ref, ymb_ref, yrf_ref, yrb_ref, bof_ref, bob_ref,
                       nw_ref, lw_ref, lb_ref, g2_ref, o_ref, *, n_heads):
    f32 = lambda ref: ref[...].astype(F32)
    t = (f32(ymf_ref) + f32(ymb_ref)) * _silu(z_ref[...])
    gw = t.shape[1] // M_GROUPS
    a_out = []
    for g in range(M_GROUPS):
        tg = t[:, g * gw:(g + 1) * gw]
        a_out.append(tg * lax.rsqrt(jnp.mean(tg * tg, axis=-1, keepdims=True) + M_NORM_EPS))
    a_out = jnp.concatenate(a_out, axis=1) * nw_ref[...]

    yr = f32(yrf_ref) + f32(yrb_ref)
    n = R_HEADSIZE
    yn = []
    for h in range(n_heads):
        yh = yr[:, h * n:(h + 1) * n]
        dlt = yh - jnp.mean(yh, axis=-1, keepdims=True)
        yn.append(dlt * lax.rsqrt(jnp.mean(dlt * dlt, axis=-1, keepdims=True) + R_LN_EPS))
    yn = jnp.concatenate(yn, axis=1) * lw_ref[...] + lb_ref[...]
    gate = _bdot(jax.nn.sigmoid(sm_ref[:, 0:g2_ref.shape[0]]), g2_ref[...])
    b_out = (yn + f32(bof_ref) + f32(bob_ref)) * gate
    o_ref[...] = jnp.concatenate([a_out, b_out], axis=1).astype(o_ref.dtype)


def _readout_ab(pr, ym, yr, bo, ro, n_heads):
    bsz, seqlen, width = pr.shape
    m = bsz * seqlen
    mix = ym[0].shape[2]
    tm = _row_tile(m, 256)
    flat = lambda t: t.reshape(m, t.shape[2])
    rowblk = lambda w, col: pl.BlockSpec((tm, w), lambda i: (i, col))
    full = lambda a: pl.BlockSpec(a.shape, lambda i: (0,) * a.ndim)
    consts = [ro['norm_w'], ro['lnx_w'], ro['lnx_b'], ro['g2']]
    return pl.pallas_call(
        functools.partial(_readout_ab_kernel, n_heads=n_heads),
        grid=(m // tm,),
        in_specs=[rowblk(mix, 2), rowblk(512, width // 512 - 1)] + [rowblk(mix, 0)] * 6
                 + [full(a) for a in consts],
        out_specs=rowblk(2 * mix, 0),
        out_shape=jax.ShapeDtypeStruct((m, 2 * mix), BF16),
        compiler_params=_cparams("arbitrary"),
    )(flat(pr), flat(pr), flat(ym[0]), flat(ym[1]), flat(yr[0]), flat(yr[1]), flat(bo[0]),
      flat(bo[1]), *consts)


def _gelu_tanh(g):
    return 0.5 * g * (1.0 + jnp.tanh(math.sqrt(2.0 / math.pi) * (g + 0.044715 * (g * g * g))))


def _readout_c_kernel(gy_ref, hf_ref, hb_ref, o_ref):
    o_ref[...] = ((hf_ref[...] + hb_ref[...]) * _gelu_tanh(gy_ref[...])).astype(o_ref.dtype)


def _readout_grid_kernel(gy_ref, hf_ref, af_ref, cf_ref, hb_ref, ab_ref, cb_ref, o_ref, *, ncol):
    for j in range(o_ref.shape[0] // ncol):
        rs = slice(j * ncol, (j + 1) * ncol)
        f32 = lambda ref: ref[rs, :].astype(F32)
        h = (f32(hf_ref) + f32(af_ref) * cf_ref[...]) + (f32(hb_ref) + f32(ab_ref) * cb_ref[...])
        o_ref[rs, :] = (h * _gelu_tanh(gy_ref[rs, :])).astype(o_ref.dtype)


def _readout_grid(p3, fwd, bwd):
    bsz, seqlen, _ = p3.shape
    cw = fwd[0].shape[2]
    ncol = fwd[2].shape[1]
    tm = _row_tile(seqlen, 8 * ncol)
    spec = pl.BlockSpec((None, tm, cw), lambda b, i: (b, i, 0))
    cspec = pl.BlockSpec((None, ncol, cw), lambda b, i: (b, 0, 0))
    return pl.pallas_call(
        functools.partial(_readout_grid_kernel, ncol=ncol),
        grid=(bsz, seqlen // tm),
        in_specs=[spec, spec, spec, cspec, spec, spec, cspec],
        out_specs=spec,
        out_shape=jax.ShapeDtypeStruct((bsz, seqlen, cw), BF16),
        compiler_params=_cparams("arbitrary", "arbitrary"),
    )(p3, *fwd, *bwd)


def _readout_c(p, hf, hb):
    m, cw = hf.shape
    tm = _row_tile(m, 512)
    spec = pl.BlockSpec((tm, cw), lambda i: (i, 0))
    return pl.pallas_call(
        _readout_c_kernel,
        grid=(m // tm,),
        in_specs=[spec, spec, spec],
        out_specs=spec,
        out_shape=jax.ShapeDtypeStruct((m, cw), BF16),
        compiler_params=_cparams("arbitrary"),
    )(p, hf, hb)


def kernel(x, c, ctx, c_ctx, ada_w, ada_b, norm1_w, norm2_w, ffn_w_gate, ffn_w_up, ffn_w_down, final_norm_w, ab_w_in, ab_w_out, m_conv_w, m_conv_b, m_dt_bias, m_a_log, m_d, m_norm_w, r_mu, r_w0, r_w2, r_a0, r_a2, r_kk, r_ka, r_rk, r_g2, r_lnx_w, r_lnx_b, c_w_in, c_w_out, c_conv_w, c_conv_b, c_wa, c_ba, c_wx, c_bx, c_lambda):
    bsz, seqlen, d = x.shape
    ctx_len = ctx.shape[1]
    depth = ada_w.shape[0]
    n_lat, n_ctx = bsz * seqlen, bsz * ctx_len
    row = lambda t: t.reshape(1, -1)

    mod = _adaln(jnp.concatenate([c_ctx[None, :], c], axis=0), ada_w, ada_b)

    xl = x.reshape(n_lat, d)
    xc = ctx.reshape(n_ctx, d)
    mix_a = m_norm_w.shape[1]
    m_heads = m_dt_bias.shape[2]
    r_heads = r_rk.shape[2]
    lora = r_w2.shape[2]
    e64 = jnp.tile(jnp.repeat(jnp.eye(m_heads, dtype=BF16), M_HEADDIM, axis=1), (3, 1))
    hmask = jnp.tile(jnp.repeat(jnp.eye(m_heads, dtype=BF16), M_CHUNK, axis=1), (3, 1))

    for layer in range(depth):
        with_ctx = layer < depth - 1
        part = lambda k: mod[layer, :, k * d:(k + 1) * d]
        lat_vec = lambda k: part(k)[1:1 + bsz, None, :]
        ctx_vec = lambda k: part(k)[0:1, None, :]
        nw1, nw2 = norm1_w[layer], norm2_w[layer]

        if layer % 2 == 0:
            e = layer // 2
            w = ab_w_in[e].astype(BF16)
            o1 = mix_a
            o2 = o1 + mix_a + 2 * M_GROUPS * M_STATE
            o3 = o2 + m_heads
            o4 = o3 + 3 * r_heads * R_HEADSIZE + 2 * lora
            rkv_w = 3 * r_heads * R_HEADSIZE
            zeros = lambda n: jnp.zeros((d, n), w.dtype)
            w_in = jnp.concatenate(
                [w[:, o1:o2], w[:, :o1], w[:, o3:o3 + rkv_w], w[:, o4:],
                 w[:, o3 + rkv_w:o3 + rkv_w + lora], zeros(128 - lora),
                 w[:, o3 + rkv_w + lora:o4], w[:, o2:o3], zeros(128 - lora - m_heads)],
                axis=1)
            width = w_in.shape[1]
            w_out = ab_w_out[e].astype(BF16)
            g2w = r_g2[e].astype(BF16)

            def project(xs, sh, sc, rows_per_vec, seg_len):
                pr = _norm_mm(xs, nw1, sh, sc, [w_in], rows_per_vec, F32)
                pr = pr.reshape(bsz, seg_len, width)
                dt_t = jnp.swapaxes(pr[:, :, width - 32:width - 32 + m_heads], 1, 2)
                return pr, dt_t

            pr_l, dtt_l = project(xl, lat_vec(0), lat_vec(1), seqlen, seqlen)
            pr_c, dtt_c = project(xc, ctx_vec(0), ctx_vec(1), n_ctx, ctx_len)

            ym_l = ym_c = yr_l = yr_c = bo_l = bo_c = None
            for dr in range(2):
                rev = dr == 1
                mp = dict(conv_w=m_conv_w[e, dr], conv_b=row(m_conv_b[e, dr]),
                          dtb_row=row(m_dt_bias[e, dr]), dtb_col=m_dt_bias[e, dr].reshape(-1, 1),
                          alog_row=row(m_a_log[e, dr]), alog_col=m_a_log[e, dr].reshape(-1, 1),
                          dskip=row(jnp.repeat(m_d[e, dr], M_HEADDIM)), e64=e64, hmask=hmask)
                h0 = jnp.zeros((bsz, M_GROUPS, m_heads // M_GROUPS * M_HEADDIM, M_STATE), F32)
                y_c, h_c = _ssd_direction(pr_c, dtt_c, h0, mp, rev)
                y_l, _ = _ssd_direction(pr_l, dtt_l, h_c, mp, rev)
                mu = r_mu[e, dr]
                mu_sm = (jnp.zeros((512,), F32).at[256:256 + lora].set(mu[rkv_w:rkv_w + lora])
                         .at[384:384 + lora].set(mu[rkv_w + lora:]))
                rp = dict(mu_rkv=row(mu[:rkv_w]), mu_sm=row(mu_sm), w0=row(r_w0[e, dr]),
                          w2=r_w2[e, dr].astype(BF16), a0=row(r_a0[e, dr]), a2=r_a2[e, dr].astype(BF16),
                          kk=row(r_kk[e, dr]), ka=row(r_ka[e, dr]), rk=row(r_rk[e, dr]))
                s0 = jnp.zeros((bsz, r_heads // R_GROUP, R_HEADSIZE, R_GROUP * R_HEADSIZE), F32)
                v_c, b_c, s_c = _rwkv_direction(pr_c, s0, rp, rev)
                v_l, b_l, _ = _rwkv_direction(pr_l, s_c, rp, rev)
                if dr == 0:
                    ym_l, ym_c, yr_l, yr_c, bo_l, bo_c = y_l, y_c, v_l, v_c, b_l, b_c
                else:
                    ym_l, yr_l, bo_l = (ym_l, y_l), (yr_l, v_l), (bo_l, b_l)
                    ym_c, yr_c, bo_c = (ym_c, y_c), (yr_c, v_c), (bo_c, b_c)

            ro = dict(norm_w=row(m_norm_w[e]), lnx_w=row(r_lnx_w[e]), lnx_b=row(r_lnx_b[e]), g2=g2w)
            act_l = _readout_ab(pr_l, ym_l, yr_l, bo_l, ro, r_heads)
            xl = _mm_res(act_l, w_out, xl, lat_vec(2), seqlen)
            if with_ctx:
                act_c = _readout_ab(pr_c, ym_c, yr_c, bo_c, ro, r_heads)
                xc = _mm_res(act_c, w_out, xc, ctx_vec(2), n_ctx)
        else:
            o = layer // 2
            w_out = c_w_out[o].astype(BF16)
            cw = w_out.shape[0]
            p_l = _norm_mm(xl, nw1, lat_vec(0), lat_vec(1), [c_w_in], seqlen, F32, layer=o)
            p_c = _norm_mm(xc, nw1, ctx_vec(0), ctx_vec(1), [c_w_in], n_ctx, F32, layer=o)
            p3_l = p_l.reshape(bsz, seqlen, 2 * cw)
            hs_l, hs_c = [], []
            for dr in range(2):
                rev = dr == 1
                cp = dict(conv_w=c_conv_w[o, dr], conv_b=row(c_conv_b[o, dr]),
                          wa=c_wa[o, dr].astype(BF16), ba=row(c_ba[o, dr]),
                          wx=c_wx[o, dr].astype(BF16), bx=row(c_bx[o, dr]), lam=row(c_lambda[o, dr]))
                h0 = jnp.zeros((bsz, HALO, cw), F32)
                h_c, s_c = _rglru_direction(p_c.reshape(bsz, ctx_len, 2 * cw), h0, cp, rev,
                                            _row_tile(ctx_len, 128))
                hs_l.append(_rglru_grid_direction(p3_l, s_c, cp, rev)[:3])
                hs_c.append(h_c.reshape(n_ctx, cw))
            act_l = _readout_grid(p3_l, hs_l[0], hs_l[1]).reshape(n_lat, cw)
            xl = _mm_res(act_l, w_out, xl, lat_vec(2), seqlen)
            if with_ctx:
                act_c = _readout_c(p_c, hs_c[0], hs_c[1])
                xc = _mm_res(act_c, w_out, xc, ctx_vec(2), n_ctx)

        ffn_up = [ffn_w_gate, ffn_w_up]
        wd = ffn_w_down[layer].astype(BF16)
        act = _norm_mm(xl, nw2, lat_vec(3), lat_vec(4), ffn_up, seqlen, BF16, layer=layer)
        xl = _mm_res(act, wd, xl, lat_vec(5), seqlen)
        if with_ctx:
            act = _norm_mm(xc, nw2, ctx_vec(3), ctx_vec(4), ffn_up, n_ctx, BF16, layer=layer)
            xc = _mm_res(act, wd, xc, ctx_vec(5), n_ctx)

    return _final_norm(xl, final_norm_w).reshape(bsz, seqlen, d)
```

```python
import functools
import math

import jax
import jax.numpy as jnp
from jax import lax
from jax.experimental import pallas as pl
from jax.experimental.pallas import tpu as pltpu

F32 = jnp.float32
BF16 = jnp.bfloat16

NORM_EPS = 1e-6
GRID_W = 64
M_HEADDIM = 64
M_GROUPS = 4
M_STATE = 128
M_CONV = 4
M_CHUNK = 128
M_NORM_EPS = 1e-5
R_HEADSIZE = 64
R_CHUNK = 64
R_GROUP = 4
R_SUB = 4
R_LN_EPS = 64e-5
LOW_W = 512
LOW_WL = 256
LOW_AL = 384
LOW_DT = 480
C_HEADS = 8
C_CONV = 4
C_ROWS_PER_STEP = 4
RG_C = 8.0

SCAN_OUT_DTYPE = BF16
VMEM_LIMIT_BYTES = 56 * 1024 * 1024
HALO = 8


def _cparams(*sem):
    return pltpu.CompilerParams(dimension_semantics=sem, vmem_limit_bytes=VMEM_LIMIT_BYTES)


def _bdot(a, b):
    return jnp.dot(a.astype(BF16), b.astype(BF16), preferred_element_type=F32)


def _bdot_nt(a, b):
    return lax.dot_general(a.astype(BF16), b.astype(BF16), (((1,), (1,)), ((), ())),
                           preferred_element_type=F32)


def _split3(x):
    hi = x.astype(BF16)
    r1 = x - hi.astype(F32)
    mid = r1.astype(BF16)
    lo = (r1 - mid.astype(F32)).astype(BF16)
    return hi, mid, lo


def _tri_cumsum(tri_bf16, x, parts=3):
    hi, mid, lo = _split3(x)
    dot = lambda p: jnp.dot(tri_bf16, p, preferred_element_type=F32)
    if parts == 2:
        return dot(hi) + dot(mid)
    return dot(hi) + (dot(mid) + dot(lo))


def _silu(x):
    return x * jax.nn.sigmoid(x)


def _softplus(x):
    return jnp.maximum(x, 0.0) + jnp.log1p(jnp.exp(-jnp.abs(x)))


def _adaln_kernel(cb_ref, w_ref, b_ref, o_ref, *, n_vec, tn):
    rows = []
    for v in range(n_vec):
        cols = []
        for j in range(tn // 128):
            wj = w_ref[:, j * 128:(j + 1) * 128]
            cols.append(jnp.sum(wj * cb_ref[v], axis=0, keepdims=True))
        rows.append(jnp.concatenate(cols, axis=1))
    pad = jnp.zeros((HALO - n_vec, tn), F32)
    o_ref[...] = jnp.concatenate(rows + [pad], axis=0) + b_ref[...]


def _adaln(cond, ada_w, ada_b):
    depth, d, n6 = ada_w.shape
    n_vec = cond.shape[0]
    tn = 1024
    cb = jnp.broadcast_to(_silu(cond)[:, :, None], (n_vec, d, 128))
    return pl.pallas_call(
        functools.partial(_adaln_kernel, n_vec=n_vec, tn=tn),
        grid=(depth, n6 // tn),
        in_specs=[pl.BlockSpec((n_vec, d, 128), lambda l, j: (0, 0, 0)),
                  pl.BlockSpec((None, d, tn), lambda l, j: (l, 0, j)),
                  pl.BlockSpec((None, 1, tn), lambda l, j: (l, 0, j))],
        out_specs=pl.BlockSpec((None, HALO, tn), lambda l, j: (l, 0, j)),
        out_shape=jax.ShapeDtypeStruct((depth, HALO, n6), F32),
        compiler_params=_cparams("arbitrary", "arbitrary"),
    )(cb, ada_w, ada_b.reshape(depth, 1, n6))


def _modulated(x_ref, nw_ref, sh_ref, sc_ref):
    x = x_ref[...]
    y = x * lax.rsqrt(jnp.mean(x * x, axis=-1, keepdims=True) + NORM_EPS)
    return (y * nw_ref[...]) * (1.0 + sc_ref[...]) + sh_ref[...]


def _norm_mm_kernel(x_ref, nw_ref, sh_ref, sc_ref, w_ref, o_ref, h_ref):
    @pl.when(pl.program_id(1) == 0)
    def _():
        h_ref[...] = _modulated(x_ref, nw_ref, sh_ref, sc_ref).astype(BF16)

    o_ref[...] = jnp.dot(h_ref[...], w_ref[...].astype(BF16),
                         preferred_element_type=F32).astype(o_ref.dtype)


def _norm_swiglu_kernel(x_ref, nw_ref, sh_ref, sc_ref, wg_ref, wu_ref, o_ref, h_ref):
    @pl.when(pl.program_id(1) == 0)
    def _():
        h_ref[...] = _modulated(x_ref, nw_ref, sh_ref, sc_ref).astype(BF16)

    h = h_ref[...]
    g = jnp.dot(h, wg_ref[...].astype(BF16), preferred_element_type=F32)
    u = jnp.dot(h, wu_ref[...].astype(BF16), preferred_element_type=F32)
    o_ref[...] = (_silu(g) * u).astype(o_ref.dtype)


def _row_tile(m, cap):
    t = cap
    while m % t:
        t //= 2
    return t


def _norm_mm(x, nw, shift, scale, weights, rows_per_vec, out_dtype, tm_cap=1024, tn=512, layer=0):
    m, d = x.shape
    n = weights[0].shape[-1]
    tm = _row_tile(rows_per_vec, tm_cap)
    vec_map = lambda i, j: ((i * tm) // rows_per_vec, 0, 0)
    kern = _norm_mm_kernel if len(weights) == 1 else _norm_swiglu_kernel

    def w_spec(w):
        if w.ndim == 3:
            return pl.BlockSpec((None, d, tn), lambda i, j: (layer, 0, j))
        return pl.BlockSpec((d, tn), lambda i, j: (0, j))
    return pl.pallas_call(
        kern,
        grid=(m // tm, n // tn),
        in_specs=[pl.BlockSpec((tm, d), lambda i, j: (i, 0)),
                  pl.BlockSpec((1, d), lambda i, j: (0, 0)),
                  pl.BlockSpec((None, 1, d), vec_map),
                  pl.BlockSpec((None, 1, d), vec_map)]
                 + [w_spec(w) for w in weights],
        out_specs=pl.BlockSpec((tm, tn), lambda i, j: (i, j)),
        out_shape=jax.ShapeDtypeStruct((m, n), out_dtype),
        scratch_shapes=[pltpu.VMEM((tm, d), BF16)],
        compiler_params=_cparams("arbitrary", "arbitrary"),
    )(x, nw.reshape(1, d), shift, scale, *weights)


def _mm_res_kernel(a_ref, w_ref, res_ref, g_ref, o_ref):
    acc = jnp.dot(a_ref[...], w_ref[...], preferred_element_type=F32)
    o_ref[...] = res_ref[...] + g_ref[...] * acc


def _mm_res(a, w, res, gate, rows_per_vec, tm_cap=1024, tn=512):
    m, k = a.shape
    n = w.shape[1]
    tm = _row_tile(rows_per_vec, tm_cap)
    return pl.pallas_call(
        _mm_res_kernel,
        grid=(m // tm, n // tn),
        in_specs=[pl.BlockSpec((tm, k), lambda i, j: (i, 0)),
                  pl.BlockSpec((k, tn), lambda i, j: (0, j)),
                  pl.BlockSpec((tm, tn), lambda i, j: (i, j)),
                  pl.BlockSpec((None, 1, tn), lambda i, j: ((i * tm) // rows_per_vec, 0, j))],
        out_specs=pl.BlockSpec((tm, tn), lambda i, j: (i, j)),
        out_shape=jax.ShapeDtypeStruct((m, n), F32),
        compiler_params=_cparams("arbitrary", "arbitrary"),
    )(a, w, res, gate)


def _final_norm_kernel(x_ref, w_ref, o_ref):
    x = x_ref[...]
    o_ref[...] = x * lax.rsqrt(jnp.mean(x * x, axis=-1, keepdims=True) + NORM_EPS) * w_ref[...]


def _final_norm(x, w):
    m, d = x.shape
    tm = _row_tile(m, 512)
    return pl.pallas_call(
        _final_norm_kernel,
        grid=(m // tm,),
        in_specs=[pl.BlockSpec((tm, d), lambda i: (i, 0)), pl.BlockSpec((1, d), lambda i: (0, 0))],
        out_specs=pl.BlockSpec((tm, d), lambda i: (i, 0)),
        out_shape=jax.ShapeDtypeStruct((m, d), F32),
        compiler_params=_cparams("arbitrary"),
    )(x, w.reshape(1, d))


def _scan_block(i, n_blocks, rev):
    return (n_blocks - 1 - i) if rev else i


def _halo_block(i, n_blocks, rows, rev):
    per = rows // HALO
    if rev:
        return jnp.minimum((_scan_block(i, n_blocks, True) + 1) * per, n_blocks * per - 1)
    return jnp.maximum(i * per - 1, 0)


def _stage_with_halo(scr_ref, x_ref, halo_ref, keep, rows, rev):
    if rev:
        scr_ref[0:rows, :] = x_ref[...]
        scr_ref[rows:rows + HALO, :] = halo_ref[...] * keep
    else:
        scr_ref[0:HALO, :] = halo_ref[...] * keep
        scr_ref[HALO:HALO + rows, :] = x_ref[...]


def _lagged(scr_ref, lag, rows, rev):
    start = lag if rev else HALO - lag
    return scr_ref[start:start + rows, :]


def _rwkv_kernel(rkv_ref, sm_ref, rkv_h_ref, sm_h_ref, mu_rkv_ref, mu_sm_ref, w0_ref, w2_ref,
                 a0_ref, a2_ref, kkw_ref, kaw_ref, rkw_ref, s0_ref,
                 y_ref, bonus_ref, sl_ref, s_scr, rkv_scr, sm_scr, *, rev, n_heads, lora, nsub):
    i = pl.program_id(1)
    c = R_CHUNK
    rows = nsub * c
    n = R_HEADSIZE
    hw = n_heads * n

    @pl.when(i == 0)
    def _():
        s_scr[...] = s0_ref[...]

    keep = (i > 0).astype(F32)
    _stage_with_halo(rkv_scr, rkv_ref, rkv_h_ref, keep, rows, rev)
    _stage_with_halo(sm_scr, sm_ref, sm_h_ref, keep, rows, rev)
    x = rkv_ref[...]
    f = x + (_lagged(rkv_scr, 1, rows, rev) - x) * mu_rkv_ref[...]
    xs = sm_ref[...]
    fs = xs + (_lagged(sm_scr, 1, rows, rev) - xs) * mu_sm_ref[...]
    r = f[:, :hw]
    k = f[:, hw:2 * hw]
    v = f[:, 2 * hw:3 * hw]
    wl = fs[:, LOW_WL:LOW_WL + lora]
    al = fs[:, LOW_AL:LOW_AL + lora]

    ww = w0_ref[...] + _bdot(jnp.tanh(wl), w2_ref[...])
    lw = -math.exp(-0.5) * jax.nn.sigmoid(ww)
    a = jax.nn.sigmoid(a0_ref[...] + _bdot(al, a2_ref[...]))
    kkr = k * kkw_ref[...]
    k2 = k * (1.0 + (a - 1.0) * kaw_ref[...])
    rk = r * k2 * rkw_ref[...]

    cshift = int(math.log2(c))
    ti = lax.broadcasted_iota(jnp.int32, (rows, rows), 0)
    tj = lax.broadcasted_iota(jnp.int32, (rows, rows), 1)
    same_chunk = (ti >> cshift) == (tj >> cshift)
    upto = same_chunk & ((tj >= ti) if rev else (tj <= ti))
    g = _tri_cumsum(upto.astype(BF16), lw, parts=2)
    rsl = [slice(k * c, (k + 1) * c) for k in range(nsub)]
    gtot = [g[k * c:k * c + 1, :] if rev else g[(k + 1) * c - 1:(k + 1) * c, :] for k in range(nsub)]
    eg = jnp.exp(g)
    eneg = jnp.exp(-g)
    egp = jnp.exp(g - lw)
    etail = jnp.exp(jnp.concatenate([jnp.broadcast_to(t, (c, hw)) for t in gtot], axis=0) - g)
    dg = [jnp.exp(t) for t in gtot]
    rd_all = r * eg
    kp_all = k2 * eneg
    kpp_all = k2 * etail

    gw = R_GROUP * n
    groups = range(hw // gw)
    gsl = [slice(q * gw, (q + 1) * gw) for q in groups]
    shift = int(math.log2(n))
    bi = lax.broadcasted_iota(jnp.int32, (gw, gw), 0) >> shift
    bj = lax.broadcasted_iota(jnp.int32, (gw, gw), 1) >> shift
    same_head = bi == bj
    ones_bd = same_head.astype(BF16)
    lane_head = lax.broadcasted_iota(jnp.int32, (1, 128), 1) >> shift
    keep_f32 = [(lane_head == j).astype(F32) for j in range(128 // n)]
    keep_lanes = [t.astype(BF16) for t in keep_f32]

    def bdiag(t):
        tb = t.astype(BF16)
        zero = jnp.zeros((c, 128), BF16)
        rows = []
        for h in range(R_GROUP):
            lt, j = divmod(h * n, 128)
            piece = tb[:, lt * 128:(lt + 1) * 128] * keep_lanes[j // n]
            rows.append(jnp.concatenate([piece if u == lt else zero for u in range(gw // 128)], axis=1))
        return jnp.concatenate(rows, axis=0)

    def diag_blocks(full):
        tiles = []
        for lt in range(gw // 128):
            acc = None
            for j in range(128 // n):
                h = lt * (128 // n) + j
                part = full[h * n:(h + 1) * n, lt * 128:(lt + 1) * 128] * keep_f32[j]
                acc = part if acc is None else acc + part
            tiles.append(acc)
        return jnp.concatenate(tiles, axis=1)

    def head_sums(t):
        hi, mid, _ = _split3(t)
        dot = lambda p: jnp.dot(p, ones_bd, preferred_element_type=F32)
        return dot(hi) + dot(mid)

    tdot = lambda p, q: lax.dot_general(p.astype(BF16), q.astype(BF16), (((0,), (0,)), ((), ())),
                                        preferred_element_type=F32)
    gi = lax.broadcasted_iota(jnp.int32, (c, gw), 0)
    gj = lax.broadcasted_iota(jnp.int32, (c, gw), 1) & (c - 1)
    before_g = (gj > gi) if rev else (gj < gi)
    upto_g = (gj >= gi) if rev else (gj <= gi)
    eye_g = (gi == gj).astype(F32)

    kk_all = jnp.concatenate(
        [kkr[:, s] * lax.rsqrt(jnp.maximum(head_sums(kkr[:, s] * kkr[:, s]), 1e-24)) for s in gsl], axis=1)
    bh_all = kk_all * a
    kkd_all = kk_all * egp
    bp_all = bh_all * eneg
    bpp_all = bh_all * etail
    bonus_ref[...] = (jnp.concatenate([head_sums(rk[:, s]) for s in gsl], axis=1) * v
                      ).astype(bonus_ref.dtype)

    units = [(k, q) for k in range(nsub) for q in groups]
    un = range(len(units))
    cut = lambda t: [t[rsl[k], gsl[q]] for k, q in units]
    kkd, bp, bpp, rd, vq, kp, kpp = (cut(t) for t in (kkd_all, bp_all, bpp_all, rd_all, v, kp_all, kpp_all))
    amat = [_bdot_nt(jnp.concatenate([kkd[u], rd[u]], axis=0),
                     jnp.concatenate([bdiag(kp[u]), bdiag(bp[u])], axis=0)) for u in un]
    mk = [jnp.where(before_g, m[:c, :gw], 0.0) for m in amat]
    nk = [jnp.where(upto_g, m[c:, :gw], 0.0) for m in amat]
    nb = [jnp.where(upto_g, m[c:, gw:], 0.0) for m in amat]
    p2 = [jnp.where(before_g, -m[:c, gw:], 0.0) for m in amat]
    tinv = [eye_g + t for t in p2]
    p2 = [_bdot(t, bdiag(t)) for t in p2]
    for _ in range(cshift - 2):
        sq = [_bdot(jnp.concatenate([tinv[u], p2[u]], axis=0), bdiag(p2[u])) for u in un]
        tinv = [tinv[u] + sq[u][:c] for u in un]
        p2 = [t[c:] for t in sq]
    tinv = [tinv[u] + _bdot(tinv[u], bdiag(p2[u])) for u in un]
    mnv = [_bdot(jnp.concatenate([mk[u], nk[u]], axis=0), bdiag(vq[u])) for u in un]
    wu = [_bdot(tinv[u], jnp.concatenate([bdiag(kkd[u]), bdiag(mnv[u][:c])], axis=1))
          for u in un]
    nwu = [_bdot(nb[u], jnp.concatenate([bdiag(wu[u][:, :gw]), bdiag(wu[u][:, gw:])], axis=1))
           for u in un]
    wub = [tdot(wu[u], bpp[u]) for u in un]
    pmat = [bdiag(diag_blocks(t[:gw])) for t in wub]
    dmat = [diag_blocks(tdot(vq[u], kpp[u])) - diag_blocks(wub[u][gw:]) for u in un]
    qt = [rd[u] - nwu[u][:, :gw] for u in un]
    y0 = [mnv[u][c:] - nwu[u][:, gw:] for u in un]
    state = [s_scr[q] for q in groups]
    for k in (reversed(range(nsub)) if rev else range(nsub)):
        for q in groups:
            u = k * len(groups) + q
            y_ref[rsl[k], gsl[q]] = (y0[u] + _bdot_nt(qt[u], bdiag(state[q]))).astype(y_ref.dtype)
            state[q] = state[q] * dg[k][:, gsl[q]] - _bdot(state[q], pmat[u]) + dmat[u]
    for q in groups:
        s_scr[q] = state[q]

    @pl.when(i == pl.num_programs(1) - 1)
    def _():
        sl_ref[...] = s_scr[...]


def _rwkv_direction(pr, s0, p, rev):
    bsz, seqlen, width = pr.shape
    hw = s0.shape[1] * s0.shape[3]
    n_heads = hw // R_HEADSIZE
    nsub = R_SUB if seqlen % (R_SUB * R_CHUNK) == 0 else 1
    c = nsub * R_CHUNK
    nblk = seqlen // c
    sm_col = width // LOW_W - 1
    blk = lambda i: _scan_block(i, nblk, rev)
    halo = lambda i: _halo_block(i, nblk, c, rev)
    vec = lambda w: pl.BlockSpec((1, w), lambda b, i: (0, 0))
    lora = p['w2'].shape[0]
    state_spec = pl.BlockSpec((None,) + s0.shape[1:], lambda b, i: (b, 0, 0, 0))
    return pl.pallas_call(
        functools.partial(_rwkv_kernel, rev=rev, n_heads=n_heads, lora=lora, nsub=nsub),
        grid=(bsz, nblk),
        in_specs=[pl.BlockSpec((None, c, 3 * hw), lambda b, i: (b, blk(i), 1)),
                  pl.BlockSpec((None, c, LOW_W), lambda b, i: (b, blk(i), sm_col)),
                  pl.BlockSpec((None, HALO, 3 * hw), lambda b, i: (b, halo(i), 1)),
                  pl.BlockSpec((None, HALO, LOW_W), lambda b, i: (b, halo(i), sm_col)),
                  vec(3 * hw), vec(LOW_W), vec(hw),
                  pl.BlockSpec((lora, hw), lambda b, i: (0, 0)),
                  vec(hw),
                  pl.BlockSpec((lora, hw), lambda b, i: (0, 0)),
                  vec(hw), vec(hw), vec(hw),
                  state_spec],
        out_specs=[pl.BlockSpec((None, c, hw), lambda b, i: (b, blk(i), 0)),
                   pl.BlockSpec((None, c, hw), lambda b, i: (b, blk(i), 0)),
                   state_spec],
        out_shape=[jax.ShapeDtypeStruct((bsz, seqlen, hw), SCAN_OUT_DTYPE),
                   jax.ShapeDtypeStruct((bsz, seqlen, hw), SCAN_OUT_DTYPE),
                   jax.ShapeDtypeStruct(s0.shape, F32)],
        scratch_shapes=[pltpu.VMEM(s0.shape[1:], F32),
                        pltpu.VMEM((c + HALO, 3 * hw), F32),
                        pltpu.VMEM((c + HALO, LOW_W), F32)],
        compiler_params=_cparams("arbitrary", "arbitrary"),
    )(pr, pr, pr, pr, p['mu_rkv'], p['mu_sm'], p['w0'], p['w2'], p['a0'], p['a2'],
      p['kk'], p['ka'], p['rk'], s0)


def _expand_cols(x, e_bf16):
    hi, mid, lo = _split3(x)
    dot = lambda p: jnp.dot(p, e_bf16, preferred_element_type=F32)
    return dot(hi) + (dot(mid) + dot(lo))


def _causal_conv(scr_ref, w_ref, b_ref, rows, taps, rev):
    acc = b_ref[...] + w_ref[taps - 1:taps, :] * _lagged(scr_ref, 0, rows, rev)
    for kk in range(taps - 1):
        acc = acc + w_ref[kk:kk + 1, :] * _lagged(scr_ref, taps - 1 - kk, rows, rev)
    return acc


def _ssd_kernel(xbc_ref, halo_ref, sm_ref, dtt_ref, cw_ref, cb_ref, dtb_row_ref, dtb_col_ref,
                alog_row_ref, alog_col_ref, dskip_ref, e64_ref, hmask_ref, h0_ref,
                y_ref, hl_ref, h_scr, x_scr, *, rev, n_heads):
    i = pl.program_id(1)
    c = M_CHUNK
    p = M_HEADDIM
    ns = M_STATE
    mix = n_heads * p
    hpg = n_heads // M_GROUPS
    gw = hpg * p

    @pl.when(i == 0)
    def _():
        h_scr[...] = h0_ref[...]

    keep = (i > 0).astype(F32)
    _stage_with_halo(x_scr, xbc_ref, halo_ref, keep, c, rev)
    xbc = _silu(_causal_conv(x_scr, cw_ref, cb_ref, c, M_CONV, rev))
    xs = xbc[:, :mix]
    bm = [xbc[:, mix + g * ns:mix + (g + 1) * ns] for g in range(M_GROUPS)]
    cm = [xbc[:, mix + (M_GROUPS + g) * ns:mix + (M_GROUPS + g + 1) * ns] for g in range(M_GROUPS)]

    ti = lax.broadcasted_iota(jnp.int32, (c, c), 0)
    tj = lax.broadcasted_iota(jnp.int32, (c, c), 1)
    upto = (tj >= ti) if rev else (tj <= ti)
    upto_t = (ti >= tj) if rev else (ti <= tj)
    end = 0 if rev else c - 1

    dt = _softplus(sm_ref[:, LOW_DT:LOW_DT + n_heads] + dtb_row_ref[...])
    a_cs = _tri_cumsum(upto.astype(BF16), dt * -jnp.exp(alog_row_ref[...]))
    dt_t = _softplus(dtt_ref[...] + dtb_col_ref[...])
    a_cs_t = _expand_cols(dt_t * -jnp.exp(alog_col_ref[...]), upto_t.astype(BF16))
    a_end_t = a_cs_t[:, end:end + 1]
    w_t = dt_t * jnp.exp(a_end_t - a_cs_t)
    grow_t = jnp.broadcast_to(jnp.exp(a_end_t), (n_heads, ns))

    parts = [t.astype(F32) for t in _split3(a_cs)]
    ones = jnp.ones((c, n_heads), F32)
    lhs = jnp.concatenate(parts + [ones] * 3, axis=1).astype(BF16)
    hmask = hmask_ref[...]
    parts_t = jnp.concatenate(list(_split3(a_cs_t)), axis=0)
    rhs = jnp.concatenate([hmask, -jnp.concatenate([parts_t] * n_heads, axis=1) * hmask], axis=0)
    seg = jnp.dot(lhs, rhs, preferred_element_type=F32)
    ea_full = jnp.exp(jnp.dot(lhs[:, :3 * n_heads], e64_ref[...], preferred_element_type=F32))

    lane_head = lax.broadcasted_iota(jnp.int32, (1, gw), 1) >> int(math.log2(p))
    keep_lanes = [(lane_head == j).astype(BF16) for j in range(hpg)]
    xs_t = jnp.transpose(xs)
    y_diag, y_off = [], []
    for g in range(M_GROUPS):
        heads = range(g * hpg, (g + 1) * hpg)
        cb = _bdot_nt(cm[g], bm[g])
        scores = []
        for h in heads:
            decay = jnp.where(upto, jnp.exp(jnp.minimum(seg[:, h * c:(h + 1) * c], 0.0)), 0.0)
            scores.append((cb * decay * dt_t[h:h + 1, :]).astype(BF16))
        xg = xs[:, g * gw:(g + 1) * gw].astype(BF16)
        x_bd = jnp.concatenate([xg * keep_lanes[j] for j in range(hpg)], axis=0)
        y_diag.append(jnp.dot(jnp.concatenate(scores, axis=1), x_bd, preferred_element_type=F32))
        h_in = h_scr[g]
        y_off.append(_bdot_nt(cm[g], h_in))
        wrows = jnp.concatenate([jnp.broadcast_to(w_t[h:h + 1, :], (p, c)) for h in heads], axis=0)
        grow = jnp.concatenate([jnp.broadcast_to(grow_t[h:h + 1, :], (p, ns)) for h in heads], axis=0)
        h_scr[g] = h_in * grow + _bdot(xs_t[g * gw:(g + 1) * gw, :] * wrows, bm[g])
    y_ref[...] = (jnp.concatenate(y_diag, axis=1) + jnp.concatenate(y_off, axis=1) * ea_full
                  + xs * dskip_ref[...]).astype(y_ref.dtype)

    @pl.when(i == pl.num_programs(1) - 1)
    def _():
        hl_ref[...] = h_scr[...]


def _ssd_direction(pr, dt_t, h0, p, rev):
    bsz, seqlen, width = pr.shape
    n_heads = dt_t.shape[1]
    mix = n_heads * M_HEADDIM
    xw = mix + 2 * M_GROUPS * M_STATE
    c = M_CHUNK
    nblk = seqlen // c
    sm_col = width // LOW_W - 1
    blk = lambda i: _scan_block(i, nblk, rev)
    halo = lambda i: _halo_block(i, nblk, c, rev)
    full = lambda a: pl.BlockSpec(a.shape, lambda b, i: (0,) * a.ndim)
    consts = [p['conv_w'], p['conv_b'], p['dtb_row'], p['dtb_col'], p['alog_row'], p['alog_col'],
              p['dskip'], p['e64'], p['hmask']]
    state_spec = pl.BlockSpec((None,) + h0.shape[1:], lambda b, i: (b, 0, 0, 0))
    return pl.pallas_call(
        functools.partial(_ssd_kernel, rev=rev, n_heads=n_heads),
        grid=(bsz, nblk),
        in_specs=[pl.BlockSpec((None, c, xw), lambda b, i: (b, blk(i), 0)),
                  pl.BlockSpec((None, HALO, xw), lambda b, i: (b, halo(i), 0)),
                  pl.BlockSpec((None, c, LOW_W), lambda b, i: (b, blk(i), sm_col)),
                  pl.BlockSpec((None, n_heads, c), lambda b, i: (b, 0, blk(i)))]
                 + [full(a) for a in consts] + [state_spec],
        out_specs=[pl.BlockSpec((None, c, mix), lambda b, i: (b, blk(i), 0)), state_spec],
        out_shape=[jax.ShapeDtypeStruct((bsz, seqlen, mix), SCAN_OUT_DTYPE),
                   jax.ShapeDtypeStruct(h0.shape, F32)],
        scratch_shapes=[pltpu.VMEM(h0.shape[1:], F32), pltpu.VMEM((c + HALO, xw), F32)],
        compiler_params=_cparams("arbitrary", "arbitrary"),
    )(pr, pr, pr, dt_t, *consts, h0)


def _rglru_kernel(x_ref, halo_ref, cw_ref, cb_ref, wa_ref, ba_ref, wx_ref, bx_ref, lam_ref, h0_ref,
                  o_ref, hl_ref, h_scr, x_scr, a_scr, u_scr, *, rev, tb):
    i = pl.program_id(1)

    @pl.when(i == 0)
    def _():
        h_scr[...] = h0_ref[...]

    keep = (i > 0).astype(F32)
    _stage_with_halo(x_scr, x_ref, halo_ref, keep, tb, rev)
    xc = _causal_conv(x_scr, cw_ref, cb_ref, tb, C_CONV, rev)
    a_scr[...], u_scr[...] = _rglru_gates(xc, wa_ref, ba_ref, wx_ref, bx_ref, lam_ref)

    def step(t, h):
        tt = (tb - 1 - t) if rev else t
        h = a_scr[pl.ds(tt, 1), :] * h + u_scr[pl.ds(tt, 1), :]
        o_ref[pl.ds(tt, 1), :] = h
        return h

    h_scr[0:1, :] = lax.fori_loop(0, tb, step, h_scr[0:1, :], unroll=8)

    @pl.when(i == pl.num_programs(1) - 1)
    def _():
        hl_ref[...] = h_scr[...]


def _rglru_direction(xv, h0, p, rev, tb):
    bsz, rows, wtot = xv.shape
    width = h0.shape[2]
    ncol = wtot // (2 * width)
    nrb = rows // tb
    total = ncol * nrb
    per = tb // HALO

    def pos(i):
        s = _scan_block(i, total, rev)
        return s // nrb, s % nrb

    def x_map(b, i):
        col, q = pos(i)
        return b, q, 2 * col + 1

    def halo_map(b, i):
        s = _scan_block(i, total, rev)
        sp = jnp.minimum(s + 1, total - 1) if rev else jnp.maximum(s - 1, 0)
        col, q = sp // nrb, sp % nrb
        return b, (q * per if rev else (q + 1) * per - 1), 2 * col + 1

    def o_map(b, i):
        col, q = pos(i)
        return b, q, col

    full = lambda a: pl.BlockSpec(a.shape, lambda b, i: (0,) * a.ndim)
    consts = [p['conv_w'], p['conv_b'], p['wa'], p['ba'], p['wx'], p['bx'], p['lam']]
    state_spec = pl.BlockSpec((None, HALO, width), lambda b, i: (b, 0, 0))
    return pl.pallas_call(
        functools.partial(_rglru_kernel, rev=rev, tb=tb),
        grid=(bsz, total),
        in_specs=[pl.BlockSpec((None, tb, width), x_map), pl.BlockSpec((None, HALO, width), halo_map)]
                 + [full(a) for a in consts] + [state_spec],
        out_specs=[pl.BlockSpec((None, tb, width), o_map), state_spec],
        out_shape=[jax.ShapeDtypeStruct((bsz, rows, ncol * width), F32),
                   jax.ShapeDtypeStruct(h0.shape, F32)],
        scratch_shapes=[pltpu.VMEM((HALO, width), F32), pltpu.VMEM((tb + HALO, width), F32),
                        pltpu.VMEM((tb, width), F32), pltpu.VMEM((tb, width), F32)],
        compiler_params=_cparams("arbitrary", "arbitrary"),
    )(xv, xv, *consts, h0)


def _rglru_gates(xc, wa_ref, ba_ref, wx_ref, bx_ref, lam_ref):
    blk = xc.shape[1] // C_HEADS
    xh = [xc[:, h * blk:(h + 1) * blk] for h in range(C_HEADS)]
    gate_r = jax.nn.sigmoid(
        jnp.concatenate([_bdot(xh[h], wa_ref[h]) for h in range(C_HEADS)], axis=1) + ba_ref[...])
    gate_i = jax.nn.sigmoid(
        jnp.concatenate([_bdot(xh[h], wx_ref[h]) for h in range(C_HEADS)], axis=1) + bx_ref[...])
    log_a = -RG_C * gate_r * _softplus(-lam_ref[...])
    a = jnp.exp(log_a)
    one_m_a2 = jnp.maximum(-jnp.tanh(log_a) * (a * a + 1.0), 0.0)
    mult = jnp.where(one_m_a2 > 0.0, one_m_a2 * lax.rsqrt(one_m_a2), 0.0)
    return a, mult * gate_i * xc


def _rglru_grid_kernel(x_ref, e0_ref, e1_ref, e2_ref, cw_ref, cb_ref, wa_ref, ba_ref, wx_ref,
                       bx_ref, lam_ref, h0_ref, hloc_ref, aprod_ref, cin_ref, hl_ref,
                       ring_scr, h_scr, ap_scr, stage_scr, *, rev, ncol):
    i = pl.program_id(1)

    def from_previous_column(e_ref):
        _stage_with_halo(stage_scr, e_ref, jnp.zeros((HALO, e_ref.shape[1]), F32), 0.0, ncol, rev)
        return _lagged(stage_scr, 1, ncol, rev)

    @pl.when(i == 0)
    def _():
        ring_scr[0] = from_previous_column(e2_ref)
        ring_scr[1] = from_previous_column(e1_ref)
        ring_scr[2] = from_previous_column(e0_ref)
        h_scr[...] = jnp.zeros(h_scr.shape, F32)
        ap_scr[...] = jnp.ones(ap_scr.shape, F32)

    nrow = x_ref.shape[0] // ncol
    order = list(reversed(range(nrow))) if rev else list(range(nrow))
    rsl = [slice(m * ncol, (m + 1) * ncol) for m in range(nrow)]
    hist = [ring_scr[2], ring_scr[1], ring_scr[0]] + [x_ref[rsl[m], :] for m in order]
    xc = [cb_ref[...] + cw_ref[3:4, :] * hist[t + 3] + cw_ref[2:3, :] * hist[t + 2]
          + cw_ref[1:2, :] * hist[t + 1] + cw_ref[0:1, :] * hist[t] for t in range(nrow)]
    for lag in range(3):
        ring_scr[lag] = hist[-1 - lag]
    a, u = _rglru_gates(jnp.concatenate(xc, axis=0), wa_ref, ba_ref, wx_ref, bx_ref, lam_ref)
    h, ap = h_scr[...], ap_scr[...]
    for t, m in enumerate(order):
        ts = slice(t * ncol, (t + 1) * ncol)
        h = a[ts, :] * h + u[ts, :]
        ap = a[ts, :] * ap
        hloc_ref[rsl[m], :] = h.astype(hloc_ref.dtype)
        aprod_ref[rsl[m], :] = ap.astype(aprod_ref.dtype)
    h_scr[...] = h
    ap_scr[...] = ap

    @pl.when(i == pl.num_programs(1) - 1)
    def _():
        def step(t, carry):
            col = (ncol - 1 - t) if rev else t
            cin_ref[pl.ds(col, 1), :] = carry
            return h_scr[pl.ds(col, 1), :] + ap_scr[pl.ds(col, 1), :] * carry

        last = lax.fori_loop(0, ncol, step, h0_ref[0:1, :], unroll=8)
        hl_ref[...] = jnp.zeros(hl_ref.shape, F32)
        hl_ref[0:1, :] = last


def _rglru_grid_direction(p3, h0, p, rev):
    bsz, seqlen, _ = p3.shape
    width = h0.shape[2]
    ncol = GRID_W
    rows = seqlen // ncol
    nrow = C_ROWS_PER_STEP if rows % C_ROWS_PER_STEP == 0 else 1
    nblk = rows // nrow
    blk = lambda i: _scan_block(i, nblk, rev)
    edge = lambda k: pl.BlockSpec((None, ncol, width),
                                  lambda b, i: (b, (2 - k) if rev else (rows - 3 + k), 1))
    full = lambda a: pl.BlockSpec(a.shape, lambda b, i: (0,) * a.ndim)
    consts = [p['conv_w'], p['conv_b'], p['wa'], p['ba'], p['wx'], p['bx'], p['lam']]
    state_spec = pl.BlockSpec((None, HALO, width), lambda b, i: (b, 0, 0))
    row_spec = pl.BlockSpec((None, nrow * ncol, width), lambda b, i: (b, blk(i), 0))
    return pl.pallas_call(
        functools.partial(_rglru_grid_kernel, rev=rev, ncol=ncol),
        grid=(bsz, nblk),
        in_specs=[pl.BlockSpec((None, nrow * ncol, width), lambda b, i: (b, blk(i), 1)),
                  edge(0), edge(1), edge(2)] + [full(a) for a in consts] + [state_spec],
        out_specs=[row_spec, row_spec,
                   pl.BlockSpec((None, ncol, width), lambda b, i: (b, 0, 0)), state_spec],
        out_shape=[jax.ShapeDtypeStruct((bsz, seqlen, width), SCAN_OUT_DTYPE),
                   jax.ShapeDtypeStruct((bsz, seqlen, width), SCAN_OUT_DTYPE),
                   jax.ShapeDtypeStruct((bsz, ncol, width), F32),
                   jax.ShapeDtypeStruct(h0.shape, F32)],
        scratch_shapes=[pltpu.VMEM((3, ncol, width), F32), pltpu.VMEM((ncol, width), F32),
                        pltpu.VMEM((ncol, width), F32), pltpu.VMEM((ncol + HALO, width), F32)],
        compiler_params=_cparams("arbitrary", "arbitrary"),
    )(p3, p3, p3, p3, *consts, h0)


def _readout_ab_kernel(z_ref, sm_ref, ymf_ref, ymb_ref, yrf_ref, yrb_ref, bof_ref, bob_ref,
                       nw_ref, lw_ref, lb_ref, g2_ref, o_ref, *, n_heads):
    f32 = lambda ref: ref[...].astype(F32)
    t = (f32(ymf_ref) + f32(ymb_ref)) * _silu(z_ref[...])
    gw = t.shape[1] // M_GROUPS
    a_out = []
    for g in range(M_GROUPS):
        tg = t[:, g * gw:(g + 1) * gw]
        a_out.append(tg * lax.rsqrt(jnp.mean(tg * tg, axis=-1, keepdims=True) + M_NORM_EPS))
    a_out = jnp.concatenate(a_out, axis=1) * nw_ref[...]

    yr = f32(yrf_ref) + f32(yrb_ref)
    n = R_HEADSIZE
    yn = []
    for h in range(n_heads):
        yh = yr[:, h * n:(h + 1) * n]
        dlt = yh - jnp.mean(yh, axis=-1, keepdims=True)
        yn.append(dlt * lax.rsqrt(jnp.mean(dlt * dlt, axis=-1, keepdims=True) + R_LN_EPS))
    yn = jnp.concatenate(yn, axis=1) * lw_ref[...] + lb_ref[...]
    gate = _bdot(jax.nn.sigmoid(sm_ref[:, 0:g2_ref.shape[0]]), g2_ref[...])
    b_out = (yn + f32(bof_ref) + f32(bob_ref)) * gate
    o_ref[...] = jnp.concatenate([a_out, b_out], axis=1).astype(o_ref.dtype)


def _readout_ab(pr, ym, yr, bo, ro, n_heads):
    bsz, seqlen, width = pr.shape
    m = bsz * seqlen
    mix = ym[0].shape[2]
    tm = _row_tile(m, 256)
    flat = lambda t: t.reshape(m, t.shape[2])
    rowblk = lambda w, col: pl.BlockSpec((tm, w), lambda i: (i, col))
    full = lambda a: pl.BlockSpec(a.shape, lambda i: (0,) * a.ndim)
    consts = [ro['norm_w'], ro['lnx_w'], ro['lnx_b'], ro['g2']]
    return pl.pallas_call(
        functools.partial(_readout_ab_kernel, n_heads=n_heads),
        grid=(m // tm,),
        in_specs=[rowblk(mix, 2), rowblk(LOW_W, width // LOW_W - 1)] + [rowblk(mix, 0)] * 6
                 + [full(a) for a in consts],
        out_specs=rowblk(2 * mix, 0),
        out_shape=jax.ShapeDtypeStruct((m, 2 * mix), BF16),
        compiler_params=_cparams("arbitrary"),
    )(flat(pr), flat(pr), flat(ym[0]), flat(ym[1]), flat(yr[0]), flat(yr[1]), flat(bo[0]),
      flat(bo[1]), *consts)


def _gelu_tanh(g):
    return 0.5 * g * (1.0 + jnp.tanh(math.sqrt(2.0 / math.pi) * (g + 0.044715 * (g * g * g))))


def _readout_c_kernel(gy_ref, hf_ref, hb_ref, o_ref):
    o_ref[...] = ((hf_ref[...] + hb_ref[...]) * _gelu_tanh(gy_ref[...])).astype(o_ref.dtype)


def _readout_grid_kernel(gy_ref, hf_ref, af_ref, cf_ref, hb_ref, ab_ref, cb_ref, o_ref, *, ncol):
    for j in range(o_ref.shape[0] // ncol):
        rs = slice(j * ncol, (j + 1) * ncol)
        f32 = lambda ref: ref[rs, :].astype(F32)
        h = (f32(hf_ref) + f32(af_ref) * cf_ref[...]) + (f32(hb_ref) + f32(ab_ref) * cb_ref[...])
        o_ref[rs, :] = (h * _gelu_tanh(gy_ref[rs, :])).astype(o_ref.dtype)


def _readout_grid(p3, fwd, bwd):
    bsz, seqlen, _ = p3.shape
    cw = fwd[0].shape[2]
    ncol = fwd[2].shape[1]
    tm = _row_tile(seqlen, 8 * ncol)
    spec = pl.BlockSpec((None, tm, cw), lambda b, i: (b, i, 0))
    cspec = pl.BlockSpec((None, ncol, cw), lambda b, i: (b, 0, 0))
    return pl.pallas_call(
        functools.partial(_readout_grid_kernel, ncol=ncol),
        grid=(bsz, seqlen // tm),
        in_specs=[spec, spec, spec, cspec, spec, spec, cspec],
        out_specs=spec,
        out_shape=jax.ShapeDtypeStruct((bsz, seqlen, cw), BF16),
        compiler_params=_cparams("arbitrary", "arbitrary"),
    )(p3, *fwd, *bwd)


def _readout_c(p, hf, hb):
    m, cw = hf.shape
    tm = _row_tile(m, 512)
    spec = pl.BlockSpec((tm, cw), lambda i: (i, 0))
    return pl.pallas_call(
        _readout_c_kernel,
        grid=(m // tm,),
        in_specs=[spec, spec, spec],
        out_specs=spec,
        out_shape=jax.ShapeDtypeStruct((m, cw), BF16),
        compiler_params=_cparams("arbitrary"),
    )(p, hf, hb)


def kernel(x, c, ctx, c_ctx, ada_w, ada_b, norm1_w, norm2_w, ffn_w_gate, ffn_w_up, ffn_w_down, final_norm_w, ab_w_in, ab_w_out, m_conv_w, m_conv_b, m_dt_bias, m_a_log, m_d, m_norm_w, r_mu, r_w0, r_w2, r_a0, r_a2, r_kk, r_ka, r_rk, r_g2, r_lnx_w, r_lnx_b, c_w_in, c_w_out, c_conv_w, c_conv_b, c_wa, c_ba, c_wx, c_bx, c_lambda):
    bsz, seqlen, d = x.shape
    ctx_len = ctx.shape[1]
    depth = ada_w.shape[0]
    n_lat, n_ctx = bsz * seqlen, bsz * ctx_len
    row = lambda t: t.reshape(1, -1)

    mod = _adaln(jnp.concatenate([c_ctx[None, :], c], axis=0), ada_w, ada_b)

    xl = x.reshape(n_lat, d)
    xc = ctx.reshape(n_ctx, d)
    mix_a = m_norm_w.shape[1]
    m_heads = m_dt_bias.shape[2]
    r_heads = r_rk.shape[2]
    lora = r_w2.shape[2]
    e64 = jnp.tile(jnp.repeat(jnp.eye(m_heads, dtype=BF16), M_HEADDIM, axis=1), (3, 1))
    hmask = jnp.tile(jnp.repeat(jnp.eye(m_heads, dtype=BF16), M_CHUNK, axis=1), (3, 1))

    for layer in range(depth):
        with_ctx = layer < depth - 1
        part = lambda k: mod[layer, :, k * d:(k + 1) * d]
        lat_vec = lambda k: part(k)[1:1 + bsz, None, :]
        ctx_vec = lambda k: part(k)[0:1, None, :]
        nw1, nw2 = norm1_w[layer], norm2_w[layer]

        if layer % 2 == 0:
            e = layer // 2
            w = ab_w_in[e].astype(BF16)
            o1 = mix_a
            o2 = o1 + mix_a + 2 * M_GROUPS * M_STATE
            o3 = o2 + m_heads
            o4 = o3 + 3 * r_heads * R_HEADSIZE + 2 * lora
            rkv_w = 3 * r_heads * R_HEADSIZE
            zeros = lambda n: jnp.zeros((d, n), w.dtype)
            w_in = jnp.concatenate(
                [w[:, o1:o2], w[:, :o1], w[:, o3:o3 + rkv_w], w[:, o4:],
                 w[:, o3 + rkv_w:o3 + rkv_w + lora], zeros(LOW_AL - LOW_WL - lora),
                 w[:, o3 + rkv_w + lora:o4], zeros(LOW_DT - LOW_AL - lora), w[:, o2:o3],
                 zeros(LOW_W - LOW_DT - m_heads)],
                axis=1)
            width = w_in.shape[1]
            w_out = ab_w_out[e].astype(BF16)
            g2w = r_g2[e].astype(BF16)

            def project(xs, sh, sc, rows_per_vec, seg_len):
                pr = _norm_mm(xs, nw1, sh, sc, [w_in], rows_per_vec, F32)
                pr = pr.reshape(bsz, seg_len, width)
                dt_col = width - LOW_W + LOW_DT
                dt_t = jnp.swapaxes(pr[:, :, dt_col:dt_col + m_heads], 1, 2)
                return pr, dt_t

            pr_l, dtt_l = project(xl, lat_vec(0), lat_vec(1), seqlen, seqlen)
            pr_c, dtt_c = project(xc, ctx_vec(0), ctx_vec(1), n_ctx, ctx_len)

            ym_l = ym_c = yr_l = yr_c = bo_l = bo_c = None
            for dr in range(2):
                rev = dr == 1
                mp = dict(conv_w=m_conv_w[e, dr], conv_b=row(m_conv_b[e, dr]),
                          dtb_row=row(m_dt_bias[e, dr]), dtb_col=m_dt_bias[e, dr].reshape(-1, 1),
                          alog_row=row(m_a_log[e, dr]), alog_col=m_a_log[e, dr].reshape(-1, 1),
                          dskip=row(jnp.repeat(m_d[e, dr], M_HEADDIM)), e64=e64, hmask=hmask)
                h0 = jnp.zeros((bsz, M_GROUPS, m_heads // M_GROUPS * M_HEADDIM, M_STATE), F32)
                y_c, h_c = _ssd_direction(pr_c, dtt_c, h0, mp, rev)
                y_l, _ = _ssd_direction(pr_l, dtt_l, h_c, mp, rev)
                mu = r_mu[e, dr]
                mu_sm = (jnp.zeros((LOW_W,), F32).at[LOW_WL:LOW_WL + lora].set(mu[rkv_w:rkv_w + lora])
                         .at[LOW_AL:LOW_AL + lora].set(mu[rkv_w + lora:]))
                rp = dict(mu_rkv=row(mu[:rkv_w]), mu_sm=row(mu_sm), w0=row(r_w0[e, dr]),
                          w2=r_w2[e, dr].astype(BF16), a0=row(r_a0[e, dr]), a2=r_a2[e, dr].astype(BF16),
                          kk=row(r_kk[e, dr]), ka=row(r_ka[e, dr]), rk=row(r_rk[e, dr]))
                s0 = jnp.zeros((bsz, r_heads // R_GROUP, R_HEADSIZE, R_GROUP * R_HEADSIZE), F32)
                v_c, b_c, s_c = _rwkv_direction(pr_c, s0, rp, rev)
                v_l, b_l, _ = _rwkv_direction(pr_l, s_c, rp, rev)
                if dr == 0:
                    ym_l, ym_c, yr_l, yr_c, bo_l, bo_c = y_l, y_c, v_l, v_c, b_l, b_c
                else:
                    ym_l, yr_l, bo_l = (ym_l, y_l), (yr_l, v_l), (bo_l, b_l)
                    ym_c, yr_c, bo_c = (ym_c, y_c), (yr_c, v_c), (bo_c, b_c)

            ro = dict(norm_w=row(m_norm_w[e]), lnx_w=row(r_lnx_w[e]), lnx_b=row(r_lnx_b[e]), g2=g2w)
            act_l = _readout_ab(pr_l, ym_l, yr_l, bo_l, ro, r_heads)
            xl = _mm_res(act_l, w_out, xl, lat_vec(2), seqlen)
            if with_ctx:
                act_c = _readout_ab(pr_c, ym_c, yr_c, bo_c, ro, r_heads)
                xc = _mm_res(act_c, w_out, xc, ctx_vec(2), n_ctx)
        else:
            o = layer // 2
            w_out = c_w_out[o].astype(BF16)
            cw = w_out.shape[0]
            p_l = _norm_mm(xl, nw1, lat_vec(0), lat_vec(1), [c_w_in], seqlen, F32, layer=o)
            p_c = _norm_mm(xc, nw1, ctx_vec(0), ctx_vec(1), [c_w_in], n_ctx, F32, layer=o)
            p3_l = p_l.reshape(bsz, seqlen, 2 * cw)
            hs_l, hs_c = [], []
            for dr in range(2):
                rev = dr == 1
                cp = dict(conv_w=c_conv_w[o, dr], conv_b=row(c_conv_b[o, dr]),
                          wa=c_wa[o, dr].astype(BF16), ba=row(c_ba[o, dr]),
                          wx=c_wx[o, dr].astype(BF16), bx=row(c_bx[o, dr]), lam=row(c_lambda[o, dr]))
                h0 = jnp.zeros((bsz, HALO, cw), F32)
                h_c, s_c = _rglru_direction(p_c.reshape(bsz, ctx_len, 2 * cw), h0, cp, rev,
                                            _row_tile(ctx_len, 128))
                hs_l.append(_rglru_grid_direction(p3_l, s_c, cp, rev)[:3])
                hs_c.append(h_c.reshape(n_ctx, cw))
            act_l = _readout_grid(p3_l, hs_l[0], hs_l[1]).reshape(n_lat, cw)
            xl = _mm_res(act_l, w_out, xl, lat_vec(2), seqlen)
            if with_ctx:
                act_c = _readout_c(p_c, hs_c[0], hs_c[1])
                xc = _mm_res(act_c, w_out, xc, ctx_vec(2), n_ctx)

        ffn_up = [ffn_w_gate, ffn_w_up]
        wd = ffn_w_down[layer].astype(BF16)
        act = _norm_mm(xl, nw2, lat_vec(3), lat_vec(4), ffn_up, seqlen, BF16, layer=layer)
        xl = _mm_res(act, wd, xl, lat_vec(5), seqlen)
        if with_ctx:
            act = _norm_mm(xc, nw2, ctx_vec(3), ctx_vec(4), ffn_up, n_ctx, BF16, layer=layer)
            xc = _mm_res(act, wd, xc, ctx_vec(5), n_ctx)

    return _final_norm(xl, final_norm_w).reshape(bsz, seqlen, d)
```

```python
import functools
import math

import jax
import jax.numpy as jnp
from jax import lax
from jax.experimental import pallas as pl
from jax.experimental.pallas import tpu as pltpu

F32 = jnp.float32
BF16 = jnp.bfloat16

NORM_EPS = 1e-6
GRID_W = 64
M_HEADDIM = 64
M_GROUPS = 4
M_STATE = 128
M_CONV = 4
M_CHUNK = 128
M_NORM_EPS = 1e-5
R_HEADSIZE = 64
R_CHUNK = 64
R_GROUP = 4
R_SUB = 4
R_LN_EPS = 64e-5
LOW_W = 512
LOW_WL = 256
LOW_AL = 384
LOW_DT = 480
C_HEADS = 8
C_CONV = 4
C_ROWS_PER_STEP = 8
RG_C = 8.0

SCAN_OUT_DTYPE = BF16
VMEM_LIMIT_BYTES = 56 * 1024 * 1024
HALO = 8


def _cparams(*sem):
    return pltpu.CompilerParams(dimension_semantics=sem, vmem_limit_bytes=VMEM_LIMIT_BYTES)


def _bdot(a, b):
    return jnp.dot(a.astype(BF16), b.astype(BF16), preferred_element_type=F32)


def _bdot_nt(a, b):
    return lax.dot_general(a.astype(BF16), b.astype(BF16), (((1,), (1,)), ((), ())),
                           preferred_element_type=F32)


def _split3(x):
    hi = x.astype(BF16)
    r1 = x - hi.astype(F32)
    mid = r1.astype(BF16)
    lo = (r1 - mid.astype(F32)).astype(BF16)
    return hi, mid, lo


def _tri_cumsum(tri_bf16, x, parts=3):
    hi, mid, lo = _split3(x)
    dot = lambda p: jnp.dot(tri_bf16, p, preferred_element_type=F32)
    if parts == 2:
        return dot(hi) + dot(mid)
    return dot(hi) + (dot(mid) + dot(lo))


def _silu(x):
    return x * jax.nn.sigmoid(x)


def _softplus(x):
    return jnp.maximum(x, 0.0) + jnp.log1p(jnp.exp(-jnp.abs(x)))


def _adaln_kernel(cb_ref, w_ref, b_ref, o_ref, *, n_vec, tn):
    rows = []
    for v in range(n_vec):
        cols = []
        for j in range(tn // 128):
            wj = w_ref[:, j * 128:(j + 1) * 128]
            cols.append(jnp.sum(wj * cb_ref[v], axis=0, keepdims=True))
        rows.append(jnp.concatenate(cols, axis=1))
    pad = jnp.zeros((HALO - n_vec, tn), F32)
    o_ref[...] = jnp.concatenate(rows + [pad], axis=0) + b_ref[...]


def _adaln(cond, ada_w, ada_b):
    depth, d, n6 = ada_w.shape
    n_vec = cond.shape[0]
    tn = 1024
    cb = jnp.broadcast_to(_silu(cond)[:, :, None], (n_vec, d, 128))
    return pl.pallas_call(
        functools.partial(_adaln_kernel, n_vec=n_vec, tn=tn),
        grid=(depth, n6 // tn),
        in_specs=[pl.BlockSpec((n_vec, d, 128), lambda l, j: (0, 0, 0)),
                  pl.BlockSpec((None, d, tn), lambda l, j: (l, 0, j)),
                  pl.BlockSpec((None, 1, tn), lambda l, j: (l, 0, j))],
        out_specs=pl.BlockSpec((None, HALO, tn), lambda l, j: (l, 0, j)),
        out_shape=jax.ShapeDtypeStruct((depth, HALO, n6), F32),
        compiler_params=_cparams("arbitrary", "arbitrary"),
    )(cb, ada_w, ada_b.reshape(depth, 1, n6))


def _modulated(x_ref, nw_ref, sh_ref, sc_ref):
    x = x_ref[...]
    y = x * lax.rsqrt(jnp.mean(x * x, axis=-1, keepdims=True) + NORM_EPS)
    return (y * nw_ref[...]) * (1.0 + sc_ref[...]) + sh_ref[...]


def _norm_mm_kernel(x_ref, nw_ref, sh_ref, sc_ref, w_ref, o_ref, h_ref):
    @pl.when(pl.program_id(1) == 0)
    def _():
        h_ref[...] = _modulated(x_ref, nw_ref, sh_ref, sc_ref).astype(BF16)

    o_ref[...] = jnp.dot(h_ref[...], w_ref[...].astype(BF16),
                         preferred_element_type=F32).astype(o_ref.dtype)


def _norm_swiglu_kernel(x_ref, nw_ref, sh_ref, sc_ref, wg_ref, wu_ref, o_ref, h_ref):
    @pl.when(pl.program_id(1) == 0)
    def _():
        h_ref[...] = _modulated(x_ref, nw_ref, sh_ref, sc_ref).astype(BF16)

    h = h_ref[...]
    g = jnp.dot(h, wg_ref[...].astype(BF16), preferred_element_type=F32)
    u = jnp.dot(h, wu_ref[...].astype(BF16), preferred_element_type=F32)
    o_ref[...] = (_silu(g) * u).astype(o_ref.dtype)


def _row_tile(m, cap):
    t = cap
    while m % t:
        t //= 2
    return t


def _norm_mm(x, nw, shift, scale, weights, rows_per_vec, out_dtype, tm_cap=1024, tn=512, layer=0):
    m, d = x.shape
    n = weights[0].shape[-1]
    tm = _row_tile(rows_per_vec, tm_cap)
    vec_map = lambda i, j: ((i * tm) // rows_per_vec, 0, 0)
    kern = _norm_mm_kernel if len(weights) == 1 else _norm_swiglu_kernel

    def w_spec(w):
        if w.ndim == 3:
            return pl.BlockSpec((None, d, tn), lambda i, j: (layer, 0, j))
        return pl.BlockSpec((d, tn), lambda i, j: (0, j))
    return pl.pallas_call(
        kern,
        grid=(m // tm, n // tn),
        in_specs=[pl.BlockSpec((tm, d), lambda i, j: (i, 0)),
                  pl.BlockSpec((1, d), lambda i, j: (0, 0)),
                  pl.BlockSpec((None, 1, d), vec_map),
                  pl.BlockSpec((None, 1, d), vec_map)]
                 + [w_spec(w) for w in weights],
        out_specs=pl.BlockSpec((tm, tn), lambda i, j: (i, j)),
        out_shape=jax.ShapeDtypeStruct((m, n), out_dtype),
        scratch_shapes=[pltpu.VMEM((tm, d), BF16)],
        compiler_params=_cparams("arbitrary", "arbitrary"),
    )(x, nw.reshape(1, d), shift, scale, *weights)


def _mm_res_kernel(a_ref, w_ref, res_ref, g_ref, o_ref):
    acc = jnp.dot(a_ref[...], w_ref[...], preferred_element_type=F32)
    o_ref[...] = res_ref[...] + g_ref[...] * acc


def _mm_res(a, w, res, gate, rows_per_vec, tm_cap=1024, tn=512):
    m, k = a.shape
    n = w.shape[1]
    tm = _row_tile(rows_per_vec, tm_cap)
    return pl.pallas_call(
        _mm_res_kernel,
        grid=(m // tm, n // tn),
        in_specs=[pl.BlockSpec((tm, k), lambda i, j: (i, 0)),
                  pl.BlockSpec((k, tn), lambda i, j: (0, j)),
                  pl.BlockSpec((tm, tn), lambda i, j: (i, j)),
                  pl.BlockSpec((None, 1, tn), lambda i, j: ((i * tm) // rows_per_vec, 0, j))],
        out_specs=pl.BlockSpec((tm, tn), lambda i, j: (i, j)),
        out_shape=jax.ShapeDtypeStruct((m, n), F32),
        compiler_params=_cparams("arbitrary", "arbitrary"),
    )(a, w, res, gate)


def _final_norm_kernel(x_ref, w_ref, o_ref):
    x = x_ref[...]
    o_ref[...] = x * lax.rsqrt(jnp.mean(x * x, axis=-1, keepdims=True) + NORM_EPS) * w_ref[...]


def _final_norm(x, w):
    m, d = x.shape
    tm = _row_tile(m, 512)
    return pl.pallas_call(
        _final_norm_kernel,
        grid=(m // tm,),
        in_specs=[pl.BlockSpec((tm, d), lambda i: (i, 0)), pl.BlockSpec((1, d), lambda i: (0, 0))],
        out_specs=pl.BlockSpec((tm, d), lambda i: (i, 0)),
        out_shape=jax.ShapeDtypeStruct((m, d), F32),
        compiler_params=_cparams("arbitrary"),
    )(x, w.reshape(1, d))


def _scan_block(i, n_blocks, rev):
    return (n_blocks - 1 - i) if rev else i


def _halo_block(i, n_blocks, rows, rev):
    per = rows // HALO
    if rev:
        return jnp.minimum((_scan_block(i, n_blocks, True) + 1) * per, n_blocks * per - 1)
    return jnp.maximum(i * per - 1, 0)


def _stage_with_halo(scr_ref, x_ref, halo_ref, keep, rows, rev):
    if rev:
        scr_ref[0:rows, :] = x_ref[...]
        scr_ref[rows:rows + HALO, :] = halo_ref[...] * keep
    else:
        scr_ref[0:HALO, :] = halo_ref[...] * keep
        scr_ref[HALO:HALO + rows, :] = x_ref[...]


def _lagged(scr_ref, lag, rows, rev):
    start = lag if rev else HALO - lag
    return scr_ref[start:start + rows, :]


def _rwkv_kernel(rkv_ref, sm_ref, rkv_h_ref, sm_h_ref, mu_rkv_ref, mu_sm_ref, w0_ref, w2_ref,
                 a0_ref, a2_ref, kkw_ref, kaw_ref, rkw_ref, s0_ref,
                 y_ref, bonus_ref, sl_ref, s_scr, rkv_scr, sm_scr, *, rev, n_heads, lora, nsub):
    i = pl.program_id(1)
    c = R_CHUNK
    rows = nsub * c
    n = R_HEADSIZE
    hw = n_heads * n

    @pl.when(i == 0)
    def _():
        s_scr[...] = s0_ref[...]

    keep = (i > 0).astype(F32)
    _stage_with_halo(rkv_scr, rkv_ref, rkv_h_ref, keep, rows, rev)
    _stage_with_halo(sm_scr, sm_ref, sm_h_ref, keep, rows, rev)
    x = rkv_ref[...]
    f = x + (_lagged(rkv_scr, 1, rows, rev) - x) * mu_rkv_ref[...]
    xs = sm_ref[...]
    fs = xs + (_lagged(sm_scr, 1, rows, rev) - xs) * mu_sm_ref[...]
    r = f[:, :hw]
    k = f[:, hw:2 * hw]
    v = f[:, 2 * hw:3 * hw]
    wl = fs[:, LOW_WL:LOW_WL + lora]
    al = fs[:, LOW_AL:LOW_AL + lora]

    ww = w0_ref[...] + _bdot(jnp.tanh(wl), w2_ref[...])
    lw = -math.exp(-0.5) * jax.nn.sigmoid(ww)
    a = jax.nn.sigmoid(a0_ref[...] + _bdot(al, a2_ref[...]))
    kkr = k * kkw_ref[...]
    k2 = k * (1.0 + (a - 1.0) * kaw_ref[...])
    rk = r * k2 * rkw_ref[...]

    cshift = int(math.log2(c))
    ti = lax.broadcasted_iota(jnp.int32, (rows, rows), 0)
    tj = lax.broadcasted_iota(jnp.int32, (rows, rows), 1)
    same_chunk = (ti >> cshift) == (tj >> cshift)
    upto = same_chunk & ((tj >= ti) if rev else (tj <= ti))
    g = _tri_cumsum(upto.astype(BF16), lw, parts=2)
    rsl = [slice(k * c, (k + 1) * c) for k in range(nsub)]
    gtot = [g[k * c:k * c + 1, :] if rev else g[(k + 1) * c - 1:(k + 1) * c, :] for k in range(nsub)]
    eg = jnp.exp(g)
    eneg = jnp.exp(-g)
    egp = jnp.exp(g - lw)
    etail = jnp.exp(jnp.concatenate([jnp.broadcast_to(t, (c, hw)) for t in gtot], axis=0) - g)
    dg = [jnp.exp(t) for t in gtot]
    rd_all = r * eg
    kp_all = k2 * eneg
    kpp_all = k2 * etail

    gw = R_GROUP * n
    groups = range(hw // gw)
    gsl = [slice(q * gw, (q + 1) * gw) for q in groups]
    shift = int(math.log2(n))
    bi = lax.broadcasted_iota(jnp.int32, (gw, gw), 0) >> shift
    bj = lax.broadcasted_iota(jnp.int32, (gw, gw), 1) >> shift
    same_head = bi == bj
    ones_bd = same_head.astype(BF16)
    lane_head = lax.broadcasted_iota(jnp.int32, (1, 128), 1) >> shift
    keep_f32 = [(lane_head == j).astype(F32) for j in range(128 // n)]
    keep_lanes = [t.astype(BF16) for t in keep_f32]

    def bdiag(t):
        tb = t.astype(BF16)
        zero = jnp.zeros((c, 128), BF16)
        rows = []
        for h in range(R_GROUP):
            lt, j = divmod(h * n, 128)
            piece = tb[:, lt * 128:(lt + 1) * 128] * keep_lanes[j // n]
            rows.append(jnp.concatenate([piece if u == lt else zero for u in range(gw // 128)], axis=1))
        return jnp.concatenate(rows, axis=0)

    def diag_blocks(full):
        tiles = []
        for lt in range(gw // 128):
            acc = None
            for j in range(128 // n):
                h = lt * (128 // n) + j
                part = full[h * n:(h + 1) * n, lt * 128:(lt + 1) * 128] * keep_f32[j]
                acc = part if acc is None else acc + part
            tiles.append(acc)
        return jnp.concatenate(tiles, axis=1)

    def head_sums(t):
        hi, mid, _ = _split3(t)
        dot = lambda p: jnp.dot(p, ones_bd, preferred_element_type=F32)
        return dot(hi) + dot(mid)

    tdot = lambda p, q: lax.dot_general(p.astype(BF16), q.astype(BF16), (((0,), (0,)), ((), ())),
                                        preferred_element_type=F32)
    gi = lax.broadcasted_iota(jnp.int32, (c, gw), 0)
    gj = lax.broadcasted_iota(jnp.int32, (c, gw), 1) & (c - 1)
    before_g = (gj > gi) if rev else (gj < gi)
    upto_g = (gj >= gi) if rev else (gj <= gi)
    eye_g = (gi == gj).astype(F32)

    kk_all = jnp.concatenate(
        [kkr[:, s] * lax.rsqrt(jnp.maximum(head_sums(kkr[:, s] * kkr[:, s]), 1e-24)) for s in gsl], axis=1)
    bh_all = kk_all * a
    kkd_all = kk_all * egp
    bp_all = bh_all * eneg
    bpp_all = bh_all * etail
    bonus_ref[...] = (jnp.concatenate([head_sums(rk[:, s]) for s in gsl], axis=1) * v
                      ).astype(bonus_ref.dtype)

    units = [(k, q) for k in range(nsub) for q in groups]
    un = range(len(units))
    cut = lambda t: [t[rsl[k], gsl[q]] for k, q in units]
    kkd, bp, bpp, rd, vq, kp, kpp = (cut(t) for t in (kkd_all, bp_all, bpp_all, rd_all, v, kp_all, kpp_all))
    amat = [_bdot_nt(jnp.concatenate([kkd[u], rd[u]], axis=0),
                     jnp.concatenate([bdiag(kp[u]), bdiag(bp[u])], axis=0)) for u in un]
    mk = [jnp.where(before_g, m[:c, :gw], 0.0) for m in amat]
    nk = [jnp.where(upto_g, m[c:, :gw], 0.0) for m in amat]
    nb = [jnp.where(upto_g, m[c:, gw:], 0.0) for m in amat]
    p2 = [jnp.where(before_g, -m[:c, gw:], 0.0) for m in amat]
    tinv = [eye_g + t for t in p2]
    p2 = [_bdot(t, bdiag(t)) for t in p2]
    for _ in range(cshift - 2):
        sq = [_bdot(jnp.concatenate([tinv[u], p2[u]], axis=0), bdiag(p2[u])) for u in un]
        tinv = [tinv[u] + sq[u][:c] for u in un]
        p2 = [t[c:] for t in sq]
    tinv = [tinv[u] + _bdot(tinv[u], bdiag(p2[u])) for u in un]
    mnv = [_bdot(jnp.concatenate([mk[u], nk[u]], axis=0), bdiag(vq[u])) for u in un]
    wu = [_bdot(tinv[u], jnp.concatenate([bdiag(kkd[u]), bdiag(mnv[u][:c])], axis=1))
          for u in un]
    nwu = [_bdot(nb[u], jnp.concatenate([bdiag(wu[u][:, :gw]), bdiag(wu[u][:, gw:])], axis=1))
           for u in un]
    wub = [tdot(wu[u], bpp[u]) for u in un]
    pmat = [bdiag(diag_blocks(t[:gw])) for t in wub]
    dmat = [diag_blocks(tdot(vq[u], kpp[u])) - diag_blocks(wub[u][gw:]) for u in un]
    qt = [rd[u] - nwu[u][:, :gw] for u in un]
    y0 = [mnv[u][c:] - nwu[u][:, gw:] for u in un]
    state = [s_scr[q] for q in groups]
    for k in (reversed(range(nsub)) if rev else range(nsub)):
        for q in groups:
            u = k * len(groups) + q
            y_ref[rsl[k], gsl[q]] = (y0[u] + _bdot_nt(qt[u], bdiag(state[q]))).astype(y_ref.dtype)
            state[q] = state[q] * dg[k][:, gsl[q]] - _bdot(state[q], pmat[u]) + dmat[u]
    for q in groups:
        s_scr[q] = state[q]

    @pl.when(i == pl.num_programs(1) - 1)
    def _():
        sl_ref[...] = s_scr[...]


def _rwkv_direction(pr, s0, p, rev):
    bsz, seqlen, width = pr.shape
    hw = s0.shape[1] * s0.shape[3]
    n_heads = hw // R_HEADSIZE
    nsub = R_SUB if seqlen % (R_SUB * R_CHUNK) == 0 else 1
    c = nsub * R_CHUNK
    nblk = seqlen // c
    sm_col = width // LOW_W - 1
    blk = lambda i: _scan_block(i, nblk, rev)
    halo = lambda i: _halo_block(i, nblk, c, rev)
    vec = lambda w: pl.BlockSpec((1, w), lambda b, i: (0, 0))
    lora = p['w2'].shape[0]
    state_spec = pl.BlockSpec((None,) + s0.shape[1:], lambda b, i: (b, 0, 0, 0))
    return pl.pallas_call(
        functools.partial(_rwkv_kernel, rev=rev, n_heads=n_heads, lora=lora, nsub=nsub),
        grid=(bsz, nblk),
        in_specs=[pl.BlockSpec((None, c, 3 * hw), lambda b, i: (b, blk(i), 1)),
                  pl.BlockSpec((None, c, LOW_W), lambda b, i: (b, blk(i), sm_col)),
                  pl.BlockSpec((None, HALO, 3 * hw), lambda b, i: (b, halo(i), 1)),
                  pl.BlockSpec((None, HALO, LOW_W), lambda b, i: (b, halo(i), sm_col)),
                  vec(3 * hw), vec(LOW_W), vec(hw),
                  pl.BlockSpec((lora, hw), lambda b, i: (0, 0)),
                  vec(hw),
                  pl.BlockSpec((lora, hw), lambda b, i: (0, 0)),
                  vec(hw), vec(hw), vec(hw),
                  state_spec],
        out_specs=[pl.BlockSpec((None, c, hw), lambda b, i: (b, blk(i), 0)),
                   pl.BlockSpec((None, c, hw), lambda b, i: (b, blk(i), 0)),
                   state_spec],
        out_shape=[jax.ShapeDtypeStruct((bsz, seqlen, hw), SCAN_OUT_DTYPE),
                   jax.ShapeDtypeStruct((bsz, seqlen, hw), SCAN_OUT_DTYPE),
                   jax.ShapeDtypeStruct(s0.shape, F32)],
        scratch_shapes=[pltpu.VMEM(s0.shape[1:], F32),
                        pltpu.VMEM((c + HALO, 3 * hw), F32),
                        pltpu.VMEM((c + HALO, LOW_W), F32)],
        compiler_params=_cparams("arbitrary", "arbitrary"),
    )(pr, pr, pr, pr, p['mu_rkv'], p['mu_sm'], p['w0'], p['w2'], p['a0'], p['a2'],
      p['kk'], p['ka'], p['rk'], s0)


def _expand_cols(x, e_bf16):
    hi, mid, lo = _split3(x)
    dot = lambda p: jnp.dot(p, e_bf16, preferred_element_type=F32)
    return dot(hi) + (dot(mid) + dot(lo))


def _causal_conv(scr_ref, w_ref, b_ref, rows, taps, rev):
    acc = b_ref[...] + w_ref[taps - 1:taps, :] * _lagged(scr_ref, 0, rows, rev)
    for kk in range(taps - 1):
        acc = acc + w_ref[kk:kk + 1, :] * _lagged(scr_ref, taps - 1 - kk, rows, rev)
    return acc


def _ssd_kernel(xbc_ref, halo_ref, sm_ref, dtt_ref, cw_ref, cb_ref, dtb_row_ref, dtb_col_ref,
                alog_row_ref, alog_col_ref, dskip_ref, e64_ref, hmask_ref, h0_ref,
                y_ref, hl_ref, h_scr, x_scr, *, rev, n_heads):
    i = pl.program_id(1)
    c = M_CHUNK
    p = M_HEADDIM
    ns = M_STATE
    mix = n_heads * p
    hpg = n_heads // M_GROUPS
    gw = hpg * p

    @pl.when(i == 0)
    def _():
        h_scr[...] = h0_ref[...]

    keep = (i > 0).astype(F32)
    _stage_with_halo(x_scr, xbc_ref, halo_ref, keep, c, rev)
    xbc = _silu(_causal_conv(x_scr, cw_ref, cb_ref, c, M_CONV, rev))
    xs = xbc[:, :mix]
    bm = [xbc[:, mix + g * ns:mix + (g + 1) * ns] for g in range(M_GROUPS)]
    cm = [xbc[:, mix + (M_GROUPS + g) * ns:mix + (M_GROUPS + g + 1) * ns] for g in range(M_GROUPS)]

    ti = lax.broadcasted_iota(jnp.int32, (c, c), 0)
    tj = lax.broadcasted_iota(jnp.int32, (c, c), 1)
    upto = (tj >= ti) if rev else (tj <= ti)
    upto_t = (ti >= tj) if rev else (ti <= tj)
    end = 0 if rev else c - 1

    dt = _softplus(sm_ref[:, LOW_DT:LOW_DT + n_heads] + dtb_row_ref[...])
    a_cs = _tri_cumsum(upto.astype(BF16), dt * -jnp.exp(alog_row_ref[...]))
    dt_t = _softplus(dtt_ref[...] + dtb_col_ref[...])
    a_cs_t = _expand_cols(dt_t * -jnp.exp(alog_col_ref[...]), upto_t.astype(BF16))
    a_end_t = a_cs_t[:, end:end + 1]
    w_t = dt_t * jnp.exp(a_end_t - a_cs_t)
    grow_t = jnp.broadcast_to(jnp.exp(a_end_t), (n_heads, ns))

    parts = [t.astype(F32) for t in _split3(a_cs)]
    ones = jnp.ones((c, n_heads), F32)
    lhs = jnp.concatenate(parts + [ones] * 3, axis=1).astype(BF16)
    hmask = hmask_ref[...]
    parts_t = jnp.concatenate(list(_split3(a_cs_t)), axis=0)
    rhs = jnp.concatenate([hmask, -jnp.concatenate([parts_t] * n_heads, axis=1) * hmask], axis=0)
    seg = jnp.dot(lhs, rhs, preferred_element_type=F32)
    ea_full = jnp.exp(jnp.dot(lhs[:, :3 * n_heads], e64_ref[...], preferred_element_type=F32))

    lane_head = lax.broadcasted_iota(jnp.int32, (1, gw), 1) >> int(math.log2(p))
    keep_lanes = [(lane_head == j).astype(BF16) for j in range(hpg)]
    xs_t = jnp.transpose(xs)
    y_diag, y_off = [], []
    for g in range(M_GROUPS):
        heads = range(g * hpg, (g + 1) * hpg)
        cb = _bdot_nt(cm[g], bm[g])
        scores = []
        for h in heads:
            decay = jnp.where(upto, jnp.exp(jnp.minimum(seg[:, h * c:(h + 1) * c], 0.0)), 0.0)
            scores.append((cb * decay * dt_t[h:h + 1, :]).astype(BF16))
        xg = xs[:, g * gw:(g + 1) * gw].astype(BF16)
        x_bd = jnp.concatenate([xg * keep_lanes[j] for j in range(hpg)], axis=0)
        y_diag.append(jnp.dot(jnp.concatenate(scores, axis=1), x_bd, preferred_element_type=F32))
        h_in = h_scr[g]
        y_off.append(_bdot_nt(cm[g], h_in))
        wrows = jnp.concatenate([jnp.broadcast_to(w_t[h:h + 1, :], (p, c)) for h in heads], axis=0)
        grow = jnp.concatenate([jnp.broadcast_to(grow_t[h:h + 1, :], (p, ns)) for h in heads], axis=0)
        h_scr[g] = h_in * grow + _bdot(xs_t[g * gw:(g + 1) * gw, :] * wrows, bm[g])
    y_ref[...] = (jnp.concatenate(y_diag, axis=1) + jnp.concatenate(y_off, axis=1) * ea_full
                  + xs * dskip_ref[...]).astype(y_ref.dtype)

    @pl.when(i == pl.num_programs(1) - 1)
    def _():
        hl_ref[...] = h_scr[...]


def _ssd_direction(pr, dt_t, h0, p, rev):
    bsz, seqlen, width = pr.shape
    n_heads = dt_t.shape[1]
    mix = n_heads * M_HEADDIM
    xw = mix + 2 * M_GROUPS * M_STATE
    c = M_CHUNK
    nblk = seqlen // c
    sm_col = width // LOW_W - 1
    blk = lambda i: _scan_block(i, nblk, rev)
    halo = lambda i: _halo_block(i, nblk, c, rev)
    full = lambda a: pl.BlockSpec(a.shape, lambda b, i: (0,) * a.ndim)
    consts = [p['conv_w'], p['conv_b'], p['dtb_row'], p['dtb_col'], p['alog_row'], p['alog_col'],
              p['dskip'], p['e64'], p['hmask']]
    state_spec = pl.BlockSpec((None,) + h0.shape[1:], lambda b, i: (b, 0, 0, 0))
    return pl.pallas_call(
        functools.partial(_ssd_kernel, rev=rev, n_heads=n_heads),
        grid=(bsz, nblk),
        in_specs=[pl.BlockSpec((None, c, xw), lambda b, i: (b, blk(i), 0)),
                  pl.BlockSpec((None, HALO, xw), lambda b, i: (b, halo(i), 0)),
                  pl.BlockSpec((None, c, LOW_W), lambda b, i: (b, blk(i), sm_col)),
                  pl.BlockSpec((None, n_heads, c), lambda b, i: (b, 0, blk(i)))]
                 + [full(a) for a in consts] + [state_spec],
        out_specs=[pl.BlockSpec((None, c, mix), lambda b, i: (b, blk(i), 0)), state_spec],
        out_shape=[jax.ShapeDtypeStruct((bsz, seqlen, mix), SCAN_OUT_DTYPE),
                   jax.ShapeDtypeStruct(h0.shape, F32)],
        scratch_shapes=[pltpu.VMEM(h0.shape[1:], F32), pltpu.VMEM((c + HALO, xw), F32)],
        compiler_params=_cparams("arbitrary", "arbitrary"),
    )(pr, pr, pr, dt_t, *consts, h0)


def _rglru_kernel(x_ref, halo_ref, cw_ref, cb_ref, wa_ref, ba_ref, wx_ref, bx_ref, lam_ref, h0_ref,
                  o_ref, hl_ref, h_scr, x_scr, a_scr, u_scr, *, rev, tb):
    i = pl.program_id(1)

    @pl.when(i == 0)
    def _():
        h_scr[...] = h0_ref[...]

    keep = (i > 0).astype(F32)
    _stage_with_halo(x_scr, x_ref, halo_ref, keep, tb, rev)
    xc = _causal_conv(x_scr, cw_ref, cb_ref, tb, C_CONV, rev)
    a_scr[...], u_scr[...] = _rglru_gates(xc, wa_ref, ba_ref, wx_ref, bx_ref, lam_ref)

    def step(t, h):
        tt = (tb - 1 - t) if rev else t
        h = a_scr[pl.ds(tt, 1), :] * h + u_scr[pl.ds(tt, 1), :]
        o_ref[pl.ds(tt, 1), :] = h
        return h

    h_scr[0:1, :] = lax.fori_loop(0, tb, step, h_scr[0:1, :], unroll=8)

    @pl.when(i == pl.num_programs(1) - 1)
    def _():
        hl_ref[...] = h_scr[...]


def _rglru_direction(xv, h0, p, rev, tb):
    bsz, rows, wtot = xv.shape
    width = h0.shape[2]
    ncol = wtot // (2 * width)
    nrb = rows // tb
    total = ncol * nrb
    per = tb // HALO

    def pos(i):
        s = _scan_block(i, total, rev)
        return s // nrb, s % nrb

    def x_map(b, i):
        col, q = pos(i)
        return b, q, 2 * col + 1

    def halo_map(b, i):
        s = _scan_block(i, total, rev)
        sp = jnp.minimum(s + 1, total - 1) if rev else jnp.maximum(s - 1, 0)
        col, q = sp // nrb, sp % nrb
        return b, (q * per if rev else (q + 1) * per - 1), 2 * col + 1

    def o_map(b, i):
        col, q = pos(i)
        return b, q, col

    full = lambda a: pl.BlockSpec(a.shape, lambda b, i: (0,) * a.ndim)
    consts = [p['conv_w'], p['conv_b'], p['wa'], p['ba'], p['wx'], p['bx'], p['lam']]
    state_spec = pl.BlockSpec((None, HALO, width), lambda b, i: (b, 0, 0))
    return pl.pallas_call(
        functools.partial(_rglru_kernel, rev=rev, tb=tb),
        grid=(bsz, total),
        in_specs=[pl.BlockSpec((None, tb, width), x_map), pl.BlockSpec((None, HALO, width), halo_map)]
                 + [full(a) for a in consts] + [state_spec],
        out_specs=[pl.BlockSpec((None, tb, width), o_map), state_spec],
        out_shape=[jax.ShapeDtypeStruct((bsz, rows, ncol * width), F32),
                   jax.ShapeDtypeStruct(h0.shape, F32)],
        scratch_shapes=[pltpu.VMEM((HALO, width), F32), pltpu.VMEM((tb + HALO, width), F32),
                        pltpu.VMEM((tb, width), F32), pltpu.VMEM((tb, width), F32)],
        compiler_params=_cparams("arbitrary", "arbitrary"),
    )(xv, xv, *consts, h0)


def _rglru_gates(xc, wa_ref, ba_ref, wx_ref, bx_ref, lam_ref):
    blk = xc.shape[1] // C_HEADS
    xh = [xc[:, h * blk:(h + 1) * blk] for h in range(C_HEADS)]
    gate_r = jax.nn.sigmoid(
        jnp.concatenate([_bdot(xh[h], wa_ref[h]) for h in range(C_HEADS)], axis=1) + ba_ref[...])
    gate_i = jax.nn.sigmoid(
        jnp.concatenate([_bdot(xh[h], wx_ref[h]) for h in range(C_HEADS)], axis=1) + bx_ref[...])
    log_a = -RG_C * gate_r * _softplus(-lam_ref[...])
    a = jnp.exp(log_a)
    one_m_a2 = jnp.maximum(-jnp.tanh(log_a) * (a * a + 1.0), 0.0)
    mult = jnp.where(one_m_a2 > 0.0, one_m_a2 * lax.rsqrt(one_m_a2), 0.0)
    return a, mult * gate_i * xc


def _rglru_grid_kernel(x_ref, e0_ref, e1_ref, e2_ref, cw_ref, cb_ref, wa_ref, ba_ref, wx_ref,
                       bx_ref, lam_ref, h0_ref, hloc_ref, aprod_ref, cin_ref, hl_ref,
                       ring_scr, h_scr, ap_scr, stage_scr, *, rev, ncol):
    i = pl.program_id(1)

    def from_previous_column(e_ref):
        _stage_with_halo(stage_scr, e_ref, jnp.zeros((HALO, e_ref.shape[1]), F32), 0.0, ncol, rev)
        return _lagged(stage_scr, 1, ncol, rev)

    @pl.when(i == 0)
    def _():
        ring_scr[0] = from_previous_column(e2_ref)
        ring_scr[1] = from_previous_column(e1_ref)
        ring_scr[2] = from_previous_column(e0_ref)
        h_scr[...] = jnp.zeros(h_scr.shape, F32)
        ap_scr[...] = jnp.ones(ap_scr.shape, F32)

    nrow = x_ref.shape[0] // ncol
    order = list(reversed(range(nrow))) if rev else list(range(nrow))
    rsl = [slice(m * ncol, (m + 1) * ncol) for m in range(nrow)]
    hist = [ring_scr[2], ring_scr[1], ring_scr[0]] + [x_ref[rsl[m], :] for m in order]
    xc = [cb_ref[...] + cw_ref[3:4, :] * hist[t + 3] + cw_ref[2:3, :] * hist[t + 2]
          + cw_ref[1:2, :] * hist[t + 1] + cw_ref[0:1, :] * hist[t] for t in range(nrow)]
    for lag in range(3):
        ring_scr[lag] = hist[-1 - lag]
    a, u = _rglru_gates(jnp.concatenate(xc, axis=0), wa_ref, ba_ref, wx_ref, bx_ref, lam_ref)
    h, ap = h_scr[...], ap_scr[...]
    for t, m in enumerate(order):
        ts = slice(t * ncol, (t + 1) * ncol)
        h = a[ts, :] * h + u[ts, :]
        ap = a[ts, :] * ap
        hloc_ref[rsl[m], :] = h.astype(hloc_ref.dtype)
        aprod_ref[rsl[m], :] = ap.astype(aprod_ref.dtype)
    h_scr[...] = h
    ap_scr[...] = ap

    @pl.when(i == pl.num_programs(1) - 1)
    def _():
        def step(t, carry):
            col = (ncol - 1 - t) if rev else t
            cin_ref[pl.ds(col, 1), :] = carry
            return h_scr[pl.ds(col, 1), :] + ap_scr[pl.ds(col, 1), :] * carry

        last = lax.fori_loop(0, ncol, step, h0_ref[0:1, :], unroll=8)
        hl_ref[...] = jnp.zeros(hl_ref.shape, F32)
        hl_ref[0:1, :] = last


def _rglru_grid_direction(p3, h0, p, rev):
    bsz, seqlen, _ = p3.shape
    width = h0.shape[2]
    ncol = GRID_W
    rows = seqlen // ncol
    nrow = C_ROWS_PER_STEP if rows % C_ROWS_PER_STEP == 0 else 1
    nblk = rows // nrow
    blk = lambda i: _scan_block(i, nblk, rev)
    edge = lambda k: pl.BlockSpec((None, ncol, width),
                                  lambda b, i: (b, (2 - k) if rev else (rows - 3 + k), 1))
    full = lambda a: pl.BlockSpec(a.shape, lambda b, i: (0,) * a.ndim)
    consts = [p['conv_w'], p['conv_b'], p['wa'], p['ba'], p['wx'], p['bx'], p['lam']]
    state_spec = pl.BlockSpec((None, HALO, width), lambda b, i: (b, 0, 0))
    row_spec = pl.BlockSpec((None, nrow * ncol, width), lambda b, i: (b, blk(i), 0))
    return pl.pallas_call(
        functools.partial(_rglru_grid_kernel, rev=rev, ncol=ncol),
        grid=(bsz, nblk),
        in_specs=[pl.BlockSpec((None, nrow * ncol, width), lambda b, i: (b, blk(i), 1)),
                  edge(0), edge(1), edge(2)] + [full(a) for a in consts] + [state_spec],
        out_specs=[row_spec, row_spec,
                   pl.BlockSpec((None, ncol, width), lambda b, i: (b, 0, 0)), state_spec],
        out_shape=[jax.ShapeDtypeStruct((bsz, seqlen, width), SCAN_OUT_DTYPE),
                   jax.ShapeDtypeStruct((bsz, seqlen, width), SCAN_OUT_DTYPE),
                   jax.ShapeDtypeStruct((bsz, ncol, width), F32),
                   jax.ShapeDtypeStruct(h0.shape, F32)],
        scratch_shapes=[pltpu.VMEM((3, ncol, width), F32), pltpu.VMEM((ncol, width), F32),
                        pltpu.VMEM((ncol, width), F32), pltpu.VMEM((ncol + HALO, width), F32)],
        compiler_params=_cparams("arbitrary", "arbitrary"),
    )(p3, p3, p3, p3, *consts, h0)


def _readout_ab_kernel(z_ref, sm_ref, ymf_ref, ymb_ref, yrf_ref, yrb_ref, bof_ref, bob_ref,
                       nw_ref, lw_ref, lb_ref, g2_ref, o_ref, *, n_heads):
    f32 = lambda ref: ref[...].astype(F32)
    t = (f32(ymf_ref) + f32(ymb_ref)) * _silu(z_ref[...])
    gw = t.shape[1] // M_GROUPS
    a_out = []
    for g in range(M_GROUPS):
        tg = t[:, g * gw:(g + 1) * gw]
        a_out.append(tg * lax.rsqrt(jnp.mean(tg * tg, axis=-1, keepdims=True) + M_NORM_EPS))
    a_out = jnp.concatenate(a_out, axis=1) * nw_ref[...]

    yr = f32(yrf_ref) + f32(yrb_ref)
    n = R_HEADSIZE
    yn = []
    for h in range(n_heads):
        yh = yr[:, h * n:(h + 1) * n]
        dlt = yh - jnp.mean(yh, axis=-1, keepdims=True)
        yn.append(dlt * lax.rsqrt(jnp.mean(dlt * dlt, axis=-1, keepdims=True) + R_LN_EPS))
    yn = jnp.concatenate(yn, axis=1) * lw_ref[...] + lb_ref[...]
    gate = _bdot(jax.nn.sigmoid(sm_ref[:, 0:g2_ref.shape[0]]), g2_ref[...])
    b_out = (yn + f32(bof_ref) + f32(bob_ref)) * gate
    o_ref[...] = jnp.concatenate([a_out, b_out], axis=1).astype(o_ref.dtype)


def _readout_ab(pr, ym, yr, bo, ro, n_heads):
    bsz, seqlen, width = pr.shape
    m = bsz * seqlen
    mix = ym[0].shape[2]
    tm = _row_tile(m, 256)
    flat = lambda t: t.reshape(m, t.shape[2])
    rowblk = lambda w, col: pl.BlockSpec((tm, w), lambda i: (i, col))
    full = lambda a: pl.BlockSpec(a.shape, lambda i: (0,) * a.ndim)
    consts = [ro['norm_w'], ro['lnx_w'], ro['lnx_b'], ro['g2']]
    return pl.pallas_call(
        functools.partial(_readout_ab_kernel, n_heads=n_heads),
        grid=(m // tm,),
        in_specs=[rowblk(mix, 2), rowblk(LOW_W, width // LOW_W - 1)] + [rowblk(mix, 0)] * 6
                 + [full(a) for a in consts],
        out_specs=rowblk(2 * mix, 0),
        out_shape=jax.ShapeDtypeStruct((m, 2 * mix), BF16),
        compiler_params=_cparams("arbitrary"),
    )(flat(pr), flat(pr), flat(ym[0]), flat(ym[1]), flat(yr[0]), flat(yr[1]), flat(bo[0]),
      flat(bo[1]), *consts)


def _gelu_tanh(g):
    return 0.5 * g * (1.0 + jnp.tanh(math.sqrt(2.0 / math.pi) * (g + 0.044715 * (g * g * g))))


def _readout_c_kernel(gy_ref, hf_ref, hb_ref, o_ref):
    o_ref[...] = ((hf_ref[...] + hb_ref[...]) * _gelu_tanh(gy_ref[...])).astype(o_ref.dtype)


def _readout_grid_kernel(gy_ref, hf_ref, af_ref, cf_ref, hb_ref, ab_ref, cb_ref, o_ref, *, ncol):
    for j in range(o_ref.shape[0] // ncol):
        rs = slice(j * ncol, (j + 1) * ncol)
        f32 = lambda ref: ref[rs, :].astype(F32)
        h = (f32(hf_ref) + f32(af_ref) * cf_ref[...]) + (f32(hb_ref) + f32(ab_ref) * cb_ref[...])
        o_ref[rs, :] = (h * _gelu_tanh(gy_ref[rs, :])).astype(o_ref.dtype)


def _readout_grid(p3, fwd, bwd):
    bsz, seqlen, _ = p3.shape
    cw = fwd[0].shape[2]
    ncol = fwd[2].shape[1]
    tm = _row_tile(seqlen, 8 * ncol)
    spec = pl.BlockSpec((None, tm, cw), lambda b, i: (b, i, 0))
    cspec = pl.BlockSpec((None, ncol, cw), lambda b, i: (b, 0, 0))
    return pl.pallas_call(
        functools.partial(_readout_grid_kernel, ncol=ncol),
        grid=(bsz, seqlen // tm),
        in_specs=[spec, spec, spec, cspec, spec, spec, cspec],
        out_specs=spec,
        out_shape=jax.ShapeDtypeStruct((bsz, seqlen, cw), BF16),
        compiler_params=_cparams("arbitrary", "arbitrary"),
    )(p3, *fwd, *bwd)


def _readout_c(p, hf, hb):
    m, cw = hf.shape
    tm = _row_tile(m, 512)
    spec = pl.BlockSpec((tm, cw), lambda i: (i, 0))
    return pl.pallas_call(
        _readout_c_kernel,
        grid=(m // tm,),
        in_specs=[spec, spec, spec],
        out_specs=spec,
        out_shape=jax.ShapeDtypeStruct((m, cw), BF16),
        compiler_params=_cparams("arbitrary"),
    )(p, hf, hb)


def kernel(x, c, ctx, c_ctx, ada_w, ada_b, norm1_w, norm2_w, ffn_w_gate, ffn_w_up, ffn_w_down, final_norm_w, ab_w_in, ab_w_out, m_conv_w, m_conv_b, m_dt_bias, m_a_log, m_d, m_norm_w, r_mu, r_w0, r_w2, r_a0, r_a2, r_kk, r_ka, r_rk, r_g2, r_lnx_w, r_lnx_b, c_w_in, c_w_out, c_conv_w, c_conv_b, c_wa, c_ba, c_wx, c_bx, c_lambda):
    bsz, seqlen, d = x.shape
    ctx_len = ctx.shape[1]
    depth = ada_w.shape[0]
    n_lat, n_ctx = bsz * seqlen, bsz * ctx_len
    row = lambda t: t.reshape(1, -1)

    mod = _adaln(jnp.concatenate([c_ctx[None, :], c], axis=0), ada_w, ada_b)

    xl = x.reshape(n_lat, d)
    xc = ctx.reshape(n_ctx, d)
    mix_a = m_norm_w.shape[1]
    m_heads = m_dt_bias.shape[2]
    r_heads = r_rk.shape[2]
    lora = r_w2.shape[2]
    e64 = jnp.tile(jnp.repeat(jnp.eye(m_heads, dtype=BF16), M_HEADDIM, axis=1), (3, 1))
    hmask = jnp.tile(jnp.repeat(jnp.eye(m_heads, dtype=BF16), M_CHUNK, axis=1), (3, 1))

    for layer in range(depth):
        with_ctx = layer < depth - 1
        part = lambda k: mod[layer, :, k * d:(k + 1) * d]
        lat_vec = lambda k: part(k)[1:1 + bsz, None, :]
        ctx_vec = lambda k: part(k)[0:1, None, :]
        nw1, nw2 = norm1_w[layer], norm2_w[layer]

        if layer % 2 == 0:
            e = layer // 2
            w = ab_w_in[e].astype(BF16)
            o1 = mix_a
            o2 = o1 + mix_a + 2 * M_GROUPS * M_STATE
            o3 = o2 + m_heads
            o4 = o3 + 3 * r_heads * R_HEADSIZE + 2 * lora
            rkv_w = 3 * r_heads * R_HEADSIZE
            zeros = lambda n: jnp.zeros((d, n), w.dtype)
            w_in = jnp.concatenate(
                [w[:, o1:o2], w[:, :o1], w[:, o3:o3 + rkv_w], w[:, o4:],
                 w[:, o3 + rkv_w:o3 + rkv_w + lora], zeros(LOW_AL - LOW_WL - lora),
                 w[:, o3 + rkv_w + lora:o4], zeros(LOW_DT - LOW_AL - lora), w[:, o2:o3],
                 zeros(LOW_W - LOW_DT - m_heads)],
                axis=1)
            width = w_in.shape[1]
            w_out = ab_w_out[e].astype(BF16)
            g2w = r_g2[e].astype(BF16)

            def project(xs, sh, sc, rows_per_vec, seg_len):
                pr = _norm_mm(xs, nw1, sh, sc, [w_in], rows_per_vec, F32)
                pr = pr.reshape(bsz, seg_len, width)
                dt_col = width - LOW_W + LOW_DT
                dt_t = jnp.swapaxes(pr[:, :, dt_col:dt_col + m_heads], 1, 2)
                return pr, dt_t

            pr_l, dtt_l = project(xl, lat_vec(0), lat_vec(1), seqlen, seqlen)
            pr_c, dtt_c = project(xc, ctx_vec(0), ctx_vec(1), n_ctx, ctx_len)

            ym_l = ym_c = yr_l = yr_c = bo_l = bo_c = None
            for dr in range(2):
                rev = dr == 1
                mp = dict(conv_w=m_conv_w[e, dr], conv_b=row(m_conv_b[e, dr]),
                          dtb_row=row(m_dt_bias[e, dr]), dtb_col=m_dt_bias[e, dr].reshape(-1, 1),
                          alog_row=row(m_a_log[e, dr]), alog_col=m_a_log[e, dr].reshape(-1, 1),
                          dskip=row(jnp.repeat(m_d[e, dr], M_HEADDIM)), e64=e64, hmask=hmask)
                h0 = jnp.zeros((bsz, M_GROUPS, m_heads // M_GROUPS * M_HEADDIM, M_STATE), F32)
                y_c, h_c = _ssd_direction(pr_c, dtt_c, h0, mp, rev)
                y_l, _ = _ssd_direction(pr_l, dtt_l, h_c, mp, rev)
                mu = r_mu[e, dr]
                mu_sm = (jnp.zeros((LOW_W,), F32).at[LOW_WL:LOW_WL + lora].set(mu[rkv_w:rkv_w + lora])
                         .at[LOW_AL:LOW_AL + lora].set(mu[rkv_w + lora:]))
                rp = dict(mu_rkv=row(mu[:rkv_w]), mu_sm=row(mu_sm), w0=row(r_w0[e, dr]),
                          w2=r_w2[e, dr].astype(BF16), a0=row(r_a0[e, dr]), a2=r_a2[e, dr].astype(BF16),
                          kk=row(r_kk[e, dr]), ka=row(r_ka[e, dr]), rk=row(r_rk[e, dr]))
                s0 = jnp.zeros((bsz, r_heads // R_GROUP, R_HEADSIZE, R_GROUP * R_HEADSIZE), F32)
                v_c, b_c, s_c = _rwkv_direction(pr_c, s0, rp, rev)
                v_l, b_l, _ = _rwkv_direction(pr_l, s_c, rp, rev)
                if dr == 0:
                    ym_l, ym_c, yr_l, yr_c, bo_l, bo_c = y_l, y_c, v_l, v_c, b_l, b_c
                else:
                    ym_l, yr_l, bo_l = (ym_l, y_l), (yr_l, v_l), (bo_l, b_l)
                    ym_c, yr_c, bo_c = (ym_c, y_c), (yr_c, v_c), (bo_c, b_c)

            ro = dict(norm_w=row(m_norm_w[e]), lnx_w=row(r_lnx_w[e]), lnx_b=row(r_lnx_b[e]), g2=g2w)
            act_l = _readout_ab(pr_l, ym_l, yr_l, bo_l, ro, r_heads)
            xl = _mm_res(act_l, w_out, xl, lat_vec(2), seqlen, tm_cap=2048)
            if with_ctx:
                act_c = _readout_ab(pr_c, ym_c, yr_c, bo_c, ro, r_heads)
                xc = _mm_res(act_c, w_out, xc, ctx_vec(2), n_ctx)
        else:
            o = layer // 2
            w_out = c_w_out[o].astype(BF16)
            cw = w_out.shape[0]
            p_l = _norm_mm(xl, nw1, lat_vec(0), lat_vec(1), [c_w_in], seqlen, F32, layer=o)
            p_c = _norm_mm(xc, nw1, ctx_vec(0), ctx_vec(1), [c_w_in], n_ctx, F32, layer=o)
            p3_l = p_l.reshape(bsz, seqlen, 2 * cw)
            hs_l, hs_c = [], []
            for dr in range(2):
                rev = dr == 1
                cp = dict(conv_w=c_conv_w[o, dr], conv_b=row(c_conv_b[o, dr]),
                          wa=c_wa[o, dr].astype(BF16), ba=row(c_ba[o, dr]),
                          wx=c_wx[o, dr].astype(BF16), bx=row(c_bx[o, dr]), lam=row(c_lambda[o, dr]))
                h0 = jnp.zeros((bsz, HALO, cw), F32)
                h_c, s_c = _rglru_direction(p_c.reshape(bsz, ctx_len, 2 * cw), h0, cp, rev,
                                            _row_tile(ctx_len, 128))
                hs_l.append(_rglru_grid_direction(p3_l, s_c, cp, rev)[:3])
                hs_c.append(h_c.reshape(n_ctx, cw))
            act_l = _readout_grid(p3_l, hs_l[0], hs_l[1]).reshape(n_lat, cw)
            xl = _mm_res(act_l, w_out, xl, lat_vec(2), seqlen, tm_cap=2048)
            if with_ctx:
                act_c = _readout_c(p_c, hs_c[0], hs_c[1])
                xc = _mm_res(act_c, w_out, xc, ctx_vec(2), n_ctx)

        ffn_up = [ffn_w_gate, ffn_w_up]
        wd = ffn_w_down[layer].astype(BF16)
        act = _norm_mm(xl, nw2, lat_vec(3), lat_vec(4), ffn_up, seqlen, BF16, layer=layer)
        xl = _mm_res(act, wd, xl, lat_vec(5), seqlen)
        if with_ctx:
            act = _norm_mm(xc, nw2, ctx_vec(3), ctx_vec(4), ffn_up, n_ctx, BF16, layer=layer)
            xc = _mm_res(act, wd, xc, ctx_vec(5), n_ctx)

    return _final_norm(xl, final_norm_w).reshape(bsz, seqlen, d)
```

```python
import functools
import math

import jax
import jax.numpy as jnp
from jax import lax
from jax.experimental import pallas as pl
from jax.experimental.pallas import tpu as pltpu

F32 = jnp.float32
BF16 = jnp.bfloat16

NORM_EPS = 1e-6
GRID_W = 64
M_HEADDIM = 64
M_GROUPS = 4
M_STATE = 128
M_CONV = 4
M_CHUNK = 128
M_NORM_EPS = 1e-5
R_HEADSIZE = 64
R_CHUNK = 64
R_GROUP = 4
R_SUB = 4
R_LN_EPS = 64e-5
LOW_W = 512
LOW_WL = 256
LOW_AL = 384
LOW_DT = 480
C_HEADS = 8
C_CONV = 4
C_ROWS_PER_STEP = 8
RG_C = 8.0

SCAN_OUT_DTYPE = BF16
VMEM_LIMIT_BYTES = 56 * 1024 * 1024
HALO = 8


def _cparams(*sem):
    return pltpu.CompilerParams(dimension_semantics=sem, vmem_limit_bytes=VMEM_LIMIT_BYTES)


def _bdot(a, b):
    return jnp.dot(a.astype(BF16), b.astype(BF16), preferred_element_type=F32)


def _bdot_nt(a, b):
    return lax.dot_general(a.astype(BF16), b.astype(BF16), (((1,), (1,)), ((), ())),
                           preferred_element_type=F32)


def _split3(x):
    hi = x.astype(BF16)
    r1 = x - hi.astype(F32)
    mid = r1.astype(BF16)
    lo = (r1 - mid.astype(F32)).astype(BF16)
    return hi, mid, lo


def _tri_cumsum(tri_bf16, x, parts=3):
    hi, mid, lo = _split3(x)
    dot = lambda p: jnp.dot(tri_bf16, p, preferred_element_type=F32)
    if parts == 2:
        return dot(hi) + dot(mid)
    return dot(hi) + (dot(mid) + dot(lo))


def _silu(x):
    return x * jax.nn.sigmoid(x)


def _softplus(x):
    return jnp.maximum(x, 0.0) + jnp.log1p(jnp.exp(-jnp.abs(x)))


def _adaln_kernel(cb_ref, w_ref, b_ref, o_ref, *, n_vec, tn):
    rows = []
    for v in range(n_vec):
        cols = []
        for j in range(tn // 128):
            wj = w_ref[:, j * 128:(j + 1) * 128]
            cols.append(jnp.sum(wj * cb_ref[v], axis=0, keepdims=True))
        rows.append(jnp.concatenate(cols, axis=1))
    pad = jnp.zeros((HALO - n_vec, tn), F32)
    o_ref[...] = jnp.concatenate(rows + [pad], axis=0) + b_ref[...]


def _adaln(cond, ada_w, ada_b):
    depth, d, n6 = ada_w.shape
    n_vec = cond.shape[0]
    tn = 1024
    cb = jnp.broadcast_to(_silu(cond)[:, :, None], (n_vec, d, 128))
    return pl.pallas_call(
        functools.partial(_adaln_kernel, n_vec=n_vec, tn=tn),
        grid=(depth, n6 // tn),
        in_specs=[pl.BlockSpec((n_vec, d, 128), lambda l, j: (0, 0, 0)),
                  pl.BlockSpec((None, d, tn), lambda l, j: (l, 0, j)),
                  pl.BlockSpec((None, 1, tn), lambda l, j: (l, 0, j))],
        out_specs=pl.BlockSpec((None, HALO, tn), lambda l, j: (l, 0, j)),
        out_shape=jax.ShapeDtypeStruct((depth, HALO, n6), F32),
        compiler_params=_cparams("arbitrary", "arbitrary"),
    )(cb, ada_w, ada_b.reshape(depth, 1, n6))


def _modulated(x_ref, nw_ref, sh_ref, sc_ref):
    x = x_ref[...]
    y = x * lax.rsqrt(jnp.mean(x * x, axis=-1, keepdims=True) + NORM_EPS)
    return (y * nw_ref[...]) * (1.0 + sc_ref[...]) + sh_ref[...]


def _norm_mm_kernel(x_ref, nw_ref, sh_ref, sc_ref, w_ref, o_ref, h_ref):
    @pl.when(pl.program_id(1) == 0)
    def _():
        h_ref[...] = _modulated(x_ref, nw_ref, sh_ref, sc_ref).astype(BF16)

    o_ref[...] = jnp.dot(h_ref[...], w_ref[...].astype(BF16),
                         preferred_element_type=F32).astype(o_ref.dtype)


def _norm_swiglu_kernel(x_ref, nw_ref, sh_ref, sc_ref, wg_ref, wu_ref, o_ref, h_ref):
    @pl.when(pl.program_id(1) == 0)
    def _():
        h_ref[...] = _modulated(x_ref, nw_ref, sh_ref, sc_ref).astype(BF16)

    h = h_ref[...]
    g = jnp.dot(h, wg_ref[...].astype(BF16), preferred_element_type=F32)
    u = jnp.dot(h, wu_ref[...].astype(BF16), preferred_element_type=F32)
    o_ref[...] = (_silu(g) * u).astype(o_ref.dtype)


def _row_tile(m, cap):
    t = cap
    while m % t:
        t //= 2
    return t


def _norm_mm(x, nw, shift, scale, weights, rows_per_vec, out_dtype, tm_cap=1024, tn=512, layer=0):
    m, d = x.shape
    n = weights[0].shape[-1]
    tm = _row_tile(rows_per_vec, tm_cap)
    vec_map = lambda i, j: ((i * tm) // rows_per_vec, 0, 0)
    kern = _norm_mm_kernel if len(weights) == 1 else _norm_swiglu_kernel

    def w_spec(w):
        if w.ndim == 3:
            return pl.BlockSpec((None, d, tn), lambda i, j: (layer, 0, j))
        return pl.BlockSpec((d, tn), lambda i, j: (0, j))
    return pl.pallas_call(
        kern,
        grid=(m // tm, n // tn),
        in_specs=[pl.BlockSpec((tm, d), lambda i, j: (i, 0)),
                  pl.BlockSpec((1, d), lambda i, j: (0, 0)),
                  pl.BlockSpec((None, 1, d), vec_map),
                  pl.BlockSpec((None, 1, d), vec_map)]
                 + [w_spec(w) for w in weights],
        out_specs=pl.BlockSpec((tm, tn), lambda i, j: (i, j)),
        out_shape=jax.ShapeDtypeStruct((m, n), out_dtype),
        scratch_shapes=[pltpu.VMEM((tm, d), BF16)],
        compiler_params=_cparams("arbitrary", "arbitrary"),
    )(x, nw.reshape(1, d), shift, scale, *weights)


def _mm_res_kernel(a_ref, w_ref, res_ref, g_ref, o_ref):
    acc = jnp.dot(a_ref[...], w_ref[...], preferred_element_type=F32)
    o_ref[...] = res_ref[...] + g_ref[...] * acc


def _mm_res(a, w, res, gate, rows_per_vec, tm_cap=1024, tn=512):
    m, k = a.shape
    n = w.shape[1]
    tm = _row_tile(rows_per_vec, tm_cap)
    return pl.pallas_call(
        _mm_res_kernel,
        grid=(m // tm, n // tn),
        in_specs=[pl.BlockSpec((tm, k), lambda i, j: (i, 0)),
                  pl.BlockSpec((k, tn), lambda i, j: (0, j)),
                  pl.BlockSpec((tm, tn), lambda i, j: (i, j)),
                  pl.BlockSpec((None, 1, tn), lambda i, j: ((i * tm) // rows_per_vec, 0, j))],
        out_specs=pl.BlockSpec((tm, tn), lambda i, j: (i, j)),
        out_shape=jax.ShapeDtypeStruct((m, n), F32),
        compiler_params=_cparams("arbitrary", "arbitrary"),
    )(a, w, res, gate)


def _final_norm_kernel(x_ref, w_ref, o_ref):
    x = x_ref[...]
    o_ref[...] = x * lax.rsqrt(jnp.mean(x * x, axis=-1, keepdims=True) + NORM_EPS) * w_ref[...]


def _final_norm(x, w):
    m, d = x.shape
    tm = _row_tile(m, 512)
    return pl.pallas_call(
        _final_norm_kernel,
        grid=(m // tm,),
        in_specs=[pl.BlockSpec((tm, d), lambda i: (i, 0)), pl.BlockSpec((1, d), lambda i: (0, 0))],
        out_specs=pl.BlockSpec((tm, d), lambda i: (i, 0)),
        out_shape=jax.ShapeDtypeStruct((m, d), F32),
        compiler_params=_cparams("arbitrary"),
    )(x, w.reshape(1, d))


def _scan_block(i, n_blocks, rev):
    return (n_blocks - 1 - i) if rev else i


def _halo_block(i, n_blocks, rows, rev):
    per = rows // HALO
    if rev:
        return jnp.minimum((_scan_block(i, n_blocks, True) + 1) * per, n_blocks * per - 1)
    return jnp.maximum(i * per - 1, 0)


def _stage_with_halo(scr_ref, x_ref, halo_ref, keep, rows, rev):
    if rev:
        scr_ref[0:rows, :] = x_ref[...]
        scr_ref[rows:rows + HALO, :] = halo_ref[...] * keep
    else:
        scr_ref[0:HALO, :] = halo_ref[...] * keep
        scr_ref[HALO:HALO + rows, :] = x_ref[...]


def _lagged(scr_ref, lag, rows, rev):
    start = lag if rev else HALO - lag
    return scr_ref[start:start + rows, :]


def _rwkv_kernel(rkv_ref, sm_ref, rkv_h_ref, sm_h_ref, mu_rkv_ref, mu_sm_ref, w0_ref, w2_ref,
                 a0_ref, a2_ref, kkw_ref, kaw_ref, rkw_ref, s0_ref,
                 y_ref, bonus_ref, sl_ref, s_scr, rkv_scr, sm_scr, *, rev, n_heads, lora, nsub):
    i = pl.program_id(1)
    c = R_CHUNK
    rows = nsub * c
    n = R_HEADSIZE
    hw = n_heads * n

    @pl.when(i == 0)
    def _():
        s_scr[...] = s0_ref[...]

    keep = (i > 0).astype(F32)
    _stage_with_halo(rkv_scr, rkv_ref, rkv_h_ref, keep, rows, rev)
    _stage_with_halo(sm_scr, sm_ref, sm_h_ref, keep, rows, rev)
    x = rkv_ref[...]
    f = x + (_lagged(rkv_scr, 1, rows, rev) - x) * mu_rkv_ref[...]
    xs = sm_ref[...]
    fs = xs + (_lagged(sm_scr, 1, rows, rev) - xs) * mu_sm_ref[...]
    r = f[:, :hw]
    k = f[:, hw:2 * hw]
    v = f[:, 2 * hw:3 * hw]
    wl = fs[:, LOW_WL:LOW_WL + lora]
    al = fs[:, LOW_AL:LOW_AL + lora]

    ww = w0_ref[...] + _bdot(jnp.tanh(wl), w2_ref[...])
    lw = -math.exp(-0.5) * jax.nn.sigmoid(ww)
    a = jax.nn.sigmoid(a0_ref[...] + _bdot(al, a2_ref[...]))
    kkr = k * kkw_ref[...]
    k2 = k * (1.0 + (a - 1.0) * kaw_ref[...])
    rk = r * k2 * rkw_ref[...]

    cshift = int(math.log2(c))
    ti = lax.broadcasted_iota(jnp.int32, (rows, rows), 0)
    tj = lax.broadcasted_iota(jnp.int32, (rows, rows), 1)
    same_chunk = (ti >> cshift) == (tj >> cshift)
    upto = same_chunk & ((tj >= ti) if rev else (tj <= ti))
    g = _tri_cumsum(upto.astype(BF16), lw, parts=2)
    rsl = [slice(k * c, (k + 1) * c) for k in range(nsub)]
    gtot = [g[k * c:k * c + 1, :] if rev else g[(k + 1) * c - 1:(k + 1) * c, :] for k in range(nsub)]
    eg = jnp.exp(g)
    eneg = jnp.exp(-g)
    egp = jnp.exp(g - lw)
    etail = jnp.exp(jnp.concatenate([jnp.broadcast_to(t, (c, hw)) for t in gtot], axis=0) - g)
    dg = [jnp.exp(t) for t in gtot]
    rd_all = r * eg
    kp_all = k2 * eneg
    kpp_all = k2 * etail

    gw = R_GROUP * n
    groups = range(hw // gw)
    gsl = [slice(q * gw, (q + 1) * gw) for q in groups]
    shift = int(math.log2(n))
    lane_head = lax.broadcasted_iota(jnp.int32, (1, 128), 1) >> shift
    keep_f32 = [(lane_head == j).astype(F32) for j in range(128 // n)]
    keep_lanes = [t.astype(BF16) for t in keep_f32]

    def bdiag(t):
        tb = t.astype(BF16)
        zero = jnp.zeros((c, 128), BF16)
        rows = []
        for h in range(R_GROUP):
            lt, j = divmod(h * n, 128)
            piece = tb[:, lt * 128:(lt + 1) * 128] * keep_lanes[j // n]
            rows.append(jnp.concatenate([piece if u == lt else zero for u in range(gw // 128)], axis=1))
        return jnp.concatenate(rows, axis=0)

    def diag_blocks(full):
        tiles = []
        for lt in range(gw // 128):
            acc = None
            for j in range(128 // n):
                h = lt * (128 // n) + j
                part = full[h * n:(h + 1) * n, lt * 128:(lt + 1) * 128] * keep_f32[j]
                acc = part if acc is None else acc + part
            tiles.append(acc)
        return jnp.concatenate(tiles, axis=1)

    def head_sums(t):
        tiles = []
        for lt in range(gw // 128):
            x = t[:, lt * 128:(lt + 1) * 128]
            acc = None
            for keep_j in keep_f32:
                part = jnp.sum(x * keep_j, axis=-1, keepdims=True) * keep_j
                acc = part if acc is None else acc + part
            tiles.append(acc)
        return jnp.concatenate(tiles, axis=1)

    tdot = lambda p, q: lax.dot_general(p.astype(BF16), q.astype(BF16), (((0,), (0,)), ((), ())),
                                        preferred_element_type=F32)
    gi = lax.broadcasted_iota(jnp.int32, (c, gw), 0)
    gj = lax.broadcasted_iota(jnp.int32, (c, gw), 1) & (c - 1)
    before_g = (gj > gi) if rev else (gj < gi)
    upto_g = (gj >= gi) if rev else (gj <= gi)
    eye_g = (gi == gj).astype(F32)

    kk_all = jnp.concatenate(
        [kkr[:, s] * lax.rsqrt(jnp.maximum(head_sums(kkr[:, s] * kkr[:, s]), 1e-24)) for s in gsl], axis=1)
    bh_all = kk_all * a
    kkd_all = kk_all * egp
    bp_all = bh_all * eneg
    bpp_all = bh_all * etail
    bonus_ref[...] = (jnp.concatenate([head_sums(rk[:, s]) for s in gsl], axis=1) * v
                      ).astype(bonus_ref.dtype)

    units = [(k, q) for k in range(nsub) for q in groups]
    un = range(len(units))
    cut = lambda t: [t[rsl[k], gsl[q]] for k, q in units]
    kkd, bp, bpp, rd, vq, kp, kpp = (cut(t) for t in (kkd_all, bp_all, bpp_all, rd_all, v, kp_all, kpp_all))
    amat = [_bdot_nt(jnp.concatenate([kkd[u], rd[u]], axis=0),
                     jnp.concatenate([bdiag(kp[u]), bdiag(bp[u])], axis=0)) for u in un]
    mk = [jnp.where(before_g, m[:c, :gw], 0.0) for m in amat]
    nk = [jnp.where(upto_g, m[c:, :gw], 0.0) for m in amat]
    nb = [jnp.where(upto_g, m[c:, gw:], 0.0) for m in amat]
    p2 = [jnp.where(before_g, -m[:c, gw:], 0.0) for m in amat]
    tinv = [eye_g + t for t in p2]
    p2 = [_bdot(t, bdiag(t)) for t in p2]
    for _ in range(cshift - 2):
        sq = [_bdot(jnp.concatenate([tinv[u], p2[u]], axis=0), bdiag(p2[u])) for u in un]
        tinv = [tinv[u] + sq[u][:c] for u in un]
        p2 = [t[c:] for t in sq]
    tinv = [tinv[u] + _bdot(tinv[u], bdiag(p2[u])) for u in un]
    mnv = [_bdot(jnp.concatenate([mk[u], nk[u]], axis=0), bdiag(vq[u])) for u in un]
    wu = [_bdot(tinv[u], jnp.concatenate([bdiag(kkd[u]), bdiag(mnv[u][:c])], axis=1))
          for u in un]
    nwu = [_bdot(nb[u], jnp.concatenate([bdiag(wu[u][:, :gw]), bdiag(wu[u][:, gw:])], axis=1))
           for u in un]
    wub = [tdot(wu[u], bpp[u]) for u in un]
    pmat = [bdiag(diag_blocks(t[:gw])) for t in wub]
    dmat = [diag_blocks(tdot(vq[u], kpp[u])) - diag_blocks(wub[u][gw:]) for u in un]
    qt = [rd[u] - nwu[u][:, :gw] for u in un]
    y0 = [mnv[u][c:] - nwu[u][:, gw:] for u in un]
    state = [s_scr[q] for q in groups]
    for k in (reversed(range(nsub)) if rev else range(nsub)):
        for q in groups:
            u = k * len(groups) + q
            y_ref[rsl[k], gsl[q]] = (y0[u] + _bdot_nt(qt[u], bdiag(state[q]))).astype(y_ref.dtype)
            state[q] = state[q] * dg[k][:, gsl[q]] - _bdot(state[q], pmat[u]) + dmat[u]
    for q in groups:
        s_scr[q] = state[q]

    @pl.when(i == pl.num_programs(1) - 1)
    def _():
        sl_ref[...] = s_scr[...]


def _rwkv_direction(pr, s0, p, rev):
    bsz, seqlen, width = pr.shape
    hw = s0.shape[1] * s0.shape[3]
    n_heads = hw // R_HEADSIZE
    nsub = R_SUB if seqlen % (R_SUB * R_CHUNK) == 0 else 1
    c = nsub * R_CHUNK
    nblk = seqlen // c
    sm_col = width // LOW_W - 1
    blk = lambda i: _scan_block(i, nblk, rev)
    halo = lambda i: _halo_block(i, nblk, c, rev)
    vec = lambda w: pl.BlockSpec((1, w), lambda b, i: (0, 0))
    lora = p['w2'].shape[0]
    state_spec = pl.BlockSpec((None,) + s0.shape[1:], lambda b, i: (b, 0, 0, 0))
    return pl.pallas_call(
        functools.partial(_rwkv_kernel, rev=rev, n_heads=n_heads, lora=lora, nsub=nsub),
        grid=(bsz, nblk),
        in_specs=[pl.BlockSpec((None, c, 3 * hw), lambda b, i: (b, blk(i), 1)),
                  pl.BlockSpec((None, c, LOW_W), lambda b, i: (b, blk(i), sm_col)),
                  pl.BlockSpec((None, HALO, 3 * hw), lambda b, i: (b, halo(i), 1)),
                  pl.BlockSpec((None, HALO, LOW_W), lambda b, i: (b, halo(i), sm_col)),
                  vec(3 * hw), vec(LOW_W), vec(hw),
                  pl.BlockSpec((lora, hw), lambda b, i: (0, 0)),
                  vec(hw),
                  pl.BlockSpec((lora, hw), lambda b, i: (0, 0)),
                  vec(hw), vec(hw), vec(hw),
                  state_spec],
        out_specs=[pl.BlockSpec((None, c, hw), lambda b, i: (b, blk(i), 0)),
                   pl.BlockSpec((None, c, hw), lambda b, i: (b, blk(i), 0)),
                   state_spec],
        out_shape=[jax.ShapeDtypeStruct((bsz, seqlen, hw), SCAN_OUT_DTYPE),
                   jax.ShapeDtypeStruct((bsz, seqlen, hw), SCAN_OUT_DTYPE),
                   jax.ShapeDtypeStruct(s0.shape, F32)],
        scratch_shapes=[pltpu.VMEM(s0.shape[1:], F32),
                        pltpu.VMEM((c + HALO, 3 * hw), F32),
                        pltpu.VMEM((c + HALO, LOW_W), F32)],
        compiler_params=_cparams("arbitrary", "arbitrary"),
    )(pr, pr, pr, pr, p['mu_rkv'], p['mu_sm'], p['w0'], p['w2'], p['a0'], p['a2'],
      p['kk'], p['ka'], p['rk'], s0)


def _expand_cols(x, e_bf16):
    hi, mid, lo = _split3(x)
    dot = lambda p: jnp.dot(p, e_bf16, preferred_element_type=F32)
    return dot(hi) + (dot(mid) + dot(lo))


def _causal_conv(scr_ref, w_ref, b_ref, rows, taps, rev):
    acc = b_ref[...] + w_ref[taps - 1:taps, :] * _lagged(scr_ref, 0, rows, rev)
    for kk in range(taps - 1):
        acc = acc + w_ref[kk:kk + 1, :] * _lagged(scr_ref, taps - 1 - kk, rows, rev)
    return acc


def _ssd_kernel(xbc_ref, halo_ref, sm_ref, dtt_ref, cw_ref, cb_ref, dtb_row_ref, dtb_col_ref,
                alog_row_ref, alog_col_ref, dskip_ref, e64_ref, hmask_ref, h0_ref,
                y_ref, hl_ref, h_scr, x_scr, *, rev, n_heads):
    i = pl.program_id(1)
    c = M_CHUNK
    p = M_HEADDIM
    ns = M_STATE
    mix = n_heads * p
    hpg = n_heads // M_GROUPS
    gw = hpg * p

    @pl.when(i == 0)
    def _():
        h_scr[...] = h0_ref[...]

    keep = (i > 0).astype(F32)
    _stage_with_halo(x_scr, xbc_ref, halo_ref, keep, c, rev)
    xbc = _silu(_causal_conv(x_scr, cw_ref, cb_ref, c, M_CONV, rev))
    xs = xbc[:, :mix]
    bm = [xbc[:, mix + g * ns:mix + (g + 1) * ns] for g in range(M_GROUPS)]
    cm = [xbc[:, mix + (M_GROUPS + g) * ns:mix + (M_GROUPS + g + 1) * ns] for g in range(M_GROUPS)]

    ti = lax.broadcasted_iota(jnp.int32, (c, c), 0)
    tj = lax.broadcasted_iota(jnp.int32, (c, c), 1)
    upto = (tj >= ti) if rev else (tj <= ti)
    upto_t = (ti >= tj) if rev else (ti <= tj)
    end = 0 if rev else c - 1

    dt = _softplus(sm_ref[:, LOW_DT:LOW_DT + n_heads] + dtb_row_ref[...])
    a_cs = _tri_cumsum(upto.astype(BF16), dt * -jnp.exp(alog_row_ref[...]))
    dt_t = _softplus(dtt_ref[...] + dtb_col_ref[...])
    a_cs_t = _expand_cols(dt_t * -jnp.exp(alog_col_ref[...]), upto_t.astype(BF16))
    a_end_t = a_cs_t[:, end:end + 1]
    w_t = dt_t * jnp.exp(a_end_t - a_cs_t)
    grow_t = jnp.broadcast_to(jnp.exp(a_end_t), (n_heads, ns))

    parts = [t.astype(F32) for t in _split3(a_cs)]
    ones = jnp.ones((c, n_heads), F32)
    lhs = jnp.concatenate(parts + [ones] * 3, axis=1).astype(BF16)
    hmask = hmask_ref[...]
    parts_t = jnp.concatenate(list(_split3(a_cs_t)), axis=0)
    rhs = jnp.concatenate([hmask, -jnp.concatenate([parts_t] * n_heads, axis=1) * hmask], axis=0)
    seg = jnp.dot(lhs, rhs, preferred_element_type=F32)
    ea_full = jnp.exp(jnp.dot(lhs[:, :3 * n_heads], e64_ref[...], preferred_element_type=F32))

    lane_head = lax.broadcasted_iota(jnp.int32, (1, gw), 1) >> int(math.log2(p))
    keep_lanes = [(lane_head == j).astype(BF16) for j in range(hpg)]
    xs_t = jnp.transpose(xs)
    y_diag, y_off = [], []
    for g in range(M_GROUPS):
        heads = range(g * hpg, (g + 1) * hpg)
        cb = _bdot_nt(cm[g], bm[g])
        scores = []
        for h in heads:
            decay = jnp.where(upto, jnp.exp(jnp.minimum(seg[:, h * c:(h + 1) * c], 0.0)), 0.0)
            scores.append((cb * decay * dt_t[h:h + 1, :]).astype(BF16))
        xg = xs[:, g * gw:(g + 1) * gw].astype(BF16)
        x_bd = jnp.concatenate([xg * keep_lanes[j] for j in range(hpg)], axis=0)
        y_diag.append(jnp.dot(jnp.concatenate(scores, axis=1), x_bd, preferred_element_type=F32))
        h_in = h_scr[g]
        y_off.append(_bdot_nt(cm[g], h_in))
        wrows = jnp.concatenate([jnp.broadcast_to(w_t[h:h + 1, :], (p, c)) for h in heads], axis=0)
        grow = jnp.concatenate([jnp.broadcast_to(grow_t[h:h + 1, :], (p, ns)) for h in heads], axis=0)
        h_scr[g] = h_in * grow + _bdot(xs_t[g * gw:(g + 1) * gw, :] * wrows, bm[g])
    y_ref[...] = (jnp.concatenate(y_diag, axis=1) + jnp.concatenate(y_off, axis=1) * ea_full
                  + xs * dskip_ref[...]).astype(y_ref.dtype)

    @pl.when(i == pl.num_programs(1) - 1)
    def _():
        hl_ref[...] = h_scr[...]


def _ssd_direction(pr, dt_t, h0, p, rev):
    bsz, seqlen, width = pr.shape
    n_heads = dt_t.shape[1]
    mix = n_heads * M_HEADDIM
    xw = mix + 2 * M_GROUPS * M_STATE
    c = M_CHUNK
    nblk = seqlen // c
    sm_col = width // LOW_W - 1
    blk = lambda i: _scan_block(i, nblk, rev)
    halo = lambda i: _halo_block(i, nblk, c, rev)
    full = lambda a: pl.BlockSpec(a.shape, lambda b, i: (0,) * a.ndim)
    consts = [p['conv_w'], p['conv_b'], p['dtb_row'], p['dtb_col'], p['alog_row'], p['alog_col'],
              p['dskip'], p['e64'], p['hmask']]
    state_spec = pl.BlockSpec((None,) + h0.shape[1:], lambda b, i: (b, 0, 0, 0))
    return pl.pallas_call(
        functools.partial(_ssd_kernel, rev=rev, n_heads=n_heads),
        grid=(bsz, nblk),
        in_specs=[pl.BlockSpec((None, c, xw), lambda b, i: (b, blk(i), 0)),
                  pl.BlockSpec((None, HALO, xw), lambda b, i: (b, halo(i), 0)),
                  pl.BlockSpec((None, c, LOW_W), lambda b, i: (b, blk(i), sm_col)),
                  pl.BlockSpec((None, n_heads, c), lambda b, i: (b, 0, blk(i)))]
                 + [full(a) for a in consts] + [state_spec],
        out_specs=[pl.BlockSpec((None, c, mix), lambda b, i: (b, blk(i), 0)), state_spec],
        out_shape=[jax.ShapeDtypeStruct((bsz, seqlen, mix), SCAN_OUT_DTYPE),
                   jax.ShapeDtypeStruct(h0.shape, F32)],
        scratch_shapes=[pltpu.VMEM(h0.shape[1:], F32), pltpu.VMEM((c + HALO, xw), F32)],
        compiler_params=_cparams("arbitrary", "arbitrary"),
    )(pr, pr, pr, dt_t, *consts, h0)


def _rglru_kernel(x_ref, halo_ref, cw_ref, cb_ref, wa_ref, ba_ref, wx_ref, bx_ref, lam_ref, h0_ref,
                  o_ref, hl_ref, h_scr, x_scr, a_scr, u_scr, *, rev, tb):
    i = pl.program_id(1)

    @pl.when(i == 0)
    def _():
        h_scr[...] = h0_ref[...]

    keep = (i > 0).astype(F32)
    _stage_with_halo(x_scr, x_ref, halo_ref, keep, tb, rev)
    xc = _causal_conv(x_scr, cw_ref, cb_ref, tb, C_CONV, rev)
    a_scr[...], u_scr[...] = _rglru_gates(xc, wa_ref, ba_ref, wx_ref, bx_ref, lam_ref)

    def step(t, h):
        tt = (tb - 1 - t) if rev else t
        h = a_scr[pl.ds(tt, 1), :] * h + u_scr[pl.ds(tt, 1), :]
        o_ref[pl.ds(tt, 1), :] = h
        return h

    h_scr[0:1, :] = lax.fori_loop(0, tb, step, h_scr[0:1, :], unroll=8)

    @pl.when(i == pl.num_programs(1) - 1)
    def _():
        hl_ref[...] = h_scr[...]


def _rglru_direction(xv, h0, p, rev, tb):
    bsz, rows, wtot = xv.shape
    width = h0.shape[2]
    ncol = wtot // (2 * width)
    nrb = rows // tb
    total = ncol * nrb
    per = tb // HALO

    def pos(i):
        s = _scan_block(i, total, rev)
        return s // nrb, s % nrb

    def x_map(b, i):
        col, q = pos(i)
        return b, q, 2 * col + 1

    def halo_map(b, i):
        s = _scan_block(i, total, rev)
        sp = jnp.minimum(s + 1, total - 1) if rev else jnp.maximum(s - 1, 0)
        col, q = sp // nrb, sp % nrb
        return b, (q * per if rev else (q + 1) * per - 1), 2 * col + 1

    def o_map(b, i):
        col, q = pos(i)
        return b, q, col

    full = lambda a: pl.BlockSpec(a.shape, lambda b, i: (0,) * a.ndim)
    consts = [p['conv_w'], p['conv_b'], p['wa'], p['ba'], p['wx'], p['bx'], p['lam']]
    state_spec = pl.BlockSpec((None, HALO, width), lambda b, i: (b, 0, 0))
    return pl.pallas_call(
        functools.partial(_rglru_kernel, rev=rev, tb=tb),
        grid=(bsz, total),
        in_specs=[pl.BlockSpec((None, tb, width), x_map), pl.BlockSpec((None, HALO, width), halo_map)]
                 + [full(a) for a in consts] + [state_spec],
        out_specs=[pl.BlockSpec((None, tb, width), o_map), state_spec],
        out_shape=[jax.ShapeDtypeStruct((bsz, rows, ncol * width), F32),
                   jax.ShapeDtypeStruct(h0.shape, F32)],
        scratch_shapes=[pltpu.VMEM((HALO, width), F32), pltpu.VMEM((tb + HALO, width), F32),
                        pltpu.VMEM((tb, width), F32), pltpu.VMEM((tb, width), F32)],
        compiler_params=_cparams("arbitrary", "arbitrary"),
    )(xv, xv, *consts, h0)


def _rglru_gates(xc, wa_ref, ba_ref, wx_ref, bx_ref, lam_ref):
    blk = xc.shape[1] // C_HEADS
    xh = [xc[:, h * blk:(h + 1) * blk] for h in range(C_HEADS)]
    gate_r = jax.nn.sigmoid(
        jnp.concatenate([_bdot(xh[h], wa_ref[h]) for h in range(C_HEADS)], axis=1) + ba_ref[...])
    gate_i = jax.nn.sigmoid(
        jnp.concatenate([_bdot(xh[h], wx_ref[h]) for h in range(C_HEADS)], axis=1) + bx_ref[...])
    log_a = -RG_C * gate_r * _softplus(-lam_ref[...])
    a = jnp.exp(log_a)
    one_m_a2 = jnp.maximum(-jnp.tanh(log_a) * (a * a + 1.0), 0.0)
    mult = jnp.where(one_m_a2 > 0.0, one_m_a2 * lax.rsqrt(one_m_a2), 0.0)
    return a, mult * gate_i * xc


def _rglru_grid_kernel(x_ref, e0_ref, e1_ref, e2_ref, cw_ref, cb_ref, wa_ref, ba_ref, wx_ref,
                       bx_ref, lam_ref, h0_ref, hloc_ref, aprod_ref, cin_ref, hl_ref,
                       ring_scr, h_scr, ap_scr, stage_scr, *, rev, ncol):
    i = pl.program_id(1)

    def from_previous_column(e_ref):
        _stage_with_halo(stage_scr, e_ref, jnp.zeros((HALO, e_ref.shape[1]), F32), 0.0, ncol, rev)
        return _lagged(stage_scr, 1, ncol, rev)

    @pl.when(i == 0)
    def _():
        ring_scr[0] = from_previous_column(e2_ref)
        ring_scr[1] = from_previous_column(e1_ref)
        ring_scr[2] = from_previous_column(e0_ref)
        h_scr[...] = jnp.zeros(h_scr.shape, F32)
        ap_scr[...] = jnp.ones(ap_scr.shape, F32)

    nrow = x_ref.shape[0] // ncol
    order = list(reversed(range(nrow))) if rev else list(range(nrow))
    rsl = [slice(m * ncol, (m + 1) * ncol) for m in range(nrow)]
    hist = [ring_scr[2], ring_scr[1], ring_scr[0]] + [x_ref[rsl[m], :] for m in order]
    xc = [cb_ref[...] + cw_ref[3:4, :] * hist[t + 3] + cw_ref[2:3, :] * hist[t + 2]
          + cw_ref[1:2, :] * hist[t + 1] + cw_ref[0:1, :] * hist[t] for t in range(nrow)]
    for lag in range(3):
        ring_scr[lag] = hist[-1 - lag]
    a, u = _rglru_gates(jnp.concatenate(xc, axis=0), wa_ref, ba_ref, wx_ref, bx_ref, lam_ref)
    h, ap = h_scr[...], ap_scr[...]
    for t, m in enumerate(order):
        ts = slice(t * ncol, (t + 1) * ncol)
        h = a[ts, :] * h + u[ts, :]
        ap = a[ts, :] * ap
        hloc_ref[rsl[m], :] = h.astype(hloc_ref.dtype)
        aprod_ref[rsl[m], :] = ap.astype(aprod_ref.dtype)
    h_scr[...] = h
    ap_scr[...] = ap

    @pl.when(i == pl.num_programs(1) - 1)
    def _():
        def step(t, carry):
            col = (ncol - 1 - t) if rev else t
            cin_ref[pl.ds(col, 1), :] = carry
            return h_scr[pl.ds(col, 1), :] + ap_scr[pl.ds(col, 1), :] * carry

        last = lax.fori_loop(0, ncol, step, h0_ref[0:1, :], unroll=8)
        hl_ref[...] = jnp.zeros(hl_ref.shape, F32)
        hl_ref[0:1, :] = last


def _rglru_grid_direction(p3, h0, p, rev):
    bsz, seqlen, _ = p3.shape
    width = h0.shape[2]
    ncol = GRID_W
    rows = seqlen // ncol
    nrow = C_ROWS_PER_STEP if rows % C_ROWS_PER_STEP == 0 else 1
    nblk = rows // nrow
    blk = lambda i: _scan_block(i, nblk, rev)
    edge = lambda k: pl.BlockSpec((None, ncol, width),
                                  lambda b, i: (b, (2 - k) if rev else (rows - 3 + k), 1))
    full = lambda a: pl.BlockSpec(a.shape, lambda b, i: (0,) * a.ndim)
    consts = [p['conv_w'], p['conv_b'], p['wa'], p['ba'], p['wx'], p['bx'], p['lam']]
    state_spec = pl.BlockSpec((None, HALO, width), lambda b, i: (b, 0, 0))
    row_spec = pl.BlockSpec((None, nrow * ncol, width), lambda b, i: (b, blk(i), 0))
    return pl.pallas_call(
        functools.partial(_rglru_grid_kernel, rev=rev, ncol=ncol),
        grid=(bsz, nblk),
        in_specs=[pl.BlockSpec((None, nrow * ncol, width), lambda b, i: (b, blk(i), 1)),
                  edge(0), edge(1), edge(2)] + [full(a) for a in consts] + [state_spec],
        out_specs=[row_spec, row_spec,
                   pl.BlockSpec((None, ncol, width), lambda b, i: (b, 0, 0)), state_spec],
        out_shape=[jax.ShapeDtypeStruct((bsz, seqlen, width), SCAN_OUT_DTYPE),
                   jax.ShapeDtypeStruct((bsz, seqlen, width), SCAN_OUT_DTYPE),
                   jax.ShapeDtypeStruct((bsz, ncol, width), F32),
                   jax.ShapeDtypeStruct(h0.shape, F32)],
        scratch_shapes=[pltpu.VMEM((3, ncol, width), F32), pltpu.VMEM((ncol, width), F32),
                        pltpu.VMEM((ncol, width), F32), pltpu.VMEM((ncol + HALO, width), F32)],
        compiler_params=_cparams("arbitrary", "arbitrary"),
    )(p3, p3, p3, p3, *consts, h0)


def _readout_ab_kernel(z_ref, sm_ref, ymf_ref, ymb_ref, yrf_ref, yrb_ref, bof_ref, bob_ref,
                       nw_ref, lw_ref, lb_ref, g2_ref, o_ref, *, n_heads):
    f32 = lambda ref: ref[...].astype(F32)
    t = (f32(ymf_ref) + f32(ymb_ref)) * _silu(z_ref[...])
    gw = t.shape[1] // M_GROUPS
    a_out = []
    for g in range(M_GROUPS):
        tg = t[:, g * gw:(g + 1) * gw]
        a_out.append(tg * lax.rsqrt(jnp.mean(tg * tg, axis=-1, keepdims=True) + M_NORM_EPS))
    a_out = jnp.concatenate(a_out, axis=1) * nw_ref[...]

    yr = f32(yrf_ref) + f32(yrb_ref)
    n = R_HEADSIZE
    yn = []
    for h in range(n_heads):
        yh = yr[:, h * n:(h + 1) * n]
        dlt = yh - jnp.mean(yh, axis=-1, keepdims=True)
        yn.append(dlt * lax.rsqrt(jnp.mean(dlt * dlt, axis=-1, keepdims=True) + R_LN_EPS))
    yn = jnp.concatenate(yn, axis=1) * lw_ref[...] + lb_ref[...]
    gate = _bdot(jax.nn.sigmoid(sm_ref[:, 0:g2_ref.shape[0]]), g2_ref[...])
    b_out = (yn + f32(bof_ref) + f32(bob_ref)) * gate
    o_ref[...] = jnp.concatenate([a_out, b_out], axis=1).astype(o_ref.dtype)


def _readout_ab(pr, ym, yr, bo, ro, n_heads):
    bsz, seqlen, width = pr.shape
    m = bsz * seqlen
    mix = ym[0].shape[2]
    tm = _row_tile(m, 256)
    flat = lambda t: t.reshape(m, t.shape[2])
    rowblk = lambda w, col: pl.BlockSpec((tm, w), lambda i: (i, col))
    full = lambda a: pl.BlockSpec(a.shape, lambda i: (0,) * a.ndim)
    consts = [ro['norm_w'], ro['lnx_w'], ro['lnx_b'], ro['g2']]
    return pl.pallas_call(
        functools.partial(_readout_ab_kernel, n_heads=n_heads),
        grid=(m // tm,),
        in_specs=[rowblk(mix, 2), rowblk(LOW_W, width // LOW_W - 1)] + [rowblk(mix, 0)] * 6
                 + [full(a) for a in consts],
        out_specs=rowblk(2 * mix, 0),
        out_shape=jax.ShapeDtypeStruct((m, 2 * mix), BF16),
        compiler_params=_cparams("arbitrary"),
    )(flat(pr), flat(pr), flat(ym[0]), flat(ym[1]), flat(yr[0]), flat(yr[1]), flat(bo[0]),
      flat(bo[1]), *consts)


def _gelu_tanh(g):
    return 0.5 * g * (1.0 + jnp.tanh(math.sqrt(2.0 / math.pi) * (g + 0.044715 * (g * g * g))))


def _readout_c_kernel(gy_ref, hf_ref, hb_ref, o_ref):
    o_ref[...] = ((hf_ref[...] + hb_ref[...]) * _gelu_tanh(gy_ref[...])).astype(o_ref.dtype)


def _readout_grid_kernel(gy_ref, hf_ref, af_ref, cf_ref, hb_ref, ab_ref, cb_ref, o_ref, *, ncol):
    for j in range(o_ref.shape[0] // ncol):
        rs = slice(j * ncol, (j + 1) * ncol)
        f32 = lambda ref: ref[rs, :].astype(F32)
        h = (f32(hf_ref) + f32(af_ref) * cf_ref[...]) + (f32(hb_ref) + f32(ab_ref) * cb_ref[...])
        o_ref[rs, :] = (h * _gelu_tanh(gy_ref[rs, :])).astype(o_ref.dtype)


def _readout_grid(p3, fwd, bwd):
    bsz, seqlen, _ = p3.shape
    cw = fwd[0].shape[2]
    ncol = fwd[2].shape[1]
    tm = _row_tile(seqlen, 8 * ncol)
    spec = pl.BlockSpec((None, tm, cw), lambda b, i: (b, i, 0))
    cspec = pl.BlockSpec((None, ncol, cw), lambda b, i: (b, 0, 0))
    return pl.pallas_call(
        functools.partial(_readout_grid_kernel, ncol=ncol),
        grid=(bsz, seqlen // tm),
        in_specs=[spec, spec, spec, cspec, spec, spec, cspec],
        out_specs=spec,
        out_shape=jax.ShapeDtypeStruct((bsz, seqlen, cw), BF16),
        compiler_params=_cparams("arbitrary", "arbitrary"),
    )(p3, *fwd, *bwd)


def _readout_c(p, hf, hb):
    m, cw = hf.shape
    tm = _row_tile(m, 512)
    spec = pl.BlockSpec((tm, cw), lambda i: (i, 0))
    return pl.pallas_call(
        _readout_c_kernel,
        grid=(m // tm,),
        in_specs=[spec, spec, spec],
        out_specs=spec,
        out_shape=jax.ShapeDtypeStruct((m, cw), BF16),
        compiler_params=_cparams("arbitrary"),
    )(p, hf, hb)


def kernel(x, c, ctx, c_ctx, ada_w, ada_b, norm1_w, norm2_w, ffn_w_gate, ffn_w_up, ffn_w_down, final_norm_w, ab_w_in, ab_w_out, m_conv_w, m_conv_b, m_dt_bias, m_a_log, m_d, m_norm_w, r_mu, r_w0, r_w2, r_a0, r_a2, r_kk, r_ka, r_rk, r_g2, r_lnx_w, r_lnx_b, c_w_in, c_w_out, c_conv_w, c_conv_b, c_wa, c_ba, c_wx, c_bx, c_lambda):
    bsz, seqlen, d = x.shape
    ctx_len = ctx.shape[1]
    depth = ada_w.shape[0]
    n_lat, n_ctx = bsz * seqlen, bsz * ctx_len
    row = lambda t: t.reshape(1, -1)

    mod = _adaln(jnp.concatenate([c_ctx[None, :], c], axis=0), ada_w, ada_b)

    xl = x.reshape(n_lat, d)
    xc = ctx.reshape(n_ctx, d)
    mix_a = m_norm_w.shape[1]
    m_heads = m_dt_bias.shape[2]
    r_heads = r_rk.shape[2]
    lora = r_w2.shape[2]
    e64 = jnp.tile(jnp.repeat(jnp.eye(m_heads, dtype=BF16), M_HEADDIM, axis=1), (3, 1))
    hmask = jnp.tile(jnp.repeat(jnp.eye(m_heads, dtype=BF16), M_CHUNK, axis=1), (3, 1))

    for layer in range(depth):
        with_ctx = layer < depth - 1
        part = lambda k: mod[layer, :, k * d:(k + 1) * d]
        lat_vec = lambda k: part(k)[1:1 + bsz, None, :]
        ctx_vec = lambda k: part(k)[0:1, None, :]
        nw1, nw2 = norm1_w[layer], norm2_w[layer]

        if layer % 2 == 0:
            e = layer // 2
            w = ab_w_in[e].astype(BF16)
            o1 = mix_a
            o2 = o1 + mix_a + 2 * M_GROUPS * M_STATE
            o3 = o2 + m_heads
            o4 = o3 + 3 * r_heads * R_HEADSIZE + 2 * lora
            rkv_w = 3 * r_heads * R_HEADSIZE
            zeros = lambda n: jnp.zeros((d, n), w.dtype)
            w_in = jnp.concatenate(
                [w[:, o1:o2], w[:, :o1], w[:, o3:o3 + rkv_w], w[:, o4:],
                 w[:, o3 + rkv_w:o3 + rkv_w + lora], zeros(LOW_AL - LOW_WL - lora),
                 w[:, o3 + rkv_w + lora:o4], zeros(LOW_DT - LOW_AL - lora), w[:, o2:o3],
                 zeros(LOW_W - LOW_DT - m_heads)],
                axis=1)
            width = w_in.shape[1]
            w_out = ab_w_out[e].astype(BF16)
            g2w = r_g2[e].astype(BF16)

            def project(xs, sh, sc, rows_per_vec, seg_len):
                pr = _norm_mm(xs, nw1, sh, sc, [w_in], rows_per_vec, F32)
                pr = pr.reshape(bsz, seg_len, width)
                dt_col = width - LOW_W + LOW_DT
                dt_t = jnp.swapaxes(pr[:, :, dt_col:dt_col + m_heads], 1, 2)
                return pr, dt_t

            pr_l, dtt_l = project(xl, lat_vec(0), lat_vec(1), seqlen, seqlen)
            pr_c, dtt_c = project(xc, ctx_vec(0), ctx_vec(1), n_ctx, ctx_len)

            ym_l = ym_c = yr_l = yr_c = bo_l = bo_c = None
            for dr in range(2):
                rev = dr == 1
                mp = dict(conv_w=m_conv_w[e, dr], conv_b=row(m_conv_b[e, dr]),
                          dtb_row=row(m_dt_bias[e, dr]), dtb_col=m_dt_bias[e, dr].reshape(-1, 1),
                          alog_row=row(m_a_log[e, dr]), alog_col=m_a_log[e, dr].reshape(-1, 1),
                          dskip=row(jnp.repeat(m_d[e, dr], M_HEADDIM)), e64=e64, hmask=hmask)
                h0 = jnp.zeros((bsz, M_GROUPS, m_heads // M_GROUPS * M_HEADDIM, M_STATE), F32)
                y_c, h_c = _ssd_direction(pr_c, dtt_c, h0, mp, rev)
                y_l, _ = _ssd_direction(pr_l, dtt_l, h_c, mp, rev)
                mu = r_mu[e, dr]
                mu_sm = (jnp.zeros((LOW_W,), F32).at[LOW_WL:LOW_WL + lora].set(mu[rkv_w:rkv_w + lora])
                         .at[LOW_AL:LOW_AL + lora].set(mu[rkv_w + lora:]))
                rp = dict(mu_rkv=row(mu[:rkv_w]), mu_sm=row(mu_sm), w0=row(r_w0[e, dr]),
                          w2=r_w2[e, dr].astype(BF16), a0=row(r_a0[e, dr]), a2=r_a2[e, dr].astype(BF16),
                          kk=row(r_kk[e, dr]), ka=row(r_ka[e, dr]), rk=row(r_rk[e, dr]))
                s0 = jnp.zeros((bsz, r_heads // R_GROUP, R_HEADSIZE, R_GROUP * R_HEADSIZE), F32)
                v_c, b_c, s_c = _rwkv_direction(pr_c, s0, rp, rev)
                v_l, b_l, _ = _rwkv_direction(pr_l, s_c, rp, rev)
                if dr == 0:
                    ym_l, ym_c, yr_l, yr_c, bo_l, bo_c = y_l, y_c, v_l, v_c, b_l, b_c
                else:
                    ym_l, yr_l, bo_l = (ym_l, y_l), (yr_l, v_l), (bo_l, b_l)
                    ym_c, yr_c, bo_c = (ym_c, y_c), (yr_c, v_c), (bo_c, b_c)

            ro = dict(norm_w=row(m_norm_w[e]), lnx_w=row(r_lnx_w[e]), lnx_b=row(r_lnx_b[e]), g2=g2w)
            act_l = _readout_ab(pr_l, ym_l, yr_l, bo_l, ro, r_heads)
            xl = _mm_res(act_l, w_out, xl, lat_vec(2), seqlen, tm_cap=2048)
            if with_ctx:
                act_c = _readout_ab(pr_c, ym_c, yr_c, bo_c, ro, r_heads)
                xc = _mm_res(act_c, w_out, xc, ctx_vec(2), n_ctx)
        else:
            o = layer // 2
            w_out = c_w_out[o].astype(BF16)
            cw = w_out.shape[0]
            p_l = _norm_mm(xl, nw1, lat_vec(0), lat_vec(1), [c_w_in], seqlen, F32, layer=o)
            p_c = _norm_mm(xc, nw1, ctx_vec(0), ctx_vec(1), [c_w_in], n_ctx, F32, layer=o)
            p3_l = p_l.reshape(bsz, seqlen, 2 * cw)
            hs_l, hs_c = [], []
            for dr in range(2):
                rev = dr == 1
                cp = dict(conv_w=c_conv_w[o, dr], conv_b=row(c_conv_b[o, dr]),
                          wa=c_wa[o, dr].astype(BF16), ba=row(c_ba[o, dr]),
                          wx=c_wx[o, dr].astype(BF16), bx=row(c_bx[o, dr]), lam=row(c_lambda[o, dr]))
                h0 = jnp.zeros((bsz, HALO, cw), F32)
                h_c, s_c = _rglru_direction(p_c.reshape(bsz, ctx_len, 2 * cw), h0, cp, rev,
                                            _row_tile(ctx_len, 128))
                hs_l.append(_rglru_grid_direction(p3_l, s_c, cp, rev)[:3])
                hs_c.append(h_c.reshape(n_ctx, cw))
            act_l = _readout_grid(p3_l, hs_l[0], hs_l[1]).reshape(n_lat, cw)
            xl = _mm_res(act_l, w_out, xl, lat_vec(2), seqlen, tm_cap=2048)
            if with_ctx:
                act_c = _readout_c(p_c, hs_c[0], hs_c[1])
                xc = _mm_res(act_c, w_out, xc, ctx_vec(2), n_ctx)

        ffn_up = [ffn_w_gate, ffn_w_up]
        wd = ffn_w_down[layer].astype(BF16)
        act = _norm_mm(xl, nw2, lat_vec(3), lat_vec(4), ffn_up, seqlen, BF16, layer=layer)
        xl = _mm_res(act, wd, xl, lat_vec(5), seqlen)
        if with_ctx:
            act = _norm_mm(xc, nw2, ctx_vec(3), ctx_vec(4), ffn_up, n_ctx, BF16, layer=layer)
            xc = _mm_res(act, wd, xc, ctx_vec(5), n_ctx)

    return _final_norm(xl, final_norm_w).reshape(bsz, seqlen, d)
```

```python
import functools
import math

import jax
import jax.numpy as jnp
from jax import lax
from jax.experimental import pallas as pl
from jax.experimental.pallas import tpu as pltpu

F32 = jnp.float32
BF16 = jnp.bfloat16

NORM_EPS = 1e-6
GRID_W = 64
M_HEADDIM = 64
M_GROUPS = 4
M_STATE = 128
M_CONV = 4
M_CHUNK = 128
M_NORM_EPS = 1e-5
R_HEADSIZE = 64
R_CHUNK = 64
R_GROUP = 4
R_SUB = 4
R_LN_EPS = 64e-5
LOW_W = 512
LOW_WL = 256
LOW_AL = 384
LOW_DT = 480
C_HEADS = 8
C_CONV = 4
C_ROWS_PER_STEP = 8
RG_C = 8.0

SCAN_OUT_DTYPE = BF16
VMEM_LIMIT_BYTES = 56 * 1024 * 1024
HALO = 8


def _cparams(*sem):
    return pltpu.CompilerParams(dimension_semantics=sem, vmem_limit_bytes=VMEM_LIMIT_BYTES)


def _bdot(a, b):
    return jnp.dot(a.astype(BF16), b.astype(BF16), preferred_element_type=F32)


def _bdot_nt(a, b):
    return lax.dot_general(a.astype(BF16), b.astype(BF16), (((1,), (1,)), ((), ())),
                           preferred_element_type=F32)


def _split3(x):
    hi = x.astype(BF16)
    r1 = x - hi.astype(F32)
    mid = r1.astype(BF16)
    lo = (r1 - mid.astype(F32)).astype(BF16)
    return hi, mid, lo


def _tri_cumsum(tri_bf16, x, parts=3):
    hi, mid, lo = _split3(x)
    dot = lambda p: jnp.dot(tri_bf16, p, preferred_element_type=F32)
    if parts == 2:
        return dot(hi) + dot(mid)
    return dot(hi) + (dot(mid) + dot(lo))


def _silu(x):
    return x * jax.nn.sigmoid(x)


def _softplus(x):
    return jnp.maximum(x, 0.0) + jnp.log1p(jnp.exp(-jnp.abs(x)))


def _adaln_kernel(cb_ref, w_ref, b_ref, o_ref, *, n_vec, tn):
    rows = []
    for v in range(n_vec):
        cols = []
        for j in range(tn // 128):
            wj = w_ref[:, j * 128:(j + 1) * 128]
            cols.append(jnp.sum(wj * cb_ref[v], axis=0, keepdims=True))
        rows.append(jnp.concatenate(cols, axis=1))
    pad = jnp.zeros((HALO - n_vec, tn), F32)
    o_ref[...] = jnp.concatenate(rows + [pad], axis=0) + b_ref[...]


def _adaln(cond, ada_w, ada_b):
    depth, d, n6 = ada_w.shape
    n_vec = cond.shape[0]
    tn = 1024
    cb = jnp.broadcast_to(_silu(cond)[:, :, None], (n_vec, d, 128))
    return pl.pallas_call(
        functools.partial(_adaln_kernel, n_vec=n_vec, tn=tn),
        grid=(depth, n6 // tn),
        in_specs=[pl.BlockSpec((n_vec, d, 128), lambda l, j: (0, 0, 0)),
                  pl.BlockSpec((None, d, tn), lambda l, j: (l, 0, j)),
                  pl.BlockSpec((None, 1, tn), lambda l, j: (l, 0, j))],
        out_specs=pl.BlockSpec((None, HALO, tn), lambda l, j: (l, 0, j)),
        out_shape=jax.ShapeDtypeStruct((depth, HALO, n6), F32),
        compiler_params=_cparams("arbitrary", "arbitrary"),
    )(cb, ada_w, ada_b.reshape(depth, 1, n6))


def _modulated(x_ref, nw_ref, sh_ref, sc_ref):
    x = x_ref[...]
    y = x * lax.rsqrt(jnp.mean(x * x, axis=-1, keepdims=True) + NORM_EPS)
    return y * (nw_ref[...] * (1.0 + sc_ref[...])) + sh_ref[...]


def _norm_mm_kernel(x_ref, nw_ref, sh_ref, sc_ref, w_ref, o_ref, h_ref):
    @pl.when(pl.program_id(1) == 0)
    def _():
        h_ref[...] = _modulated(x_ref, nw_ref, sh_ref, sc_ref).astype(BF16)

    o_ref[...] = jnp.dot(h_ref[...], w_ref[...].astype(BF16),
                         preferred_element_type=F32).astype(o_ref.dtype)


def _norm_swiglu_kernel(x_ref, nw_ref, sh_ref, sc_ref, wg_ref, wu_ref, o_ref, h_ref):
    @pl.when(pl.program_id(1) == 0)
    def _():
        h_ref[...] = _modulated(x_ref, nw_ref, sh_ref, sc_ref).astype(BF16)

    h = h_ref[...]
    g = jnp.dot(h, wg_ref[...].astype(BF16), preferred_element_type=F32)
    u = jnp.dot(h, wu_ref[...].astype(BF16), preferred_element_type=F32)
    o_ref[...] = (_silu(g) * u).astype(o_ref.dtype)


def _row_tile(m, cap):
    t = cap
    while m % t:
        t //= 2
    return t


def _norm_mm(x, nw, shift, scale, weights, rows_per_vec, out_dtype, tm_cap=1024, tn=512, layer=0):
    m, d = x.shape
    n = weights[0].shape[-1]
    tm = _row_tile(rows_per_vec, tm_cap)
    vec_map = lambda i, j: ((i * tm) // rows_per_vec, 0, 0)
    kern = _norm_mm_kernel if len(weights) == 1 else _norm_swiglu_kernel

    def w_spec(w):
        if w.ndim == 3:
            return pl.BlockSpec((None, d, tn), lambda i, j: (layer, 0, j))
        return pl.BlockSpec((d, tn), lambda i, j: (0, j))
    return pl.pallas_call(
        kern,
        grid=(m // tm, n // tn),
        in_specs=[pl.BlockSpec((tm, d), lambda i, j: (i, 0)),
                  pl.BlockSpec((1, d), lambda i, j: (0, 0)),
                  pl.BlockSpec((None, 1, d), vec_map),
                  pl.BlockSpec((None, 1, d), vec_map)]
                 + [w_spec(w) for w in weights],
        out_specs=pl.BlockSpec((tm, tn), lambda i, j: (i, j)),
        out_shape=jax.ShapeDtypeStruct((m, n), out_dtype),
        scratch_shapes=[pltpu.VMEM((tm, d), BF16)],
        compiler_params=_cparams("arbitrary", "arbitrary"),
    )(x, nw.reshape(1, d), shift, scale, *weights)


def _mm_res_kernel(a_ref, w_ref, res_ref, g_ref, o_ref):
    acc = jnp.dot(a_ref[...], w_ref[...], preferred_element_type=F32)
    o_ref[...] = res_ref[...] + g_ref[...] * acc


def _mm_res(a, w, res, gate, rows_per_vec, tm_cap=1024, tn=512):
    m, k = a.shape
    n = w.shape[1]
    tm = _row_tile(rows_per_vec, tm_cap)
    return pl.pallas_call(
        _mm_res_kernel,
        grid=(m // tm, n // tn),
        in_specs=[pl.BlockSpec((tm, k), lambda i, j: (i, 0)),
                  pl.BlockSpec((k, tn), lambda i, j: (0, j)),
                  pl.BlockSpec((tm, tn), lambda i, j: (i, j)),
                  pl.BlockSpec((None, 1, tn), lambda i, j: ((i * tm) // rows_per_vec, 0, j))],
        out_specs=pl.BlockSpec((tm, tn), lambda i, j: (i, j)),
        out_shape=jax.ShapeDtypeStruct((m, n), F32),
        compiler_params=_cparams("arbitrary", "arbitrary"),
    )(a, w, res, gate)


def _final_norm_kernel(x_ref, w_ref, o_ref):
    x = x_ref[...]
    o_ref[...] = x * lax.rsqrt(jnp.mean(x * x, axis=-1, keepdims=True) + NORM_EPS) * w_ref[...]


def _final_norm(x, w):
    m, d = x.shape
    tm = _row_tile(m, 512)
    return pl.pallas_call(
        _final_norm_kernel,
        grid=(m // tm,),
        in_specs=[pl.BlockSpec((tm, d), lambda i: (i, 0)), pl.BlockSpec((1, d), lambda i: (0, 0))],
        out_specs=pl.BlockSpec((tm, d), lambda i: (i, 0)),
        out_shape=jax.ShapeDtypeStruct((m, d), F32),
        compiler_params=_cparams("arbitrary"),
    )(x, w.reshape(1, d))


def _scan_block(i, n_blocks, rev):
    return (n_blocks - 1 - i) if rev else i


def _halo_block(i, n_blocks, rows, rev):
    per = rows // HALO
    if rev:
        return jnp.minimum((_scan_block(i, n_blocks, True) + 1) * per, n_blocks * per - 1)
    return jnp.maximum(i * per - 1, 0)


def _stage_with_halo(scr_ref, x_ref, halo_ref, keep, rows, rev):
    if rev:
        scr_ref[0:rows, :] = x_ref[...]
        scr_ref[rows:rows + HALO, :] = halo_ref[...] * keep
    else:
        scr_ref[0:HALO, :] = halo_ref[...] * keep
        scr_ref[HALO:HALO + rows, :] = x_ref[...]


def _lagged(scr_ref, lag, rows, rev):
    start = lag if rev else HALO - lag
    return scr_ref[start:start + rows, :]


def _rwkv_kernel(rkv_ref, sm_ref, rkv_h_ref, sm_h_ref, mu_rkv_ref, mu_sm_ref, w0_ref, w2_ref,
                 a0_ref, a2_ref, kkw_ref, kaw_ref, rkw_ref, s0_ref,
                 y_ref, bonus_ref, sl_ref, s_scr, rkv_scr, sm_scr, *, rev, n_heads, lora, nsub):
    i = pl.program_id(1)
    c = R_CHUNK
    rows = nsub * c
    n = R_HEADSIZE
    hw = n_heads * n

    @pl.when(i == 0)
    def _():
        s_scr[...] = s0_ref[...]

    keep = (i > 0).astype(F32)
    _stage_with_halo(rkv_scr, rkv_ref, rkv_h_ref, keep, rows, rev)
    _stage_with_halo(sm_scr, sm_ref, sm_h_ref, keep, rows, rev)
    x = rkv_ref[...]
    f = x + (_lagged(rkv_scr, 1, rows, rev) - x) * mu_rkv_ref[...]
    xs = sm_ref[...]
    fs = xs + (_lagged(sm_scr, 1, rows, rev) - xs) * mu_sm_ref[...]
    r = f[:, :hw]
    k = f[:, hw:2 * hw]
    v = f[:, 2 * hw:3 * hw]
    wl = fs[:, LOW_WL:LOW_WL + lora]
    al = fs[:, LOW_AL:LOW_AL + lora]

    ww = w0_ref[...] + _bdot(jnp.tanh(wl), w2_ref[...])
    lw = -math.exp(-0.5) * jax.nn.sigmoid(ww)
    a = jax.nn.sigmoid(a0_ref[...] + _bdot(al, a2_ref[...]))
    kkr = k * kkw_ref[...]
    k2 = k * (1.0 + (a - 1.0) * kaw_ref[...])
    rk = r * k2 * rkw_ref[...]

    cshift = int(math.log2(c))
    ti = lax.broadcasted_iota(jnp.int32, (rows, rows), 0)
    tj = lax.broadcasted_iota(jnp.int32, (rows, rows), 1)
    same_chunk = (ti >> cshift) == (tj >> cshift)
    upto = same_chunk & ((tj >= ti) if rev else (tj <= ti))
    g = _tri_cumsum(upto.astype(BF16), lw, parts=2)
    rsl = [slice(k * c, (k + 1) * c) for k in range(nsub)]
    gtot = [g[k * c:k * c + 1, :] if rev else g[(k + 1) * c - 1:(k + 1) * c, :] for k in range(nsub)]
    eg = jnp.exp(g)
    eneg = jnp.exp(-g)
    egp = jnp.exp(g - lw)
    etail = jnp.exp(jnp.concatenate([jnp.broadcast_to(t, (c, hw)) for t in gtot], axis=0) - g)
    dg = [jnp.exp(t) for t in gtot]
    rd_all = r * eg
    kp_all = k2 * eneg
    kpp_all = k2 * etail

    gw = R_GROUP * n
    groups = range(hw // gw)
    gsl = [slice(q * gw, (q + 1) * gw) for q in groups]
    shift = int(math.log2(n))
    lane_head = lax.broadcasted_iota(jnp.int32, (1, 128), 1) >> shift
    keep_f32 = [(lane_head == j).astype(F32) for j in range(128 // n)]
    keep_lanes = [t.astype(BF16) for t in keep_f32]

    def bdiag(t):
        tb = t.astype(BF16)
        zero = jnp.zeros((c, 128), BF16)
        rows = []
        for h in range(R_GROUP):
            lt, j = divmod(h * n, 128)
            piece = tb[:, lt * 128:(lt + 1) * 128] * keep_lanes[j // n]
            rows.append(jnp.concatenate([piece if u == lt else zero for u in range(gw // 128)], axis=1))
        return jnp.concatenate(rows, axis=0)

    def diag_blocks(full):
        tiles = []
        for lt in range(gw // 128):
            acc = None
            for j in range(128 // n):
                h = lt * (128 // n) + j
                part = full[h * n:(h + 1) * n, lt * 128:(lt + 1) * 128] * keep_f32[j]
                acc = part if acc is None else acc + part
            tiles.append(acc)
        return jnp.concatenate(tiles, axis=1)

    def head_sums(t):
        tiles = []
        for lt in range(gw // 128):
            x = t[:, lt * 128:(lt + 1) * 128]
            acc = None
            for keep_j in keep_f32:
                part = jnp.sum(x * keep_j, axis=-1, keepdims=True) * keep_j
                acc = part if acc is None else acc + part
            tiles.append(acc)
        return jnp.concatenate(tiles, axis=1)

    tdot = lambda p, q: lax.dot_general(p.astype(BF16), q.astype(BF16), (((0,), (0,)), ((), ())),
                                        preferred_element_type=F32)
    gi = lax.broadcasted_iota(jnp.int32, (c, gw), 0)
    gj = lax.broadcasted_iota(jnp.int32, (c, gw), 1) & (c - 1)
    before_g = (gj > gi) if rev else (gj < gi)
    upto_g = (gj >= gi) if rev else (gj <= gi)
    eye_g = (gi == gj).astype(F32)

    kk_all = jnp.concatenate(
        [kkr[:, s] * lax.rsqrt(jnp.maximum(head_sums(kkr[:, s] * kkr[:, s]), 1e-24)) for s in gsl], axis=1)
    bh_all = kk_all * a
    kkd_all = kk_all * egp
    bp_all = bh_all * eneg
    bpp_all = bh_all * etail
    bonus_ref[...] = (jnp.concatenate([head_sums(rk[:, s]) for s in gsl], axis=1) * v
                      ).astype(bonus_ref.dtype)

    units = [(k, q) for k in range(nsub) for q in groups]
    un = range(len(units))
    cut = lambda t: [t[rsl[k], gsl[q]] for k, q in units]
    kkd, bp, bpp, rd, vq, kp, kpp = (cut(t) for t in (kkd_all, bp_all, bpp_all, rd_all, v, kp_all, kpp_all))
    amat = [_bdot_nt(jnp.concatenate([kkd[u], rd[u]], axis=0),
                     jnp.concatenate([bdiag(kp[u]), bdiag(bp[u])], axis=0)) for u in un]
    mk = [jnp.where(before_g, m[:c, :gw], 0.0) for m in amat]
    nk = [jnp.where(upto_g, m[c:, :gw], 0.0) for m in amat]
    nb = [jnp.where(upto_g, m[c:, gw:], 0.0) for m in amat]
    p2 = [jnp.where(before_g, -m[:c, gw:], 0.0) for m in amat]
    tinv = [eye_g + t for t in p2]
    p2 = [_bdot(t, bdiag(t)) for t in p2]
    for _ in range(cshift - 2):
        sq = [_bdot(jnp.concatenate([tinv[u], p2[u]], axis=0), bdiag(p2[u])) for u in un]
        tinv = [tinv[u] + sq[u][:c] for u in un]
        p2 = [t[c:] for t in sq]
    tinv = [tinv[u] + _bdot(tinv[u], bdiag(p2[u])) for u in un]
    mnv = [_bdot(jnp.concatenate([mk[u], nk[u]], axis=0), bdiag(vq[u])) for u in un]
    wu = [_bdot(tinv[u], jnp.concatenate([bdiag(kkd[u]), bdiag(mnv[u][:c])], axis=1))
          for u in un]
    nwu = [_bdot(nb[u], jnp.concatenate([bdiag(wu[u][:, :gw]), bdiag(wu[u][:, gw:])], axis=1))
           for u in un]
    wub = [tdot(wu[u], bpp[u]) for u in un]
    pmat = [bdiag(diag_blocks(t[:gw])) for t in wub]
    dmat = [diag_blocks(tdot(vq[u], kpp[u])) - diag_blocks(wub[u][gw:]) for u in un]
    qt = [rd[u] - nwu[u][:, :gw] for u in un]
    y0 = [mnv[u][c:] - nwu[u][:, gw:] for u in un]
    state = [s_scr[q] for q in groups]
    for k in (reversed(range(nsub)) if rev else range(nsub)):
        for q in groups:
            u = k * len(groups) + q
            y_ref[rsl[k], gsl[q]] = (y0[u] + _bdot_nt(qt[u], bdiag(state[q]))).astype(y_ref.dtype)
            state[q] = state[q] * dg[k][:, gsl[q]] - _bdot(state[q], pmat[u]) + dmat[u]
    for q in groups:
        s_scr[q] = state[q]

    @pl.when(i == pl.num_programs(1) - 1)
    def _():
        sl_ref[...] = s_scr[...]


def _rwkv_direction(pr, s0, p, rev):
    bsz, seqlen, width = pr.shape
    hw = s0.shape[1] * s0.shape[3]
    n_heads = hw // R_HEADSIZE
    nsub = R_SUB if seqlen % (R_SUB * R_CHUNK) == 0 else 1
    c = nsub * R_CHUNK
    nblk = seqlen // c
    sm_col = width // LOW_W - 1
    blk = lambda i: _scan_block(i, nblk, rev)
    halo = lambda i: _halo_block(i, nblk, c, rev)
    vec = lambda w: pl.BlockSpec((1, w), lambda b, i: (0, 0))
    lora = p['w2'].shape[0]
    state_spec = pl.BlockSpec((None,) + s0.shape[1:], lambda b, i: (b, 0, 0, 0))
    return pl.pallas_call(
        functools.partial(_rwkv_kernel, rev=rev, n_heads=n_heads, lora=lora, nsub=nsub),
        grid=(bsz, nblk),
        in_specs=[pl.BlockSpec((None, c, 3 * hw), lambda b, i: (b, blk(i), 1)),
                  pl.BlockSpec((None, c, LOW_W), lambda b, i: (b, blk(i), sm_col)),
                  pl.BlockSpec((None, HALO, 3 * hw), lambda b, i: (b, halo(i), 1)),
                  pl.BlockSpec((None, HALO, LOW_W), lambda b, i: (b, halo(i), sm_col)),
                  vec(3 * hw), vec(LOW_W), vec(hw),
                  pl.BlockSpec((lora, hw), lambda b, i: (0, 0)),
                  vec(hw),
                  pl.BlockSpec((lora, hw), lambda b, i: (0, 0)),
                  vec(hw), vec(hw), vec(hw),
                  state_spec],
        out_specs=[pl.BlockSpec((None, c, hw), lambda b, i: (b, blk(i), 0)),
                   pl.BlockSpec((None, c, hw), lambda b, i: (b, blk(i), 0)),
                   state_spec],
        out_shape=[jax.ShapeDtypeStruct((bsz, seqlen, hw), SCAN_OUT_DTYPE),
                   jax.ShapeDtypeStruct((bsz, seqlen, hw), SCAN_OUT_DTYPE),
                   jax.ShapeDtypeStruct(s0.shape, F32)],
        scratch_shapes=[pltpu.VMEM(s0.shape[1:], F32),
                        pltpu.VMEM((c + HALO, 3 * hw), F32),
                        pltpu.VMEM((c + HALO, LOW_W), F32)],
        compiler_params=_cparams("arbitrary", "arbitrary"),
    )(pr, pr, pr, pr, p['mu_rkv'], p['mu_sm'], p['w0'], p['w2'], p['a0'], p['a2'],
      p['kk'], p['ka'], p['rk'], s0)


def _expand_cols(x, e_bf16):
    hi, mid, lo = _split3(x)
    dot = lambda p: jnp.dot(p, e_bf16, preferred_element_type=F32)
    return dot(hi) + (dot(mid) + dot(lo))


def _causal_conv(scr_ref, w_ref, b_ref, rows, taps, rev):
    acc = b_ref[...] + w_ref[taps - 1:taps, :] * _lagged(scr_ref, 0, rows, rev)
    for kk in range(taps - 1):
        acc = acc + w_ref[kk:kk + 1, :] * _lagged(scr_ref, taps - 1 - kk, rows, rev)
    return acc


def _ssd_kernel(xbc_ref, halo_ref, sm_ref, dtt_ref, cw_ref, cb_ref, dtb_row_ref, dtb_col_ref,
                alog_row_ref, alog_col_ref, dskip_ref, e64_ref, hmask_ref, h0_ref,
                y_ref, hl_ref, h_scr, x_scr, *, rev, n_heads):
    i = pl.program_id(1)
    c = M_CHUNK
    p = M_HEADDIM
    ns = M_STATE
    mix = n_heads * p
    hpg = n_heads // M_GROUPS
    gw = hpg * p

    @pl.when(i == 0)
    def _():
        h_scr[...] = h0_ref[...]

    keep = (i > 0).astype(F32)
    _stage_with_halo(x_scr, xbc_ref, halo_ref, keep, c, rev)
    xbc = _silu(_causal_conv(x_scr, cw_ref, cb_ref, c, M_CONV, rev))
    xs = xbc[:, :mix]
    bm = [xbc[:, mix + g * ns:mix + (g + 1) * ns] for g in range(M_GROUPS)]
    cm = [xbc[:, mix + (M_GROUPS + g) * ns:mix + (M_GROUPS + g + 1) * ns] for g in range(M_GROUPS)]

    ti = lax.broadcasted_iota(jnp.int32, (c, c), 0)
    tj = lax.broadcasted_iota(jnp.int32, (c, c), 1)
    upto = (tj >= ti) if rev else (tj <= ti)
    upto_t = (ti >= tj) if rev else (ti <= tj)
    end = 0 if rev else c - 1

    dt = _softplus(sm_ref[:, LOW_DT:LOW_DT + n_heads] + dtb_row_ref[...])
    a_cs = _tri_cumsum(upto.astype(BF16), dt * -jnp.exp(alog_row_ref[...]))
    dt_t = _softplus(dtt_ref[...] + dtb_col_ref[...])
    a_cs_t = _expand_cols(dt_t * -jnp.exp(alog_col_ref[...]), upto_t.astype(BF16))
    a_end_t = a_cs_t[:, end:end + 1]
    w_t = dt_t * jnp.exp(a_end_t - a_cs_t)
    grow_t = jnp.broadcast_to(jnp.exp(a_end_t), (n_heads, ns))

    parts = [t.astype(F32) for t in _split3(a_cs)]
    ones = jnp.ones((c, n_heads), F32)
    lhs = jnp.concatenate(parts + [ones] * 3, axis=1).astype(BF16)
    hmask = hmask_ref[...]
    parts_t = jnp.concatenate(list(_split3(a_cs_t)), axis=0)
    rhs = jnp.concatenate([hmask, -jnp.concatenate([parts_t] * n_heads, axis=1) * hmask], axis=0)
    seg = jnp.dot(lhs, rhs, preferred_element_type=F32)
    ea_full = jnp.exp(jnp.dot(lhs[:, :3 * n_heads], e64_ref[...], preferred_element_type=F32))

    lane_head = lax.broadcasted_iota(jnp.int32, (1, gw), 1) >> int(math.log2(p))
    keep_lanes = [(lane_head == j).astype(BF16) for j in range(hpg)]
    xs_t = jnp.transpose(xs)
    y_diag, y_off = [], []
    for g in range(M_GROUPS):
        heads = range(g * hpg, (g + 1) * hpg)
        cb = _bdot_nt(cm[g], bm[g])
        scores = []
        for h in heads:
            decay = jnp.where(upto, jnp.exp(jnp.minimum(seg[:, h * c:(h + 1) * c], 0.0)), 0.0)
            scores.append((cb * decay * dt_t[h:h + 1, :]).astype(BF16))
        xg = xs[:, g * gw:(g + 1) * gw].astype(BF16)
        x_bd = jnp.concatenate([xg * keep_lanes[j] for j in range(hpg)], axis=0)
        y_diag.append(jnp.dot(jnp.concatenate(scores, axis=1), x_bd, preferred_element_type=F32))
        h_in = h_scr[g]
        y_off.append(_bdot_nt(cm[g], h_in))
        wrows = jnp.concatenate([jnp.broadcast_to(w_t[h:h + 1, :], (p, c)) for h in heads], axis=0)
        grow = jnp.concatenate([jnp.broadcast_to(grow_t[h:h + 1, :], (p, ns)) for h in heads], axis=0)
        h_scr[g] = h_in * grow + _bdot(xs_t[g * gw:(g + 1) * gw, :] * wrows, bm[g])
    y_ref[...] = (jnp.concatenate(y_diag, axis=1) + jnp.concatenate(y_off, axis=1) * ea_full
                  + xs * dskip_ref[...]).astype(y_ref.dtype)

    @pl.when(i == pl.num_programs(1) - 1)
    def _():
        hl_ref[...] = h_scr[...]


def _ssd_direction(pr, dt_t, h0, p, rev):
    bsz, seqlen, width = pr.shape
    n_heads = dt_t.shape[1]
    mix = n_heads * M_HEADDIM
    xw = mix + 2 * M_GROUPS * M_STATE
    c = M_CHUNK
    nblk = seqlen // c
    sm_col = width // LOW_W - 1
    blk = lambda i: _scan_block(i, nblk, rev)
    halo = lambda i: _halo_block(i, nblk, c, rev)
    full = lambda a: pl.BlockSpec(a.shape, lambda b, i: (0,) * a.ndim)
    consts = [p['conv_w'], p['conv_b'], p['dtb_row'], p['dtb_col'], p['alog_row'], p['alog_col'],
              p['dskip'], p['e64'], p['hmask']]
    state_spec = pl.BlockSpec((None,) + h0.shape[1:], lambda b, i: (b, 0, 0, 0))
    return pl.pallas_call(
        functools.partial(_ssd_kernel, rev=rev, n_heads=n_heads),
        grid=(bsz, nblk),
        in_specs=[pl.BlockSpec((None, c, xw), lambda b, i: (b, blk(i), 0)),
                  pl.BlockSpec((None, HALO, xw), lambda b, i: (b, halo(i), 0)),
                  pl.BlockSpec((None, c, LOW_W), lambda b, i: (b, blk(i), sm_col)),
                  pl.BlockSpec((None, n_heads, c), lambda b, i: (b, 0, blk(i)))]
                 + [full(a) for a in consts] + [state_spec],
        out_specs=[pl.BlockSpec((None, c, mix), lambda b, i: (b, blk(i), 0)), state_spec],
        out_shape=[jax.ShapeDtypeStruct((bsz, seqlen, mix), SCAN_OUT_DTYPE),
                   jax.ShapeDtypeStruct(h0.shape, F32)],
        scratch_shapes=[pltpu.VMEM(h0.shape[1:], F32), pltpu.VMEM((c + HALO, xw), F32)],
        compiler_params=_cparams("arbitrary", "arbitrary"),
    )(pr, pr, pr, dt_t, *consts, h0)


def _rglru_kernel(x_ref, halo_ref, cw_ref, cb_ref, wa_ref, ba_ref, wx_ref, bx_ref, lam_ref, h0_ref,
                  o_ref, hl_ref, h_scr, x_scr, a_scr, u_scr, *, rev, tb):
    i = pl.program_id(1)

    @pl.when(i == 0)
    def _():
        h_scr[...] = h0_ref[...]

    keep = (i > 0).astype(F32)
    _stage_with_halo(x_scr, x_ref, halo_ref, keep, tb, rev)
    xc = _causal_conv(x_scr, cw_ref, cb_ref, tb, C_CONV, rev)
    a_scr[...], u_scr[...] = _rglru_gates(xc, wa_ref, ba_ref, wx_ref, bx_ref, lam_ref)

    def step(t, h):
        tt = (tb - 1 - t) if rev else t
        h = a_scr[pl.ds(tt, 1), :] * h + u_scr[pl.ds(tt, 1), :]
        o_ref[pl.ds(tt, 1), :] = h
        return h

    h_scr[0:1, :] = lax.fori_loop(0, tb, step, h_scr[0:1, :], unroll=8)

    @pl.when(i == pl.num_programs(1) - 1)
    def _():
        hl_ref[...] = h_scr[...]


def _rglru_direction(xv, h0, p, rev, tb):
    bsz, rows, wtot = xv.shape
    width = h0.shape[2]
    ncol = wtot // (2 * width)
    nrb = rows // tb
    total = ncol * nrb
    per = tb // HALO

    def pos(i):
        s = _scan_block(i, total, rev)
        return s // nrb, s % nrb

    def x_map(b, i):
        col, q = pos(i)
        return b, q, 2 * col + 1

    def halo_map(b, i):
        s = _scan_block(i, total, rev)
        sp = jnp.minimum(s + 1, total - 1) if rev else jnp.maximum(s - 1, 0)
        col, q = sp // nrb, sp % nrb
        return b, (q * per if rev else (q + 1) * per - 1), 2 * col + 1

    def o_map(b, i):
        col, q = pos(i)
        return b, q, col

    full = lambda a: pl.BlockSpec(a.shape, lambda b, i: (0,) * a.ndim)
    consts = [p['conv_w'], p['conv_b'], p['wa'], p['ba'], p['wx'], p['bx'], p['lam']]
    state_spec = pl.BlockSpec((None, HALO, width), lambda b, i: (b, 0, 0))
    return pl.pallas_call(
        functools.partial(_rglru_kernel, rev=rev, tb=tb),
        grid=(bsz, total),
        in_specs=[pl.BlockSpec((None, tb, width), x_map), pl.BlockSpec((None, HALO, width), halo_map)]
                 + [full(a) for a in consts] + [state_spec],
        out_specs=[pl.BlockSpec((None, tb, width), o_map), state_spec],
        out_shape=[jax.ShapeDtypeStruct((bsz, rows, ncol * width), F32),
                   jax.ShapeDtypeStruct(h0.shape, F32)],
        scratch_shapes=[pltpu.VMEM((HALO, width), F32), pltpu.VMEM((tb + HALO, width), F32),
                        pltpu.VMEM((tb, width), F32), pltpu.VMEM((tb, width), F32)],
        compiler_params=_cparams("arbitrary", "arbitrary"),
    )(xv, xv, *consts, h0)


def _rglru_gates(xc, wa_ref, ba_ref, wx_ref, bx_ref, lam_ref):
    blk = xc.shape[1] // C_HEADS
    xh = [xc[:, h * blk:(h + 1) * blk] for h in range(C_HEADS)]
    gate_r = jax.nn.sigmoid(
        jnp.concatenate([_bdot(xh[h], wa_ref[h]) for h in range(C_HEADS)], axis=1) + ba_ref[...])
    gate_i = jax.nn.sigmoid(
        jnp.concatenate([_bdot(xh[h], wx_ref[h]) for h in range(C_HEADS)], axis=1) + bx_ref[...])
    log_a = -RG_C * gate_r * _softplus(-lam_ref[...])
    a = jnp.exp(log_a)
    one_m_a2 = jnp.maximum(-jnp.tanh(log_a) * (a * a + 1.0), 0.0)
    mult = jnp.where(one_m_a2 > 0.0, one_m_a2 * lax.rsqrt(one_m_a2), 0.0)
    return a, mult * gate_i * xc


def _rglru_grid_kernel(x_ref, e0_ref, e1_ref, e2_ref, cw_ref, cb_ref, wa_ref, ba_ref, wx_ref,
                       bx_ref, lam_ref, h0_ref, hloc_ref, aprod_ref, cin_ref, hl_ref,
                       ring_scr, h_scr, ap_scr, stage_scr, *, rev, ncol):
    i = pl.program_id(1)

    def from_previous_column(e_ref):
        _stage_with_halo(stage_scr, e_ref, jnp.zeros((HALO, e_ref.shape[1]), F32), 0.0, ncol, rev)
        return _lagged(stage_scr, 1, ncol, rev)

    @pl.when(i == 0)
    def _():
        ring_scr[0] = from_previous_column(e2_ref)
        ring_scr[1] = from_previous_column(e1_ref)
        ring_scr[2] = from_previous_column(e0_ref)
        h_scr[...] = jnp.zeros(h_scr.shape, F32)
        ap_scr[...] = jnp.ones(ap_scr.shape, F32)

    nrow = x_ref.shape[0] // ncol
    order = list(reversed(range(nrow))) if rev else list(range(nrow))
    rsl = [slice(m * ncol, (m + 1) * ncol) for m in range(nrow)]
    hist = [ring_scr[2], ring_scr[1], ring_scr[0]] + [x_ref[rsl[m], :] for m in order]
    xc = [cb_ref[...] + cw_ref[3:4, :] * hist[t + 3] + cw_ref[2:3, :] * hist[t + 2]
          + cw_ref[1:2, :] * hist[t + 1] + cw_ref[0:1, :] * hist[t] for t in range(nrow)]
    for lag in range(3):
        ring_scr[lag] = hist[-1 - lag]
    a, u = _rglru_gates(jnp.concatenate(xc, axis=0), wa_ref, ba_ref, wx_ref, bx_ref, lam_ref)
    h, ap = h_scr[...], ap_scr[...]
    for t, m in enumerate(order):
        ts = slice(t * ncol, (t + 1) * ncol)
        h = a[ts, :] * h + u[ts, :]
        ap = a[ts, :] * ap
        hloc_ref[rsl[m], :] = h.astype(hloc_ref.dtype)
        aprod_ref[rsl[m], :] = ap.astype(aprod_ref.dtype)
    h_scr[...] = h
    ap_scr[...] = ap

    @pl.when(i == pl.num_programs(1) - 1)
    def _():
        def step(t, carry):
            col = (ncol - 1 - t) if rev else t
            cin_ref[pl.ds(col, 1), :] = carry
            return h_scr[pl.ds(col, 1), :] + ap_scr[pl.ds(col, 1), :] * carry

        last = lax.fori_loop(0, ncol, step, h0_ref[0:1, :], unroll=8)
        hl_ref[...] = jnp.zeros(hl_ref.shape, F32)
        hl_ref[0:1, :] = last


def _rglru_grid_direction(p3, h0, p, rev):
    bsz, seqlen, _ = p3.shape
    width = h0.shape[2]
    ncol = GRID_W
    rows = seqlen // ncol
    nrow = C_ROWS_PER_STEP if rows % C_ROWS_PER_STEP == 0 else 1
    nblk = rows // nrow
    blk = lambda i: _scan_block(i, nblk, rev)
    edge = lambda k: pl.BlockSpec((None, ncol, width),
                                  lambda b, i: (b, (2 - k) if rev else (rows - 3 + k), 1))
    full = lambda a: pl.BlockSpec(a.shape, lambda b, i: (0,) * a.ndim)
    consts = [p['conv_w'], p['conv_b'], p['wa'], p['ba'], p['wx'], p['bx'], p['lam']]
    state_spec = pl.BlockSpec((None, HALO, width), lambda b, i: (b, 0, 0))
    row_spec = pl.BlockSpec((None, nrow * ncol, width), lambda b, i: (b, blk(i), 0))
    return pl.pallas_call(
        functools.partial(_rglru_grid_kernel, rev=rev, ncol=ncol),
        grid=(bsz, nblk),
        in_specs=[pl.BlockSpec((None, nrow * ncol, width), lambda b, i: (b, blk(i), 1)),
                  edge(0), edge(1), edge(2)] + [full(a) for a in consts] + [state_spec],
        out_specs=[row_spec, row_spec,
                   pl.BlockSpec((None, ncol, width), lambda b, i: (b, 0, 0)), state_spec],
        out_shape=[jax.ShapeDtypeStruct((bsz, seqlen, width), SCAN_OUT_DTYPE),
                   jax.ShapeDtypeStruct((bsz, seqlen, width), SCAN_OUT_DTYPE),
                   jax.ShapeDtypeStruct((bsz, ncol, width), F32),
                   jax.ShapeDtypeStruct(h0.shape, F32)],
        scratch_shapes=[pltpu.VMEM((3, ncol, width), F32), pltpu.VMEM((ncol, width), F32),
                        pltpu.VMEM((ncol, width), F32), pltpu.VMEM((ncol + HALO, width), F32)],
        compiler_params=_cparams("arbitrary", "arbitrary"),
    )(p3, p3, p3, p3, *consts, h0)


def _readout_ab_kernel(z_ref, sm_ref, ymf_ref, ymb_ref, yrf_ref, yrb_ref, bof_ref, bob_ref,
                       nw_ref, lw_ref, lb_ref, g2_ref, o_ref, *, n_heads):
    f32 = lambda ref: ref[...].astype(F32)
    t = (f32(ymf_ref) + f32(ymb_ref)) * _silu(z_ref[...])
    gw = t.shape[1] // M_GROUPS
    a_out = []
    for g in range(M_GROUPS):
        tg = t[:, g * gw:(g + 1) * gw]
        a_out.append(tg * lax.rsqrt(jnp.mean(tg * tg, axis=-1, keepdims=True) + M_NORM_EPS))
    a_out = jnp.concatenate(a_out, axis=1) * nw_ref[...]

    yr = f32(yrf_ref) + f32(yrb_ref)
    n = R_HEADSIZE
    yn = []
    for h in range(n_heads):
        yh = yr[:, h * n:(h + 1) * n]
        dlt = yh - jnp.mean(yh, axis=-1, keepdims=True)
        yn.append(dlt * lax.rsqrt(jnp.mean(dlt * dlt, axis=-1, keepdims=True) + R_LN_EPS))
    yn = jnp.concatenate(yn, axis=1) * lw_ref[...] + lb_ref[...]
    gate = _bdot(jax.nn.sigmoid(sm_ref[:, 0:g2_ref.shape[0]]), g2_ref[...])
    b_out = (yn + f32(bof_ref) + f32(bob_ref)) * gate
    o_ref[...] = jnp.concatenate([a_out, b_out], axis=1).astype(o_ref.dtype)


def _readout_ab(pr, ym, yr, bo, ro, n_heads):
    bsz, seqlen, width = pr.shape
    m = bsz * seqlen
    mix = ym[0].shape[2]
    tm = _row_tile(m, 512)
    flat = lambda t: t.reshape(m, t.shape[2])
    rowblk = lambda w, col: pl.BlockSpec((tm, w), lambda i: (i, col))
    full = lambda a: pl.BlockSpec(a.shape, lambda i: (0,) * a.ndim)
    consts = [ro['norm_w'], ro['lnx_w'], ro['lnx_b'], ro['g2']]
    return pl.pallas_call(
        functools.partial(_readout_ab_kernel, n_heads=n_heads),
        grid=(m // tm,),
        in_specs=[rowblk(mix, 2), rowblk(LOW_W, width // LOW_W - 1)] + [rowblk(mix, 0)] * 6
                 + [full(a) for a in consts],
        out_specs=rowblk(2 * mix, 0),
        out_shape=jax.ShapeDtypeStruct((m, 2 * mix), BF16),
        compiler_params=_cparams("arbitrary"),
    )(flat(pr), flat(pr), flat(ym[0]), flat(ym[1]), flat(yr[0]), flat(yr[1]), flat(bo[0]),
      flat(bo[1]), *consts)


def _gelu_tanh(g):
    return 0.5 * g * (1.0 + jnp.tanh(math.sqrt(2.0 / math.pi) * (g + 0.044715 * (g * g * g))))


def _readout_c_kernel(gy_ref, hf_ref, hb_ref, o_ref):
    o_ref[...] = ((hf_ref[...] + hb_ref[...]) * _gelu_tanh(gy_ref[...])).astype(o_ref.dtype)


def _readout_grid_kernel(gy_ref, hf_ref, af_ref, cf_ref, hb_ref, ab_ref, cb_ref, o_ref, *, ncol):
    for j in range(o_ref.shape[0] // ncol):
        rs = slice(j * ncol, (j + 1) * ncol)
        f32 = lambda ref: ref[rs, :].astype(F32)
        h = (f32(hf_ref) + f32(af_ref) * cf_ref[...]) + (f32(hb_ref) + f32(ab_ref) * cb_ref[...])
        o_ref[rs, :] = (h * _gelu_tanh(gy_ref[rs, :])).astype(o_ref.dtype)


def _readout_grid(p3, fwd, bwd):
    bsz, seqlen, _ = p3.shape
    cw = fwd[0].shape[2]
    ncol = fwd[2].shape[1]
    tm = _row_tile(seqlen, 8 * ncol)
    spec = pl.BlockSpec((None, tm, cw), lambda b, i: (b, i, 0))
    cspec = pl.BlockSpec((None, ncol, cw), lambda b, i: (b, 0, 0))
    return pl.pallas_call(
        functools.partial(_readout_grid_kernel, ncol=ncol),
        grid=(bsz, seqlen // tm),
        in_specs=[spec, spec, spec, cspec, spec, spec, cspec],
        out_specs=spec,
        out_shape=jax.ShapeDtypeStruct((bsz, seqlen, cw), BF16),
        compiler_params=_cparams("arbitrary", "arbitrary"),
    )(p3, *fwd, *bwd)


def _readout_c(p, hf, hb):
    m, cw = hf.shape
    tm = _row_tile(m, 512)
    spec = pl.BlockSpec((tm, cw), lambda i: (i, 0))
    return pl.pallas_call(
        _readout_c_kernel,
        grid=(m // tm,),
        in_specs=[spec, spec, spec],
        out_specs=spec,
        out_shape=jax.ShapeDtypeStruct((m, cw), BF16),
        compiler_params=_cparams("arbitrary"),
    )(p, hf, hb)


def kernel(x, c, ctx, c_ctx, ada_w, ada_b, norm1_w, norm2_w, ffn_w_gate, ffn_w_up, ffn_w_down, final_norm_w, ab_w_in, ab_w_out, m_conv_w, m_conv_b, m_dt_bias, m_a_log, m_d, m_norm_w, r_mu, r_w0, r_w2, r_a0, r_a2, r_kk, r_ka, r_rk, r_g2, r_lnx_w, r_lnx_b, c_w_in, c_w_out, c_conv_w, c_conv_b, c_wa, c_ba, c_wx, c_bx, c_lambda):
    bsz, seqlen, d = x.shape
    ctx_len = ctx.shape[1]
    depth = ada_w.shape[0]
    n_lat, n_ctx = bsz * seqlen, bsz * ctx_len
    row = lambda t: t.reshape(1, -1)

    mod = _adaln(jnp.concatenate([c_ctx[None, :], c], axis=0), ada_w, ada_b)

    xl = x.reshape(n_lat, d)
    xc = ctx.reshape(n_ctx, d)
    mix_a = m_norm_w.shape[1]
    m_heads = m_dt_bias.shape[2]
    r_heads = r_rk.shape[2]
    lora = r_w2.shape[2]
    e64 = jnp.tile(jnp.repeat(jnp.eye(m_heads, dtype=BF16), M_HEADDIM, axis=1), (3, 1))
    hmask = jnp.tile(jnp.repeat(jnp.eye(m_heads, dtype=BF16), M_CHUNK, axis=1), (3, 1))

    for layer in range(depth):
        with_ctx = layer < depth - 1
        part = lambda k: mod[layer, :, k * d:(k + 1) * d]
        lat_vec = lambda k: part(k)[1:1 + bsz, None, :]
        ctx_vec = lambda k: part(k)[0:1, None, :]
        nw1, nw2 = norm1_w[layer], norm2_w[layer]

        if layer % 2 == 0:
            e = layer // 2
            w = ab_w_in[e].astype(BF16)
            o1 = mix_a
            o2 = o1 + mix_a + 2 * M_GROUPS * M_STATE
            o3 = o2 + m_heads
            o4 = o3 + 3 * r_heads * R_HEADSIZE + 2 * lora
            rkv_w = 3 * r_heads * R_HEADSIZE
            zeros = lambda n: jnp.zeros((d, n), w.dtype)
            w_in = jnp.concatenate(
                [w[:, o1:o2], w[:, :o1], w[:, o3:o3 + rkv_w], w[:, o4:],
                 w[:, o3 + rkv_w:o3 + rkv_w + lora], zeros(LOW_AL - LOW_WL - lora),
                 w[:, o3 + rkv_w + lora:o4], zeros(LOW_DT - LOW_AL - lora), w[:, o2:o3],
                 zeros(LOW_W - LOW_DT - m_heads)],
                axis=1)
            width = w_in.shape[1]
            w_out = ab_w_out[e].astype(BF16)
            g2w = r_g2[e].astype(BF16)

            def project(xs, sh, sc, rows_per_vec, seg_len):
                pr = _norm_mm(xs, nw1, sh, sc, [w_in], rows_per_vec, F32)
                pr = pr.reshape(bsz, seg_len, width)
                dt_col = width - LOW_W + LOW_DT
                dt_t = jnp.swapaxes(pr[:, :, dt_col:dt_col + m_heads], 1, 2)
                return pr, dt_t

            pr_l, dtt_l = project(xl, lat_vec(0), lat_vec(1), seqlen, seqlen)
            pr_c, dtt_c = project(xc, ctx_vec(0), ctx_vec(1), n_ctx, ctx_len)

            ym_l = ym_c = yr_l = yr_c = bo_l = bo_c = None
            for dr in range(2):
                rev = dr == 1
                mp = dict(conv_w=m_conv_w[e, dr], conv_b=row(m_conv_b[e, dr]),
                          dtb_row=row(m_dt_bias[e, dr]), dtb_col=m_dt_bias[e, dr].reshape(-1, 1),
                          alog_row=row(m_a_log[e, dr]), alog_col=m_a_log[e, dr].reshape(-1, 1),
                          dskip=row(jnp.repeat(m_d[e, dr], M_HEADDIM)), e64=e64, hmask=hmask)
                h0 = jnp.zeros((bsz, M_GROUPS, m_heads // M_GROUPS * M_HEADDIM, M_STATE), F32)
                y_c, h_c = _ssd_direction(pr_c, dtt_c, h0, mp, rev)
                y_l, _ = _ssd_direction(pr_l, dtt_l, h_c, mp, rev)
                mu = r_mu[e, dr]
                mu_sm = (jnp.zeros((LOW_W,), F32).at[LOW_WL:LOW_WL + lora].set(mu[rkv_w:rkv_w + lora])
                         .at[LOW_AL:LOW_AL + lora].set(mu[rkv_w + lora:]))
                rp = dict(mu_rkv=row(mu[:rkv_w]), mu_sm=row(mu_sm), w0=row(r_w0[e, dr]),
                          w2=r_w2[e, dr].astype(BF16), a0=row(r_a0[e, dr]), a2=r_a2[e, dr].astype(BF16),
                          kk=row(r_kk[e, dr]), ka=row(r_ka[e, dr]), rk=row(r_rk[e, dr]))
                s0 = jnp.zeros((bsz, r_heads // R_GROUP, R_HEADSIZE, R_GROUP * R_HEADSIZE), F32)
                v_c, b_c, s_c = _rwkv_direction(pr_c, s0, rp, rev)
                v_l, b_l, _ = _rwkv_direction(pr_l, s_c, rp, rev)
                if dr == 0:
                    ym_l, ym_c, yr_l, yr_c, bo_l, bo_c = y_l, y_c, v_l, v_c, b_l, b_c
                else:
                    ym_l, yr_l, bo_l = (ym_l, y_l), (yr_l, v_l), (bo_l, b_l)
                    ym_c, yr_c, bo_c = (ym_c, y_c), (yr_c, v_c), (bo_c, b_c)

            ro = dict(norm_w=row(m_norm_w[e]), lnx_w=row(r_lnx_w[e]), lnx_b=row(r_lnx_b[e]), g2=g2w)
            act_l = _readout_ab(pr_l, ym_l, yr_l, bo_l, ro, r_heads)
            xl = _mm_res(act_l, w_out, xl, lat_vec(2), seqlen, tm_cap=2048)
            if with_ctx:
                act_c = _readout_ab(pr_c, ym_c, yr_c, bo_c, ro, r_heads)
                xc = _mm_res(act_c, w_out, xc, ctx_vec(2), n_ctx)
        else:
            o = layer // 2
            w_out = c_w_out[o].astype(BF16)
            cw = w_out.shape[0]
            p_l = _norm_mm(xl, nw1, lat_vec(0), lat_vec(1), [c_w_in], seqlen, F32, layer=o)
            p_c = _norm_mm(xc, nw1, ctx_vec(0), ctx_vec(1), [c_w_in], n_ctx, F32, layer=o)
            p3_l = p_l.reshape(bsz, seqlen, 2 * cw)
            hs_l, hs_c = [], []
            for dr in range(2):
                rev = dr == 1
                cp = dict(conv_w=c_conv_w[o, dr], conv_b=row(c_conv_b[o, dr]),
                          wa=c_wa[o, dr].astype(BF16), ba=row(c_ba[o, dr]),
                          wx=c_wx[o, dr].astype(BF16), bx=row(c_bx[o, dr]), lam=row(c_lambda[o, dr]))
                h0 = jnp.zeros((bsz, HALO, cw), F32)
                h_c, s_c = _rglru_direction(p_c.reshape(bsz, ctx_len, 2 * cw), h0, cp, rev,
                                            _row_tile(ctx_len, 128))
                hs_l.append(_rglru_grid_direction(p3_l, s_c, cp, rev)[:3])
                hs_c.append(h_c.reshape(n_ctx, cw))
            act_l = _readout_grid(p3_l, hs_l[0], hs_l[1]).reshape(n_lat, cw)
            xl = _mm_res(act_l, w_out, xl, lat_vec(2), seqlen, tm_cap=2048)
            if with_ctx:
                act_c = _readout_c(p_c, hs_c[0], hs_c[1])
                xc = _mm_res(act_c, w_out, xc, ctx_vec(2), n_ctx)

        ffn_up = [ffn_w_gate, ffn_w_up]
        wd = ffn_w_down[layer].astype(BF16)
        act = _norm_mm(xl, nw2, lat_vec(3), lat_vec(4), ffn_up, seqlen, BF16, layer=layer)
        xl = _mm_res(act, wd, xl, lat_vec(5), seqlen)
        if with_ctx:
            act = _norm_mm(xc, nw2, ctx_vec(3), ctx_vec(4), ffn_up, n_ctx, BF16, layer=layer)
            xc = _mm_res(act, wd, xc, ctx_vec(5), n_ctx)

    return _final_norm(xl, final_norm_w).reshape(bsz, seqlen, d)
```
